```python
import jax, jax.numpy as jnp
from jax import lax
import numpy as np

D_MODEL = 1024
BATCH = 16
SEQ = 2048
DEPTH = 1

HEAD_DIM = 64
MIX_WIDTH = D_MODEL
RWKV_WIDTH = MIX_WIDTH // 2
MOBA_WIDTH = MIX_WIDTH - RWKV_WIDTH
RWKV_HEADS = RWKV_WIDTH // HEAD_DIM
MOBA_HEADS = MOBA_WIDTH // HEAD_DIM
DECAY_LORA = max(32, int(round(1.8 * D_MODEL ** 0.5 / 32)) * 32)
AAA_LORA = max(32, int(round(1.8 * D_MODEL ** 0.5 / 32)) * 32)
GATE_LORA = max(32, int(round(0.6 * D_MODEL ** 0.8 / 32)) * 32)
RWKV_LN_EPS = 64e-5
RWKV_PROJ = 3 * RWKV_WIDTH + DECAY_LORA + AAA_LORA + GATE_LORA
MOBA_PROJ = 3 * MOBA_WIDTH
IN_COLS = RWKV_PROJ + MOBA_PROJ
MOBA_BLOCK = 256
MOBA_TOPK = 3
MOBA_QUERY_CHUNK = 16
N_EXPERTS = 32
TOP_K = 4
D_FF = D_MODEL
SWIGLU_LIMIT = 7.0
SWIGLU_ALPHA = 1.702
EXPERT_ROWS = 128
NORM_EPS = 1e-6

kernel_name = 'hybrid_rwkv7_moba_moe_adaln_block'


def rms_norm(x, g):
    xf = x.astype(jnp.float32)
    y = xf * lax.rsqrt(jnp.mean(xf * xf, axis=-1, keepdims=True) + NORM_EPS)
    return (y * g.astype(jnp.float32)).astype(x.dtype)


def token_shift(p):
    return jnp.pad(p, ((0, 0), (1, 0), (0, 0)))[:, :-1]


def rwkv7_time_mix(p, mu, w0, w_up, a0, a_up, g_up, k_k, k_a, r_k, ln_g, ln_b):
    B, S, _ = p.shape
    H, N, W = RWKV_HEADS, HEAD_DIM, RWKV_WIDTH
    f32 = jnp.float32
    p = p + (token_shift(p) - p) * mu
    r, k, v, xw, xa, xg = jnp.split(
        p, [W, 2 * W, 3 * W, 3 * W + DECAY_LORA, 3 * W + DECAY_LORA + AAA_LORA], axis=-1)
    w_log = -jax.nn.softplus(-(w0 + jnp.tanh(xw) @ w_up)) - 0.5
    decay = jnp.exp(-jnp.exp(w_log.astype(f32)))
    a = jax.nn.sigmoid(a0 + xa @ a_up)
    g = jax.nn.sigmoid(xg) @ g_up
    heads = lambda t: t.reshape(B, S, H, N).astype(f32)
    kk = heads(k * k_k)
    kk = kk / jnp.maximum(jnp.linalg.norm(kk, axis=-1, keepdims=True), 1e-12)
    k = k * (1.0 + (a - 1.0) * k_a)
    r_h, k_h, v_h, w_h, a_h = heads(r), heads(k), heads(v), heads(decay), heads(a)

    def step(state, inp):
        r_t, w_t, k_t, v_t, kk_t, a_t = inp
        sa = jnp.einsum('bhij,bhj->bhi', state, -kk_t)
        state = (state * w_t[:, :, None, :]
                 + sa[..., None] * (kk_t * a_t)[:, :, None, :]
                 + v_t[..., None] * k_t[:, :, None, :])
        y_t = jnp.einsum('bhij,bhj->bhi', state, r_t)
        return state, y_t

    sf = lambda t: jnp.moveaxis(t, 1, 0)
    state0 = jnp.zeros((B, H, N, N), f32)
    _, y = lax.scan(step, state0, (sf(r_h), sf(w_h), sf(k_h), sf(v_h), sf(kk), sf(a_h)))
    y = jnp.moveaxis(y, 0, 1)
    mean = jnp.mean(y, axis=-1, keepdims=True)
    var = jnp.mean(jnp.square(y - mean), axis=-1, keepdims=True)
    y = ((y - mean) * lax.rsqrt(var + RWKV_LN_EPS)).reshape(B, S, W) * ln_g + ln_b
    bonus = jnp.sum(r_h * k_h * r_k, axis=-1, keepdims=True) * v_h
    y = (y + bonus.reshape(B, S, W)) * g
    return y.astype(p.dtype)


def moba_attention(p, q_norm_g, k_norm_g):
    B, S, _ = p.shape
    H, N, BLK, QC = MOBA_HEADS, HEAD_DIM, MOBA_BLOCK, MOBA_QUERY_CHUNK
    f32 = jnp.float32
    to_heads = lambda t: t.reshape(B, S, H, N).transpose(0, 2, 1, 3)
    q, k, v = [to_heads(t) for t in jnp.split(p, 3, axis=-1)]
    q = rms_norm(q, q_norm_g)
    k = rms_norm(k, k_norm_g)
    NB = -(-S // BLK)
    n_sel = min(MOBA_TOPK, NB)
    pad = NB * BLK - S
    kb = jnp.pad(k, ((0, 0), (0, 0), (0, pad), (0, 0))).reshape(B, H, NB, BLK, N)
    vb = jnp.pad(v, ((0, 0), (0, 0), (0, pad), (0, 0))).reshape(B, H, NB, BLK, N)
    kmean = jnp.mean(kb.astype(f32), axis=3)
    slopes = jnp.exp2(-8.0 * (jnp.arange(H, dtype=f32) + 1.0) / H)
    scale = HEAD_DIM ** -0.5
    bi = jnp.arange(B)[:, None, None, None]
    hi = jnp.arange(H)[None, :, None, None]
    n_chunks = S // QC
    q_chunks = jnp.moveaxis(q.reshape(B, H, n_chunks, QC, N), 2, 0)

    def attend_chunk(args):
        qi, ci = args
        t0 = ci * QC
        j = t0 // BLK
        tq = t0 + jnp.arange(QC)
        gate = jnp.einsum('bhqd,bhnd->bhqn', qi.astype(f32), kmean)
        gate = jnp.where(jnp.arange(NB) < j, gate, -jnp.inf)
        _, sel = lax.top_k(gate, n_sel)
        ksel = kb[bi, hi, sel]
        vsel = vb[bi, hi, sel]
        kpos_sel = sel[..., None] * BLK + jnp.arange(BLK)
        s_sel = (jnp.einsum('bhqd,bhqmkd->bhqmk', qi, ksel).astype(f32) * scale
                 - slopes[:, None, None, None] * (tq[:, None, None] - kpos_sel))
        s_sel = jnp.where((jnp.arange(n_sel) < j)[:, None], s_sel, -jnp.inf)
        k_own = lax.dynamic_index_in_dim(kb, j, axis=2, keepdims=False)
        v_own = lax.dynamic_index_in_dim(vb, j, axis=2, keepdims=False)
        kpos_own = j * BLK + jnp.arange(BLK)
        s_own = (jnp.einsum('bhqd,bhkd->bhqk', qi, k_own).astype(f32) * scale
                 - slopes[:, None, None] * (tq[:, None] - kpos_own[None, :]))
        s_own = jnp.where(kpos_own[None, :] <= tq[:, None], s_own, -jnp.inf)
        s = jnp.concatenate([s_sel.reshape(B, H, QC, n_sel * BLK), s_own], axis=-1)
        prob = jax.nn.softmax(s, axis=-1)
        p_sel = prob[..., :n_sel * BLK].reshape(B, H, QC, n_sel, BLK)
        p_own = prob[..., n_sel * BLK:]
        out = (jnp.einsum('bhqmk,bhqmkd->bhqd', p_sel, vsel)
               + jnp.einsum('bhqk,bhkd->bhqd', p_own, v_own))
        return out.astype(qi.dtype)

    out = lax.map(attend_chunk, (q_chunks, jnp.arange(n_chunks, dtype=jnp.int32)))
    out = jnp.moveaxis(out, 0, 2).reshape(B, H, S, N)
    return out.transpose(0, 2, 1, 3).reshape(B, S, MOBA_WIDTH)


def moe_ffn(h, w_router, b_router, w_gate_up, b_gate_up, w_down, b_down):
    T, D = h.shape
    logits = (h @ w_router + b_router).astype(jnp.float32)
    top_val, top_idx = lax.top_k(logits, TOP_K)
    weights = jax.nn.softmax(top_val, axis=-1)
    M = T * TOP_K
    slot_expert = top_idx.reshape(M)
    slot_token = jnp.arange(M, dtype=jnp.int32) // TOP_K
    slot_weight = weights.reshape(M)
    order = jnp.argsort(slot_expert)
    e_sorted = slot_expert[order]
    counts = jnp.bincount(slot_expert, length=N_EXPERTS)
    padded = (counts + EXPERT_ROWS - 1) // EXPERT_ROWS * EXPERT_ROWS
    pad_end = jnp.cumsum(padded)
    pad_start = pad_end - padded
    start = jnp.cumsum(counts) - counts
    dest = pad_start[e_sorted] + jnp.arange(M) - start[e_sorted]
    P = M + N_EXPERTS * EXPERT_ROWS
    n_blocks = P // EXPERT_ROWS
    buf_token = jnp.zeros((P,), jnp.int32).at[dest].set(slot_token[order])
    buf_weight = jnp.zeros((P,), jnp.float32).at[dest].set(slot_weight[order])
    blk_expert = jnp.minimum(
        jnp.searchsorted(pad_end, jnp.arange(n_blocks) * EXPERT_ROWS, side='right'), N_EXPERTS - 1)

    def expert_block(args):
        tok, e = args
        xb = h[tok]
        gu = xb @ w_gate_up[e] + b_gate_up[e]
        gate = jnp.minimum(gu[:, :D_FF], SWIGLU_LIMIT)
        up = jnp.clip(gu[:, D_FF:], -SWIGLU_LIMIT, SWIGLU_LIMIT)
        act = (up + 1.0) * gate * jax.nn.sigmoid(SWIGLU_ALPHA * gate)
        return act @ w_down[e] + b_down[e]

    y = lax.map(expert_block, (buf_token.reshape(n_blocks, EXPERT_ROWS), blk_expert)).reshape(P, D)
    return jax.ops.segment_sum(y * buf_weight[:, None], buf_token, num_segments=T)


def hybrid_layer(x, c, w_ada, b_ada, norm1_g, w_in, rwkv_mu, rwkv_w0, rwkv_w_up, rwkv_a0,
                 rwkv_a_up, rwkv_g_up, rwkv_k_k, rwkv_k_a, rwkv_r_k, rwkv_ln_g, rwkv_ln_b,
                 q_norm_g, k_norm_g, w_out, norm2_g, w_router, b_router, w_gate_up,
                 b_gate_up, w_down, b_down):
    B, S, D = x.shape
    mods = jax.nn.silu(c) @ w_ada + b_ada
    shift1, scale1, gate1, shift2, scale2, gate2 = jnp.split(mods[:, None, :], 6, axis=-1)
    h = rms_norm(x, norm1_g) * (1.0 + scale1) + shift1
    proj = h @ w_in
    y_rwkv = rwkv7_time_mix(proj[..., :RWKV_PROJ], rwkv_mu, rwkv_w0, rwkv_w_up, rwkv_a0,
                            rwkv_a_up, rwkv_g_up, rwkv_k_k, rwkv_k_a, rwkv_r_k,
                            rwkv_ln_g, rwkv_ln_b)
    y_moba = moba_attention(proj[..., RWKV_PROJ:], q_norm_g, k_norm_g)
    x = x + gate1 * (jnp.concatenate([y_rwkv, y_moba], axis=-1) @ w_out)
    h = rms_norm(x, norm2_g) * (1.0 + scale2) + shift2
    y = moe_ffn(h.reshape(B * S, D), w_router, b_router, w_gate_up, b_gate_up, w_down, b_down)
    return x + gate2 * y.reshape(B, S, D).astype(x.dtype)


def setup_inputs(seed: int = 0) -> dict:
    key = jax.random.key(seed)
    ks = jax.random.split(key, 27)
    nrm = lambda k, shape, s: jax.random.normal(k, shape, jnp.float32) * s
    L = DEPTH
    return {
        'x': nrm(ks[0], (BATCH, SEQ, D_MODEL), 1.0),
        'c': nrm(ks[1], (BATCH, D_MODEL), 1.0),
        'w_ada': nrm(ks[2], (L, D_MODEL, 6 * D_MODEL), 0.5 * D_MODEL ** -0.5),
        'b_ada': nrm(ks[3], (L, 6 * D_MODEL), 0.02),
        'norm1_g': 1.0 + nrm(ks[4], (L, D_MODEL), 0.05),
        'w_in': nrm(ks[5], (L, D_MODEL, IN_COLS), D_MODEL ** -0.5),
        'rwkv_mu': jax.random.uniform(ks[6], (L, RWKV_PROJ), jnp.float32),
        'rwkv_w0': jax.random.uniform(ks[7], (L, RWKV_WIDTH), jnp.float32, minval=-6.0, maxval=0.0),
        'rwkv_w_up': nrm(ks[8], (L, DECAY_LORA, RWKV_WIDTH), 0.3 * DECAY_LORA ** -0.5),
        'rwkv_a0': nrm(ks[9], (L, RWKV_WIDTH), 0.5),
        'rwkv_a_up': nrm(ks[10], (L, AAA_LORA, RWKV_WIDTH), 0.5 * AAA_LORA ** -0.5),
        'rwkv_g_up': nrm(ks[11], (L, GATE_LORA, RWKV_WIDTH), GATE_LORA ** -0.5),
        'rwkv_k_k': 0.85 + nrm(ks[12], (L, RWKV_WIDTH), 0.05),
        'rwkv_k_a': 1.0 + nrm(ks[13], (L, RWKV_WIDTH), 0.05),
        'rwkv_r_k': nrm(ks[14], (L, RWKV_HEADS, HEAD_DIM), 0.1),
        'rwkv_ln_g': 1.0 + nrm(ks[15], (L, RWKV_WIDTH), 0.05),
        'rwkv_ln_b': nrm(ks[16], (L, RWKV_WIDTH), 0.02),
        'q_norm_g': 1.0 + nrm(ks[17], (L, HEAD_DIM), 0.05),
        'k_norm_g': 1.0 + nrm(ks[18], (L, HEAD_DIM), 0.05),
        'w_out': nrm(ks[19], (L, MIX_WIDTH, D_MODEL), MIX_WIDTH ** -0.5),
        'norm2_g': 1.0 + nrm(ks[20], (L, D_MODEL), 0.05),
        'w_router': nrm(ks[21], (L, D_MODEL, N_EXPERTS), D_MODEL ** -0.5),
        'b_router': nrm(ks[22], (L, N_EXPERTS), 0.01),
        'w_gate_up': nrm(ks[23], (L, N_EXPERTS, D_MODEL, 2 * D_FF), D_MODEL ** -0.5),
        'b_gate_up': nrm(ks[24], (L, N_EXPERTS, 2 * D_FF), 0.01),
        'w_down': nrm(ks[25], (L, N_EXPERTS, D_FF, D_MODEL), D_FF ** -0.5),
        'b_down': nrm(ks[26], (L, N_EXPERTS, D_MODEL), 0.01),
    }


def reference(x, c, w_ada, b_ada, norm1_g, w_in, rwkv_mu, rwkv_w0, rwkv_w_up, rwkv_a0,
              rwkv_a_up, rwkv_g_up, rwkv_k_k, rwkv_k_a, rwkv_r_k, rwkv_ln_g, rwkv_ln_b,
              q_norm_g, k_norm_g, w_out, norm2_g, w_router, b_router, w_gate_up,
              b_gate_up, w_down, b_down):
    for l in range(DEPTH):
        x = hybrid_layer(x, c, w_ada[l], b_ada[l], norm1_g[l], w_in[l], rwkv_mu[l], rwkv_w0[l],
                         rwkv_w_up[l], rwkv_a0[l], rwkv_a_up[l], rwkv_g_up[l], rwkv_k_k[l],
                         rwkv_k_a[l], rwkv_r_k[l], rwkv_ln_g[l], rwkv_ln_b[l], q_norm_g[l],
                         k_norm_g[l], w_out[l], norm2_g[l], w_router[l], b_router[l],
                         w_gate_up[l], b_gate_up[l], w_down[l], b_down[l])
    return x
```

```python
import functools

import jax
import jax.numpy as jnp
from jax import lax
from jax.experimental import pallas as pl
from jax.experimental.pallas import tpu as pltpu

F32 = jnp.float32
BF16 = jnp.bfloat16

D_MODEL = 1024
HEAD_DIM = 64
RWKV_WIDTH = 512
MOBA_WIDTH = 512
RWKV_HEADS = RWKV_WIDTH // HEAD_DIM
MOBA_HEADS = MOBA_WIDTH // HEAD_DIM
DECAY_LORA = 64
AAA_LORA = 64
GATE_LORA = 160
RWKV_LN_EPS = 64e-5
RWKV_PROJ = 3 * RWKV_WIDTH + DECAY_LORA + AAA_LORA + GATE_LORA
MOBA_PROJ = 3 * MOBA_WIDTH
MOBA_BLOCK = 256
MOBA_TOPK = 3
N_EXPERTS = 32
TOP_K = 4
D_FF = D_MODEL
SWIGLU_LIMIT = 7.0
SWIGLU_ALPHA = 1.702
NORM_EPS = 1e-6

LANES = 128
SUBLANES = 8
XW_OFF = 3 * RWKV_WIDTH
XA_OFF = XW_OFF + LANES
XG_OFF = XA_OFF + LANES
RWKV_COLS = XG_OFF + 2 * LANES
CHUNK = 64
EXPERT_TILE = 512
VMEM_LIMIT = 56 * 1024 * 1024

NN = (((1,), (0,)), ((), ()))
NT = (((1,), (1,)), ((), ()))
TN = (((0,), (0,)), ((), ()))


def _dot(a, b, dims=NN):
    return lax.dot_general(a, b, dims, preferred_element_type=F32)


def _split(a):
    hi = a.astype(BF16)
    lo = (a - hi.astype(F32)).astype(BF16)
    return hi, lo


def _mm(a, b, dims=NN, passes=1):
    if passes == 1:
        return _dot(a.astype(BF16), b.astype(BF16), dims)
    a_hi, a_lo = _split(a)
    b_hi, b_lo = _split(b)
    return _dot(a_hi, b_hi, dims) + (_dot(a_hi, b_lo, dims) + _dot(a_lo, b_hi, dims))


def _split3(a):
    hi = a.astype(BF16)
    r = a - hi.astype(F32)
    mid = r.astype(BF16)
    lo = (r - mid.astype(F32)).astype(BF16)
    return hi, mid, lo


def _mm_exact_rhs(a, b_bf16, dims=NN):
    hi, mid, lo = _split3(a)
    return _dot(hi, b_bf16, dims) + (_dot(mid, b_bf16, dims) + _dot(lo, b_bf16, dims))


def _mm_exact_lhs(a_bf16, b, dims=NN):
    hi, mid, lo = _split3(b)
    return _dot(a_bf16, hi, dims) + (_dot(a_bf16, mid, dims) + _dot(a_bf16, lo, dims))


def _iota2(shape, dim):
    return lax.broadcasted_iota(jnp.int32, shape, dim)


def _group_ones(n, group):
    return (_iota2((n, n), 0) // group == _iota2((n, n), 1) // group).astype(BF16)


def _sigmoid(x):
    return 1.0 / (1.0 + jnp.exp(-x))


def _params(*sem):
    return pltpu.CompilerParams(dimension_semantics=sem, vmem_limit_bytes=VMEM_LIMIT)


def _ada_kernel(c_ref, w_ref, b_ref, o_ref):
    c = c_ref[...]
    o_ref[...] = _mm(c * _sigmoid(c), w_ref[...], passes=3) + b_ref[...]


def _ada(c, w_ada, b_ada):
    B, D = c.shape
    n_out = w_ada.shape[1]
    tn = 1024
    return pl.pallas_call(
        _ada_kernel,
        grid=(n_out // tn,),
        in_specs=[pl.BlockSpec((B, D), lambda j: (0, 0)),
                  pl.BlockSpec((D, tn), lambda j: (0, j)),
                  pl.BlockSpec((1, tn), lambda j: (0, j))],
        out_specs=pl.BlockSpec((B, tn), lambda j: (0, j)),
        out_shape=jax.ShapeDtypeStruct((B, n_out), F32),
        compiler_params=_params("arbitrary"),
        name="ada",
    )(c, w_ada, b_ada.reshape(1, n_out))


def _rms_modulate(x, g, scale, shift):
    y = x * lax.rsqrt(jnp.mean(x * x, axis=-1, keepdims=True) + NORM_EPS)
    return (y * g) * (1.0 + scale) + shift


def _in_proj_kernel(x_ref, g_ref, scale_ref, shift_ref, w_ref, pr_ref, pm_ref):
    h = _rms_modulate(x_ref[...], g_ref[...], scale_ref[0], shift_ref[0])
    proj = _dot(h.astype(BF16), w_ref[...])
    pr_ref[...] = proj[:, :RWKV_COLS]
    pm_ref[...] = proj[:, RWKV_COLS:]


def _in_proj(x2, norm_g, scale, shift, w_in_b, S):
    T, D = x2.shape
    tm = 256
    per_b = S // tm
    n_cols = w_in_b.shape[1]
    return pl.pallas_call(
        _in_proj_kernel,
        grid=(T // tm,),
        in_specs=[pl.BlockSpec((tm, D), lambda i: (i, 0)),
                  pl.BlockSpec((1, D), lambda i: (0, 0)),
                  pl.BlockSpec((1, 1, D), lambda i: (i // per_b, 0, 0)),
                  pl.BlockSpec((1, 1, D), lambda i: (i // per_b, 0, 0)),
                  pl.BlockSpec((D, n_cols), lambda i: (0, 0))],
        out_specs=[pl.BlockSpec((tm, RWKV_COLS), lambda i: (i, 0)),
                   pl.BlockSpec((tm, MOBA_PROJ), lambda i: (i, 0))],
        out_shape=[jax.ShapeDtypeStruct((T, RWKV_COLS), F32),
                   jax.ShapeDtypeStruct((T, MOBA_PROJ), F32)],
        compiler_params=_params("arbitrary"),
        name="in_proj",
    )(x2, norm_g.reshape(1, D), scale, shift, w_in_b)


RWKV_PASSES = 3


def _unit_lower_inverse(L, passes):
    n = L.shape[0]
    r = _iota2((n, n), 0)
    c = _iota2((n, n), 1)
    eye = (r == c).astype(F32)
    Ld = jnp.where(r // 8 == c // 8, L, 0.0)
    Nb = L - Ld
    mm = functools.partial(_mm, passes=passes)
    L2 = mm(Ld, Ld)
    L4 = mm(L2, L2)
    T0 = eye + Ld + L2 + mm(Ld, L2)
    T0 = T0 + mm(T0, L4)
    M1 = mm(T0, Nb)
    M2 = mm(M1, M1)
    M4 = mm(M2, M2)
    X = T0 + mm(M4, T0)
    X = X + mm(M2, X)
    return X + mm(M1, X)


def _rwkv_kernel(p_ref, mu_ref, w0_ref, a0_ref, kk_ref, ka_ref, rk_ref, lng_ref, lnb_ref,
                 wup_ref, aup_ref, gup_ref, y_ref, carry_ref, state_ref, *, ts):
    t = pl.program_id(1)
    W = RWKV_WIDTH
    passes = RWKV_PASSES

    @pl.when(t == 0)
    def _():
        carry_ref[...] = jnp.zeros_like(carry_ref)
        state_ref[...] = jnp.zeros_like(state_ref)

    p = p_ref[0]
    row = _iota2((ts, 1), 0)
    prev = jnp.where(row == 0, carry_ref[...], pltpu.roll(p, 1, 0))
    carry_ref[...] = p[ts - 1:ts, :]
    pm = p + (prev - p) * mu_ref[...]
    r = pm[:, 0:W]
    k = pm[:, W:2 * W]
    v = pm[:, 2 * W:3 * W]
    xw = pm[:, XW_OFF:XA_OFF]
    xa = pm[:, XA_OFF:XG_OFF]
    xg = pm[:, XG_OFF:RWKV_COLS]

    z = -(w0_ref[...] + _mm(jnp.tanh(xw), wup_ref[...], passes=3))
    softplus = jnp.maximum(z, 0.0) + jnp.log(1.0 + jnp.exp(-jnp.abs(z)))
    logd = -jnp.exp(-softplus - 0.5)
    alpha = _sigmoid(a0_ref[...] + _mm(xa, aup_ref[...], passes=3))
    gate = _mm(_sigmoid(xg), gup_ref[...])

    head_ones = _group_ones(W, HEAD_DIM)
    kk = k * kk_ref[...]
    kk_norm = jnp.sqrt(_mm_exact_rhs(kk * kk, head_ones))
    kk = kk / jnp.maximum(kk_norm, 1e-12)
    kmod = k * (1.0 + (alpha - 1.0) * ka_ref[...])
    bonus = _mm_exact_rhs(r * kmod * rk_ref[...], head_ones) * v

    tr = _iota2((ts, ts), 0)
    tc = _iota2((ts, ts), 1)
    cum = ((tr // CHUNK == tc // CHUNK) & (tc <= tr)).astype(BF16)
    logp = _mm_exact_lhs(cum, logd)
    inv_p = jnp.exp(-logp)
    a_t = -kk * jnp.exp(logp - logd)
    b_t = kk * alpha * inv_p
    k_t = kmod * inv_p
    r_t = r * jnp.exp(logp)

    n2 = 2 * CHUNK
    lane = _iota2((1, LANES), 1)
    m0 = lane < HEAD_DIM
    sr = _iota2((n2, n2), 0)
    sc = _iota2((n2, n2), 1)
    strict_lower = sc < sr
    incl_lower = sc <= sr
    mm = functools.partial(_mm, passes=passes)

    def stack(xt):
        return jnp.concatenate([jnp.where(m0, xt, 0.0), jnp.where(m0, 0.0, xt)], axis=0)

    y_chunks = []
    for ci in range(ts // CHUNK):
        rows = slice(ci * CHUNK, (ci + 1) * CHUNK)
        last = (ci + 1) * CHUNK - 1
        y_pairs = []
        for pi in range(RWKV_HEADS // 2):
            cols = slice(pi * LANES, (pi + 1) * LANES)
            pc = jnp.exp(logp[last:last + 1, cols])
            a_s = stack(a_t[rows, cols])
            r_s = stack(r_t[rows, cols])
            b_s = stack(b_t[rows, cols])
            k_s = stack(k_t[rows, cols])
            v_s = stack(v[rows, cols])
            ar = jnp.concatenate([a_s, r_s], axis=0)
            bk = jnp.concatenate([b_s, k_s], axis=0)
            gram = mm(ar, bk, NT)
            l_ab = jnp.where(strict_lower, gram[:n2, :n2], 0.0)
            l_ak = jnp.where(strict_lower, gram[:n2, n2:], 0.0)
            m_rb = jnp.where(incl_lower, gram[n2:, :n2], 0.0)
            m_rk = jnp.where(incl_lower, gram[n2:, n2:], 0.0)
            s0 = state_ref[pi]
            ar_s0 = mm(ar, s0, NT)
            rhs = ar_s0[:n2] + mm(l_ak, v_s)
            u = mm(_unit_lower_inverse(l_ab, passes), rhs)
            uv = jnp.concatenate([u, v_s], axis=0)
            y_s = ar_s0[n2:] + mm(jnp.concatenate([m_rb, m_rk], axis=1), uv)
            y_pairs.append(y_s[:CHUNK] + y_s[CHUNK:])
            state_ref[pi] = s0 * pc + mm(uv, bk * pc, TN)
        y_chunks.append(jnp.concatenate(y_pairs, axis=1))
    y = jnp.concatenate(y_chunks, axis=0) if len(y_chunks) > 1 else y_chunks[0]

    mean = _mm_exact_rhs(y, head_ones) * (1.0 / HEAD_DIM)
    yc = y - mean
    var = _mm_exact_rhs(yc * yc, head_ones) * (1.0 / HEAD_DIM)
    yn = yc * lax.rsqrt(var + RWKV_LN_EPS) * lng_ref[...] + lnb_ref[...]
    y_ref[0] = (yn + bonus) * gate


def _rwkv(p_rwkv, prm, B, S, ts=CHUNK):
    W = RWKV_WIDTH
    vec = lambda n: pl.BlockSpec((1, n), lambda b, t: (0, 0))
    mat = lambda m, n: pl.BlockSpec((m, n), lambda b, t: (0, 0))
    return pl.pallas_call(
        functools.partial(_rwkv_kernel, ts=ts),
        grid=(B, S // ts),
        in_specs=[pl.BlockSpec((1, ts, RWKV_COLS), lambda b, t: (b, t, 0)),
                  vec(RWKV_COLS)] + [vec(W)] * 7 + [mat(LANES, W), mat(LANES, W), mat(2 * LANES, W)],
        out_specs=pl.BlockSpec((1, ts, W), lambda b, t: (b, t, 0)),
        out_shape=jax.ShapeDtypeStruct((B, S, W), F32),
        scratch_shapes=[pltpu.VMEM((1, RWKV_COLS), F32),
                        pltpu.VMEM((RWKV_HEADS // 2, LANES, LANES), F32)],
        compiler_params=_params("arbitrary", "arbitrary"),
        name="rwkv",
    )(p_rwkv.reshape(B, S, RWKV_COLS), *prm)


def _moba_kernel(slopes_ref, q_ref, k_ref, v_ref, qg_ref, kg_ref, o_ref, *, S):
    pair = pl.program_id(1)
    NB = S // MOBA_BLOCK
    BLK = MOBA_BLOCK
    n_sel = min(MOBA_TOPK, NB)
    scale = HEAD_DIM ** -0.5
    head_ones = _group_ones(LANES, HEAD_DIM)
    lane = _iota2((1, LANES), 1)

    def head_norm(x, g):
        ss = _mm_exact_rhs(x * x, head_ones)
        return x * lax.rsqrt(ss * (1.0 / HEAD_DIM) + NORM_EPS) * g

    qn = head_norm(q_ref[0], qg_ref[...])
    kn = head_norm(k_ref[0], kg_ref[...])
    kmean = jnp.mean(kn.reshape(NB, BLK, LANES), axis=1)
    q_t = qn.T
    q_tb = q_t.astype(BF16)
    v_tb = v_ref[0].T.astype(BF16)

    blk_of_q = _iota2((1, S), 1) // BLK
    nidx = _iota2((NB, 1), 0)
    valid = nidx < blk_of_q
    tq_all = _iota2((1, S), 1).astype(F32)
    kpos_all = _iota2((S, 1), 0).astype(F32)

    out_rows = []
    for h in range(2):
        hmask = (lane // HEAD_DIM) == h
        slope = slopes_ref[pair * 2 + h]
        gate = _mm(jnp.where(hmask, kmean, 0.0), q_t, passes=3)
        gate = jnp.where(valid, gate, -jnp.inf)
        rank = jnp.zeros((NB, S), jnp.int32)
        for m in range(NB):
            gm = gate[m:m + 1, :]
            ahead = (gm > gate) | ((gm == gate) & (m < nidx))
            rank = rank + ahead.astype(jnp.int32)
        sel = valid & (rank < n_sel)
        k_hb = jnp.where(hmask, kn, 0.0).astype(BF16)
        v_h = v_tb[h * HEAD_DIM:(h + 1) * HEAD_DIM, :]
        out_blocks = []
        for i in range(NB):
            qs = slice(i * BLK, (i + 1) * BLK)
            tq = tq_all[:, qs]
            tiles = []
            m_run = None
            for n in range(i + 1):
                ks = slice(n * BLK, (n + 1) * BLK)
                kpos = kpos_all[ks, :]
                s = _dot(k_hb[ks, :], q_tb[:, qs]) * scale - slope * (tq - kpos)
                if n < i:
                    s = jnp.where(sel[n:n + 1, qs], s, -jnp.inf)
                else:
                    s = jnp.where(kpos <= tq, s, -jnp.inf)
                tiles.append(s)
                mx = jnp.max(s, axis=0, keepdims=True)
                m_run = mx if m_run is None else jnp.maximum(m_run, mx)
            l_run = jnp.zeros((1, BLK), F32)
            acc = jnp.zeros((HEAD_DIM, BLK), F32)
            for n in range(i + 1):
                ks = slice(n * BLK, (n + 1) * BLK)
                pt = jnp.exp(tiles[n] - m_run)
                l_run = l_run + jnp.sum(pt, axis=0, keepdims=True)
                acc = acc + _dot(v_h[:, ks], pt.astype(BF16))
            out_blocks.append(acc / l_run)
        out_rows.append(jnp.concatenate(out_blocks, axis=1))
    o_ref[0] = jnp.concatenate(out_rows, axis=0).T


def _moba(p_moba, q_norm_g, k_norm_g, B, S):
    pairs = MOBA_HEADS // 2
    col = lambda off: pl.BlockSpec((1, S, LANES), lambda b, p: (b, 0, off + p))
    gain = pl.BlockSpec((1, LANES), lambda b, p: (0, 0))
    tile2 = lambda g: jnp.concatenate([g, g]).reshape(1, LANES)
    p3 = p_moba.reshape(B, S, MOBA_PROJ)
    slopes = jnp.exp2(-8.0 * (jnp.arange(MOBA_HEADS, dtype=F32) + 1.0) / MOBA_HEADS)
    return pl.pallas_call(
        functools.partial(_moba_kernel, S=S),
        grid=(B, pairs),
        in_specs=[pl.BlockSpec(memory_space=pltpu.SMEM), col(0), col(pairs), col(2 * pairs),
                  gain, gain],
        out_specs=pl.BlockSpec((1, S, LANES), lambda b, p: (b, 0, p)),
        out_shape=jax.ShapeDtypeStruct((B, S, MOBA_WIDTH), F32),
        compiler_params=_params("arbitrary", "arbitrary"),
        name="moba",
    )(slopes, p3, p3, p3, tile2(q_norm_g), tile2(k_norm_g))


def _out_proj_kernel(yr_ref, ym_ref, x_ref, gate_ref, g_ref, scale_ref, shift_ref, w_ref,
                     wr_ref, br_ref, x1_ref, h2_ref, logit_ref):
    W = RWKV_WIDTH
    mix = (_dot(yr_ref[...].astype(BF16), w_ref[0:W, :])
           + _dot(ym_ref[...].astype(BF16), w_ref[W:, :]))
    x1 = x_ref[...] + gate_ref[0] * mix
    x1_ref[...] = x1
    h2 = _rms_modulate(x1, g_ref[...], scale_ref[0], shift_ref[0])
    h2_ref[...] = h2
    logit_ref[...] = _mm(h2, wr_ref[...], passes=3) + br_ref[...]


def _out_proj(y_rwkv, y_moba, x2, gate1, norm_g, scale, shift, w_out_b, w_router, b_router, S):
    T, D = x2.shape
    tm = 256
    per_b = S // tm
    rows = lambda n: pl.BlockSpec((tm, n), lambda i: (i, 0))
    mod = pl.BlockSpec((1, 1, D), lambda i: (i // per_b, 0, 0))
    full = lambda m, n: pl.BlockSpec((m, n), lambda i: (0, 0))
    return pl.pallas_call(
        _out_proj_kernel,
        grid=(T // tm,),
        in_specs=[rows(RWKV_WIDTH), rows(MOBA_WIDTH), rows(D), mod, full(1, D), mod, mod,
                  full(D, D), full(D, N_EXPERTS), full(1, N_EXPERTS)],
        out_specs=[rows(D), rows(D), rows(N_EXPERTS)],
        out_shape=[jax.ShapeDtypeStruct((T, D), F32), jax.ShapeDtypeStruct((T, D), F32),
                   jax.ShapeDtypeStruct((T, N_EXPERTS), F32)],
        compiler_params=_params("arbitrary"),
        name="out_proj",
    )(y_rwkv, y_moba, x2, gate1, norm_g.reshape(1, D), scale, shift, w_out_b, w_router,
      b_router.reshape(1, N_EXPERTS))


def _experts_kernel(meta_ref, src_cur_ref, src_nxt_ref, dst_ref, h_hbm, wgu_ref, bgu_ref, wd_ref,
                    bd_ref, y_hbm, xbuf, obuf, gsem, ssem, *, n_tiles):
    i = pl.program_id(0)
    n_used = meta_ref[0]
    tm = EXPERT_TILE
    slot = i % 2
    valid = lambda j: meta_ref[1 + n_tiles + j]

    def gather(idx_ref, s):
        def body(r, carry):
            pltpu.make_async_copy(h_hbm.at[pl.ds(idx_ref[0, 0, r], 1)], xbuf.at[s, pl.ds(r, 1)],
                                  gsem.at[s]).start()
            return carry
        lax.fori_loop(0, tm, body, 0)

    def wait_gather(s):
        pltpu.make_async_copy(h_hbm.at[pl.ds(0, tm)], xbuf.at[s], gsem.at[s]).wait()

    def wait_scatter(s, n):
        n_tiled = pl.multiple_of((n // SUBLANES) * SUBLANES, SUBLANES)

        @pl.when(n_tiled > 0)
        def _():
            pltpu.make_async_copy(obuf.at[s, pl.ds(0, n_tiled)], y_hbm.at[pl.ds(0, n_tiled)],
                                  ssem.at[s]).wait()

        def body(r, carry):
            pltpu.make_async_copy(obuf.at[s, pl.ds(0, 1)], y_hbm.at[pl.ds(0, 1)], ssem.at[s]).wait()
            return carry
        lax.fori_loop(n_tiled, n, body, 0)

    @pl.when(i == 0)
    def _():
        gather(src_cur_ref, 0)

    @pl.when(i + 1 < n_used)
    def _():
        gather(src_nxt_ref, 1 - slot)

    @pl.when(i < n_used)
    def _():
        wait_gather(slot)

        @pl.when(i >= 2)
        def _():
            wait_scatter(slot, valid(i - 2))

        xb = xbuf[slot].astype(BF16)
        gu = _dot(xb, wgu_ref[0]) + bgu_ref[0]
        g = jnp.minimum(gu[:, :D_FF], SWIGLU_LIMIT)
        up = jnp.clip(gu[:, D_FF:], -SWIGLU_LIMIT, SWIGLU_LIMIT)
        act = (up + 1.0) * g * _sigmoid(SWIGLU_ALPHA * g)
        obuf[slot] = _dot(act.astype(BF16), wd_ref[0]) + bd_ref[0]

        def body(r, carry):
            pltpu.make_async_copy(obuf.at[slot, pl.ds(r, 1)], y_hbm.at[pl.ds(dst_ref[0, 0, r], 1)],
                                  ssem.at[slot]).start()
            return carry
        lax.fori_loop(0, valid(i), body, 0)

    @pl.when(i == n_used - 1)
    def _():
        wait_scatter(slot, valid(i))

        @pl.when(i >= 1)
        def _():
            wait_scatter(1 - slot, valid(i - 1))


def _experts(h2, src_tok, dst_row, tile_expert, tile_valid, n_used, wgu_b, bgu, wd_b, bd):
    T, D = h2.shape
    tm = EXPERT_TILE
    n_tiles = src_tok.shape[0] // tm
    src2 = src_tok.reshape(n_tiles, 1, tm)
    dst2 = dst_row.reshape(n_tiles, 1, tm)
    meta = jnp.concatenate([n_used.reshape(1), tile_expert, tile_valid]).astype(jnp.int32)
    smem_row = lambda f: pl.BlockSpec((1, 1, tm), f, memory_space=pltpu.SMEM)
    e_of = lambda i, m: m[1 + i]
    grid_spec = pltpu.PrefetchScalarGridSpec(
        num_scalar_prefetch=1,
        grid=(n_tiles,),
        in_specs=[smem_row(lambda i, m: (i, 0, 0)),
                  smem_row(lambda i, m: (jnp.minimum(i + 1, n_tiles - 1), 0, 0)),
                  smem_row(lambda i, m: (i, 0, 0)),
                  pl.BlockSpec(memory_space=pl.ANY),
                  pl.BlockSpec((1, D, 2 * D_FF), lambda i, m: (e_of(i, m), 0, 0)),
                  pl.BlockSpec((1, 1, 2 * D_FF), lambda i, m: (e_of(i, m), 0, 0)),
                  pl.BlockSpec((1, D_FF, D), lambda i, m: (e_of(i, m), 0, 0)),
                  pl.BlockSpec((1, 1, D), lambda i, m: (e_of(i, m), 0, 0))],
        out_specs=pl.BlockSpec(memory_space=pl.ANY),
        scratch_shapes=[pltpu.VMEM((2, tm, D), F32), pltpu.VMEM((2, tm, D), F32),
                        pltpu.SemaphoreType.DMA((2,)), pltpu.SemaphoreType.DMA((2,))],
    )
    return pl.pallas_call(
        functools.partial(_experts_kernel, n_tiles=n_tiles),
        grid_spec=grid_spec,
        out_shape=jax.ShapeDtypeStruct((T * TOP_K, D), F32),
        compiler_params=_params("arbitrary"),
        name="experts",
    )(meta, src2, src2, dst2, h2, wgu_b, bgu.reshape(N_EXPERTS, 1, 2 * D_FF), wd_b,
      bd.reshape(N_EXPERTS, 1, D))


def _route(logits):
    T = logits.shape[0]
    tm = EXPERT_TILE
    M = T * TOP_K
    top_val, top_idx = lax.top_k(logits, TOP_K)
    weights = jax.nn.softmax(top_val, axis=-1)
    slot_expert = top_idx.reshape(M)
    order = jnp.argsort(slot_expert)
    e_sorted = slot_expert[order]
    counts = jnp.bincount(slot_expert, length=N_EXPERTS)
    padded = (counts + tm - 1) // tm * tm
    pad_end = jnp.cumsum(padded)
    pad_start = pad_end - padded
    start = jnp.cumsum(counts) - counts
    dest = pad_start[e_sorted] + jnp.arange(M) - start[e_sorted]
    P = M + N_EXPERTS * tm
    n_tiles = P // tm
    src_tok = jnp.zeros((P,), jnp.int32).at[dest].set((order // TOP_K).astype(jnp.int32))
    dst_row = jnp.zeros((P,), jnp.int32).at[dest].set(order.astype(jnp.int32))
    tile_start = jnp.arange(n_tiles) * tm
    tile_expert = jnp.minimum(jnp.searchsorted(pad_end, tile_start, side='right'), N_EXPERTS - 1)
    tile_valid = jnp.clip(counts[tile_expert] - (tile_start - pad_start[tile_expert]), 0, tm)
    n_used = pad_end[-1] // tm
    return (weights, src_tok, dst_row, tile_expert.astype(jnp.int32), tile_valid.astype(jnp.int32),
            n_used.astype(jnp.int32))


def _combine_kernel(x1_ref, y_ref, w_ref, gate_ref, o_ref):
    D = D_MODEL
    w = w_ref[...]
    acc = w[:, 0:1] * y_ref[:, 0:D]
    for j in range(1, TOP_K):
        acc = acc + w[:, j:j + 1] * y_ref[:, j * D:(j + 1) * D]
    o_ref[...] = x1_ref[...] + gate_ref[0] * acc


def _combine(x1, y_slots, weights, gate2, S):
    T, D = x1.shape
    tm = 256
    per_b = S // tm
    y4 = y_slots.reshape(y_slots.shape[0] // TOP_K, TOP_K * D)
    return pl.pallas_call(
        _combine_kernel,
        grid=(T // tm,),
        in_specs=[pl.BlockSpec((tm, D), lambda i: (i, 0)),
                  pl.BlockSpec((tm, TOP_K * D), lambda i: (i, 0)),
                  pl.BlockSpec((tm, TOP_K), lambda i: (i, 0)),
                  pl.BlockSpec((1, 1, D), lambda i: (i // per_b, 0, 0))],
        out_specs=pl.BlockSpec((tm, D), lambda i: (i, 0)),
        out_shape=jax.ShapeDtypeStruct((T, D), F32),
        compiler_params=_params("arbitrary"),
        name="combine",
    )(x1, y4, weights, gate2)


def _pad_cols(w, n):
    return jnp.pad(w, ((0, 0), (0, n - w.shape[1])))


def _pad_rows(w, n):
    return jnp.pad(w, ((0, n - w.shape[0]), (0, 0)))


def _layer(x, c, w_ada, b_ada, norm1_g, w_in, rwkv_mu, rwkv_w0, rwkv_w_up, rwkv_a0, rwkv_a_up,
           rwkv_g_up, rwkv_k_k, rwkv_k_a, rwkv_r_k, rwkv_ln_g, rwkv_ln_b, q_norm_g, k_norm_g,
           w_out, norm2_g, w_router, b_router, w_gate_up, b_gate_up, w_down, b_down):
    B, S, D = x.shape
    T = B * S
    W = RWKV_WIDTH
    x2 = x.reshape(T, D)

    mods = _ada(c, w_ada, b_ada)
    shift1, scale1, gate1, shift2, scale2, gate2 = [
        mods[:, j * D:(j + 1) * D].reshape(B, 1, D) for j in range(6)]

    pieces = [(w_in[:, :XW_OFF], XW_OFF),
              (w_in[:, XW_OFF:XW_OFF + DECAY_LORA], LANES),
              (w_in[:, XW_OFF + DECAY_LORA:XW_OFF + DECAY_LORA + AAA_LORA], LANES),
              (w_in[:, XW_OFF + DECAY_LORA + AAA_LORA:RWKV_PROJ], 2 * LANES),
              (w_in[:, RWKV_PROJ:], MOBA_PROJ)]
    w_in_b = jnp.concatenate([_pad_cols(w, n) for w, n in pieces], axis=1).astype(BF16)
    mu_pieces = [(rwkv_mu[None, :XW_OFF], XW_OFF),
                 (rwkv_mu[None, XW_OFF:XW_OFF + DECAY_LORA], LANES),
                 (rwkv_mu[None, XW_OFF + DECAY_LORA:XW_OFF + DECAY_LORA + AAA_LORA], LANES),
                 (rwkv_mu[None, XW_OFF + DECAY_LORA + AAA_LORA:], 2 * LANES)]
    mu = jnp.concatenate([_pad_cols(m, n) for m, n in mu_pieces], axis=1)

    p_rwkv, p_moba = _in_proj(x2, norm1_g, scale1, shift1, w_in_b, S)

    row = lambda a: a.reshape(1, W)
    prm = (mu, row(rwkv_w0), row(rwkv_a0), row(rwkv_k_k), row(rwkv_k_a), row(rwkv_r_k),
           row(rwkv_ln_g), row(rwkv_ln_b), _pad_rows(rwkv_w_up, LANES), _pad_rows(rwkv_a_up, LANES),
           _pad_rows(rwkv_g_up, 2 * LANES))
    y_rwkv = _rwkv(p_rwkv, prm, B, S)
    y_moba = _moba(p_moba, q_norm_g, k_norm_g, B, S)

    x1, h2, logits = _out_proj(y_rwkv.reshape(T, W), y_moba.reshape(T, MOBA_WIDTH), x2, gate1,
                               norm2_g, scale2, shift2, w_out.astype(BF16), w_router, b_router, S)

    weights, src_tok, dst_row, tile_expert, tile_valid, n_used = _route(logits)
    y_slots = _experts(h2, src_tok, dst_row, tile_expert, tile_valid, n_used,
                       w_gate_up.astype(BF16), b_gate_up, w_down.astype(BF16), b_down)
    out = _combine(x1, y_slots, weights, gate2, S)
    return out.reshape(B, S, D)


def kernel(x, c, w_ada, b_ada, norm1_g, w_in, rwkv_mu, rwkv_w0, rwkv_w_up, rwkv_a0, rwkv_a_up, rwkv_g_up, rwkv_k_k, rwkv_k_a, rwkv_r_k, rwkv_ln_g, rwkv_ln_b, q_norm_g, k_norm_g, w_out, norm2_g, w_router, b_router, w_gate_up, b_gate_up, w_down, b_down):
    for l in range(w_ada.shape[0]):
        x = _layer(x, c, w_ada[l], b_ada[l], norm1_g[l], w_in[l], rwkv_mu[l], rwkv_w0[l],
                   rwkv_w_up[l], rwkv_a0[l], rwkv_a_up[l], rwkv_g_up[l], rwkv_k_k[l], rwkv_k_a[l],
                   rwkv_r_k[l], rwkv_ln_g[l], rwkv_ln_b[l], q_norm_g[l], k_norm_g[l], w_out[l],
                   norm2_g[l], w_router[l], b_router[l], w_gate_up[l], b_gate_up[l], w_down[l],
                   b_down[l])
    return x
```

```python
import functools

import jax
import jax.numpy as jnp
from jax import lax
from jax.experimental import pallas as pl
from jax.experimental.pallas import tpu as pltpu

F32 = jnp.float32
BF16 = jnp.bfloat16

D_MODEL = 1024
HEAD_DIM = 64
RWKV_WIDTH = 512
MOBA_WIDTH = 512
RWKV_HEADS = RWKV_WIDTH // HEAD_DIM
MOBA_HEADS = MOBA_WIDTH // HEAD_DIM
DECAY_LORA = 64
AAA_LORA = 64
GATE_LORA = 160
RWKV_LN_EPS = 64e-5
RWKV_PROJ = 3 * RWKV_WIDTH + DECAY_LORA + AAA_LORA + GATE_LORA
MOBA_PROJ = 3 * MOBA_WIDTH
MOBA_BLOCK = 256
MOBA_TOPK = 3
N_EXPERTS = 32
TOP_K = 4
D_FF = D_MODEL
SWIGLU_LIMIT = 7.0
SWIGLU_ALPHA = 1.702
NORM_EPS = 1e-6

LANES = 128
SUBLANES = 8
XW_OFF = 3 * RWKV_WIDTH
XA_OFF = XW_OFF + LANES
XG_OFF = XA_OFF + LANES
RWKV_COLS = XG_OFF + 2 * LANES
CHUNK = 64
EXPERT_TILE = 512
VMEM_LIMIT = 56 * 1024 * 1024

NN = (((1,), (0,)), ((), ()))
NT = (((1,), (1,)), ((), ()))
TN = (((0,), (0,)), ((), ()))


def _dot(a, b, dims=NN):
    return lax.dot_general(a, b, dims, preferred_element_type=F32)


def _split(a):
    hi = a.astype(BF16)
    lo = (a - hi.astype(F32)).astype(BF16)
    return hi, lo


def _mm(a, b, dims=NN, passes=1):
    if passes == 1:
        return _dot(a.astype(BF16), b.astype(BF16), dims)
    a_hi, a_lo = _split(a)
    b_hi, b_lo = _split(b)
    return _dot(a_hi, b_hi, dims) + (_dot(a_hi, b_lo, dims) + _dot(a_lo, b_hi, dims))


def _split3(a):
    hi = a.astype(BF16)
    r = a - hi.astype(F32)
    mid = r.astype(BF16)
    lo = (r - mid.astype(F32)).astype(BF16)
    return hi, mid, lo


def _mm_exact_rhs(a, b_bf16, dims=NN):
    hi, mid, lo = _split3(a)
    return _dot(hi, b_bf16, dims) + (_dot(mid, b_bf16, dims) + _dot(lo, b_bf16, dims))


def _mm_exact_lhs(a_bf16, b, dims=NN):
    hi, mid, lo = _split3(b)
    return _dot(a_bf16, hi, dims) + (_dot(a_bf16, mid, dims) + _dot(a_bf16, lo, dims))


def _iota2(shape, dim):
    return lax.broadcasted_iota(jnp.int32, shape, dim)


def _group_ones(n, group):
    return (_iota2((n, n), 0) // group == _iota2((n, n), 1) // group).astype(BF16)


def _sigmoid(x):
    return 1.0 / (1.0 + jnp.exp(-x))


def _params(*sem):
    return pltpu.CompilerParams(dimension_semantics=sem, vmem_limit_bytes=VMEM_LIMIT)


def _ada_kernel(c_ref, w_ref, b_ref, o_ref):
    c = c_ref[...]
    o_ref[...] = _mm(c * _sigmoid(c), w_ref[...], passes=3) + b_ref[...]


def _ada(c, w_ada, b_ada):
    B, D = c.shape
    n_out = w_ada.shape[1]
    tn = 1024
    return pl.pallas_call(
        _ada_kernel,
        grid=(n_out // tn,),
        in_specs=[pl.BlockSpec((B, D), lambda j: (0, 0)),
                  pl.BlockSpec((D, tn), lambda j: (0, j)),
                  pl.BlockSpec((1, tn), lambda j: (0, j))],
        out_specs=pl.BlockSpec((B, tn), lambda j: (0, j)),
        out_shape=jax.ShapeDtypeStruct((B, n_out), F32),
        compiler_params=_params("arbitrary"),
        name="ada",
    )(c, w_ada, b_ada.reshape(1, n_out))


def _rms_modulate(x, g, scale, shift):
    y = x * lax.rsqrt(jnp.mean(x * x, axis=-1, keepdims=True) + NORM_EPS)
    return (y * g) * (1.0 + scale) + shift


def _in_proj_kernel(x_ref, g_ref, scale_ref, shift_ref, w_ref, pr_ref, pm_ref):
    h = _rms_modulate(x_ref[...], g_ref[...], scale_ref[0], shift_ref[0])
    proj = _dot(h.astype(BF16), w_ref[...])
    pr_ref[...] = proj[:, :RWKV_COLS]
    pm_ref[...] = proj[:, RWKV_COLS:]


def _in_proj(x2, norm_g, scale, shift, w_in_b, S):
    T, D = x2.shape
    tm = 256
    per_b = S // tm
    n_cols = w_in_b.shape[1]
    return pl.pallas_call(
        _in_proj_kernel,
        grid=(T // tm,),
        in_specs=[pl.BlockSpec((tm, D), lambda i: (i, 0)),
                  pl.BlockSpec((1, D), lambda i: (0, 0)),
                  pl.BlockSpec((1, 1, D), lambda i: (i // per_b, 0, 0)),
                  pl.BlockSpec((1, 1, D), lambda i: (i // per_b, 0, 0)),
                  pl.BlockSpec((D, n_cols), lambda i: (0, 0))],
        out_specs=[pl.BlockSpec((tm, RWKV_COLS), lambda i: (i, 0)),
                   pl.BlockSpec((tm, MOBA_PROJ), lambda i: (i, 0))],
        out_shape=[jax.ShapeDtypeStruct((T, RWKV_COLS), F32),
                   jax.ShapeDtypeStruct((T, MOBA_PROJ), F32)],
        compiler_params=_params("arbitrary"),
        name="in_proj",
    )(x2, norm_g.reshape(1, D), scale, shift, w_in_b)


RWKV_TILE = 128


def _unit_lower_inverses(Ls):
    n = Ls[0].shape[0]
    r = _iota2((n, n), 0)
    c = _iota2((n, n), 1)
    eye = (r == c).astype(F32)
    in_block = r // 8 == c // 8
    b16 = lambda xs: [x.astype(BF16) for x in xs]
    Ld = [jnp.where(in_block, L, 0.0) for L in Ls]
    Ld_b = b16(Ld)
    Nb_b = [(L - d).astype(BF16) for L, d in zip(Ls, Ld)]
    L2 = [_dot(d, d) for d in Ld_b]
    L2_b = b16(L2)
    L4_b = b16([_dot(x, x) for x in L2_b])
    T0 = [eye + d + l2 + _dot(db, l2b) for d, l2, db, l2b in zip(Ld, L2, Ld_b, L2_b)]
    T0 = [t + _dot(t.astype(BF16), l4b) for t, l4b in zip(T0, L4_b)]
    T0_b = b16(T0)
    M1_b = b16([_dot(t, nb) for t, nb in zip(T0_b, Nb_b)])
    M2_b = b16([_dot(m, m) for m in M1_b])
    M4_b = b16([_dot(m, m) for m in M2_b])
    X = [t + _dot(m4, tb) for t, m4, tb in zip(T0, M4_b, T0_b)]
    X = [x + _dot(m2, x.astype(BF16)) for x, m2 in zip(X, M2_b)]
    return [x + _dot(m1, x.astype(BF16)) for x, m1 in zip(X, M1_b)]


def _rwkv_kernel(p_ref, mu_ref, w0_ref, a0_ref, kk_ref, ka_ref, rk_ref, lng_ref, lnb_ref,
                 wup_ref, aup_ref, gup_ref, y_ref, carry_ref, state_ref, *, ts):
    t = pl.program_id(1)
    W = RWKV_WIDTH

    @pl.when(t == 0)
    def _():
        carry_ref[...] = jnp.zeros_like(carry_ref)
        state_ref[...] = jnp.zeros_like(state_ref)

    p = p_ref[0]
    row = _iota2((ts, 1), 0)
    prev = jnp.where(row == 0, carry_ref[...], pltpu.roll(p, 1, 0))
    carry_ref[...] = p[ts - 1:ts, :]
    pm = p + (prev - p) * mu_ref[...]
    r = pm[:, 0:W]
    k = pm[:, W:2 * W]
    v = pm[:, 2 * W:3 * W]
    xw = pm[:, XW_OFF:XA_OFF]
    xa = pm[:, XA_OFF:XG_OFF]
    xg = pm[:, XG_OFF:RWKV_COLS]

    z = -(w0_ref[...] + _mm(jnp.tanh(xw), wup_ref[...], passes=3))
    softplus = jnp.maximum(z, 0.0) + jnp.log(1.0 + jnp.exp(-jnp.abs(z)))
    logd = -jnp.exp(-softplus - 0.5)
    alpha = _sigmoid(a0_ref[...] + _mm(xa, aup_ref[...], passes=3))
    gate = _mm(_sigmoid(xg), gup_ref[...])

    head_ones = _group_ones(W, HEAD_DIM)
    kk = k * kk_ref[...]
    kk_norm = jnp.sqrt(_mm_exact_rhs(kk * kk, head_ones))
    kk = kk / jnp.maximum(kk_norm, 1e-12)
    kmod = k * (1.0 + (alpha - 1.0) * ka_ref[...])
    bonus = _mm_exact_rhs(r * kmod * rk_ref[...], head_ones) * v

    tr = _iota2((ts, ts), 0)
    tc = _iota2((ts, ts), 1)
    cum = ((tr // CHUNK == tc // CHUNK) & (tc <= tr)).astype(BF16)
    logp = _mm_exact_lhs(cum, logd)
    inv_p = jnp.exp(-logp)
    a_t = -kk * jnp.exp(logp - logd)
    b_t = kk * alpha * inv_p
    k_t = kmod * inv_p
    r_t = r * jnp.exp(logp)

    n2 = 2 * CHUNK
    n_chunks = ts // CHUNK
    n_pairs = RWKV_HEADS // 2
    inst = [(ci, pi) for ci in range(n_chunks) for pi in range(n_pairs)]
    lane = _iota2((1, LANES), 1)
    m0 = lane < HEAD_DIM
    sr = _iota2((n2, n2), 0)
    sc = _iota2((n2, n2), 1)
    strict_lower = sc < sr
    incl_lower = sc <= sr

    def stacked(x, ci, pi):
        xt = x[ci * CHUNK:(ci + 1) * CHUNK, pi * LANES:(pi + 1) * LANES]
        return jnp.concatenate([jnp.where(m0, xt, 0.0), jnp.where(m0, 0.0, xt)], axis=0)

    pcs = [jnp.exp(logp[(ci + 1) * CHUNK - 1:(ci + 1) * CHUNK, pi * LANES:(pi + 1) * LANES])
           for ci, pi in inst]
    r_s = [stacked(r_t, ci, pi) for ci, pi in inst]
    a_b = [stacked(a_t, ci, pi).astype(BF16) for ci, pi in inst]
    r_b = [x.astype(BF16) for x in r_s]
    b_s = [stacked(b_t, ci, pi) for ci, pi in inst]
    k_s = [stacked(k_t, ci, pi) for ci, pi in inst]
    v_b = [stacked(v, ci, pi).astype(BF16) for ci, pi in inst]
    ar_b = [jnp.concatenate([a, rr], axis=0) for a, rr in zip(a_b, r_b)]
    bk_b = [jnp.concatenate([b.astype(BF16), kx.astype(BF16)], axis=0) for b, kx in zip(b_s, k_s)]
    gram = [_dot(ar, bk, NT) for ar, bk in zip(ar_b, bk_b)]
    l_ab = [jnp.where(strict_lower, g[:n2, :n2], 0.0) for g in gram]
    l_ak_b = [jnp.where(strict_lower, g[:n2, n2:], 0.0).astype(BF16) for g in gram]
    m_rb_b = [jnp.where(incl_lower, g[n2:, :n2], 0.0).astype(BF16) for g in gram]
    m_rk_b = [jnp.where(incl_lower, g[n2:, n2:], 0.0).astype(BF16) for g in gram]
    t_b = [x.astype(BF16) for x in _unit_lower_inverses(l_ab)]
    lakv_b = [_dot(l, vv).astype(BF16) for l, vv in zip(l_ak_b, v_b)]
    wu_b = [_dot(tb, jnp.concatenate([a, lv], axis=1)).astype(BF16)
            for tb, a, lv in zip(t_b, a_b, lakv_b)]
    mwu = [_dot(m, wu) for m, wu in zip(m_rb_b, wu_b)]
    q_b = [(rs + x[:, :LANES]).astype(BF16) for rs, x in zip(r_s, mwu)]
    y0 = [x[:, LANES:] + _dot(m, vv) for x, m, vv in zip(mwu, m_rk_b, v_b)]
    bp_b = [(b * pc).astype(BF16) for b, pc in zip(b_s, pcs)]
    kp_b = [(kx * pc).astype(BF16) for kx, pc in zip(k_s, pcs)]
    gh = [_dot(wu, bp, TN) for wu, bp in zip(wu_b, bp_b)]
    g_b = [x[:LANES].astype(BF16) for x in gh]
    h = [x[LANES:] + _dot(vv, kp, TN) for x, vv, kp in zip(gh, v_b, kp_b)]

    y_chunks = []
    for ci in range(n_chunks):
        ids = [ci * n_pairs + pi for pi in range(n_pairs)]
        s0 = [state_ref[pi] for pi in range(n_pairs)]
        s0_b = [s.astype(BF16) for s in s0]
        y_s = [_dot(q_b[n], sb, NT) + y0[n] for n, sb in zip(ids, s0_b)]
        s1 = [s * pcs[n] + _dot(sb, g_b[n]) + h[n] for n, s, sb in zip(ids, s0, s0_b)]
        for pi in range(n_pairs):
            state_ref[pi] = s1[pi]
        y_chunks.append(jnp.concatenate([x[:CHUNK] + x[CHUNK:] for x in y_s], axis=1))
    y = jnp.concatenate(y_chunks, axis=0) if n_chunks > 1 else y_chunks[0]

    mean = _mm_exact_rhs(y, head_ones) * (1.0 / HEAD_DIM)
    yc = y - mean
    var = _mm_exact_rhs(yc * yc, head_ones) * (1.0 / HEAD_DIM)
    yn = yc * lax.rsqrt(var + RWKV_LN_EPS) * lng_ref[...] + lnb_ref[...]
    y_ref[0] = (yn + bonus) * gate


def _rwkv(p_rwkv, prm, B, S, ts=RWKV_TILE):
    W = RWKV_WIDTH
    vec = lambda n: pl.BlockSpec((1, n), lambda b, t: (0, 0))
    mat = lambda m, n: pl.BlockSpec((m, n), lambda b, t: (0, 0))
    return pl.pallas_call(
        functools.partial(_rwkv_kernel, ts=ts),
        grid=(B, S // ts),
        in_specs=[pl.BlockSpec((1, ts, RWKV_COLS), lambda b, t: (b, t, 0)),
                  vec(RWKV_COLS)] + [vec(W)] * 7 + [mat(LANES, W), mat(LANES, W), mat(2 * LANES, W)],
        out_specs=pl.BlockSpec((1, ts, W), lambda b, t: (b, t, 0)),
        out_shape=jax.ShapeDtypeStruct((B, S, W), F32),
        scratch_shapes=[pltpu.VMEM((1, RWKV_COLS), F32),
                        pltpu.VMEM((RWKV_HEADS // 2, LANES, LANES), F32)],
        compiler_params=_params("arbitrary", "arbitrary"),
        name="rwkv",
    )(p_rwkv.reshape(B, S, RWKV_COLS), *prm)


def _moba_kernel(slopes_ref, q_ref, k_ref, v_ref, qg_ref, kg_ref, o_ref, *, S):
    pair = pl.program_id(1)
    NB = S // MOBA_BLOCK
    BLK = MOBA_BLOCK
    n_sel = min(MOBA_TOPK, NB)
    scale = HEAD_DIM ** -0.5
    head_ones = _group_ones(LANES, HEAD_DIM)
    lane = _iota2((1, LANES), 1)

    def head_norm(x, g):
        ss = _mm_exact_rhs(x * x, head_ones)
        return x * lax.rsqrt(ss * (1.0 / HEAD_DIM) + NORM_EPS) * g

    qn = head_norm(q_ref[0], qg_ref[...])
    kn = head_norm(k_ref[0], kg_ref[...])
    kmean = jnp.mean(kn.reshape(NB, BLK, LANES), axis=1)
    q_t = qn.T
    q_tb = q_t.astype(BF16)
    v_tb = v_ref[0].T.astype(BF16)

    blk_of_q = _iota2((1, S), 1) // BLK
    nidx = _iota2((NB, 1), 0)
    valid = nidx < blk_of_q
    tq_all = _iota2((1, S), 1).astype(F32)
    kpos_all = _iota2((S, 1), 0).astype(F32)

    out_rows = []
    for h in range(2):
        hmask = (lane // HEAD_DIM) == h
        slope = slopes_ref[pair * 2 + h]
        gate = _mm(jnp.where(hmask, kmean, 0.0), q_t, passes=3)
        gate = jnp.where(valid, gate, -jnp.inf)
        rank = jnp.zeros((NB, S), jnp.int32)
        for m in range(NB):
            gm = gate[m:m + 1, :]
            ahead = (gm > gate) | ((gm == gate) & (m < nidx))
            rank = rank + ahead.astype(jnp.int32)
        sel = valid & (rank < n_sel)
        k_hb = jnp.where(hmask, kn, 0.0).astype(BF16)
        v_h = v_tb[h * HEAD_DIM:(h + 1) * HEAD_DIM, :]
        out_blocks = []
        for i in range(NB):
            qs = slice(i * BLK, (i + 1) * BLK)
            tq = tq_all[:, qs]
            tiles = []
            m_run = None
            for n in range(i + 1):
                ks = slice(n * BLK, (n + 1) * BLK)
                kpos = kpos_all[ks, :]
                s = _dot(k_hb[ks, :], q_tb[:, qs]) * scale - slope * (tq - kpos)
                if n < i:
                    s = jnp.where(sel[n:n + 1, qs], s, -jnp.inf)
                else:
                    s = jnp.where(kpos <= tq, s, -jnp.inf)
                tiles.append(s)
                mx = jnp.max(s, axis=0, keepdims=True)
                m_run = mx if m_run is None else jnp.maximum(m_run, mx)
            l_run = jnp.zeros((1, BLK), F32)
            acc = jnp.zeros((HEAD_DIM, BLK), F32)
            for n in range(i + 1):
                ks = slice(n * BLK, (n + 1) * BLK)
                pt = jnp.exp(tiles[n] - m_run)
                l_run = l_run + jnp.sum(pt, axis=0, keepdims=True)
                acc = acc + _dot(v_h[:, ks], pt.astype(BF16))
            out_blocks.append(acc / l_run)
        out_rows.append(jnp.concatenate(out_blocks, axis=1))
    o_ref[0] = jnp.concatenate(out_rows, axis=0).T


def _moba(p_moba, q_norm_g, k_norm_g, B, S):
    pairs = MOBA_HEADS // 2
    col = lambda off: pl.BlockSpec((1, S, LANES), lambda b, p: (b, 0, off + p))
    gain = pl.BlockSpec((1, LANES), lambda b, p: (0, 0))
    tile2 = lambda g: jnp.concatenate([g, g]).reshape(1, LANES)
    p3 = p_moba.reshape(B, S, MOBA_PROJ)
    slopes = jnp.exp2(-8.0 * (jnp.arange(MOBA_HEADS, dtype=F32) + 1.0) / MOBA_HEADS)
    return pl.pallas_call(
        functools.partial(_moba_kernel, S=S),
        grid=(B, pairs),
        in_specs=[pl.BlockSpec(memory_space=pltpu.SMEM), col(0), col(pairs), col(2 * pairs),
                  gain, gain],
        out_specs=pl.BlockSpec((1, S, LANES), lambda b, p: (b, 0, p)),
        out_shape=jax.ShapeDtypeStruct((B, S, MOBA_WIDTH), F32),
        compiler_params=_params("arbitrary", "arbitrary"),
        name="moba",
    )(slopes, p3, p3, p3, tile2(q_norm_g), tile2(k_norm_g))


def _out_proj_kernel(yr_ref, ym_ref, x_ref, gate_ref, g_ref, scale_ref, shift_ref, w_ref,
                     wr_ref, br_ref, x1_ref, h2_ref, logit_ref):
    W = RWKV_WIDTH
    mix = (_dot(yr_ref[...].astype(BF16), w_ref[0:W, :])
           + _dot(ym_ref[...].astype(BF16), w_ref[W:, :]))
    x1 = x_ref[...] + gate_ref[0] * mix
    x1_ref[...] = x1
    h2 = _rms_modulate(x1, g_ref[...], scale_ref[0], shift_ref[0])
    h2_ref[...] = h2
    logit_ref[...] = _mm(h2, wr_ref[...], passes=3) + br_ref[...]


def _out_proj(y_rwkv, y_moba, x2, gate1, norm_g, scale, shift, w_out_b, w_router, b_router, S):
    T, D = x2.shape
    tm = 256
    per_b = S // tm
    rows = lambda n: pl.BlockSpec((tm, n), lambda i: (i, 0))
    mod = pl.BlockSpec((1, 1, D), lambda i: (i // per_b, 0, 0))
    full = lambda m, n: pl.BlockSpec((m, n), lambda i: (0, 0))
    return pl.pallas_call(
        _out_proj_kernel,
        grid=(T // tm,),
        in_specs=[rows(RWKV_WIDTH), rows(MOBA_WIDTH), rows(D), mod, full(1, D), mod, mod,
                  full(D, D), full(D, N_EXPERTS), full(1, N_EXPERTS)],
        out_specs=[rows(D), rows(D), rows(N_EXPERTS)],
        out_shape=[jax.ShapeDtypeStruct((T, D), F32), jax.ShapeDtypeStruct((T, D), F32),
                   jax.ShapeDtypeStruct((T, N_EXPERTS), F32)],
        compiler_params=_params("arbitrary"),
        name="out_proj",
    )(y_rwkv, y_moba, x2, gate1, norm_g.reshape(1, D), scale, shift, w_out_b, w_router,
      b_router.reshape(1, N_EXPERTS))


def _experts_kernel(meta_ref, src_cur_ref, src_nxt_ref, dst_ref, h_hbm, wgu_ref, bgu_ref, wd_ref,
                    bd_ref, y_hbm, xbuf, obuf, gsem, ssem, *, n_tiles):
    i = pl.program_id(0)
    n_used = meta_ref[0]
    tm = EXPERT_TILE
    slot = i % 2
    valid = lambda j: meta_ref[1 + n_tiles + j]

    def gather(idx_ref, s):
        def body(r, carry):
            pltpu.make_async_copy(h_hbm.at[pl.ds(idx_ref[0, 0, r], 1)], xbuf.at[s, pl.ds(r, 1)],
                                  gsem.at[s]).start()
            return carry
        lax.fori_loop(0, tm, body, 0)

    def wait_gather(s):
        pltpu.make_async_copy(h_hbm.at[pl.ds(0, tm)], xbuf.at[s], gsem.at[s]).wait()

    def wait_scatter(s, n):
        n_tiled = pl.multiple_of((n // SUBLANES) * SUBLANES, SUBLANES)

        @pl.when(n_tiled > 0)
        def _():
            pltpu.make_async_copy(obuf.at[s, pl.ds(0, n_tiled)], y_hbm.at[pl.ds(0, n_tiled)],
                                  ssem.at[s]).wait()

        def body(r, carry):
            pltpu.make_async_copy(obuf.at[s, pl.ds(0, 1)], y_hbm.at[pl.ds(0, 1)], ssem.at[s]).wait()
            return carry
        lax.fori_loop(n_tiled, n, body, 0)

    @pl.when(i == 0)
    def _():
        gather(src_cur_ref, 0)

    @pl.when(i + 1 < n_used)
    def _():
        gather(src_nxt_ref, 1 - slot)

    @pl.when(i < n_used)
    def _():
        wait_gather(slot)

        @pl.when(i >= 2)
        def _():
            wait_scatter(slot, valid(i - 2))

        xb = xbuf[slot].astype(BF16)
        gu = _dot(xb, wgu_ref[0]) + bgu_ref[0]
        g = jnp.minimum(gu[:, :D_FF], SWIGLU_LIMIT)
        up = jnp.clip(gu[:, D_FF:], -SWIGLU_LIMIT, SWIGLU_LIMIT)
        act = (up + 1.0) * g * _sigmoid(SWIGLU_ALPHA * g)
        obuf[slot] = _dot(act.astype(BF16), wd_ref[0]) + bd_ref[0]

        def body(r, carry):
            pltpu.make_async_copy(obuf.at[slot, pl.ds(r, 1)], y_hbm.at[pl.ds(dst_ref[0, 0, r], 1)],
                                  ssem.at[slot]).start()
            return carry
        lax.fori_loop(0, valid(i), body, 0)

    @pl.when(i == n_used - 1)
    def _():
        wait_scatter(slot, valid(i))

        @pl.when(i >= 1)
        def _():
            wait_scatter(1 - slot, valid(i - 1))


def _experts(h2, src_tok, dst_row, tile_expert, tile_valid, n_used, wgu_b, bgu, wd_b, bd):
    T, D = h2.shape
    tm = EXPERT_TILE
    n_tiles = src_tok.shape[0] // tm
    src2 = src_tok.reshape(n_tiles, 1, tm)
    dst2 = dst_row.reshape(n_tiles, 1, tm)
    meta = jnp.concatenate([n_used.reshape(1), tile_expert, tile_valid]).astype(jnp.int32)
    smem_row = lambda f: pl.BlockSpec((1, 1, tm), f, memory_space=pltpu.SMEM)
    e_of = lambda i, m: m[1 + i]
    grid_spec = pltpu.PrefetchScalarGridSpec(
        num_scalar_prefetch=1,
        grid=(n_tiles,),
        in_specs=[smem_row(lambda i, m: (i, 0, 0)),
                  smem_row(lambda i, m: (jnp.minimum(i + 1, n_tiles - 1), 0, 0)),
                  smem_row(lambda i, m: (i, 0, 0)),
                  pl.BlockSpec(memory_space=pl.ANY),
                  pl.BlockSpec((1, D, 2 * D_FF), lambda i, m: (e_of(i, m), 0, 0)),
                  pl.BlockSpec((1, 1, 2 * D_FF), lambda i, m: (e_of(i, m), 0, 0)),
                  pl.BlockSpec((1, D_FF, D), lambda i, m: (e_of(i, m), 0, 0)),
                  pl.BlockSpec((1, 1, D), lambda i, m: (e_of(i, m), 0, 0))],
        out_specs=pl.BlockSpec(memory_space=pl.ANY),
        scratch_shapes=[pltpu.VMEM((2, tm, D), F32), pltpu.VMEM((2, tm, D), F32),
                        pltpu.SemaphoreType.DMA((2,)), pltpu.SemaphoreType.DMA((2,))],
    )
    return pl.pallas_call(
        functools.partial(_experts_kernel, n_tiles=n_tiles),
        grid_spec=grid_spec,
        out_shape=jax.ShapeDtypeStruct((T * TOP_K, D), F32),
        compiler_params=_params("arbitrary"),
        name="experts",
    )(meta, src2, src2, dst2, h2, wgu_b, bgu.reshape(N_EXPERTS, 1, 2 * D_FF), wd_b,
      bd.reshape(N_EXPERTS, 1, D))


def _route(logits):
    T = logits.shape[0]
    tm = EXPERT_TILE
    M = T * TOP_K
    top_val, top_idx = lax.top_k(logits, TOP_K)
    weights = jax.nn.softmax(top_val, axis=-1)
    slot_expert = top_idx.T.reshape(M)
    order = jnp.argsort(slot_expert).astype(jnp.int32)
    counts = jnp.bincount(slot_expert, length=N_EXPERTS)
    padded = (counts + tm - 1) // tm * tm
    pad_end = jnp.cumsum(padded)
    pad_start = pad_end - padded
    start = jnp.cumsum(counts) - counts
    P = M + N_EXPERTS * tm
    n_tiles = P // tm
    tile_start = jnp.arange(n_tiles) * tm
    tile_expert = jnp.minimum(jnp.searchsorted(pad_end, tile_start, side='right'), N_EXPERTS - 1)
    tile_valid = jnp.clip(counts[tile_expert] - (tile_start - pad_start[tile_expert]), 0, tm)
    n_used = pad_end[-1] // tm
    r = jnp.arange(tm)[None, :]
    sorted_pos = (start[tile_expert] + tile_start - pad_start[tile_expert])[:, None] + r
    slot = order[jnp.clip(sorted_pos, 0, M - 1)]
    is_real = r < tile_valid[:, None]
    dst_row = jnp.where(is_real, slot, 0).astype(jnp.int32).reshape(P)
    src_tok = jnp.where(is_real, slot % T, 0).astype(jnp.int32).reshape(P)
    return (weights, src_tok, dst_row, tile_expert.astype(jnp.int32), tile_valid.astype(jnp.int32),
            n_used.astype(jnp.int32))


def _combine_kernel(x1_ref, y0_ref, y1_ref, y2_ref, y3_ref, w_ref, gate_ref, o_ref):
    w = w_ref[...]
    acc = w[:, 0:1] * y0_ref[...]
    for j, y_ref in enumerate((y1_ref, y2_ref, y3_ref), start=1):
        acc = acc + w[:, j:j + 1] * y_ref[...]
    o_ref[...] = x1_ref[...] + gate_ref[0] * acc


def _combine(x1, y_slots, weights, gate2, S):
    T, D = x1.shape
    tm = 256
    per_b = S // tm
    n_t = T // tm
    plane = lambda k: pl.BlockSpec((tm, D), lambda i: (k * n_t + i, 0))
    return pl.pallas_call(
        _combine_kernel,
        grid=(n_t,),
        in_specs=[pl.BlockSpec((tm, D), lambda i: (i, 0))] + [plane(k) for k in range(TOP_K)]
                 + [pl.BlockSpec((tm, TOP_K), lambda i: (i, 0)),
                    pl.BlockSpec((1, 1, D), lambda i: (i // per_b, 0, 0))],
        out_specs=pl.BlockSpec((tm, D), lambda i: (i, 0)),
        out_shape=jax.ShapeDtypeStruct((T, D), F32),
        compiler_params=_params("arbitrary"),
        name="combine",
    )(x1, y_slots, y_slots, y_slots, y_slots, weights, gate2)


def _pad_cols(w, n):
    return jnp.pad(w, ((0, 0), (0, n - w.shape[1])))


def _pad_rows(w, n):
    return jnp.pad(w, ((0, n - w.shape[0]), (0, 0)))


def _layer(x, c, w_ada, b_ada, norm1_g, w_in, rwkv_mu, rwkv_w0, rwkv_w_up, rwkv_a0, rwkv_a_up,
           rwkv_g_up, rwkv_k_k, rwkv_k_a, rwkv_r_k, rwkv_ln_g, rwkv_ln_b, q_norm_g, k_norm_g,
           w_out, norm2_g, w_router, b_router, w_gate_up, b_gate_up, w_down, b_down):
    B, S, D = x.shape
    T = B * S
    W = RWKV_WIDTH
    x2 = x.reshape(T, D)

    mods = _ada(c, w_ada, b_ada)
    shift1, scale1, gate1, shift2, scale2, gate2 = [
        mods[:, j * D:(j + 1) * D].reshape(B, 1, D) for j in range(6)]

    pieces = [(w_in[:, :XW_OFF], XW_OFF),
              (w_in[:, XW_OFF:XW_OFF + DECAY_LORA], LANES),
              (w_in[:, XW_OFF + DECAY_LORA:XW_OFF + DECAY_LORA + AAA_LORA], LANES),
              (w_in[:, XW_OFF + DECAY_LORA + AAA_LORA:RWKV_PROJ], 2 * LANES),
              (w_in[:, RWKV_PROJ:], MOBA_PROJ)]
    w_in_b = jnp.concatenate([_pad_cols(w, n) for w, n in pieces], axis=1).astype(BF16)
    mu_pieces = [(rwkv_mu[None, :XW_OFF], XW_OFF),
                 (rwkv_mu[None, XW_OFF:XW_OFF + DECAY_LORA], LANES),
                 (rwkv_mu[None, XW_OFF + DECAY_LORA:XW_OFF + DECAY_LORA + AAA_LORA], LANES),
                 (rwkv_mu[None, XW_OFF + DECAY_LORA + AAA_LORA:], 2 * LANES)]
    mu = jnp.concatenate([_pad_cols(m, n) for m, n in mu_pieces], axis=1)

    p_rwkv, p_moba = _in_proj(x2, norm1_g, scale1, shift1, w_in_b, S)

    row = lambda a: a.reshape(1, W)
    prm = (mu, row(rwkv_w0), row(rwkv_a0), row(rwkv_k_k), row(rwkv_k_a), row(rwkv_r_k),
           row(rwkv_ln_g), row(rwkv_ln_b), _pad_rows(rwkv_w_up, LANES), _pad_rows(rwkv_a_up, LANES),
           _pad_rows(rwkv_g_up, 2 * LANES))
    y_rwkv = _rwkv(p_rwkv, prm, B, S)
    y_moba = _moba(p_moba, q_norm_g, k_norm_g, B, S)

    x1, h2, logits = _out_proj(y_rwkv.reshape(T, W), y_moba.reshape(T, MOBA_WIDTH), x2, gate1,
                               norm2_g, scale2, shift2, w_out.astype(BF16), w_router, b_router, S)

    weights, src_tok, dst_row, tile_expert, tile_valid, n_used = _route(logits)
    y_slots = _experts(h2, src_tok, dst_row, tile_expert, tile_valid, n_used,
                       w_gate_up.astype(BF16), b_gate_up, w_down.astype(BF16), b_down)
    out = _combine(x1, y_slots, weights, gate2, S)
    return out.reshape(B, S, D)


def kernel(x, c, w_ada, b_ada, norm1_g, w_in, rwkv_mu, rwkv_w0, rwkv_w_up, rwkv_a0, rwkv_a_up, rwkv_g_up, rwkv_k_k, rwkv_k_a, rwkv_r_k, rwkv_ln_g, rwkv_ln_b, q_norm_g, k_norm_g, w_out, norm2_g, w_router, b_router, w_gate_up, b_gate_up, w_down, b_down):
    for l in range(w_ada.shape[0]):
        x = _layer(x, c, w_ada[l], b_ada[l], norm1_g[l], w_in[l], rwkv_mu[l], rwkv_w0[l],
                   rwkv_w_up[l], rwkv_a0[l], rwkv_a_up[l], rwkv_g_up[l], rwkv_k_k[l], rwkv_k_a[l],
                   rwkv_r_k[l], rwkv_ln_g[l], rwkv_ln_b[l], q_norm_g[l], k_norm_g[l], w_out[l],
                   norm2_g[l], w_router[l], b_router[l], w_gate_up[l], b_gate_up[l], w_down[l],
                   b_down[l])
    return x
```

```python
import functools

import jax
import jax.numpy as jnp
from jax import lax
from jax.experimental import pallas as pl
from jax.experimental.pallas import tpu as pltpu

F32 = jnp.float32
BF16 = jnp.bfloat16

D_MODEL = 1024
HEAD_DIM = 64
RWKV_WIDTH = 512
MOBA_WIDTH = 512
RWKV_HEADS = RWKV_WIDTH // HEAD_DIM
MOBA_HEADS = MOBA_WIDTH // HEAD_DIM
DECAY_LORA = 64
AAA_LORA = 64
GATE_LORA = 160
RWKV_LN_EPS = 64e-5
RWKV_PROJ = 3 * RWKV_WIDTH + DECAY_LORA + AAA_LORA + GATE_LORA
MOBA_PROJ = 3 * MOBA_WIDTH
MOBA_BLOCK = 256
MOBA_TOPK = 3
N_EXPERTS = 32
TOP_K = 4
D_FF = D_MODEL
SWIGLU_LIMIT = 7.0
SWIGLU_ALPHA = 1.702
NORM_EPS = 1e-6

LANES = 128
SUBLANES = 8
XW_OFF = 3 * RWKV_WIDTH
XA_OFF = XW_OFF + LANES
XG_OFF = XA_OFF + LANES
RWKV_COLS = XG_OFF + 2 * LANES
CHUNK = 64
EXPERT_TILE = 512
EXPERT_SUBSTEPS = 4
VMEM_LIMIT = 56 * 1024 * 1024

NN = (((1,), (0,)), ((), ()))
NT = (((1,), (1,)), ((), ()))
TN = (((0,), (0,)), ((), ()))


def _dot(a, b, dims=NN):
    return lax.dot_general(a, b, dims, preferred_element_type=F32)


def _split(a):
    hi = a.astype(BF16)
    lo = (a - hi.astype(F32)).astype(BF16)
    return hi, lo


def _mm(a, b, dims=NN, passes=1):
    if passes == 1:
        return _dot(a.astype(BF16), b.astype(BF16), dims)
    a_hi, a_lo = _split(a)
    b_hi, b_lo = _split(b)
    return _dot(a_hi, b_hi, dims) + (_dot(a_hi, b_lo, dims) + _dot(a_lo, b_hi, dims))


def _split3(a):
    hi = a.astype(BF16)
    r = a - hi.astype(F32)
    mid = r.astype(BF16)
    lo = (r - mid.astype(F32)).astype(BF16)
    return hi, mid, lo


def _mm_exact_rhs(a, b_bf16, dims=NN):
    hi, mid, lo = _split3(a)
    return _dot(hi, b_bf16, dims) + (_dot(mid, b_bf16, dims) + _dot(lo, b_bf16, dims))


def _mm_exact_lhs(a_bf16, b, dims=NN):
    hi, mid, lo = _split3(b)
    return _dot(a_bf16, hi, dims) + (_dot(a_bf16, mid, dims) + _dot(a_bf16, lo, dims))


def _iota2(shape, dim):
    return lax.broadcasted_iota(jnp.int32, shape, dim)


def _group_ones(n, group):
    return (_iota2((n, n), 0) // group == _iota2((n, n), 1) // group).astype(BF16)


def _sigmoid(x):
    return 1.0 / (1.0 + jnp.exp(-x))


def _params(*sem):
    return pltpu.CompilerParams(dimension_semantics=sem, vmem_limit_bytes=VMEM_LIMIT)


def _ada_kernel(c_ref, w_ref, b_ref, o_ref):
    c = c_ref[...]
    o_ref[...] = _mm(c * _sigmoid(c), w_ref[...], passes=3) + b_ref[...]


def _ada(c, w_ada, b_ada):
    B, D = c.shape
    n_out = w_ada.shape[1]
    tn = 1024
    return pl.pallas_call(
        _ada_kernel,
        grid=(n_out // tn,),
        in_specs=[pl.BlockSpec((B, D), lambda j: (0, 0)),
                  pl.BlockSpec((D, tn), lambda j: (0, j)),
                  pl.BlockSpec((1, tn), lambda j: (0, j))],
        out_specs=pl.BlockSpec((B, tn), lambda j: (0, j)),
        out_shape=jax.ShapeDtypeStruct((B, n_out), F32),
        compiler_params=_params("arbitrary"),
        name="ada",
    )(c, w_ada, b_ada.reshape(1, n_out))


def _rms_modulate(x, g, scale, shift):
    y = x * lax.rsqrt(jnp.mean(x * x, axis=-1, keepdims=True) + NORM_EPS)
    return (y * g) * (1.0 + scale) + shift


def _in_proj_kernel(x_ref, g_ref, scale_ref, shift_ref, w_ref, pr_ref, pm_ref):
    h = _rms_modulate(x_ref[...], g_ref[...], scale_ref[0], shift_ref[0])
    proj = _dot(h.astype(BF16), w_ref[...])
    pr_ref[...] = proj[:, :RWKV_COLS]
    pm_ref[...] = proj[:, RWKV_COLS:]


def _in_proj(x2, norm_g, scale, shift, w_in_b, S):
    T, D = x2.shape
    tm = 256
    per_b = S // tm
    n_cols = w_in_b.shape[1]
    return pl.pallas_call(
        _in_proj_kernel,
        grid=(T // tm,),
        in_specs=[pl.BlockSpec((tm, D), lambda i: (i, 0)),
                  pl.BlockSpec((1, D), lambda i: (0, 0)),
                  pl.BlockSpec((1, 1, D), lambda i: (i // per_b, 0, 0)),
                  pl.BlockSpec((1, 1, D), lambda i: (i // per_b, 0, 0)),
                  pl.BlockSpec((D, n_cols), lambda i: (0, 0))],
        out_specs=[pl.BlockSpec((tm, RWKV_COLS), lambda i: (i, 0)),
                   pl.BlockSpec((tm, MOBA_PROJ), lambda i: (i, 0))],
        out_shape=[jax.ShapeDtypeStruct((T, RWKV_COLS), F32),
                   jax.ShapeDtypeStruct((T, MOBA_PROJ), F32)],
        compiler_params=_params("arbitrary"),
        name="in_proj",
    )(x2, norm_g.reshape(1, D), scale, shift, w_in_b)


RWKV_TILE = 128


def _unit_lower_inverses(Ls):
    n = Ls[0].shape[0]
    r = _iota2((n, n), 0)
    c = _iota2((n, n), 1)
    eye = (r == c).astype(F32)
    in_block = r // 8 == c // 8
    b16 = lambda xs: [x.astype(BF16) for x in xs]
    Ld = [jnp.where(in_block, L, 0.0) for L in Ls]
    Ld_b = b16(Ld)
    Nb_b = [(L - d).astype(BF16) for L, d in zip(Ls, Ld)]
    L2 = [_dot(d, d) for d in Ld_b]
    L2_b = b16(L2)
    L4_b = b16([_dot(x, x) for x in L2_b])
    T0 = [eye + d + l2 + _dot(db, l2b) for d, l2, db, l2b in zip(Ld, L2, Ld_b, L2_b)]
    T0 = [t + _dot(t.astype(BF16), l4b) for t, l4b in zip(T0, L4_b)]
    T0_b = b16(T0)
    M1_b = b16([_dot(t, nb) for t, nb in zip(T0_b, Nb_b)])
    M2_b = b16([_dot(m, m) for m in M1_b])
    M4_b = b16([_dot(m, m) for m in M2_b])
    X = [t + _dot(m4, tb) for t, m4, tb in zip(T0, M4_b, T0_b)]
    X = [x + _dot(m2, x.astype(BF16)) for x, m2 in zip(X, M2_b)]
    return [x + _dot(m1, x.astype(BF16)) for x, m1 in zip(X, M1_b)]


def _rwkv_kernel(p_ref, mu_ref, w0_ref, a0_ref, kk_ref, ka_ref, rk_ref, lng_ref, lnb_ref,
                 wup_ref, aup_ref, gup_ref, y_ref, carry_ref, state_ref, *, ts):
    t = pl.program_id(1)
    W = RWKV_WIDTH

    @pl.when(t == 0)
    def _():
        carry_ref[...] = jnp.zeros_like(carry_ref)
        state_ref[...] = jnp.zeros_like(state_ref)

    p = p_ref[0]
    row = _iota2((ts, 1), 0)
    prev = jnp.where(row == 0, carry_ref[...], pltpu.roll(p, 1, 0))
    carry_ref[...] = p[ts - 1:ts, :]
    pm = p + (prev - p) * mu_ref[...]
    r = pm[:, 0:W]
    k = pm[:, W:2 * W]
    v = pm[:, 2 * W:3 * W]
    xw = pm[:, XW_OFF:XA_OFF]
    xa = pm[:, XA_OFF:XG_OFF]
    xg = pm[:, XG_OFF:RWKV_COLS]

    z = -(w0_ref[...] + _mm(jnp.tanh(xw), wup_ref[...], passes=3))
    softplus = jnp.maximum(z, 0.0) + jnp.log(1.0 + jnp.exp(-jnp.abs(z)))
    logd = -jnp.exp(-softplus - 0.5)
    alpha = _sigmoid(a0_ref[...] + _mm(xa, aup_ref[...], passes=3))
    gate = _mm(_sigmoid(xg), gup_ref[...])

    head_ones = _group_ones(W, HEAD_DIM)
    kk = k * kk_ref[...]
    kk_norm = jnp.sqrt(_mm_exact_rhs(kk * kk, head_ones))
    kk = kk / jnp.maximum(kk_norm, 1e-12)
    kmod = k * (1.0 + (alpha - 1.0) * ka_ref[...])
    bonus = _mm_exact_rhs(r * kmod * rk_ref[...], head_ones) * v

    tr = _iota2((ts, ts), 0)
    tc = _iota2((ts, ts), 1)
    cum = ((tr // CHUNK == tc // CHUNK) & (tc <= tr)).astype(BF16)
    logp = _mm_exact_lhs(cum, logd)
    inv_p = jnp.exp(-logp)
    a_t = -kk * jnp.exp(logp - logd)
    b_t = kk * alpha * inv_p
    k_t = kmod * inv_p
    r_t = r * jnp.exp(logp)

    n2 = 2 * CHUNK
    n_chunks = ts // CHUNK
    n_pairs = RWKV_HEADS // 2
    inst = [(ci, pi) for ci in range(n_chunks) for pi in range(n_pairs)]
    lane = _iota2((1, LANES), 1)
    m0 = lane < HEAD_DIM
    sr = _iota2((n2, n2), 0)
    sc = _iota2((n2, n2), 1)
    strict_lower = sc < sr
    incl_lower = sc <= sr

    def stacked(x, ci, pi):
        xt = x[ci * CHUNK:(ci + 1) * CHUNK, pi * LANES:(pi + 1) * LANES]
        return jnp.concatenate([jnp.where(m0, xt, 0.0), jnp.where(m0, 0.0, xt)], axis=0)

    pcs = [jnp.exp(logp[(ci + 1) * CHUNK - 1:(ci + 1) * CHUNK, pi * LANES:(pi + 1) * LANES])
           for ci, pi in inst]
    r_s = [stacked(r_t, ci, pi) for ci, pi in inst]
    a_b = [stacked(a_t, ci, pi).astype(BF16) for ci, pi in inst]
    r_b = [x.astype(BF16) for x in r_s]
    b_s = [stacked(b_t, ci, pi) for ci, pi in inst]
    k_s = [stacked(k_t, ci, pi) for ci, pi in inst]
    v_b = [stacked(v, ci, pi).astype(BF16) for ci, pi in inst]
    ar_b = [jnp.concatenate([a, rr], axis=0) for a, rr in zip(a_b, r_b)]
    bk_b = [jnp.concatenate([b.astype(BF16), kx.astype(BF16)], axis=0) for b, kx in zip(b_s, k_s)]
    gram = [_dot(ar, bk, NT) for ar, bk in zip(ar_b, bk_b)]
    l_ab = [jnp.where(strict_lower, g[:n2, :n2], 0.0) for g in gram]
    l_ak_b = [jnp.where(strict_lower, g[:n2, n2:], 0.0).astype(BF16) for g in gram]
    m_rb_b = [jnp.where(incl_lower, g[n2:, :n2], 0.0).astype(BF16) for g in gram]
    m_rk_b = [jnp.where(incl_lower, g[n2:, n2:], 0.0).astype(BF16) for g in gram]
    t_b = [x.astype(BF16) for x in _unit_lower_inverses(l_ab)]
    lakv_b = [_dot(l, vv).astype(BF16) for l, vv in zip(l_ak_b, v_b)]
    wu_b = [_dot(tb, jnp.concatenate([a, lv], axis=1)).astype(BF16)
            for tb, a, lv in zip(t_b, a_b, lakv_b)]
    mwu = [_dot(m, wu) for m, wu in zip(m_rb_b, wu_b)]
    q_b = [(rs + x[:, :LANES]).astype(BF16) for rs, x in zip(r_s, mwu)]
    y0 = [x[:, LANES:] + _dot(m, vv) for x, m, vv in zip(mwu, m_rk_b, v_b)]
    bp_b = [(b * pc).astype(BF16) for b, pc in zip(b_s, pcs)]
    kp_b = [(kx * pc).astype(BF16) for kx, pc in zip(k_s, pcs)]
    gh = [_dot(wu, bp, TN) for wu, bp in zip(wu_b, bp_b)]
    g_b = [x[:LANES].astype(BF16) for x in gh]
    h = [x[LANES:] + _dot(vv, kp, TN) for x, vv, kp in zip(gh, v_b, kp_b)]

    y_chunks = []
    for ci in range(n_chunks):
        ids = [ci * n_pairs + pi for pi in range(n_pairs)]
        s0 = [state_ref[pi] for pi in range(n_pairs)]
        s0_b = [s.astype(BF16) for s in s0]
        y_s = [_dot(q_b[n], sb, NT) + y0[n] for n, sb in zip(ids, s0_b)]
        s1 = [s * pcs[n] + _dot(sb, g_b[n]) + h[n] for n, s, sb in zip(ids, s0, s0_b)]
        for pi in range(n_pairs):
            state_ref[pi] = s1[pi]
        y_chunks.append(jnp.concatenate([x[:CHUNK] + x[CHUNK:] for x in y_s], axis=1))
    y = jnp.concatenate(y_chunks, axis=0) if n_chunks > 1 else y_chunks[0]

    mean = _mm_exact_rhs(y, head_ones) * (1.0 / HEAD_DIM)
    yc = y - mean
    var = _mm_exact_rhs(yc * yc, head_ones) * (1.0 / HEAD_DIM)
    yn = yc * lax.rsqrt(var + RWKV_LN_EPS) * lng_ref[...] + lnb_ref[...]
    y_ref[0] = (yn + bonus) * gate


def _rwkv(p_rwkv, prm, B, S, ts=RWKV_TILE):
    W = RWKV_WIDTH
    vec = lambda n: pl.BlockSpec((1, n), lambda b, t: (0, 0))
    mat = lambda m, n: pl.BlockSpec((m, n), lambda b, t: (0, 0))
    return pl.pallas_call(
        functools.partial(_rwkv_kernel, ts=ts),
        grid=(B, S // ts),
        in_specs=[pl.BlockSpec((1, ts, RWKV_COLS), lambda b, t: (b, t, 0)),
                  vec(RWKV_COLS)] + [vec(W)] * 7 + [mat(LANES, W), mat(LANES, W), mat(2 * LANES, W)],
        out_specs=pl.BlockSpec((1, ts, W), lambda b, t: (b, t, 0)),
        out_shape=jax.ShapeDtypeStruct((B, S, W), F32),
        scratch_shapes=[pltpu.VMEM((1, RWKV_COLS), F32),
                        pltpu.VMEM((RWKV_HEADS // 2, LANES, LANES), F32)],
        compiler_params=_params("arbitrary", "arbitrary"),
        name="rwkv",
    )(p_rwkv.reshape(B, S, RWKV_COLS), *prm)


def _moba_kernel(slopes_ref, q_ref, k_ref, v_ref, qg_ref, kg_ref, o_ref, *, S):
    pair = pl.program_id(1)
    NB = S // MOBA_BLOCK
    BLK = MOBA_BLOCK
    n_sel = min(MOBA_TOPK, NB)
    scale = HEAD_DIM ** -0.5
    head_ones = _group_ones(LANES, HEAD_DIM)
    lane = _iota2((1, LANES), 1)

    def head_norm(x, g):
        ss = _mm_exact_rhs(x * x, head_ones)
        return x * lax.rsqrt(ss * (1.0 / HEAD_DIM) + NORM_EPS) * g

    qn = head_norm(q_ref[0], qg_ref[...])
    kn = head_norm(k_ref[0], kg_ref[...])
    kmean = jnp.mean(kn.reshape(NB, BLK, LANES), axis=1)
    q_t = qn.T
    q_tb = q_t.astype(BF16)
    v_tb = v_ref[0].T.astype(BF16)

    blk_of_q = _iota2((1, S), 1) // BLK
    nidx = _iota2((NB, 1), 0)
    valid = nidx < blk_of_q
    tq_all = _iota2((1, S), 1).astype(F32)
    kpos_all = _iota2((S, 1), 0).astype(F32)

    out_rows = []
    for h in range(2):
        hmask = (lane // HEAD_DIM) == h
        slope = slopes_ref[pair * 2 + h]
        gate = _mm(jnp.where(hmask, kmean, 0.0), q_t, passes=3)
        gate = jnp.where(valid, gate, -jnp.inf)
        rank = jnp.zeros((NB, S), jnp.int32)
        for m in range(NB):
            gm = gate[m:m + 1, :]
            ahead = (gm > gate) | ((gm == gate) & (m < nidx))
            rank = rank + ahead.astype(jnp.int32)
        sel = valid & (rank < n_sel)
        k_hb = jnp.where(hmask, kn, 0.0).astype(BF16)
        v_h = v_tb[h * HEAD_DIM:(h + 1) * HEAD_DIM, :]
        out_blocks = []
        for i in range(NB):
            qs = slice(i * BLK, (i + 1) * BLK)
            tq = tq_all[:, qs]
            tiles = []
            m_run = None
            for n in range(i + 1):
                ks = slice(n * BLK, (n + 1) * BLK)
                kpos = kpos_all[ks, :]
                s = _dot(k_hb[ks, :], q_tb[:, qs]) * scale - slope * (tq - kpos)
                if n < i:
                    s = jnp.where(sel[n:n + 1, qs], s, -jnp.inf)
                else:
                    s = jnp.where(kpos <= tq, s, -jnp.inf)
                tiles.append(s)
                mx = jnp.max(s, axis=0, keepdims=True)
                m_run = mx if m_run is None else jnp.maximum(m_run, mx)
            l_run = jnp.zeros((1, BLK), F32)
            acc = jnp.zeros((HEAD_DIM, BLK), F32)
            for n in range(i + 1):
                ks = slice(n * BLK, (n + 1) * BLK)
                pt = jnp.exp(tiles[n] - m_run)
                l_run = l_run + jnp.sum(pt, axis=0, keepdims=True)
                acc = acc + _dot(v_h[:, ks], pt.astype(BF16))
            out_blocks.append(acc / l_run)
        out_rows.append(jnp.concatenate(out_blocks, axis=1))
    o_ref[0] = jnp.concatenate(out_rows, axis=0).T


def _moba(p_moba, q_norm_g, k_norm_g, B, S):
    pairs = MOBA_HEADS // 2
    col = lambda off: pl.BlockSpec((1, S, LANES), lambda b, p: (b, 0, off + p))
    gain = pl.BlockSpec((1, LANES), lambda b, p: (0, 0))
    tile2 = lambda g: jnp.concatenate([g, g]).reshape(1, LANES)
    p3 = p_moba.reshape(B, S, MOBA_PROJ)
    slopes = jnp.exp2(-8.0 * (jnp.arange(MOBA_HEADS, dtype=F32) + 1.0) / MOBA_HEADS)
    return pl.pallas_call(
        functools.partial(_moba_kernel, S=S),
        grid=(B, pairs),
        in_specs=[pl.BlockSpec(memory_space=pltpu.SMEM), col(0), col(pairs), col(2 * pairs),
                  gain, gain],
        out_specs=pl.BlockSpec((1, S, LANES), lambda b, p: (b, 0, p)),
        out_shape=jax.ShapeDtypeStruct((B, S, MOBA_WIDTH), F32),
        compiler_params=_params("arbitrary", "arbitrary"),
        name="moba",
    )(slopes, p3, p3, p3, tile2(q_norm_g), tile2(k_norm_g))


def _out_proj_kernel(yr_ref, ym_ref, x_ref, gate_ref, g_ref, scale_ref, shift_ref, w_ref,
                     wr_ref, br_ref, x1_ref, h2_ref, logit_ref):
    W = RWKV_WIDTH
    mix = (_dot(yr_ref[...].astype(BF16), w_ref[0:W, :])
           + _dot(ym_ref[...].astype(BF16), w_ref[W:, :]))
    x1 = x_ref[...] + gate_ref[0] * mix
    x1_ref[...] = x1
    h2 = _rms_modulate(x1, g_ref[...], scale_ref[0], shift_ref[0])
    h2_ref[:, 0, :] = h2
    logit_ref[...] = _mm(h2, wr_ref[...], passes=3) + br_ref[...]


def _out_proj(y_rwkv, y_moba, x2, gate1, norm_g, scale, shift, w_out_b, w_router, b_router, S):
    T, D = x2.shape
    tm = 256
    per_b = S // tm
    rows = lambda n: pl.BlockSpec((tm, n), lambda i: (i, 0))
    mod = pl.BlockSpec((1, 1, D), lambda i: (i // per_b, 0, 0))
    full = lambda m, n: pl.BlockSpec((m, n), lambda i: (0, 0))
    return pl.pallas_call(
        _out_proj_kernel,
        grid=(T // tm,),
        in_specs=[rows(RWKV_WIDTH), rows(MOBA_WIDTH), rows(D), mod, full(1, D), mod, mod,
                  full(D, D), full(D, N_EXPERTS), full(1, N_EXPERTS)],
        out_specs=[rows(D), pl.BlockSpec((tm, 1, D), lambda i: (i, 0, 0)), rows(N_EXPERTS)],
        out_shape=[jax.ShapeDtypeStruct((T, D), F32), jax.ShapeDtypeStruct((T, 1, D), F32),
                   jax.ShapeDtypeStruct((T, N_EXPERTS), F32)],
        compiler_params=_params("arbitrary"),
        name="out_proj",
    )(y_rwkv, y_moba, x2, gate1, norm_g.reshape(1, D), scale, shift, w_out_b, w_router,
      b_router.reshape(1, N_EXPERTS))


def _experts_kernel(meta_ref, src_first_ref, src_a_ref, src_b_ref, dst_a_ref, dst_b_ref, h_hbm,
                    wgu_a_ref, bgu_a_ref, wd_a_ref, bd_a_ref, wgu_b_ref, bgu_b_ref, wd_b_ref, bd_b_ref,
                    y_hbm, xbuf0, xbuf1, obuf0, obuf1, xb16, gsem, ssem, *, n_slots):
    g = pl.program_id(0)
    n_used = meta_ref[0]
    tm = EXPERT_TILE
    n_sub = EXPERT_SUBSTEPS
    rows_sub = tm // n_sub
    cols_sub = D_FF // n_sub
    n_dump = y_hbm.shape[0] - n_slots

    def gather_row(src_ref, r, dst_buf, sem):
        pltpu.make_async_copy(h_hbm.at[src_ref[0, 0, r]], dst_buf.at[pl.ds(r, 1)], sem).start()

    def scatter_row(dst_ref, r, src_buf, sem):
        pltpu.make_async_copy(src_buf.at[pl.ds(r, 1)], y_hbm.at[dst_ref[0, 0, r]], sem).start()

    def wait_tile_gather(buf, sem):
        pltpu.make_async_copy(h_hbm.at[pl.ds(0, tm), 0], buf, sem).wait()

    def wait_tile_scatter(buf, sem):
        pltpu.make_async_copy(buf, y_hbm.at[pl.ds(0, tm), 0], sem).wait()

    @pl.when(g == 0)
    def _():
        def body(r, carry):
            gather_row(src_first_ref, r, xbuf0, gsem.at[0])
            return carry
        lax.fori_loop(0, tm, body, 0)
        obuf1[...] = jnp.zeros_like(obuf1)
        for d in range(n_dump // tm):
            cp = pltpu.make_async_copy(obuf1, y_hbm.at[pl.ds(n_slots + d * tm, tm), 0], ssem.at[0])
            cp.start()
            cp.wait()

    def phase(ph, src_next_ref, dst_prev_ref, wgu_ref, bgu_ref, wd_ref, bd_ref):
        j = 2 * g + ph
        x_cur, x_nxt = (xbuf0, xbuf1) if ph == 0 else (xbuf1, xbuf0)
        o_cur, o_prv = (obuf0, obuf1) if ph == 0 else (obuf1, obuf0)

        @pl.when(j < n_used)
        def _():
            wait_tile_gather(x_cur, gsem.at[ph])

            @pl.when(j >= 1)
            def _():
                wait_tile_scatter(o_cur, ssem.at[ph])

            for r in range(tm):
                gather_row(src_next_ref, r, x_nxt, gsem.at[1 - ph])
                scatter_row(dst_prev_ref, r, o_prv, ssem.at[1 - ph])

            xb16[...] = x_cur[...].astype(BF16)
            o_cur[...] = jnp.broadcast_to(bd_ref[0], (tm, D_MODEL))

            for n in range(n_sub):
                cg = slice(n * cols_sub, (n + 1) * cols_sub)
                cu = slice(D_FF + n * cols_sub, D_FF + (n + 1) * cols_sub)
                xb = xb16[...]
                gate = jnp.minimum(_dot(xb, wgu_ref[0, :, cg]) + bgu_ref[0, :, cg], SWIGLU_LIMIT)
                up = jnp.clip(_dot(xb, wgu_ref[0, :, cu]) + bgu_ref[0, :, cu], -SWIGLU_LIMIT, SWIGLU_LIMIT)
                act = (up + 1.0) * gate * _sigmoid(SWIGLU_ALPHA * gate)
                o_cur[...] += _dot(act.astype(BF16), wd_ref[0, cg, :])

        @pl.when(j == n_used)
        def _():
            def body(r, carry):
                scatter_row(dst_prev_ref, r, o_prv, ssem.at[1 - ph])
                return carry
            lax.fori_loop(0, tm, body, 0)
            wait_tile_scatter(o_prv, ssem.at[1 - ph])
            wait_tile_scatter(o_cur, ssem.at[ph])
            wait_tile_gather(x_cur, gsem.at[ph])

    phase(0, src_a_ref, dst_a_ref, wgu_a_ref, bgu_a_ref, wd_a_ref, bd_a_ref)
    phase(1, src_b_ref, dst_b_ref, wgu_b_ref, bgu_b_ref, wd_b_ref, bd_b_ref)


def _experts(h2, src_tok, dst_row, tile_expert, n_used, wgu_b, bgu, wd_b, bd, n_slots, n_dump):
    D = h2.shape[-1]
    tm = EXPERT_TILE
    n_tiles = tile_expert.shape[0]
    src3 = src_tok.reshape(n_tiles + 2, 1, tm)
    dst3 = dst_row.reshape(n_tiles + 2, 1, tm)
    meta = jnp.concatenate([n_used.reshape(1), tile_expert]).astype(jnp.int32)
    smem_row = lambda f: pl.BlockSpec((1, 1, tm), f, memory_space=pltpu.SMEM)
    e_of = lambda j, m: m[1 + jnp.minimum(j, n_tiles - 1)]
    weights = lambda ph: [
        pl.BlockSpec((1, D, 2 * D_FF), lambda g, m: (e_of(2 * g + ph, m), 0, 0)),
        pl.BlockSpec((1, 1, 2 * D_FF), lambda g, m: (e_of(2 * g + ph, m), 0, 0)),
        pl.BlockSpec((1, D_FF, D), lambda g, m: (e_of(2 * g + ph, m), 0, 0)),
        pl.BlockSpec((1, 1, D), lambda g, m: (e_of(2 * g + ph, m), 0, 0))]
    grid_spec = pltpu.PrefetchScalarGridSpec(
        num_scalar_prefetch=1,
        grid=((n_tiles + 2) // 2,),
        in_specs=[smem_row(lambda g, m: (0, 0, 0)),
                  smem_row(lambda g, m: (2 * g + 1, 0, 0)),
                  smem_row(lambda g, m: (jnp.minimum(2 * g + 2, n_tiles + 1), 0, 0)),
                  smem_row(lambda g, m: (2 * g, 0, 0)),
                  smem_row(lambda g, m: (2 * g + 1, 0, 0)),
                  pl.BlockSpec(memory_space=pl.ANY)] + weights(0) + weights(1),
        out_specs=pl.BlockSpec(memory_space=pl.ANY),
        scratch_shapes=[pltpu.VMEM((tm, D), F32), pltpu.VMEM((tm, D), F32),
                        pltpu.VMEM((tm, D), F32), pltpu.VMEM((tm, D), F32),
                        pltpu.VMEM((tm, D), BF16),
                        pltpu.SemaphoreType.DMA((2,)), pltpu.SemaphoreType.DMA((2,))],
    )
    bgu3 = bgu.reshape(N_EXPERTS, 1, 2 * D_FF)
    bd3 = bd.reshape(N_EXPERTS, 1, D)
    return pl.pallas_call(
        functools.partial(_experts_kernel, n_slots=n_slots),
        grid_spec=grid_spec,
        out_shape=jax.ShapeDtypeStruct((n_slots + n_dump, 1, D), F32),
        compiler_params=_params("arbitrary"),
        name="experts",
    )(meta, src3, src3, src3, dst3, dst3, h2, wgu_b, bgu3, wd_b, bd3, wgu_b, bgu3, wd_b, bd3)


def _route(logits):
    T = logits.shape[0]
    tm = EXPERT_TILE
    M = T * TOP_K
    top_val, top_idx = lax.top_k(logits, TOP_K)
    weights = jax.nn.softmax(top_val, axis=-1)
    slot_expert = top_idx.T.reshape(M)
    order = jnp.argsort(slot_expert).astype(jnp.int32)
    counts = jnp.bincount(slot_expert, length=N_EXPERTS)
    padded = (counts + tm - 1) // tm * tm
    pad_end = jnp.cumsum(padded)
    pad_start = pad_end - padded
    start = jnp.cumsum(counts) - counts
    n_tiles = M // tm + N_EXPERTS
    tile_start = jnp.arange(n_tiles) * tm
    tile_expert = jnp.minimum(jnp.searchsorted(pad_end, tile_start, side='right'), N_EXPERTS - 1)
    tile_valid = jnp.clip(counts[tile_expert] - (tile_start - pad_start[tile_expert]), 0, tm)
    n_used = pad_end[-1] // tm
    r = jnp.arange(tm)[None, :]
    sorted_pos = (start[tile_expert] + tile_start - pad_start[tile_expert])[:, None] + r
    slot = order[jnp.clip(sorted_pos, 0, M - 1)]
    is_real = r < tile_valid[:, None]
    src_tok = jnp.where(is_real, slot % T, 0).astype(jnp.int32)
    dump = M + tm + tile_expert[:, None] * tm + (r - tile_valid[:, None])
    dst_row = jnp.where(is_real, slot, dump).astype(jnp.int32)
    lead = (M + r).astype(jnp.int32)
    zeros = jnp.zeros((1, tm), jnp.int32)
    src_ext = jnp.concatenate([src_tok, zeros, zeros], axis=0)
    dst_ext = jnp.concatenate([lead, dst_row, zeros], axis=0)
    n_dump = (N_EXPERTS + 1) * tm
    return weights, src_ext, dst_ext, tile_expert.astype(jnp.int32), n_used.astype(jnp.int32), n_dump


def _combine_kernel(x1_ref, y0_ref, y1_ref, y2_ref, y3_ref, w_ref, gate_ref, o_ref):
    w = w_ref[...]
    acc = w[:, 0:1] * y0_ref[:, 0, :]
    for j, y_ref in enumerate((y1_ref, y2_ref, y3_ref), start=1):
        acc = acc + w[:, j:j + 1] * y_ref[:, 0, :]
    o_ref[...] = x1_ref[...] + gate_ref[0] * acc


def _combine(x1, y_slots, weights, gate2, S):
    T, D = x1.shape
    tm = 256
    per_b = S // tm
    n_t = T // tm
    plane = lambda k: pl.BlockSpec((tm, 1, D), lambda i: (k * n_t + i, 0, 0))
    return pl.pallas_call(
        _combine_kernel,
        grid=(n_t,),
        in_specs=[pl.BlockSpec((tm, D), lambda i: (i, 0))] + [plane(k) for k in range(TOP_K)]
                 + [pl.BlockSpec((tm, TOP_K), lambda i: (i, 0)),
                    pl.BlockSpec((1, 1, D), lambda i: (i // per_b, 0, 0))],
        out_specs=pl.BlockSpec((tm, D), lambda i: (i, 0)),
        out_shape=jax.ShapeDtypeStruct((T, D), F32),
        compiler_params=_params("arbitrary"),
        name="combine",
    )(x1, y_slots, y_slots, y_slots, y_slots, weights, gate2)


def _pad_cols(w, n):
    return jnp.pad(w, ((0, 0), (0, n - w.shape[1])))


def _pad_rows(w, n):
    return jnp.pad(w, ((0, n - w.shape[0]), (0, 0)))


def _layer(x, c, w_ada, b_ada, norm1_g, w_in, rwkv_mu, rwkv_w0, rwkv_w_up, rwkv_a0, rwkv_a_up,
           rwkv_g_up, rwkv_k_k, rwkv_k_a, rwkv_r_k, rwkv_ln_g, rwkv_ln_b, q_norm_g, k_norm_g,
           w_out, norm2_g, w_router, b_router, w_gate_up, b_gate_up, w_down, b_down):
    B, S, D = x.shape
    T = B * S
    W = RWKV_WIDTH
    x2 = x.reshape(T, D)

    mods = _ada(c, w_ada, b_ada)
    shift1, scale1, gate1, shift2, scale2, gate2 = [
        mods[:, j * D:(j + 1) * D].reshape(B, 1, D) for j in range(6)]

    pieces = [(w_in[:, :XW_OFF], XW_OFF),
              (w_in[:, XW_OFF:XW_OFF + DECAY_LORA], LANES),
              (w_in[:, XW_OFF + DECAY_LORA:XW_OFF + DECAY_LORA + AAA_LORA], LANES),
              (w_in[:, XW_OFF + DECAY_LORA + AAA_LORA:RWKV_PROJ], 2 * LANES),
              (w_in[:, RWKV_PROJ:], MOBA_PROJ)]
    w_in_b = jnp.concatenate([_pad_cols(w, n) for w, n in pieces], axis=1).astype(BF16)
    mu_pieces = [(rwkv_mu[None, :XW_OFF], XW_OFF),
                 (rwkv_mu[None, XW_OFF:XW_OFF + DECAY_LORA], LANES),
                 (rwkv_mu[None, XW_OFF + DECAY_LORA:XW_OFF + DECAY_LORA + AAA_LORA], LANES),
                 (rwkv_mu[None, XW_OFF + DECAY_LORA + AAA_LORA:], 2 * LANES)]
    mu = jnp.concatenate([_pad_cols(m, n) for m, n in mu_pieces], axis=1)

    p_rwkv, p_moba = _in_proj(x2, norm1_g, scale1, shift1, w_in_b, S)

    row = lambda a: a.reshape(1, W)
    prm = (mu, row(rwkv_w0), row(rwkv_a0), row(rwkv_k_k), row(rwkv_k_a), row(rwkv_r_k),
           row(rwkv_ln_g), row(rwkv_ln_b), _pad_rows(rwkv_w_up, LANES), _pad_rows(rwkv_a_up, LANES),
           _pad_rows(rwkv_g_up, 2 * LANES))
    y_rwkv = _rwkv(p_rwkv, prm, B, S)
    y_moba = _moba(p_moba, q_norm_g, k_norm_g, B, S)

    x1, h2, logits = _out_proj(y_rwkv.reshape(T, W), y_moba.reshape(T, MOBA_WIDTH), x2, gate1,
                               norm2_g, scale2, shift2, w_out.astype(BF16), w_router, b_router, S)

    weights, src_tok, dst_row, tile_expert, n_used, n_dump = _route(logits)
    y_slots = _experts(h2, src_tok, dst_row, tile_expert, n_used, w_gate_up.astype(BF16), b_gate_up,
                       w_down.astype(BF16), b_down, T * TOP_K, n_dump)
    out = _combine(x1, y_slots, weights, gate2, S)
    return out.reshape(B, S, D)


def kernel(x, c, w_ada, b_ada, norm1_g, w_in, rwkv_mu, rwkv_w0, rwkv_w_up, rwkv_a0, rwkv_a_up, rwkv_g_up, rwkv_k_k, rwkv_k_a, rwkv_r_k, rwkv_ln_g, rwkv_ln_b, q_norm_g, k_norm_g, w_out, norm2_g, w_router, b_router, w_gate_up, b_gate_up, w_down, b_down):
    for l in range(w_ada.shape[0]):
        x = _layer(x, c, w_ada[l], b_ada[l], norm1_g[l], w_in[l], rwkv_mu[l], rwkv_w0[l],
                   rwkv_w_up[l], rwkv_a0[l], rwkv_a_up[l], rwkv_g_up[l], rwkv_k_k[l], rwkv_k_a[l],
                   rwkv_r_k[l], rwkv_ln_g[l], rwkv_ln_b[l], q_norm_g[l], k_norm_g[l], w_out[l],
                   norm2_g[l], w_router[l], b_router[l], w_gate_up[l], b_gate_up[l], w_down[l],
                   b_down[l])
    return x
```

```python
import functools

import jax
import jax.numpy as jnp
from jax import lax
from jax.experimental import pallas as pl
from jax.experimental.pallas import tpu as pltpu

F32 = jnp.float32
BF16 = jnp.bfloat16

D_MODEL = 1024
HEAD_DIM = 64
RWKV_WIDTH = 512
MOBA_WIDTH = 512
RWKV_HEADS = RWKV_WIDTH // HEAD_DIM
MOBA_HEADS = MOBA_WIDTH // HEAD_DIM
DECAY_LORA = 64
AAA_LORA = 64
GATE_LORA = 160
RWKV_LN_EPS = 64e-5
RWKV_PROJ = 3 * RWKV_WIDTH + DECAY_LORA + AAA_LORA + GATE_LORA
MOBA_PROJ = 3 * MOBA_WIDTH
MOBA_BLOCK = 256
MOBA_TOPK = 3
N_EXPERTS = 32
TOP_K = 4
D_FF = D_MODEL
SWIGLU_LIMIT = 7.0
SWIGLU_ALPHA = 1.702
NORM_EPS = 1e-6

LANES = 128
SUBLANES = 8
XW_OFF = 3 * RWKV_WIDTH
XA_OFF = XW_OFF + LANES
XG_OFF = XA_OFF + LANES
RWKV_COLS = XG_OFF + 2 * LANES
CHUNK = 64
EXPERT_TILE = 512
EXPERT_SUBSTEPS = 4
VMEM_LIMIT = 56 * 1024 * 1024

NN = (((1,), (0,)), ((), ()))
NT = (((1,), (1,)), ((), ()))
TN = (((0,), (0,)), ((), ()))


def _dot(a, b, dims=NN):
    return lax.dot_general(a, b, dims, preferred_element_type=F32)


def _split(a):
    hi = a.astype(BF16)
    lo = (a - hi.astype(F32)).astype(BF16)
    return hi, lo


def _mm(a, b, dims=NN, passes=1):
    if passes == 1:
        return _dot(a.astype(BF16), b.astype(BF16), dims)
    a_hi, a_lo = _split(a)
    b_hi, b_lo = _split(b)
    return _dot(a_hi, b_hi, dims) + (_dot(a_hi, b_lo, dims) + _dot(a_lo, b_hi, dims))


def _split3(a):
    hi = a.astype(BF16)
    r = a - hi.astype(F32)
    mid = r.astype(BF16)
    lo = (r - mid.astype(F32)).astype(BF16)
    return hi, mid, lo


def _mm_exact_rhs(a, b_bf16, dims=NN):
    hi, mid, lo = _split3(a)
    return _dot(hi, b_bf16, dims) + (_dot(mid, b_bf16, dims) + _dot(lo, b_bf16, dims))


def _mm_exact_lhs(a_bf16, b, dims=NN):
    hi, mid, lo = _split3(b)
    return _dot(a_bf16, hi, dims) + (_dot(a_bf16, mid, dims) + _dot(a_bf16, lo, dims))


def _iota2(shape, dim):
    return lax.broadcasted_iota(jnp.int32, shape, dim)


def _group_ones(n, group):
    return (_iota2((n, n), 0) // group == _iota2((n, n), 1) // group).astype(BF16)


def _sigmoid(x):
    return 1.0 / (1.0 + jnp.exp(-x))


def _params(*sem):
    return pltpu.CompilerParams(dimension_semantics=sem, vmem_limit_bytes=VMEM_LIMIT)


def _ada_kernel(c_ref, w_ref, b_ref, o_ref):
    c = c_ref[...]
    o_ref[...] = _mm(c * _sigmoid(c), w_ref[...], passes=3) + b_ref[...]


def _ada(c, w_ada, b_ada):
    B, D = c.shape
    n_out = w_ada.shape[1]
    tn = 1024
    return pl.pallas_call(
        _ada_kernel,
        grid=(n_out // tn,),
        in_specs=[pl.BlockSpec((B, D), lambda j: (0, 0)),
                  pl.BlockSpec((D, tn), lambda j: (0, j)),
                  pl.BlockSpec((1, tn), lambda j: (0, j))],
        out_specs=pl.BlockSpec((B, tn), lambda j: (0, j)),
        out_shape=jax.ShapeDtypeStruct((B, n_out), F32),
        compiler_params=_params("arbitrary"),
        name="ada",
    )(c, w_ada, b_ada.reshape(1, n_out))


def _rms_modulate(x, g, scale, shift):
    y = x * lax.rsqrt(jnp.mean(x * x, axis=-1, keepdims=True) + NORM_EPS)
    return (y * g) * (1.0 + scale) + shift


def _in_proj_kernel(x_ref, g_ref, scale_ref, shift_ref, w_ref, pr_ref, pm_ref):
    h = _rms_modulate(x_ref[...], g_ref[...], scale_ref[0], shift_ref[0])
    proj = _dot(h.astype(BF16), w_ref[...])
    pr_ref[...] = proj[:, :RWKV_COLS]
    pm_ref[...] = proj[:, RWKV_COLS:]


def _in_proj(x2, norm_g, scale, shift, w_in_b, S):
    T, D = x2.shape
    tm = 256
    per_b = S // tm
    n_cols = w_in_b.shape[1]
    return pl.pallas_call(
        _in_proj_kernel,
        grid=(T // tm,),
        in_specs=[pl.BlockSpec((tm, D), lambda i: (i, 0)),
                  pl.BlockSpec((1, D), lambda i: (0, 0)),
                  pl.BlockSpec((1, 1, D), lambda i: (i // per_b, 0, 0)),
                  pl.BlockSpec((1, 1, D), lambda i: (i // per_b, 0, 0)),
                  pl.BlockSpec((D, n_cols), lambda i: (0, 0))],
        out_specs=[pl.BlockSpec((tm, RWKV_COLS), lambda i: (i, 0)),
                   pl.BlockSpec((tm, MOBA_PROJ), lambda i: (i, 0))],
        out_shape=[jax.ShapeDtypeStruct((T, RWKV_COLS), F32),
                   jax.ShapeDtypeStruct((T, MOBA_PROJ), F32)],
        compiler_params=_params("arbitrary"),
        name="in_proj",
    )(x2, norm_g.reshape(1, D), scale, shift, w_in_b)


RWKV_TILE = 128


def _unit_lower_inverses(Ls):
    n = Ls[0].shape[0]
    r = _iota2((n, n), 0)
    c = _iota2((n, n), 1)
    eye = (r == c).astype(F32)
    in_block = r // 8 == c // 8
    b16 = lambda xs: [x.astype(BF16) for x in xs]
    Ld = [jnp.where(in_block, L, 0.0) for L in Ls]
    Ld_b = b16(Ld)
    Nb_b = [(L - d).astype(BF16) for L, d in zip(Ls, Ld)]
    L2 = [_dot(d, d) for d in Ld_b]
    L2_b = b16(L2)
    L4_b = b16([_dot(x, x) for x in L2_b])
    T0 = [eye + d + l2 + _dot(db, l2b) for d, l2, db, l2b in zip(Ld, L2, Ld_b, L2_b)]
    T0 = [t + _dot(t.astype(BF16), l4b) for t, l4b in zip(T0, L4_b)]
    T0_b = b16(T0)
    M1_b = b16([_dot(t, nb) for t, nb in zip(T0_b, Nb_b)])
    M2_b = b16([_dot(m, m) for m in M1_b])
    M4_b = b16([_dot(m, m) for m in M2_b])
    X = [t + _dot(m4, tb) for t, m4, tb in zip(T0, M4_b, T0_b)]
    X = [x + _dot(m2, x.astype(BF16)) for x, m2 in zip(X, M2_b)]
    return [x + _dot(m1, x.astype(BF16)) for x, m1 in zip(X, M1_b)]


def _rwkv_kernel(p_ref, mu_ref, w0_ref, a0_ref, kk_ref, ka_ref, rk_ref, lng_ref, lnb_ref,
                 wup_ref, aup_ref, gup_ref, y_ref, carry_ref, state_ref, *, ts):
    t = pl.program_id(1)
    W = RWKV_WIDTH

    @pl.when(t == 0)
    def _():
        carry_ref[...] = jnp.zeros_like(carry_ref)
        state_ref[...] = jnp.zeros_like(state_ref)

    p = p_ref[0]
    row = _iota2((ts, 1), 0)
    prev = jnp.where(row == 0, carry_ref[...], pltpu.roll(p, 1, 0))
    carry_ref[...] = p[ts - 1:ts, :]
    pm = p + (prev - p) * mu_ref[...]
    r = pm[:, 0:W]
    k = pm[:, W:2 * W]
    v = pm[:, 2 * W:3 * W]
    xw = pm[:, XW_OFF:XA_OFF]
    xa = pm[:, XA_OFF:XG_OFF]
    xg = pm[:, XG_OFF:RWKV_COLS]

    z = -(w0_ref[...] + _mm(jnp.tanh(xw), wup_ref[...], passes=3))
    softplus = jnp.maximum(z, 0.0) + jnp.log(1.0 + jnp.exp(-jnp.abs(z)))
    logd = -jnp.exp(-softplus - 0.5)
    alpha = _sigmoid(a0_ref[...] + _mm(xa, aup_ref[...], passes=3))
    gate = _mm(_sigmoid(xg), gup_ref[...])

    head_ones = _group_ones(W, HEAD_DIM)
    kk = k * kk_ref[...]
    kk_norm = jnp.sqrt(_mm_exact_rhs(kk * kk, head_ones))
    kk = kk / jnp.maximum(kk_norm, 1e-12)
    kmod = k * (1.0 + (alpha - 1.0) * ka_ref[...])
    bonus = _mm_exact_rhs(r * kmod * rk_ref[...], head_ones) * v

    tr = _iota2((ts, ts), 0)
    tc = _iota2((ts, ts), 1)
    cum = ((tr // CHUNK == tc // CHUNK) & (tc <= tr)).astype(BF16)
    logp = _mm_exact_lhs(cum, logd)
    inv_p = jnp.exp(-logp)
    a_t = -kk * jnp.exp(logp - logd)
    b_t = kk * alpha * inv_p
    k_t = kmod * inv_p
    r_t = r * jnp.exp(logp)

    n2 = 2 * CHUNK
    n_chunks = ts // CHUNK
    n_pairs = RWKV_HEADS // 2
    inst = [(ci, pi) for ci in range(n_chunks) for pi in range(n_pairs)]
    lane = _iota2((1, LANES), 1)
    m0 = lane < HEAD_DIM
    sr = _iota2((n2, n2), 0)
    sc = _iota2((n2, n2), 1)
    strict_lower = sc < sr
    incl_lower = sc <= sr

    def stacked(x, ci, pi):
        xt = x[ci * CHUNK:(ci + 1) * CHUNK, pi * LANES:(pi + 1) * LANES]
        return jnp.concatenate([jnp.where(m0, xt, 0.0), jnp.where(m0, 0.0, xt)], axis=0)

    pcs = [jnp.exp(logp[(ci + 1) * CHUNK - 1:(ci + 1) * CHUNK, pi * LANES:(pi + 1) * LANES])
           for ci, pi in inst]
    r_s = [stacked(r_t, ci, pi) for ci, pi in inst]
    a_b = [stacked(a_t, ci, pi).astype(BF16) for ci, pi in inst]
    r_b = [x.astype(BF16) for x in r_s]
    b_s = [stacked(b_t, ci, pi) for ci, pi in inst]
    k_s = [stacked(k_t, ci, pi) for ci, pi in inst]
    v_b = [stacked(v, ci, pi).astype(BF16) for ci, pi in inst]
    ar_b = [jnp.concatenate([a, rr], axis=0) for a, rr in zip(a_b, r_b)]
    bk_b = [jnp.concatenate([b.astype(BF16), kx.astype(BF16)], axis=0) for b, kx in zip(b_s, k_s)]
    gram = [_dot(ar, bk, NT) for ar, bk in zip(ar_b, bk_b)]
    l_ab = [jnp.where(strict_lower, g[:n2, :n2], 0.0) for g in gram]
    l_ak_b = [jnp.where(strict_lower, g[:n2, n2:], 0.0).astype(BF16) for g in gram]
    m_rb_b = [jnp.where(incl_lower, g[n2:, :n2], 0.0).astype(BF16) for g in gram]
    m_rk_b = [jnp.where(incl_lower, g[n2:, n2:], 0.0).astype(BF16) for g in gram]
    t_b = [x.astype(BF16) for x in _unit_lower_inverses(l_ab)]
    lakv_b = [_dot(l, vv).astype(BF16) for l, vv in zip(l_ak_b, v_b)]
    wu_b = [_dot(tb, jnp.concatenate([a, lv], axis=1)).astype(BF16)
            for tb, a, lv in zip(t_b, a_b, lakv_b)]
    mwu = [_dot(m, wu) for m, wu in zip(m_rb_b, wu_b)]
    q_b = [(rs + x[:, :LANES]).astype(BF16) for rs, x in zip(r_s, mwu)]
    y0 = [x[:, LANES:] + _dot(m, vv) for x, m, vv in zip(mwu, m_rk_b, v_b)]
    bp_b = [(b * pc).astype(BF16) for b, pc in zip(b_s, pcs)]
    kp_b = [(kx * pc).astype(BF16) for kx, pc in zip(k_s, pcs)]
    gh = [_dot(wu, bp, TN) for wu, bp in zip(wu_b, bp_b)]
    g_b = [x[:LANES].astype(BF16) for x in gh]
    h = [x[LANES:] + _dot(vv, kp, TN) for x, vv, kp in zip(gh, v_b, kp_b)]

    y_chunks = []
    for ci in range(n_chunks):
        ids = [ci * n_pairs + pi for pi in range(n_pairs)]
        s0 = [state_ref[pi] for pi in range(n_pairs)]
        s0_b = [s.astype(BF16) for s in s0]
        y_s = [_dot(q_b[n], sb, NT) + y0[n] for n, sb in zip(ids, s0_b)]
        s1 = [s * pcs[n] + _dot(sb, g_b[n]) + h[n] for n, s, sb in zip(ids, s0, s0_b)]
        for pi in range(n_pairs):
            state_ref[pi] = s1[pi]
        y_chunks.append(jnp.concatenate([x[:CHUNK] + x[CHUNK:] for x in y_s], axis=1))
    y = jnp.concatenate(y_chunks, axis=0) if n_chunks > 1 else y_chunks[0]

    mean = _mm_exact_rhs(y, head_ones) * (1.0 / HEAD_DIM)
    yc = y - mean
    var = _mm_exact_rhs(yc * yc, head_ones) * (1.0 / HEAD_DIM)
    yn = yc * lax.rsqrt(var + RWKV_LN_EPS) * lng_ref[...] + lnb_ref[...]
    y_ref[0] = (yn + bonus) * gate


def _rwkv(p_rwkv, prm, B, S, ts=RWKV_TILE):
    W = RWKV_WIDTH
    vec = lambda n: pl.BlockSpec((1, n), lambda b, t: (0, 0))
    mat = lambda m, n: pl.BlockSpec((m, n), lambda b, t: (0, 0))
    return pl.pallas_call(
        functools.partial(_rwkv_kernel, ts=ts),
        grid=(B, S // ts),
        in_specs=[pl.BlockSpec((1, ts, RWKV_COLS), lambda b, t: (b, t, 0)),
                  vec(RWKV_COLS)] + [vec(W)] * 7 + [mat(LANES, W), mat(LANES, W), mat(2 * LANES, W)],
        out_specs=pl.BlockSpec((1, ts, W), lambda b, t: (b, t, 0)),
        out_shape=jax.ShapeDtypeStruct((B, S, W), F32),
        scratch_shapes=[pltpu.VMEM((1, RWKV_COLS), F32),
                        pltpu.VMEM((RWKV_HEADS // 2, LANES, LANES), F32)],
        compiler_params=_params("arbitrary", "arbitrary"),
        name="rwkv",
    )(p_rwkv.reshape(B, S, RWKV_COLS), *prm)


def _moba_kernel(slopes_ref, q_ref, k_ref, v_ref, qg_ref, kg_ref, o_ref, *, S):
    pair = pl.program_id(1)
    NB = S // MOBA_BLOCK
    BLK = MOBA_BLOCK
    n_sel = min(MOBA_TOPK, NB)
    scale = HEAD_DIM ** -0.5
    head_ones = _group_ones(LANES, HEAD_DIM)
    lane = _iota2((1, LANES), 1)

    def head_norm(x, g):
        ss = _mm_exact_rhs(x * x, head_ones)
        return x * lax.rsqrt(ss * (1.0 / HEAD_DIM) + NORM_EPS) * g

    qn = head_norm(q_ref[0], qg_ref[...])
    kn = head_norm(k_ref[0], kg_ref[...])
    kmean = jnp.mean(kn.reshape(NB, BLK, LANES), axis=1)
    q_t = qn.T
    v_tb = v_ref[0].T.astype(BF16)

    blk_of_q = _iota2((1, S), 1) // BLK
    nidx = _iota2((NB, 1), 0)
    valid = nidx < blk_of_q
    q_blk = (blk_of_q * BLK).astype(F32)
    q_loc = (_iota2((1, S), 1) % BLK).astype(F32)
    k_blk = (_iota2((S, 1), 0) // BLK * BLK).astype(F32)
    k_loc = (_iota2((S, 1), 0) % BLK).astype(F32)
    q_lane = _iota2((LANES, 1), 0)
    causal = _iota2((BLK, BLK), 0) <= _iota2((BLK, BLK), 1)

    out_rows = []
    for h in range(2):
        hmask = (lane // HEAD_DIM) == h
        slope = slopes_ref[pair * 2 + h]
        spare = (1 - h) * HEAD_DIM
        k_aug = jnp.where(hmask, kn, 0.0)
        k_aug = jnp.where(lane == spare, slope * k_blk, k_aug)
        k_aug = jnp.where(lane == spare + 1, slope * k_loc, k_aug)
        k_aug = jnp.where((lane == spare + 2) | (lane == spare + 3), 1.0, k_aug)
        q_aug = q_t * scale
        q_aug = jnp.where((q_lane == spare) | (q_lane == spare + 1), 1.0, q_aug)
        q_aug = jnp.where(q_lane == spare + 2, -slope * q_blk, q_aug)
        q_aug = jnp.where(q_lane == spare + 3, -slope * q_loc, q_aug)
        q_hb = q_aug.astype(BF16)
        gate = _mm(jnp.where(hmask, kmean, 0.0), q_t, passes=3)
        gate = jnp.where(valid, gate, -jnp.inf)
        rank = jnp.zeros((NB, S), jnp.int32)
        for m in range(NB):
            gm = gate[m:m + 1, :]
            ahead = (gm > gate) | ((gm == gate) & (m < nidx))
            rank = rank + ahead.astype(jnp.int32)
        sel = valid & (rank < n_sel)
        k_hb = k_aug.astype(BF16)
        v_h = v_tb[h * HEAD_DIM:(h + 1) * HEAD_DIM, :]
        out_blocks = []
        for i in range(NB):
            qs = slice(i * BLK, (i + 1) * BLK)
            tiles = []
            m_run = None
            for n in range(i + 1):
                ks = slice(n * BLK, (n + 1) * BLK)
                s = _dot(k_hb[ks, :], q_hb[:, qs])
                if n < i:
                    s = jnp.where(sel[n:n + 1, qs], s, -jnp.inf)
                else:
                    s = jnp.where(causal, s, -jnp.inf)
                tiles.append(s)
                mx = jnp.max(s, axis=0, keepdims=True)
                m_run = mx if m_run is None else jnp.maximum(m_run, mx)
            l_run = jnp.zeros((1, BLK), F32)
            acc = jnp.zeros((HEAD_DIM, BLK), F32)
            for n in range(i + 1):
                ks = slice(n * BLK, (n + 1) * BLK)
                pt = jnp.exp(tiles[n] - m_run)
                l_run = l_run + jnp.sum(pt, axis=0, keepdims=True)
                acc = acc + _dot(v_h[:, ks], pt.astype(BF16))
            out_blocks.append(acc / l_run)
        out_rows.append(jnp.concatenate(out_blocks, axis=1))
    o_ref[0] = jnp.concatenate(out_rows, axis=0).T


def _moba(p_moba, q_norm_g, k_norm_g, B, S):
    pairs = MOBA_HEADS // 2
    col = lambda off: pl.BlockSpec((1, S, LANES), lambda b, p: (b, 0, off + p))
    gain = pl.BlockSpec((1, LANES), lambda b, p: (0, 0))
    tile2 = lambda g: jnp.concatenate([g, g]).reshape(1, LANES)
    p3 = p_moba.reshape(B, S, MOBA_PROJ)
    slopes = jnp.exp2(-8.0 * (jnp.arange(MOBA_HEADS, dtype=F32) + 1.0) / MOBA_HEADS)
    return pl.pallas_call(
        functools.partial(_moba_kernel, S=S),
        grid=(B, pairs),
        in_specs=[pl.BlockSpec(memory_space=pltpu.SMEM), col(0), col(pairs), col(2 * pairs),
                  gain, gain],
        out_specs=pl.BlockSpec((1, S, LANES), lambda b, p: (b, 0, p)),
        out_shape=jax.ShapeDtypeStruct((B, S, MOBA_WIDTH), F32),
        compiler_params=_params("arbitrary", "arbitrary"),
        name="moba",
    )(slopes, p3, p3, p3, tile2(q_norm_g), tile2(k_norm_g))


def _out_proj_kernel(yr_ref, ym_ref, x_ref, gate_ref, g_ref, scale_ref, shift_ref, w_ref,
                     wr_ref, br_ref, x1_ref, h2_ref, idx_ref, wgt_ref):
    W = RWKV_WIDTH
    mix = (_dot(yr_ref[...].astype(BF16), w_ref[0:W, :])
           + _dot(ym_ref[...].astype(BF16), w_ref[W:, :]))
    x1 = x_ref[...] + gate_ref[0] * mix
    x1_ref[...] = x1
    h2 = _rms_modulate(x1, g_ref[...], scale_ref[0], shift_ref[0])
    h2_ref[:, 0, :] = h2
    logits_t = (_mm(h2, wr_ref[...], passes=3) + br_ref[...]).T[:N_EXPERTS, :]
    tm = logits_t.shape[1]
    eidx = _iota2((N_EXPERTS, 1), 0)
    vals, idxs = [], []
    for _ in range(TOP_K):
        m = jnp.max(logits_t, axis=0, keepdims=True)
        idx = jnp.min(jnp.where(logits_t == m, eidx, N_EXPERTS), axis=0, keepdims=True)
        vals.append(m)
        idxs.append(idx)
        logits_t = jnp.where(eidx == idx, -jnp.inf, logits_t)
    idx_ref[...] = jnp.concatenate(idxs, axis=0)
    e = [jnp.exp(v - vals[0]) for v in vals]
    total = e[0] + e[1] + e[2] + e[3]
    wgt_t = jnp.concatenate([x / total for x in e] + [jnp.zeros((LANES - TOP_K, tm), F32)], axis=0)
    wgt_ref[...] = wgt_t.T


def _out_proj(y_rwkv, y_moba, x2, gate1, norm_g, scale, shift, w_out_b, w_router, b_router, S):
    T, D = x2.shape
    tm = 256
    per_b = S // tm
    rows = lambda n: pl.BlockSpec((tm, n), lambda i: (i, 0))
    mod = pl.BlockSpec((1, 1, D), lambda i: (i // per_b, 0, 0))
    full = lambda m, n: pl.BlockSpec((m, n), lambda i: (0, 0))
    wr = _pad_cols(w_router, LANES)
    br = jnp.concatenate([b_router, jnp.full((LANES - N_EXPERTS,), -jnp.inf, F32)]).reshape(1, LANES)
    return pl.pallas_call(
        _out_proj_kernel,
        grid=(T // tm,),
        in_specs=[rows(RWKV_WIDTH), rows(MOBA_WIDTH), rows(D), mod, full(1, D), mod, mod,
                  full(D, D), full(D, LANES), full(1, LANES)],
        out_specs=[rows(D), pl.BlockSpec((tm, 1, D), lambda i: (i, 0, 0)),
                   pl.BlockSpec((TOP_K, tm), lambda i: (0, i)), rows(LANES)],
        out_shape=[jax.ShapeDtypeStruct((T, D), F32), jax.ShapeDtypeStruct((T, 1, D), F32),
                   jax.ShapeDtypeStruct((TOP_K, T), jnp.int32), jax.ShapeDtypeStruct((T, LANES), F32)],
        compiler_params=_params("arbitrary"),
        name="out_proj",
    )(y_rwkv, y_moba, x2, gate1, norm_g.reshape(1, D), scale, shift, w_out_b, wr, br)


def _experts_kernel(meta_ref, src_first_ref, src_a_ref, src_b_ref, dst_a_ref, dst_b_ref, h_hbm,
                    wgu_a_ref, bgu_a_ref, wd_a_ref, bd_a_ref, wgu_b_ref, bgu_b_ref, wd_b_ref, bd_b_ref,
                    y_hbm, xbuf0, xbuf1, obuf0, obuf1, xb16, gsem, ssem, *, n_slots):
    g = pl.program_id(0)
    n_used = meta_ref[0]
    tm = EXPERT_TILE
    n_sub = EXPERT_SUBSTEPS
    cols_sub = D_FF // n_sub
    n_dump = y_hbm.shape[0] - n_slots

    def gather_row(src_ref, r, dst_buf, sem):
        pltpu.make_async_copy(h_hbm.at[src_ref[0, 0, r]], dst_buf.at[pl.ds(r, 1)], sem).start()

    def scatter_row(dst_ref, r, src_buf, sem):
        pltpu.make_async_copy(src_buf.at[pl.ds(r, 1)], y_hbm.at[dst_ref[0, 0, r]], sem).start()

    def wait_tile_gather(buf, sem):
        pltpu.make_async_copy(h_hbm.at[pl.ds(0, tm), 0], buf, sem).wait()

    def wait_tile_scatter(buf, sem):
        pltpu.make_async_copy(buf, y_hbm.at[pl.ds(0, tm), 0], sem).wait()

    @pl.when(g == 0)
    def _():
        def body(r, carry):
            gather_row(src_first_ref, r, xbuf0, gsem.at[0])
            return carry
        lax.fori_loop(0, tm, body, 0)
        obuf1[...] = jnp.zeros_like(obuf1)
        for d in range(n_dump // tm):
            cp = pltpu.make_async_copy(obuf1, y_hbm.at[pl.ds(n_slots + d * tm, tm), 0], ssem.at[0])
            cp.start()
            cp.wait()

    def phase(ph, src_next_ref, dst_prev_ref, wgu_ref, bgu_ref, wd_ref, bd_ref):
        j = 2 * g + ph
        x_cur, x_nxt = (xbuf0, xbuf1) if ph == 0 else (xbuf1, xbuf0)
        o_cur, o_prv = (obuf0, obuf1) if ph == 0 else (obuf1, obuf0)

        def ffn_chunk(n):
            cg = slice(n * cols_sub, (n + 1) * cols_sub)
            cu = slice(D_FF + n * cols_sub, D_FF + (n + 1) * cols_sub)
            xb = xb16[...]
            gate = jnp.minimum(_dot(xb, wgu_ref[0, :, cg]) + bgu_ref[0, :, cg], SWIGLU_LIMIT)
            up = jnp.clip(_dot(xb, wgu_ref[0, :, cu]) + bgu_ref[0, :, cu], -SWIGLU_LIMIT, SWIGLU_LIMIT)
            act = (up + 1.0) * gate * _sigmoid(SWIGLU_ALPHA * gate)
            o_cur[...] += _dot(act.astype(BF16), wd_ref[0, cg, :])

        @pl.when(j < n_used)
        def _():
            wait_tile_gather(x_cur, gsem.at[ph])

            @pl.when(j >= 1)
            def _():
                wait_tile_scatter(o_cur, ssem.at[ph])

            for r in range(tm):
                gather_row(src_next_ref, r, x_nxt, gsem.at[1 - ph])
                scatter_row(dst_prev_ref, r, o_prv, ssem.at[1 - ph])
            xb16[...] = x_cur[...].astype(BF16)
            o_cur[...] = jnp.broadcast_to(bd_ref[0], (tm, D_MODEL))

        @pl.when(j + 1 <= n_used)
        def _():
            for n in range(n_sub):
                ffn_chunk(n)

        @pl.when(j == n_used)
        def _():
            def body(r, carry):
                scatter_row(dst_prev_ref, r, o_prv, ssem.at[1 - ph])
                return carry
            lax.fori_loop(0, tm, body, 0)
            wait_tile_scatter(o_prv, ssem.at[1 - ph])
            wait_tile_scatter(o_cur, ssem.at[ph])
            wait_tile_gather(x_cur, gsem.at[ph])

    phase(0, src_a_ref, dst_a_ref, wgu_a_ref, bgu_a_ref, wd_a_ref, bd_a_ref)
    phase(1, src_b_ref, dst_b_ref, wgu_b_ref, bgu_b_ref, wd_b_ref, bd_b_ref)


def _experts(h2, src_tok, dst_row, tile_expert, n_used, wgu_b, bgu, wd_b, bd, n_slots, n_dump):
    D = h2.shape[-1]
    tm = EXPERT_TILE
    n_tiles = tile_expert.shape[0]
    src3 = src_tok.reshape(n_tiles + 2, 1, tm)
    dst3 = dst_row.reshape(n_tiles + 2, 1, tm)
    meta = jnp.concatenate([n_used.reshape(1), tile_expert]).astype(jnp.int32)
    smem_row = lambda f: pl.BlockSpec((1, 1, tm), f, memory_space=pltpu.SMEM)
    e_of = lambda j, m: m[1 + jnp.minimum(j, n_tiles - 1)]
    weights = lambda ph: [
        pl.BlockSpec((1, D, 2 * D_FF), lambda g, m: (e_of(2 * g + ph, m), 0, 0)),
        pl.BlockSpec((1, 1, 2 * D_FF), lambda g, m: (e_of(2 * g + ph, m), 0, 0)),
        pl.BlockSpec((1, D_FF, D), lambda g, m: (e_of(2 * g + ph, m), 0, 0)),
        pl.BlockSpec((1, 1, D), lambda g, m: (e_of(2 * g + ph, m), 0, 0))]
    grid_spec = pltpu.PrefetchScalarGridSpec(
        num_scalar_prefetch=1,
        grid=((n_tiles + 2) // 2,),
        in_specs=[smem_row(lambda g, m: (0, 0, 0)),
                  smem_row(lambda g, m: (2 * g + 1, 0, 0)),
                  smem_row(lambda g, m: (jnp.minimum(2 * g + 2, n_tiles + 1), 0, 0)),
                  smem_row(lambda g, m: (2 * g, 0, 0)),
                  smem_row(lambda g, m: (2 * g + 1, 0, 0)),
                  pl.BlockSpec(memory_space=pl.ANY)] + weights(0) + weights(1),
        out_specs=pl.BlockSpec(memory_space=pl.ANY),
        scratch_shapes=[pltpu.VMEM((tm, D), F32), pltpu.VMEM((tm, D), F32),
                        pltpu.VMEM((tm, D), F32), pltpu.VMEM((tm, D), F32),
                        pltpu.VMEM((tm, D), BF16),
                        pltpu.SemaphoreType.DMA((2,)), pltpu.SemaphoreType.DMA((2,))],
    )
    bgu3 = bgu.reshape(N_EXPERTS, 1, 2 * D_FF)
    bd3 = bd.reshape(N_EXPERTS, 1, D)
    return pl.pallas_call(
        functools.partial(_experts_kernel, n_slots=n_slots),
        grid_spec=grid_spec,
        out_shape=jax.ShapeDtypeStruct((n_slots + n_dump, 1, D), F32),
        compiler_params=_params("arbitrary"),
        name="experts",
    )(meta, src3, src3, src3, dst3, dst3, h2, wgu_b, bgu3, wd_b, bd3, wgu_b, bgu3, wd_b, bd3)


def _route(top_idx_t):
    T = top_idx_t.shape[1]
    tm = EXPERT_TILE
    M = T * TOP_K
    slot_expert = top_idx_t.reshape(M)
    order = jnp.argsort(slot_expert).astype(jnp.int32)
    counts = jnp.bincount(slot_expert, length=N_EXPERTS)
    padded = (counts + tm - 1) // tm * tm
    pad_end = jnp.cumsum(padded)
    pad_start = pad_end - padded
    start = jnp.cumsum(counts) - counts
    n_tiles = M // tm + N_EXPERTS
    tile_start = jnp.arange(n_tiles) * tm
    tile_expert = jnp.minimum(jnp.searchsorted(pad_end, tile_start, side='right'), N_EXPERTS - 1)
    tile_valid = jnp.clip(counts[tile_expert] - (tile_start - pad_start[tile_expert]), 0, tm)
    n_used = pad_end[-1] // tm
    r = jnp.arange(tm)[None, :]
    sorted_pos = (start[tile_expert] + tile_start - pad_start[tile_expert])[:, None] + r
    slot = order[jnp.clip(sorted_pos, 0, M - 1)]
    is_real = r < tile_valid[:, None]
    src_tok = jnp.where(is_real, slot % T, 0).astype(jnp.int32)
    dump = M + tm + tile_expert[:, None] * tm + (r - tile_valid[:, None])
    dst_row = jnp.where(is_real, slot, dump).astype(jnp.int32)
    lead = (M + r).astype(jnp.int32)
    zeros = jnp.zeros((1, tm), jnp.int32)
    src_ext = jnp.concatenate([src_tok, zeros, zeros], axis=0)
    dst_ext = jnp.concatenate([lead, dst_row, zeros], axis=0)
    n_dump = (N_EXPERTS + 1) * tm
    return src_ext, dst_ext, tile_expert.astype(jnp.int32), n_used.astype(jnp.int32), n_dump


def _combine_kernel(x1_ref, y0_ref, y1_ref, y2_ref, y3_ref, w_ref, gate_ref, o_ref):
    w = w_ref[...]
    acc = w[:, 0:1] * y0_ref[:, 0, :]
    for j, y_ref in enumerate((y1_ref, y2_ref, y3_ref), start=1):
        acc = acc + w[:, j:j + 1] * y_ref[:, 0, :]
    o_ref[...] = x1_ref[...] + gate_ref[0] * acc


def _combine(x1, y_slots, weights, gate2, S):
    T, D = x1.shape
    tm = 256
    per_b = S // tm
    n_t = T // tm
    plane = lambda k: pl.BlockSpec((tm, 1, D), lambda i: (k * n_t + i, 0, 0))
    return pl.pallas_call(
        _combine_kernel,
        grid=(n_t,),
        in_specs=[pl.BlockSpec((tm, D), lambda i: (i, 0))] + [plane(k) for k in range(TOP_K)]
                 + [pl.BlockSpec((tm, LANES), lambda i: (i, 0)),
                    pl.BlockSpec((1, 1, D), lambda i: (i // per_b, 0, 0))],
        out_specs=pl.BlockSpec((tm, D), lambda i: (i, 0)),
        out_shape=jax.ShapeDtypeStruct((T, D), F32),
        compiler_params=_params("arbitrary"),
        name="combine",
    )(x1, y_slots, y_slots, y_slots, y_slots, weights, gate2)


def _pad_cols(w, n):
    return jnp.pad(w, ((0, 0), (0, n - w.shape[1])))


def _pad_rows(w, n):
    return jnp.pad(w, ((0, n - w.shape[0]), (0, 0)))


def _layer(x, c, w_ada, b_ada, norm1_g, w_in, rwkv_mu, rwkv_w0, rwkv_w_up, rwkv_a0, rwkv_a_up,
           rwkv_g_up, rwkv_k_k, rwkv_k_a, rwkv_r_k, rwkv_ln_g, rwkv_ln_b, q_norm_g, k_norm_g,
           w_out, norm2_g, w_router, b_router, w_gate_up, b_gate_up, w_down, b_down):
    B, S, D = x.shape
    T = B * S
    W = RWKV_WIDTH
    x2 = x.reshape(T, D)

    mods = _ada(c, w_ada, b_ada)
    shift1, scale1, gate1, shift2, scale2, gate2 = [
        mods[:, j * D:(j + 1) * D].reshape(B, 1, D) for j in range(6)]

    pieces = [(w_in[:, :XW_OFF], XW_OFF),
              (w_in[:, XW_OFF:XW_OFF + DECAY_LORA], LANES),
              (w_in[:, XW_OFF + DECAY_LORA:XW_OFF + DECAY_LORA + AAA_LORA], LANES),
              (w_in[:, XW_OFF + DECAY_LORA + AAA_LORA:RWKV_PROJ], 2 * LANES),
              (w_in[:, RWKV_PROJ:], MOBA_PROJ)]
    w_in_b = jnp.concatenate([_pad_cols(w, n) for w, n in pieces], axis=1).astype(BF16)
    mu_pieces = [(rwkv_mu[None, :XW_OFF], XW_OFF),
                 (rwkv_mu[None, XW_OFF:XW_OFF + DECAY_LORA], LANES),
                 (rwkv_mu[None, XW_OFF + DECAY_LORA:XW_OFF + DECAY_LORA + AAA_LORA], LANES),
                 (rwkv_mu[None, XW_OFF + DECAY_LORA + AAA_LORA:], 2 * LANES)]
    mu = jnp.concatenate([_pad_cols(m, n) for m, n in mu_pieces], axis=1)

    p_rwkv, p_moba = _in_proj(x2, norm1_g, scale1, shift1, w_in_b, S)

    row = lambda a: a.reshape(1, W)
    prm = (mu, row(rwkv_w0), row(rwkv_a0), row(rwkv_k_k), row(rwkv_k_a), row(rwkv_r_k),
           row(rwkv_ln_g), row(rwkv_ln_b), _pad_rows(rwkv_w_up, LANES), _pad_rows(rwkv_a_up, LANES),
           _pad_rows(rwkv_g_up, 2 * LANES))
    y_rwkv = _rwkv(p_rwkv, prm, B, S)
    y_moba = _moba(p_moba, q_norm_g, k_norm_g, B, S)

    x1, h2, top_idx_t, weights = _out_proj(y_rwkv.reshape(T, W), y_moba.reshape(T, MOBA_WIDTH), x2,
                                           gate1, norm2_g, scale2, shift2, w_out.astype(BF16),
                                           w_router, b_router, S)

    src_tok, dst_row, tile_expert, n_used, n_dump = _route(top_idx_t)
    y_slots = _experts(h2, src_tok, dst_row, tile_expert, n_used, w_gate_up.astype(BF16), b_gate_up,
                       w_down.astype(BF16), b_down, T * TOP_K, n_dump)
    out = _combine(x1, y_slots, weights, gate2, S)
    return out.reshape(B, S, D)


def kernel(x, c, w_ada, b_ada, norm1_g, w_in, rwkv_mu, rwkv_w0, rwkv_w_up, rwkv_a0, rwkv_a_up, rwkv_g_up, rwkv_k_k, rwkv_k_a, rwkv_r_k, rwkv_ln_g, rwkv_ln_b, q_norm_g, k_norm_g, w_out, norm2_g, w_router, b_router, w_gate_up, b_gate_up, w_down, b_down):
    for l in range(w_ada.shape[0]):
        x = _layer(x, c, w_ada[l], b_ada[l], norm1_g[l], w_in[l], rwkv_mu[l], rwkv_w0[l],
                   rwkv_w_up[l], rwkv_a0[l], rwkv_a_up[l], rwkv_g_up[l], rwkv_k_k[l], rwkv_k_a[l],
                   rwkv_r_k[l], rwkv_ln_g[l], rwkv_ln_b[l], q_norm_g[l], k_norm_g[l], w_out[l],
                   norm2_g[l], w_router[l], b_router[l], w_gate_up[l], b_gate_up[l], w_down[l],
                   b_down[l])
    return x
```

```python
import functools

import jax
import jax.numpy as jnp
from jax import lax
from jax.experimental import pallas as pl
from jax.experimental.pallas import tpu as pltpu

F32 = jnp.float32
BF16 = jnp.bfloat16

D_MODEL = 1024
HEAD_DIM = 64
RWKV_WIDTH = 512
MOBA_WIDTH = 512
RWKV_HEADS = RWKV_WIDTH // HEAD_DIM
MOBA_HEADS = MOBA_WIDTH // HEAD_DIM
DECAY_LORA = 64
AAA_LORA = 64
GATE_LORA = 160
RWKV_LN_EPS = 64e-5
RWKV_PROJ = 3 * RWKV_WIDTH + DECAY_LORA + AAA_LORA + GATE_LORA
MOBA_PROJ = 3 * MOBA_WIDTH
MOBA_BLOCK = 256
MOBA_TOPK = 3
N_EXPERTS = 32
TOP_K = 4
D_FF = D_MODEL
SWIGLU_LIMIT = 7.0
SWIGLU_ALPHA = 1.702
NORM_EPS = 1e-6

LANES = 128
SUBLANES = 8
XW_OFF = 3 * RWKV_WIDTH
XA_OFF = XW_OFF + LANES
XG_OFF = XA_OFF + LANES
RWKV_COLS = XG_OFF + 2 * LANES
CHUNK = 64
EXPERT_TILE = 512
EXPERT_SUBSTEPS = 4
VMEM_LIMIT = 56 * 1024 * 1024

NN = (((1,), (0,)), ((), ()))
NT = (((1,), (1,)), ((), ()))
TN = (((0,), (0,)), ((), ()))


def _dot(a, b, dims=NN):
    return lax.dot_general(a, b, dims, preferred_element_type=F32)


def _split(a):
    hi = a.astype(BF16)
    lo = (a - hi.astype(F32)).astype(BF16)
    return hi, lo


def _mm(a, b, dims=NN, passes=1):
    if passes == 1:
        return _dot(a.astype(BF16), b.astype(BF16), dims)
    a_hi, a_lo = _split(a)
    b_hi, b_lo = _split(b)
    return _dot(a_hi, b_hi, dims) + (_dot(a_hi, b_lo, dims) + _dot(a_lo, b_hi, dims))


def _split3(a):
    hi = a.astype(BF16)
    r = a - hi.astype(F32)
    mid = r.astype(BF16)
    lo = (r - mid.astype(F32)).astype(BF16)
    return hi, mid, lo


def _mm_exact_rhs(a, b_bf16, dims=NN):
    hi, mid, lo = _split3(a)
    return _dot(hi, b_bf16, dims) + (_dot(mid, b_bf16, dims) + _dot(lo, b_bf16, dims))


def _mm_exact_lhs(a_bf16, b, dims=NN):
    hi, mid, lo = _split3(b)
    return _dot(a_bf16, hi, dims) + (_dot(a_bf16, mid, dims) + _dot(a_bf16, lo, dims))


def _iota2(shape, dim):
    return lax.broadcasted_iota(jnp.int32, shape, dim)


def _group_ones(n, group):
    return (_iota2((n, n), 0) // group == _iota2((n, n), 1) // group).astype(BF16)


def _sigmoid(x):
    return 1.0 / (1.0 + jnp.exp(-x))


def _params(*sem):
    return pltpu.CompilerParams(dimension_semantics=sem, vmem_limit_bytes=VMEM_LIMIT)


def _ada_kernel(c_ref, w_ref, b_ref, o_ref):
    c = c_ref[...]
    o_ref[...] = _mm(c * _sigmoid(c), w_ref[...], passes=3) + b_ref[...]


def _ada(c, w_ada, b_ada):
    B, D = c.shape
    n_out = w_ada.shape[1]
    tn = 1024
    return pl.pallas_call(
        _ada_kernel,
        grid=(n_out // tn,),
        in_specs=[pl.BlockSpec((B, D), lambda j: (0, 0)),
                  pl.BlockSpec((D, tn), lambda j: (0, j)),
                  pl.BlockSpec((1, tn), lambda j: (0, j))],
        out_specs=pl.BlockSpec((B, tn), lambda j: (0, j)),
        out_shape=jax.ShapeDtypeStruct((B, n_out), F32),
        compiler_params=_params("arbitrary"),
        name="ada",
    )(c, w_ada, b_ada.reshape(1, n_out))


def _rms_modulate(x, g, scale, shift):
    y = x * lax.rsqrt(jnp.mean(x * x, axis=-1, keepdims=True) + NORM_EPS)
    return (y * g) * (1.0 + scale) + shift


def _in_proj_kernel(x_ref, g_ref, scale_ref, shift_ref, w_ref, pr_ref, pm_ref):
    h = _rms_modulate(x_ref[...], g_ref[...], scale_ref[0], shift_ref[0])
    proj = _dot(h.astype(BF16), w_ref[...])
    pr_ref[...] = proj[:, :RWKV_COLS]
    pm_ref[...] = proj[:, RWKV_COLS:]


def _in_proj(x2, norm_g, scale, shift, w_in_b, S):
    T, D = x2.shape
    tm = 256
    per_b = S // tm
    n_cols = w_in_b.shape[1]
    return pl.pallas_call(
        _in_proj_kernel,
        grid=(T // tm,),
        in_specs=[pl.BlockSpec((tm, D), lambda i: (i, 0)),
                  pl.BlockSpec((1, D), lambda i: (0, 0)),
                  pl.BlockSpec((1, 1, D), lambda i: (i // per_b, 0, 0)),
                  pl.BlockSpec((1, 1, D), lambda i: (i // per_b, 0, 0)),
                  pl.BlockSpec((D, n_cols), lambda i: (0, 0))],
        out_specs=[pl.BlockSpec((tm, RWKV_COLS), lambda i: (i, 0)),
                   pl.BlockSpec((tm, MOBA_PROJ), lambda i: (i, 0))],
        out_shape=[jax.ShapeDtypeStruct((T, RWKV_COLS), F32),
                   jax.ShapeDtypeStruct((T, MOBA_PROJ), F32)],
        compiler_params=_params("arbitrary"),
        name="in_proj",
    )(x2, norm_g.reshape(1, D), scale, shift, w_in_b)


RWKV_TILE = 256


def _unit_lower_inverses(Ls):
    n = Ls[0].shape[0]
    r = _iota2((n, n), 0)
    c = _iota2((n, n), 1)
    eye = (r == c).astype(F32)
    in_block = r // 8 == c // 8
    b16 = lambda xs: [x.astype(BF16) for x in xs]
    Ld = [jnp.where(in_block, L, 0.0) for L in Ls]
    Ld_b = b16(Ld)
    Nb_b = [(L - d).astype(BF16) for L, d in zip(Ls, Ld)]
    L2 = [_dot(d, d) for d in Ld_b]
    L2_b = b16(L2)
    L4_b = b16([_dot(x, x) for x in L2_b])
    T0 = [eye + d + l2 + _dot(db, l2b) for d, l2, db, l2b in zip(Ld, L2, Ld_b, L2_b)]
    T0 = [t + _dot(t.astype(BF16), l4b) for t, l4b in zip(T0, L4_b)]
    T0_b = b16(T0)
    M1_b = b16([_dot(t, nb) for t, nb in zip(T0_b, Nb_b)])
    M2_b = b16([_dot(m, m) for m in M1_b])
    M4_b = b16([_dot(m, m) for m in M2_b])
    X = [t + _dot(m4, tb) for t, m4, tb in zip(T0, M4_b, T0_b)]
    X = [x + _dot(m2, x.astype(BF16)) for x, m2 in zip(X, M2_b)]
    return [x + _dot(m1, x.astype(BF16)) for x, m1 in zip(X, M1_b)]


def _rwkv_kernel(p_ref, mu_ref, w0_ref, a0_ref, kk_ref, ka_ref, rk_ref, lng_ref, lnb_ref,
                 wup_ref, aup_ref, gup_ref, y_ref, carry_ref, state_ref, *, ts):
    t = pl.program_id(1)
    W = RWKV_WIDTH

    @pl.when(t == 0)
    def _():
        carry_ref[...] = jnp.zeros_like(carry_ref)
        state_ref[...] = jnp.zeros_like(state_ref)

    p = p_ref[0]
    row = _iota2((ts, 1), 0)
    prev = jnp.where(row == 0, carry_ref[...], pltpu.roll(p, 1, 0))
    carry_ref[...] = p[ts - 1:ts, :]
    pm = p + (prev - p) * mu_ref[...]
    r = pm[:, 0:W]
    k = pm[:, W:2 * W]
    v = pm[:, 2 * W:3 * W]
    xw = pm[:, XW_OFF:XA_OFF]
    xa = pm[:, XA_OFF:XG_OFF]
    xg = pm[:, XG_OFF:RWKV_COLS]

    z = -(w0_ref[...] + _mm(jnp.tanh(xw), wup_ref[...], passes=3))
    softplus = jnp.maximum(z, 0.0) + jnp.log(1.0 + jnp.exp(-jnp.abs(z)))
    logd = -jnp.exp(-softplus - 0.5)
    alpha = _sigmoid(a0_ref[...] + _mm(xa, aup_ref[...], passes=3))
    gate = _mm(_sigmoid(xg), gup_ref[...])

    head_ones = _group_ones(W, HEAD_DIM)
    kk = k * kk_ref[...]
    kk_norm = jnp.sqrt(_mm_exact_rhs(kk * kk, head_ones))
    kk = kk / jnp.maximum(kk_norm, 1e-12)
    kmod = k * (1.0 + (alpha - 1.0) * ka_ref[...])
    bonus = _mm_exact_rhs(r * kmod * rk_ref[...], head_ones) * v

    tr = _iota2((ts, ts), 0)
    tc = _iota2((ts, ts), 1)
    cum = ((tr // CHUNK == tc // CHUNK) & (tc <= tr)).astype(BF16)
    logp = _mm_exact_lhs(cum, logd)
    inv_p = jnp.exp(-logp)
    a_t = -kk * jnp.exp(logp - logd)
    b_t = kk * alpha * inv_p
    k_t = kmod * inv_p
    r_t = r * jnp.exp(logp)

    n2 = 2 * CHUNK
    n_chunks = ts // CHUNK
    n_pairs = RWKV_HEADS // 2
    inst = [(ci, pi) for ci in range(n_chunks) for pi in range(n_pairs)]
    lane = _iota2((1, LANES), 1)
    m0 = lane < HEAD_DIM
    sr = _iota2((n2, n2), 0)
    sc = _iota2((n2, n2), 1)
    strict_lower = sc < sr
    incl_lower = sc <= sr

    def stacked(x, ci, pi):
        xt = x[ci * CHUNK:(ci + 1) * CHUNK, pi * LANES:(pi + 1) * LANES]
        return jnp.concatenate([jnp.where(m0, xt, 0.0), jnp.where(m0, 0.0, xt)], axis=0)

    pcs = [jnp.exp(logp[(ci + 1) * CHUNK - 1:(ci + 1) * CHUNK, pi * LANES:(pi + 1) * LANES])
           for ci, pi in inst]
    r_s = [stacked(r_t, ci, pi) for ci, pi in inst]
    a_b = [stacked(a_t, ci, pi).astype(BF16) for ci, pi in inst]
    r_b = [x.astype(BF16) for x in r_s]
    b_s = [stacked(b_t, ci, pi) for ci, pi in inst]
    k_s = [stacked(k_t, ci, pi) for ci, pi in inst]
    v_b = [stacked(v, ci, pi).astype(BF16) for ci, pi in inst]
    ar_b = [jnp.concatenate([a, rr], axis=0) for a, rr in zip(a_b, r_b)]
    bk_b = [jnp.concatenate([b.astype(BF16), kx.astype(BF16)], axis=0) for b, kx in zip(b_s, k_s)]
    gram = [_dot(ar, bk, NT) for ar, bk in zip(ar_b, bk_b)]
    l_ab = [jnp.where(strict_lower, g[:n2, :n2], 0.0) for g in gram]
    l_ak_b = [jnp.where(strict_lower, g[:n2, n2:], 0.0).astype(BF16) for g in gram]
    m_rb_b = [jnp.where(incl_lower, g[n2:, :n2], 0.0).astype(BF16) for g in gram]
    m_rk_b = [jnp.where(incl_lower, g[n2:, n2:], 0.0).astype(BF16) for g in gram]
    t_b = [x.astype(BF16) for x in _unit_lower_inverses(l_ab)]
    lakv_b = [_dot(l, vv).astype(BF16) for l, vv in zip(l_ak_b, v_b)]
    wu_b = [_dot(tb, jnp.concatenate([a, lv], axis=1)).astype(BF16)
            for tb, a, lv in zip(t_b, a_b, lakv_b)]
    mwu = [_dot(m, wu) for m, wu in zip(m_rb_b, wu_b)]
    q_b = [(rs + x[:, :LANES]).astype(BF16) for rs, x in zip(r_s, mwu)]
    y0 = [x[:, LANES:] + _dot(m, vv) for x, m, vv in zip(mwu, m_rk_b, v_b)]
    bp_b = [(b * pc).astype(BF16) for b, pc in zip(b_s, pcs)]
    kp_b = [(kx * pc).astype(BF16) for kx, pc in zip(k_s, pcs)]
    gh = [_dot(wu, bp, TN) for wu, bp in zip(wu_b, bp_b)]
    g_b = [x[:LANES].astype(BF16) for x in gh]
    h = [x[LANES:] + _dot(vv, kp, TN) for x, vv, kp in zip(gh, v_b, kp_b)]

    y_chunks = []
    for ci in range(n_chunks):
        ids = [ci * n_pairs + pi for pi in range(n_pairs)]
        s0 = [state_ref[pi] for pi in range(n_pairs)]
        s0_b = [s.astype(BF16) for s in s0]
        y_s = [_dot(q_b[n], sb, NT) + y0[n] for n, sb in zip(ids, s0_b)]
        s1 = [s * pcs[n] + _dot(sb, g_b[n]) + h[n] for n, s, sb in zip(ids, s0, s0_b)]
        for pi in range(n_pairs):
            state_ref[pi] = s1[pi]
        y_chunks.append(jnp.concatenate([x[:CHUNK] + x[CHUNK:] for x in y_s], axis=1))
    y = jnp.concatenate(y_chunks, axis=0) if n_chunks > 1 else y_chunks[0]

    mean = _mm_exact_rhs(y, head_ones) * (1.0 / HEAD_DIM)
    yc = y - mean
    var = _mm_exact_rhs(yc * yc, head_ones) * (1.0 / HEAD_DIM)
    yn = yc * lax.rsqrt(var + RWKV_LN_EPS) * lng_ref[...] + lnb_ref[...]
    y_ref[0] = (yn + bonus) * gate


def _rwkv(p_rwkv, prm, B, S, ts=RWKV_TILE):
    W = RWKV_WIDTH
    vec = lambda n: pl.BlockSpec((1, n), lambda b, t: (0, 0))
    mat = lambda m, n: pl.BlockSpec((m, n), lambda b, t: (0, 0))
    return pl.pallas_call(
        functools.partial(_rwkv_kernel, ts=ts),
        grid=(B, S // ts),
        in_specs=[pl.BlockSpec((1, ts, RWKV_COLS), lambda b, t: (b, t, 0)),
                  vec(RWKV_COLS)] + [vec(W)] * 7 + [mat(LANES, W), mat(LANES, W), mat(2 * LANES, W)],
        out_specs=pl.BlockSpec((1, ts, W), lambda b, t: (b, t, 0)),
        out_shape=jax.ShapeDtypeStruct((B, S, W), F32),
        scratch_shapes=[pltpu.VMEM((1, RWKV_COLS), F32),
                        pltpu.VMEM((RWKV_HEADS // 2, LANES, LANES), F32)],
        compiler_params=_params("arbitrary", "arbitrary"),
        name="rwkv",
    )(p_rwkv.reshape(B, S, RWKV_COLS), *prm)


def _moba_kernel(slopes_ref, q_ref, k_ref, v_ref, qg_ref, kg_ref, o_ref, *, S):
    pair = pl.program_id(1)
    NB = S // MOBA_BLOCK
    BLK = MOBA_BLOCK
    n_sel = min(MOBA_TOPK, NB)
    scale = HEAD_DIM ** -0.5
    head_ones = _group_ones(LANES, HEAD_DIM)
    lane = _iota2((1, LANES), 1)

    def head_norm(x, g):
        ss = _mm_exact_rhs(x * x, head_ones)
        return x * lax.rsqrt(ss * (1.0 / HEAD_DIM) + NORM_EPS) * g

    qn = head_norm(q_ref[0], qg_ref[...])
    kn = head_norm(k_ref[0], kg_ref[...])
    kmean = jnp.mean(kn.reshape(NB, BLK, LANES), axis=1)
    q_t = qn.T
    v_tb = v_ref[0].T.astype(BF16)

    blk_of_q = _iota2((1, S), 1) // BLK
    nidx = _iota2((NB, 1), 0)
    valid = nidx < blk_of_q
    q_blk = (blk_of_q * BLK).astype(F32)
    q_loc = (_iota2((1, S), 1) % BLK).astype(F32)
    k_blk = (_iota2((S, 1), 0) // BLK * BLK).astype(F32)
    k_loc = (_iota2((S, 1), 0) % BLK).astype(F32)
    q_lane = _iota2((LANES, 1), 0)
    causal = _iota2((BLK, BLK), 0) <= _iota2((BLK, BLK), 1)

    out_rows = []
    for h in range(2):
        hmask = (lane // HEAD_DIM) == h
        slope = slopes_ref[pair * 2 + h]
        spare = (1 - h) * HEAD_DIM
        k_aug = jnp.where(hmask, kn, 0.0)
        k_aug = jnp.where(lane == spare, slope * k_blk, k_aug)
        k_aug = jnp.where(lane == spare + 1, slope * k_loc, k_aug)
        k_aug = jnp.where((lane == spare + 2) | (lane == spare + 3), 1.0, k_aug)
        q_aug = q_t * scale
        q_aug = jnp.where((q_lane == spare) | (q_lane == spare + 1), 1.0, q_aug)
        q_aug = jnp.where(q_lane == spare + 2, -slope * q_blk, q_aug)
        q_aug = jnp.where(q_lane == spare + 3, -slope * q_loc, q_aug)
        q_hb = q_aug.astype(BF16)
        gate = _mm(jnp.where(hmask, kmean, 0.0), q_t, passes=3)
        gate = jnp.where(valid, gate, -jnp.inf)
        rank = jnp.zeros((NB, S), jnp.int32)
        for m in range(NB):
            gm = gate[m:m + 1, :]
            ahead = (gm > gate) | ((gm == gate) & (m < nidx))
            rank = rank + ahead.astype(jnp.int32)
        sel = valid & (rank < n_sel)
        k_hb = k_aug.astype(BF16)
        v_h = v_tb[h * HEAD_DIM:(h + 1) * HEAD_DIM, :]
        out_blocks = []
        for i in range(NB):
            qs = slice(i * BLK, (i + 1) * BLK)
            tiles = []
            m_run = None
            for n in range(i + 1):
                ks = slice(n * BLK, (n + 1) * BLK)
                s = _dot(k_hb[ks, :], q_hb[:, qs])
                if n < i:
                    s = jnp.where(sel[n:n + 1, qs], s, -jnp.inf)
                else:
                    s = jnp.where(causal, s, -jnp.inf)
                tiles.append(s)
                mx = jnp.max(s, axis=0, keepdims=True)
                m_run = mx if m_run is None else jnp.maximum(m_run, mx)
            l_run = jnp.zeros((1, BLK), F32)
            acc = jnp.zeros((HEAD_DIM, BLK), F32)
            for n in range(i + 1):
                ks = slice(n * BLK, (n + 1) * BLK)
                pt = jnp.exp(tiles[n] - m_run)
                l_run = l_run + jnp.sum(pt, axis=0, keepdims=True)
                acc = acc + _dot(v_h[:, ks], pt.astype(BF16))
            out_blocks.append(acc / l_run)
        out_rows.append(jnp.concatenate(out_blocks, axis=1))
    o_ref[0] = jnp.concatenate(out_rows, axis=0).T


def _moba(p_moba, q_norm_g, k_norm_g, B, S):
    pairs = MOBA_HEADS // 2
    col = lambda off: pl.BlockSpec((1, S, LANES), lambda b, p: (b, 0, off + p))
    gain = pl.BlockSpec((1, LANES), lambda b, p: (0, 0))
    tile2 = lambda g: jnp.concatenate([g, g]).reshape(1, LANES)
    p3 = p_moba.reshape(B, S, MOBA_PROJ)
    slopes = jnp.exp2(-8.0 * (jnp.arange(MOBA_HEADS, dtype=F32) + 1.0) / MOBA_HEADS)
    return pl.pallas_call(
        functools.partial(_moba_kernel, S=S),
        grid=(B, pairs),
        in_specs=[pl.BlockSpec(memory_space=pltpu.SMEM), col(0), col(pairs), col(2 * pairs),
                  gain, gain],
        out_specs=pl.BlockSpec((1, S, LANES), lambda b, p: (b, 0, p)),
        out_shape=jax.ShapeDtypeStruct((B, S, MOBA_WIDTH), F32),
        compiler_params=_params("arbitrary", "arbitrary"),
        name="moba",
    )(slopes, p3, p3, p3, tile2(q_norm_g), tile2(k_norm_g))


def _out_proj_kernel(yr_ref, ym_ref, x_ref, gate_ref, g_ref, scale_ref, shift_ref, w_ref,
                     wr_ref, br_ref, x1_ref, h2_ref, idx_ref, wgt_ref):
    W = RWKV_WIDTH
    mix = (_dot(yr_ref[...].astype(BF16), w_ref[0:W, :])
           + _dot(ym_ref[...].astype(BF16), w_ref[W:, :]))
    x1 = x_ref[...] + gate_ref[0] * mix
    x1_ref[...] = x1
    h2 = _rms_modulate(x1, g_ref[...], scale_ref[0], shift_ref[0])
    h2_ref[:, 0, :] = h2
    logits_t = (_mm(h2, wr_ref[...], passes=3) + br_ref[...]).T[:N_EXPERTS, :]
    tm = logits_t.shape[1]
    eidx = _iota2((N_EXPERTS, 1), 0)
    vals, idxs = [], []
    for _ in range(TOP_K):
        m = jnp.max(logits_t, axis=0, keepdims=True)
        idx = jnp.min(jnp.where(logits_t == m, eidx, N_EXPERTS), axis=0, keepdims=True)
        vals.append(m)
        idxs.append(idx)
        logits_t = jnp.where(eidx == idx, -jnp.inf, logits_t)
    idx_ref[...] = jnp.concatenate(idxs, axis=0)
    e = [jnp.exp(v - vals[0]) for v in vals]
    total = e[0] + e[1] + e[2] + e[3]
    wgt_t = jnp.concatenate([x / total for x in e] + [jnp.zeros((LANES - TOP_K, tm), F32)], axis=0)
    wgt_ref[...] = wgt_t.T


def _out_proj(y_rwkv, y_moba, x2, gate1, norm_g, scale, shift, w_out_b, w_router, b_router, S):
    T, D = x2.shape
    tm = 256
    per_b = S // tm
    rows = lambda n: pl.BlockSpec((tm, n), lambda i: (i, 0))
    mod = pl.BlockSpec((1, 1, D), lambda i: (i // per_b, 0, 0))
    full = lambda m, n: pl.BlockSpec((m, n), lambda i: (0, 0))
    wr = _pad_cols(w_router, LANES)
    br = jnp.concatenate([b_router, jnp.full((LANES - N_EXPERTS,), -jnp.inf, F32)]).reshape(1, LANES)
    return pl.pallas_call(
        _out_proj_kernel,
        grid=(T // tm,),
        in_specs=[rows(RWKV_WIDTH), rows(MOBA_WIDTH), rows(D), mod, full(1, D), mod, mod,
                  full(D, D), full(D, LANES), full(1, LANES)],
        out_specs=[rows(D), pl.BlockSpec((tm, 1, D), lambda i: (i, 0, 0)),
                   pl.BlockSpec((TOP_K, tm), lambda i: (0, i)), rows(LANES)],
        out_shape=[jax.ShapeDtypeStruct((T, D), F32), jax.ShapeDtypeStruct((T, 1, D), F32),
                   jax.ShapeDtypeStruct((TOP_K, T), jnp.int32), jax.ShapeDtypeStruct((T, LANES), F32)],
        compiler_params=_params("arbitrary"),
        name="out_proj",
    )(y_rwkv, y_moba, x2, gate1, norm_g.reshape(1, D), scale, shift, w_out_b, wr, br)


def _experts_kernel(meta_ref, src_first_ref, src_a_ref, src_b_ref, dst_a_ref, dst_b_ref, h_hbm,
                    wgu_a_ref, bgu_a_ref, wd_a_ref, bd_a_ref, wgu_b_ref, bgu_b_ref, wd_b_ref, bd_b_ref,
                    y_hbm, xbuf0, xbuf1, obuf0, obuf1, xb16, gsem, ssem, *, n_slots):
    g = pl.program_id(0)
    n_used = meta_ref[0]
    tm = EXPERT_TILE
    n_sub = EXPERT_SUBSTEPS
    cols_sub = D_FF // n_sub
    n_dump = y_hbm.shape[0] - n_slots

    def gather_row(src_ref, r, dst_buf, sem, priority=0):
        pltpu.make_async_copy(h_hbm.at[src_ref[0, 0, r]], dst_buf.at[pl.ds(r, 1)], sem).start(
            priority=priority)

    def scatter_row(dst_ref, r, src_buf, sem, priority=0):
        pltpu.make_async_copy(src_buf.at[pl.ds(r, 1)], y_hbm.at[dst_ref[0, 0, r]], sem).start(
            priority=priority)

    def wait_tile_gather(buf, sem):
        pltpu.make_async_copy(h_hbm.at[pl.ds(0, tm), 0], buf, sem).wait()

    def wait_tile_scatter(buf, sem):
        pltpu.make_async_copy(buf, y_hbm.at[pl.ds(0, tm), 0], sem).wait()

    @pl.when(g == 0)
    def _():
        def body(r, carry):
            gather_row(src_first_ref, r, xbuf0, gsem.at[0])
            return carry
        lax.fori_loop(0, tm, body, 0)
        obuf1[...] = jnp.zeros_like(obuf1)
        for d in range(n_dump // tm):
            cp = pltpu.make_async_copy(obuf1, y_hbm.at[pl.ds(n_slots + d * tm, tm), 0], ssem.at[0])
            cp.start()
            cp.wait()

    def phase(ph, src_next_ref, dst_prev_ref, wgu_ref, bgu_ref, wd_ref, bd_ref):
        j = 2 * g + ph
        x_cur, x_nxt = (xbuf0, xbuf1) if ph == 0 else (xbuf1, xbuf0)
        o_cur, o_prv = (obuf0, obuf1) if ph == 0 else (obuf1, obuf0)

        def ffn_chunk(n):
            cg = slice(n * cols_sub, (n + 1) * cols_sub)
            cu = slice(D_FF + n * cols_sub, D_FF + (n + 1) * cols_sub)
            xb = xb16[...]
            gate = jnp.minimum(_dot(xb, wgu_ref[0, :, cg]) + bgu_ref[0, :, cg], SWIGLU_LIMIT)
            up = jnp.clip(_dot(xb, wgu_ref[0, :, cu]) + bgu_ref[0, :, cu], -SWIGLU_LIMIT, SWIGLU_LIMIT)
            act = (up + 1.0) * gate * _sigmoid(SWIGLU_ALPHA * gate)
            o_cur[...] += _dot(act.astype(BF16), wd_ref[0, cg, :])

        @pl.when(j < n_used)
        def _():
            wait_tile_gather(x_cur, gsem.at[ph])

            @pl.when(j >= 1)
            def _():
                wait_tile_scatter(o_cur, ssem.at[ph])

            for r in range(tm):
                gather_row(src_next_ref, r, x_nxt, gsem.at[1 - ph], priority=r % 2)
                scatter_row(dst_prev_ref, r, o_prv, ssem.at[1 - ph], priority=r % 2)
            xb16[...] = x_cur[...].astype(BF16)
            o_cur[...] = jnp.broadcast_to(bd_ref[0], (tm, D_MODEL))

        @pl.when(j + 1 <= n_used)
        def _():
            for n in range(n_sub):
                ffn_chunk(n)

        @pl.when(j == n_used)
        def _():
            def body(r, carry):
                scatter_row(dst_prev_ref, r, o_prv, ssem.at[1 - ph])
                return carry
            lax.fori_loop(0, tm, body, 0)
            wait_tile_scatter(o_prv, ssem.at[1 - ph])
            wait_tile_scatter(o_cur, ssem.at[ph])
            wait_tile_gather(x_cur, gsem.at[ph])

    phase(0, src_a_ref, dst_a_ref, wgu_a_ref, bgu_a_ref, wd_a_ref, bd_a_ref)
    phase(1, src_b_ref, dst_b_ref, wgu_b_ref, bgu_b_ref, wd_b_ref, bd_b_ref)


def _experts(h2, src_tok, dst_row, tile_expert, n_used, wgu_b, bgu, wd_b, bd, n_slots, n_dump):
    D = h2.shape[-1]
    tm = EXPERT_TILE
    n_tiles = tile_expert.shape[0]
    src3 = src_tok.reshape(n_tiles + 2, 1, tm)
    dst3 = dst_row.reshape(n_tiles + 2, 1, tm)
    meta = jnp.concatenate([n_used.reshape(1), tile_expert]).astype(jnp.int32)
    smem_row = lambda f: pl.BlockSpec((1, 1, tm), f, memory_space=pltpu.SMEM)
    e_of = lambda j, m: m[1 + jnp.minimum(j, n_tiles - 1)]
    weights = lambda ph: [
        pl.BlockSpec((1, D, 2 * D_FF), lambda g, m: (e_of(2 * g + ph, m), 0, 0)),
        pl.BlockSpec((1, 1, 2 * D_FF), lambda g, m: (e_of(2 * g + ph, m), 0, 0)),
        pl.BlockSpec((1, D_FF, D), lambda g, m: (e_of(2 * g + ph, m), 0, 0)),
        pl.BlockSpec((1, 1, D), lambda g, m: (e_of(2 * g + ph, m), 0, 0))]
    grid_spec = pltpu.PrefetchScalarGridSpec(
        num_scalar_prefetch=1,
        grid=((n_tiles + 2) // 2,),
        in_specs=[smem_row(lambda g, m: (0, 0, 0)),
                  smem_row(lambda g, m: (2 * g + 1, 0, 0)),
                  smem_row(lambda g, m: (jnp.minimum(2 * g + 2, n_tiles + 1), 0, 0)),
                  smem_row(lambda g, m: (2 * g, 0, 0)),
                  smem_row(lambda g, m: (2 * g + 1, 0, 0)),
                  pl.BlockSpec(memory_space=pl.ANY)] + weights(0) + weights(1),
        out_specs=pl.BlockSpec(memory_space=pl.ANY),
        scratch_shapes=[pltpu.VMEM((tm, D), F32), pltpu.VMEM((tm, D), F32),
                        pltpu.VMEM((tm, D), F32), pltpu.VMEM((tm, D), F32),
                        pltpu.VMEM((tm, D), BF16),
                        pltpu.SemaphoreType.DMA((2,)), pltpu.SemaphoreType.DMA((2,))],
    )
    bgu3 = bgu.reshape(N_EXPERTS, 1, 2 * D_FF)
    bd3 = bd.reshape(N_EXPERTS, 1, D)
    return pl.pallas_call(
        functools.partial(_experts_kernel, n_slots=n_slots),
        grid_spec=grid_spec,
        out_shape=jax.ShapeDtypeStruct((n_slots + n_dump, 1, D), F32),
        compiler_params=_params("arbitrary"),
        name="experts",
    )(meta, src3, src3, src3, dst3, dst3, h2, wgu_b, bgu3, wd_b, bd3, wgu_b, bgu3, wd_b, bd3)


def _route(top_idx_t):
    T = top_idx_t.shape[1]
    tm = EXPERT_TILE
    M = T * TOP_K
    slot_expert = top_idx_t.reshape(M)
    order = jnp.argsort(slot_expert).astype(jnp.int32)
    counts = jnp.bincount(slot_expert, length=N_EXPERTS)
    padded = (counts + tm - 1) // tm * tm
    pad_end = jnp.cumsum(padded)
    pad_start = pad_end - padded
    start = jnp.cumsum(counts) - counts
    n_tiles = M // tm + N_EXPERTS
    tile_start = jnp.arange(n_tiles) * tm
    tile_expert = jnp.minimum(jnp.sum(pad_end[None, :] <= tile_start[:, None], axis=1), N_EXPERTS - 1)
    tile_valid = jnp.clip(counts[tile_expert] - (tile_start - pad_start[tile_expert]), 0, tm)
    n_used = pad_end[-1] // tm
    r = jnp.arange(tm)[None, :]
    sorted_pos = (start[tile_expert] + tile_start - pad_start[tile_expert])[:, None] + r
    slot = order[jnp.clip(sorted_pos, 0, M - 1)]
    is_real = r < tile_valid[:, None]
    src_tok = jnp.where(is_real, slot % T, 0).astype(jnp.int32)
    dump = M + tm + tile_expert[:, None] * tm + (r - tile_valid[:, None])
    dst_row = jnp.where(is_real, slot, dump).astype(jnp.int32)
    lead = (M + r).astype(jnp.int32)
    zeros = jnp.zeros((1, tm), jnp.int32)
    src_ext = jnp.concatenate([src_tok, zeros, zeros], axis=0)
    dst_ext = jnp.concatenate([lead, dst_row, zeros], axis=0)
    n_dump = (N_EXPERTS + 1) * tm
    return src_ext, dst_ext, tile_expert.astype(jnp.int32), n_used.astype(jnp.int32), n_dump


def _combine_kernel(x1_ref, y0_ref, y1_ref, y2_ref, y3_ref, w_ref, gate_ref, o_ref):
    w = w_ref[...]
    acc = w[:, 0:1] * y0_ref[:, 0, :]
    for j, y_ref in enumerate((y1_ref, y2_ref, y3_ref), start=1):
        acc = acc + w[:, j:j + 1] * y_ref[:, 0, :]
    o_ref[...] = x1_ref[...] + gate_ref[0] * acc


def _combine(x1, y_slots, weights, gate2, S):
    T, D = x1.shape
    tm = 256
    per_b = S // tm
    n_t = T // tm
    plane = lambda k: pl.BlockSpec((tm, 1, D), lambda i: (k * n_t + i, 0, 0))
    return pl.pallas_call(
        _combine_kernel,
        grid=(n_t,),
        in_specs=[pl.BlockSpec((tm, D), lambda i: (i, 0))] + [plane(k) for k in range(TOP_K)]
                 + [pl.BlockSpec((tm, LANES), lambda i: (i, 0)),
                    pl.BlockSpec((1, 1, D), lambda i: (i // per_b, 0, 0))],
        out_specs=pl.BlockSpec((tm, D), lambda i: (i, 0)),
        out_shape=jax.ShapeDtypeStruct((T, D), F32),
        compiler_params=_params("arbitrary"),
        name="combine",
    )(x1, y_slots, y_slots, y_slots, y_slots, weights, gate2)


def _pad_cols(w, n):
    return jnp.pad(w, ((0, 0), (0, n - w.shape[1])))


def _pad_rows(w, n):
    return jnp.pad(w, ((0, n - w.shape[0]), (0, 0)))


def _layer(x, c, w_ada, b_ada, norm1_g, w_in, rwkv_mu, rwkv_w0, rwkv_w_up, rwkv_a0, rwkv_a_up,
           rwkv_g_up, rwkv_k_k, rwkv_k_a, rwkv_r_k, rwkv_ln_g, rwkv_ln_b, q_norm_g, k_norm_g,
           w_out, norm2_g, w_router, b_router, w_gate_up, b_gate_up, w_down, b_down):
    B, S, D = x.shape
    T = B * S
    W = RWKV_WIDTH
    x2 = x.reshape(T, D)

    mods = _ada(c, w_ada, b_ada)
    shift1, scale1, gate1, shift2, scale2, gate2 = [
        mods[:, j * D:(j + 1) * D].reshape(B, 1, D) for j in range(6)]

    pieces = [(w_in[:, :XW_OFF], XW_OFF),
              (w_in[:, XW_OFF:XW_OFF + DECAY_LORA], LANES),
              (w_in[:, XW_OFF + DECAY_LORA:XW_OFF + DECAY_LORA + AAA_LORA], LANES),
              (w_in[:, XW_OFF + DECAY_LORA + AAA_LORA:RWKV_PROJ], 2 * LANES),
              (w_in[:, RWKV_PROJ:], MOBA_PROJ)]
    w_in_b = jnp.concatenate([_pad_cols(w, n) for w, n in pieces], axis=1).astype(BF16)
    mu_pieces = [(rwkv_mu[None, :XW_OFF], XW_OFF),
                 (rwkv_mu[None, XW_OFF:XW_OFF + DECAY_LORA], LANES),
                 (rwkv_mu[None, XW_OFF + DECAY_LORA:XW_OFF + DECAY_LORA + AAA_LORA], LANES),
                 (rwkv_mu[None, XW_OFF + DECAY_LORA + AAA_LORA:], 2 * LANES)]
    mu = jnp.concatenate([_pad_cols(m, n) for m, n in mu_pieces], axis=1)

    p_rwkv, p_moba = _in_proj(x2, norm1_g, scale1, shift1, w_in_b, S)

    row = lambda a: a.reshape(1, W)
    prm = (mu, row(rwkv_w0), row(rwkv_a0), row(rwkv_k_k), row(rwkv_k_a), row(rwkv_r_k),
           row(rwkv_ln_g), row(rwkv_ln_b), _pad_rows(rwkv_w_up, LANES), _pad_rows(rwkv_a_up, LANES),
           _pad_rows(rwkv_g_up, 2 * LANES))
    y_rwkv = _rwkv(p_rwkv, prm, B, S)
    y_moba = _moba(p_moba, q_norm_g, k_norm_g, B, S)

    x1, h2, top_idx_t, weights = _out_proj(y_rwkv.reshape(T, W), y_moba.reshape(T, MOBA_WIDTH), x2,
                                           gate1, norm2_g, scale2, shift2, w_out.astype(BF16),
                                           w_router, b_router, S)

    src_tok, dst_row, tile_expert, n_used, n_dump = _route(top_idx_t)
    y_slots = _experts(h2, src_tok, dst_row, tile_expert, n_used, w_gate_up.astype(BF16), b_gate_up,
                       w_down.astype(BF16), b_down, T * TOP_K, n_dump)
    out = _combine(x1, y_slots, weights, gate2, S)
    return out.reshape(B, S, D)


def kernel(x, c, w_ada, b_ada, norm1_g, w_in, rwkv_mu, rwkv_w0, rwkv_w_up, rwkv_a0, rwkv_a_up, rwkv_g_up, rwkv_k_k, rwkv_k_a, rwkv_r_k, rwkv_ln_g, rwkv_ln_b, q_norm_g, k_norm_g, w_out, norm2_g, w_router, b_router, w_gate_up, b_gate_up, w_down, b_down):
    for l in range(w_ada.shape[0]):
        x = _layer(x, c, w_ada[l], b_ada[l], norm1_g[l], w_in[l], rwkv_mu[l], rwkv_w0[l],
                   rwkv_w_up[l], rwkv_a0[l], rwkv_a_up[l], rwkv_g_up[l], rwkv_k_k[l], rwkv_k_a[l],
                   rwkv_r_k[l], rwkv_ln_g[l], rwkv_ln_b[l], q_norm_g[l], k_norm_g[l], w_out[l],
                   norm2_g[l], w_router[l], b_router[l], w_gate_up[l], b_gate_up[l], w_down[l],
                   b_down[l])
    return x
```

```python
import functools

import jax
import jax.numpy as jnp
from jax import lax
from jax.experimental import pallas as pl
from jax.experimental.pallas import tpu as pltpu

F32 = jnp.float32
BF16 = jnp.bfloat16

D_MODEL = 1024
HEAD_DIM = 64
RWKV_WIDTH = 512
MOBA_WIDTH = 512
RWKV_HEADS = RWKV_WIDTH // HEAD_DIM
MOBA_HEADS = MOBA_WIDTH // HEAD_DIM
DECAY_LORA = 64
AAA_LORA = 64
GATE_LORA = 160
RWKV_LN_EPS = 64e-5
RWKV_PROJ = 3 * RWKV_WIDTH + DECAY_LORA + AAA_LORA + GATE_LORA
MOBA_PROJ = 3 * MOBA_WIDTH
MOBA_BLOCK = 256
MOBA_TOPK = 3
N_EXPERTS = 32
TOP_K = 4
D_FF = D_MODEL
SWIGLU_LIMIT = 7.0
SWIGLU_ALPHA = 1.702
NORM_EPS = 1e-6

LANES = 128
SUBLANES = 8
XW_OFF = 3 * RWKV_WIDTH
XA_OFF = XW_OFF + LANES
XG_OFF = XA_OFF + LANES
RWKV_COLS = XG_OFF + 2 * LANES
CHUNK = 64
EXPERT_TILE = 512
EXPERT_SUBSTEPS = 4
VMEM_LIMIT = 56 * 1024 * 1024

NN = (((1,), (0,)), ((), ()))
NT = (((1,), (1,)), ((), ()))
TN = (((0,), (0,)), ((), ()))


def _dot(a, b, dims=NN):
    return lax.dot_general(a, b, dims, preferred_element_type=F32)


def _split(a):
    hi = a.astype(BF16)
    lo = (a - hi.astype(F32)).astype(BF16)
    return hi, lo


def _mm(a, b, dims=NN, passes=1):
    if passes == 1:
        return _dot(a.astype(BF16), b.astype(BF16), dims)
    a_hi, a_lo = _split(a)
    b_hi, b_lo = _split(b)
    return _dot(a_hi, b_hi, dims) + (_dot(a_hi, b_lo, dims) + _dot(a_lo, b_hi, dims))


def _split3(a):
    hi = a.astype(BF16)
    r = a - hi.astype(F32)
    mid = r.astype(BF16)
    lo = (r - mid.astype(F32)).astype(BF16)
    return hi, mid, lo


def _mm_exact_rhs(a, b_bf16, dims=NN):
    hi, mid, lo = _split3(a)
    return _dot(hi, b_bf16, dims) + (_dot(mid, b_bf16, dims) + _dot(lo, b_bf16, dims))


def _mm_exact_lhs(a_bf16, b, dims=NN):
    hi, mid, lo = _split3(b)
    return _dot(a_bf16, hi, dims) + (_dot(a_bf16, mid, dims) + _dot(a_bf16, lo, dims))


def _iota2(shape, dim):
    return lax.broadcasted_iota(jnp.int32, shape, dim)


def _group_ones(n, group):
    return (_iota2((n, n), 0) // group == _iota2((n, n), 1) // group).astype(BF16)


def _sigmoid(x):
    return 1.0 / (1.0 + jnp.exp(-x))


def _params(*sem):
    return pltpu.CompilerParams(dimension_semantics=sem, vmem_limit_bytes=VMEM_LIMIT)


def _ada_kernel(c_ref, w_ref, b_ref, o_ref):
    c = c_ref[...]
    o_ref[...] = _mm(c * _sigmoid(c), w_ref[...], passes=3) + b_ref[...]


def _ada(c, w_ada, b_ada):
    B, D = c.shape
    n_out = w_ada.shape[1]
    tn = 1024
    return pl.pallas_call(
        _ada_kernel,
        grid=(n_out // tn,),
        in_specs=[pl.BlockSpec((B, D), lambda j: (0, 0)),
                  pl.BlockSpec((D, tn), lambda j: (0, j)),
                  pl.BlockSpec((1, tn), lambda j: (0, j))],
        out_specs=pl.BlockSpec((B, tn), lambda j: (0, j)),
        out_shape=jax.ShapeDtypeStruct((B, n_out), F32),
        compiler_params=_params("arbitrary"),
        name="ada",
    )(c, w_ada, b_ada.reshape(1, n_out))


def _rms_modulate(x, g, scale, shift):
    y = x * lax.rsqrt(jnp.mean(x * x, axis=-1, keepdims=True) + NORM_EPS)
    return (y * g) * (1.0 + scale) + shift


def _in_proj_kernel(x_ref, g_ref, scale_ref, shift_ref, w_ref, pr_ref, pm_ref):
    h = _rms_modulate(x_ref[...], g_ref[...], scale_ref[0], shift_ref[0])
    proj = _dot(h.astype(BF16), w_ref[...])
    pr_ref[...] = proj[:, :RWKV_COLS]
    pm_ref[...] = proj[:, RWKV_COLS:]


def _in_proj(x2, norm_g, scale, shift, w_in_b, S):
    T, D = x2.shape
    tm = 256
    per_b = S // tm
    n_cols = w_in_b.shape[1]
    return pl.pallas_call(
        _in_proj_kernel,
        grid=(T // tm,),
        in_specs=[pl.BlockSpec((tm, D), lambda i: (i, 0)),
                  pl.BlockSpec((1, D), lambda i: (0, 0)),
                  pl.BlockSpec((1, 1, D), lambda i: (i // per_b, 0, 0)),
                  pl.BlockSpec((1, 1, D), lambda i: (i // per_b, 0, 0)),
                  pl.BlockSpec((D, n_cols), lambda i: (0, 0))],
        out_specs=[pl.BlockSpec((tm, RWKV_COLS), lambda i: (i, 0)),
                   pl.BlockSpec((tm, MOBA_PROJ), lambda i: (i, 0))],
        out_shape=[jax.ShapeDtypeStruct((T, RWKV_COLS), F32),
                   jax.ShapeDtypeStruct((T, MOBA_PROJ), F32)],
        compiler_params=_params("arbitrary"),
        name="in_proj",
    )(x2, norm_g.reshape(1, D), scale, shift, w_in_b)


RWKV_TILE = 256


def _unit_lower_inverses(Ls):
    n = Ls[0].shape[0]
    r = _iota2((n, n), 0)
    c = _iota2((n, n), 1)
    eye = (r == c).astype(F32)
    in_block = r // 8 == c // 8
    b16 = lambda xs: [x.astype(BF16) for x in xs]
    Ld = [jnp.where(in_block, L, 0.0) for L in Ls]
    Ld_b = b16(Ld)
    Nb_b = [(L - d).astype(BF16) for L, d in zip(Ls, Ld)]
    L2 = [_dot(d, d) for d in Ld_b]
    L2_b = b16(L2)
    L4_b = b16([_dot(x, x) for x in L2_b])
    T0 = [eye + d + l2 + _dot(db, l2b) for d, l2, db, l2b in zip(Ld, L2, Ld_b, L2_b)]
    T0 = [t + _dot(t.astype(BF16), l4b) for t, l4b in zip(T0, L4_b)]
    T0_b = b16(T0)
    M1_b = b16([_dot(t, nb) for t, nb in zip(T0_b, Nb_b)])
    M2_b = b16([_dot(m, m) for m in M1_b])
    M4_b = b16([_dot(m, m) for m in M2_b])
    X = [t + _dot(m4, tb) for t, m4, tb in zip(T0, M4_b, T0_b)]
    X = [x + _dot(m2, x.astype(BF16)) for x, m2 in zip(X, M2_b)]
    return [x + _dot(m1, x.astype(BF16)) for x, m1 in zip(X, M1_b)]


def _rwkv_kernel(p_ref, mu_ref, w0_ref, a0_ref, kk_ref, ka_ref, rk_ref, lng_ref, lnb_ref,
                 wup_ref, aup_ref, gup_ref, y_ref, carry_ref, state_ref, *, ts):
    t = pl.program_id(1)
    W = RWKV_WIDTH

    @pl.when(t == 0)
    def _():
        carry_ref[...] = jnp.zeros_like(carry_ref)
        state_ref[...] = jnp.zeros_like(state_ref)

    p = p_ref[0]
    row = _iota2((ts, 1), 0)
    prev = jnp.where(row == 0, carry_ref[...], pltpu.roll(p, 1, 0))
    carry_ref[...] = p[ts - 1:ts, :]
    pm = p + (prev - p) * mu_ref[...]
    r = pm[:, 0:W]
    k = pm[:, W:2 * W]
    v = pm[:, 2 * W:3 * W]
    xw = pm[:, XW_OFF:XA_OFF]
    xa = pm[:, XA_OFF:XG_OFF]
    xg = pm[:, XG_OFF:RWKV_COLS]

    z = -(w0_ref[...] + _mm(jnp.tanh(xw), wup_ref[...], passes=3))
    softplus = jnp.maximum(z, 0.0) + jnp.log(1.0 + jnp.exp(-jnp.abs(z)))
    logd = -jnp.exp(-softplus - 0.5)
    alpha = _sigmoid(a0_ref[...] + _mm(xa, aup_ref[...], passes=3))
    gate = _mm(_sigmoid(xg), gup_ref[...])

    head_ones = _group_ones(W, HEAD_DIM)
    kk = k * kk_ref[...]
    kk_norm = jnp.sqrt(_mm_exact_rhs(kk * kk, head_ones))
    kk = kk / jnp.maximum(kk_norm, 1e-12)
    kmod = k * (1.0 + (alpha - 1.0) * ka_ref[...])
    bonus = _mm_exact_rhs(r * kmod * rk_ref[...], head_ones) * v

    tr = _iota2((ts, ts), 0)
    tc = _iota2((ts, ts), 1)
    cum = ((tr // CHUNK == tc // CHUNK) & (tc <= tr)).astype(BF16)
    logp = _mm_exact_lhs(cum, logd)
    inv_p = jnp.exp(-logp)
    a_t = -kk * jnp.exp(logp - logd)
    b_t = kk * alpha * inv_p
    k_t = kmod * inv_p
    r_t = r * jnp.exp(logp)

    n2 = 2 * CHUNK
    n_chunks = ts // CHUNK
    n_pairs = RWKV_HEADS // 2
    inst = [(ci, pi) for ci in range(n_chunks) for pi in range(n_pairs)]
    lane = _iota2((1, LANES), 1)
    m0 = lane < HEAD_DIM
    sr = _iota2((n2, n2), 0)
    sc = _iota2((n2, n2), 1)
    strict_lower = sc < sr
    incl_lower = sc <= sr

    def stacked(x, ci, pi):
        xt = x[ci * CHUNK:(ci + 1) * CHUNK, pi * LANES:(pi + 1) * LANES]
        return jnp.concatenate([jnp.where(m0, xt, 0.0), jnp.where(m0, 0.0, xt)], axis=0)

    pcs = [jnp.exp(logp[(ci + 1) * CHUNK - 1:(ci + 1) * CHUNK, pi * LANES:(pi + 1) * LANES])
           for ci, pi in inst]
    r_s = [stacked(r_t, ci, pi) for ci, pi in inst]
    a_b = [stacked(a_t, ci, pi).astype(BF16) for ci, pi in inst]
    r_b = [x.astype(BF16) for x in r_s]
    b_s = [stacked(b_t, ci, pi) for ci, pi in inst]
    k_s = [stacked(k_t, ci, pi) for ci, pi in inst]
    v_b = [stacked(v, ci, pi).astype(BF16) for ci, pi in inst]
    ar_b = [jnp.concatenate([a, rr], axis=0) for a, rr in zip(a_b, r_b)]
    bk_b = [jnp.concatenate([b.astype(BF16), kx.astype(BF16)], axis=0) for b, kx in zip(b_s, k_s)]
    gram = [_dot(ar, bk, NT) for ar, bk in zip(ar_b, bk_b)]
    l_ab = [jnp.where(strict_lower, g[:n2, :n2], 0.0) for g in gram]
    l_ak_b = [jnp.where(strict_lower, g[:n2, n2:], 0.0).astype(BF16) for g in gram]
    m_rb_b = [jnp.where(incl_lower, g[n2:, :n2], 0.0).astype(BF16) for g in gram]
    m_rk_b = [jnp.where(incl_lower, g[n2:, n2:], 0.0).astype(BF16) for g in gram]
    t_b = [x.astype(BF16) for x in _unit_lower_inverses(l_ab)]
    lakv_b = [_dot(l, vv).astype(BF16) for l, vv in zip(l_ak_b, v_b)]
    wu_b = [_dot(tb, jnp.concatenate([a, lv], axis=1)).astype(BF16)
            for tb, a, lv in zip(t_b, a_b, lakv_b)]
    mwu = [_dot(m, wu) for m, wu in zip(m_rb_b, wu_b)]
    q_b = [(rs + x[:, :LANES]).astype(BF16) for rs, x in zip(r_s, mwu)]
    y0 = [x[:, LANES:] + _dot(m, vv) for x, m, vv in zip(mwu, m_rk_b, v_b)]
    bp_b = [(b * pc).astype(BF16) for b, pc in zip(b_s, pcs)]
    kp_b = [(kx * pc).astype(BF16) for kx, pc in zip(k_s, pcs)]
    gh = [_dot(wu, bp, TN) for wu, bp in zip(wu_b, bp_b)]
    g_b = [x[:LANES].astype(BF16) for x in gh]
    h = [x[LANES:] + _dot(vv, kp, TN) for x, vv, kp in zip(gh, v_b, kp_b)]

    y_chunks = []
    for ci in range(n_chunks):
        ids = [ci * n_pairs + pi for pi in range(n_pairs)]
        s0 = [state_ref[pi] for pi in range(n_pairs)]
        s0_b = [s.astype(BF16) for s in s0]
        y_s = [_dot(q_b[n], sb, NT) + y0[n] for n, sb in zip(ids, s0_b)]
        s1 = [s * pcs[n] + _dot(sb, g_b[n]) + h[n] for n, s, sb in zip(ids, s0, s0_b)]
        for pi in range(n_pairs):
            state_ref[pi] = s1[pi]
        y_chunks.append(jnp.concatenate([x[:CHUNK] + x[CHUNK:] for x in y_s], axis=1))
    y = jnp.concatenate(y_chunks, axis=0) if n_chunks > 1 else y_chunks[0]

    mean = _mm_exact_rhs(y, head_ones) * (1.0 / HEAD_DIM)
    yc = y - mean
    var = _mm_exact_rhs(yc * yc, head_ones) * (1.0 / HEAD_DIM)
    yn = yc * lax.rsqrt(var + RWKV_LN_EPS) * lng_ref[...] + lnb_ref[...]
    y_ref[0] = (yn + bonus) * gate


def _rwkv(p_rwkv, prm, B, S, ts=RWKV_TILE):
    W = RWKV_WIDTH
    vec = lambda n: pl.BlockSpec((1, n), lambda b, t: (0, 0))
    mat = lambda m, n: pl.BlockSpec((m, n), lambda b, t: (0, 0))
    return pl.pallas_call(
        functools.partial(_rwkv_kernel, ts=ts),
        grid=(B, S // ts),
        in_specs=[pl.BlockSpec((1, ts, RWKV_COLS), lambda b, t: (b, t, 0)),
                  vec(RWKV_COLS)] + [vec(W)] * 7 + [mat(LANES, W), mat(LANES, W), mat(2 * LANES, W)],
        out_specs=pl.BlockSpec((1, ts, W), lambda b, t: (b, t, 0)),
        out_shape=jax.ShapeDtypeStruct((B, S, W), F32),
        scratch_shapes=[pltpu.VMEM((1, RWKV_COLS), F32),
                        pltpu.VMEM((RWKV_HEADS // 2, LANES, LANES), F32)],
        compiler_params=_params("arbitrary", "arbitrary"),
        name="rwkv",
    )(p_rwkv.reshape(B, S, RWKV_COLS), *prm)


def _moba_kernel(slopes_ref, q_ref, k_ref, v_ref, qg_ref, kg_ref, o_ref, *, S):
    pair = pl.program_id(1)
    NB = S // MOBA_BLOCK
    BLK = MOBA_BLOCK
    n_sel = min(MOBA_TOPK, NB)
    scale = HEAD_DIM ** -0.5
    head_ones = _group_ones(LANES, HEAD_DIM)
    lane = _iota2((1, LANES), 1)

    def head_norm(x, g):
        ss = _mm_exact_rhs(x * x, head_ones)
        return x * lax.rsqrt(ss * (1.0 / HEAD_DIM) + NORM_EPS) * g

    qn = head_norm(q_ref[0], qg_ref[...])
    kn = head_norm(k_ref[0], kg_ref[...])
    kmean = jnp.mean(kn.reshape(NB, BLK, LANES), axis=1)
    q_t = qn.T
    v_tb = v_ref[0].T.astype(BF16)

    blk_of_q = _iota2((1, S), 1) // BLK
    nidx = _iota2((NB, 1), 0)
    valid = nidx < blk_of_q
    q_blk = (blk_of_q * BLK).astype(F32)
    q_loc = (_iota2((1, S), 1) % BLK).astype(F32)
    k_blk = (_iota2((S, 1), 0) // BLK * BLK).astype(F32)
    k_loc = (_iota2((S, 1), 0) % BLK).astype(F32)
    q_lane = _iota2((LANES, 1), 0)
    causal = _iota2((BLK, BLK), 0) <= _iota2((BLK, BLK), 1)

    out_rows = []
    for h in range(2):
        hmask = (lane // HEAD_DIM) == h
        slope = slopes_ref[pair * 2 + h]
        spare = (1 - h) * HEAD_DIM
        k_aug = jnp.where(hmask, kn, 0.0)
        k_aug = jnp.where(lane == spare, slope * k_blk, k_aug)
        k_aug = jnp.where(lane == spare + 1, slope * k_loc, k_aug)
        k_aug = jnp.where((lane == spare + 2) | (lane == spare + 3), 1.0, k_aug)
        q_aug = q_t * scale
        q_aug = jnp.where((q_lane == spare) | (q_lane == spare + 1), 1.0, q_aug)
        q_aug = jnp.where(q_lane == spare + 2, -slope * q_blk, q_aug)
        q_aug = jnp.where(q_lane == spare + 3, -slope * q_loc, q_aug)
        q_hb = q_aug.astype(BF16)
        gate = _mm(jnp.where(hmask, kmean, 0.0), q_t, passes=3)
        gate = jnp.where(valid, gate, -jnp.inf)
        rank = jnp.zeros((NB, S), jnp.int32)
        for m in range(NB):
            gm = gate[m:m + 1, :]
            ahead = (gm > gate) | ((gm == gate) & (m < nidx))
            rank = rank + ahead.astype(jnp.int32)
        sel = valid & (rank < n_sel)
        k_hb = k_aug.astype(BF16)
        v_h = v_tb[h * HEAD_DIM:(h + 1) * HEAD_DIM, :]
        out_blocks = []
        for i in range(NB):
            qs = slice(i * BLK, (i + 1) * BLK)
            tiles = []
            m_run = None
            for n in range(i + 1):
                ks = slice(n * BLK, (n + 1) * BLK)
                s = _dot(k_hb[ks, :], q_hb[:, qs])
                if n < i:
                    s = jnp.where(sel[n:n + 1, qs], s, -jnp.inf)
                else:
                    s = jnp.where(causal, s, -jnp.inf)
                tiles.append(s)
                mx = jnp.max(s, axis=0, keepdims=True)
                m_run = mx if m_run is None else jnp.maximum(m_run, mx)
            l_run = jnp.zeros((1, BLK), F32)
            acc = jnp.zeros((HEAD_DIM, BLK), F32)
            for n in range(i + 1):
                ks = slice(n * BLK, (n + 1) * BLK)
                pt = jnp.exp(tiles[n] - m_run)
                l_run = l_run + jnp.sum(pt, axis=0, keepdims=True)
                acc = acc + _dot(v_h[:, ks], pt.astype(BF16))
            out_blocks.append(acc / l_run)
        out_rows.append(jnp.concatenate(out_blocks, axis=1))
    o_ref[0] = jnp.concatenate(out_rows, axis=0).T


def _moba(p_moba, q_norm_g, k_norm_g, B, S):
    pairs = MOBA_HEADS // 2
    col = lambda off: pl.BlockSpec((1, S, LANES), lambda b, p: (b, 0, off + p))
    gain = pl.BlockSpec((1, LANES), lambda b, p: (0, 0))
    tile2 = lambda g: jnp.concatenate([g, g]).reshape(1, LANES)
    p3 = p_moba.reshape(B, S, MOBA_PROJ)
    slopes = jnp.exp2(-8.0 * (jnp.arange(MOBA_HEADS, dtype=F32) + 1.0) / MOBA_HEADS)
    return pl.pallas_call(
        functools.partial(_moba_kernel, S=S),
        grid=(B, pairs),
        in_specs=[pl.BlockSpec(memory_space=pltpu.SMEM), col(0), col(pairs), col(2 * pairs),
                  gain, gain],
        out_specs=pl.BlockSpec((1, S, LANES), lambda b, p: (b, 0, p)),
        out_shape=jax.ShapeDtypeStruct((B, S, MOBA_WIDTH), F32),
        compiler_params=_params("arbitrary", "arbitrary"),
        name="moba",
    )(slopes, p3, p3, p3, tile2(q_norm_g), tile2(k_norm_g))


def _out_proj_kernel(yr_ref, ym_ref, x_ref, gate_ref, g_ref, scale_ref, shift_ref, w_ref,
                     wr_ref, br_ref, x1_ref, h2_ref, idx_ref, wgt_ref):
    W = RWKV_WIDTH
    mix = (_dot(yr_ref[...].astype(BF16), w_ref[0:W, :])
           + _dot(ym_ref[...].astype(BF16), w_ref[W:, :]))
    x1 = x_ref[...] + gate_ref[0] * mix
    x1_ref[...] = x1
    h2 = _rms_modulate(x1, g_ref[...], scale_ref[0], shift_ref[0])
    h2_ref[:, 0, :] = h2
    logits_t = (_mm(h2, wr_ref[...], passes=3) + br_ref[...]).T[:N_EXPERTS, :]
    tm = logits_t.shape[1]
    eidx = _iota2((N_EXPERTS, 1), 0)
    vals, idxs = [], []
    for _ in range(TOP_K):
        m = jnp.max(logits_t, axis=0, keepdims=True)
        idx = jnp.min(jnp.where(logits_t == m, eidx, N_EXPERTS), axis=0, keepdims=True)
        vals.append(m)
        idxs.append(idx)
        logits_t = jnp.where(eidx == idx, -jnp.inf, logits_t)
    idx_ref[...] = jnp.concatenate(idxs, axis=0)
    e = [jnp.exp(v - vals[0]) for v in vals]
    total = e[0] + e[1] + e[2] + e[3]
    wgt_t = jnp.concatenate([x / total for x in e] + [jnp.zeros((LANES - TOP_K, tm), F32)], axis=0)
    wgt_ref[...] = wgt_t.T


def _out_proj(y_rwkv, y_moba, x2, gate1, norm_g, scale, shift, w_out_b, w_router, b_router, S):
    T, D = x2.shape
    tm = 256
    per_b = S // tm
    rows = lambda n: pl.BlockSpec((tm, n), lambda i: (i, 0))
    mod = pl.BlockSpec((1, 1, D), lambda i: (i // per_b, 0, 0))
    full = lambda m, n: pl.BlockSpec((m, n), lambda i: (0, 0))
    wr = _pad_cols(w_router, LANES)
    br = jnp.concatenate([b_router, jnp.full((LANES - N_EXPERTS,), -jnp.inf, F32)]).reshape(1, LANES)
    return pl.pallas_call(
        _out_proj_kernel,
        grid=(T // tm,),
        in_specs=[rows(RWKV_WIDTH), rows(MOBA_WIDTH), rows(D), mod, full(1, D), mod, mod,
                  full(D, D), full(D, LANES), full(1, LANES)],
        out_specs=[rows(D), pl.BlockSpec((tm, 1, D), lambda i: (i, 0, 0)),
                   pl.BlockSpec((TOP_K, tm), lambda i: (0, i)), rows(LANES)],
        out_shape=[jax.ShapeDtypeStruct((T, D), F32), jax.ShapeDtypeStruct((T, 1, D), F32),
                   jax.ShapeDtypeStruct((TOP_K, T), jnp.int32), jax.ShapeDtypeStruct((T, LANES), F32)],
        compiler_params=_params("arbitrary"),
        name="out_proj",
    )(y_rwkv, y_moba, x2, gate1, norm_g.reshape(1, D), scale, shift, w_out_b, wr, br)


def _experts_kernel(meta_ref, src_first_ref, src_a_ref, src_b_ref, dst_a_ref, dst_b_ref, h_hbm,
                    wgu_a_ref, bgu_a_ref, wd_a_ref, bd_a_ref, wgu_b_ref, bgu_b_ref, wd_b_ref, bd_b_ref,
                    y_hbm, xbuf0, xbuf1, obuf0, obuf1, xb16, gsem, ssem, *, n_slots):
    g = pl.program_id(0)
    n_used = meta_ref[0]
    tm = EXPERT_TILE
    n_sub = EXPERT_SUBSTEPS
    cols_sub = D_FF // n_sub
    n_dump = y_hbm.shape[0] - n_slots

    def gather_row(src_ref, r, dst_buf, sem, priority=0):
        pltpu.make_async_copy(h_hbm.at[src_ref[0, 0, r]], dst_buf.at[r], sem).start(
            priority=priority)

    def scatter_row(dst_ref, r, src_buf, sem, priority=0):
        pltpu.make_async_copy(src_buf.at[r], y_hbm.at[dst_ref[0, 0, r]], sem).start(
            priority=priority)

    def wait_tile_gather(buf, sem):
        pltpu.make_async_copy(h_hbm.at[pl.ds(0, tm)], buf, sem).wait()

    def wait_tile_scatter(buf, sem):
        pltpu.make_async_copy(buf, y_hbm.at[pl.ds(0, tm)], sem).wait()

    @pl.when(g == 0)
    def _():
        def body(r, carry):
            gather_row(src_first_ref, r, xbuf0, gsem.at[0])
            return carry
        lax.fori_loop(0, tm, body, 0)
        obuf1[...] = jnp.zeros_like(obuf1)
        for d in range(n_dump // tm):
            cp = pltpu.make_async_copy(obuf1, y_hbm.at[pl.ds(n_slots + d * tm, tm)], ssem.at[0])
            cp.start()
            cp.wait()

    def phase(ph, src_next_ref, dst_prev_ref, wgu_ref, bgu_ref, wd_ref, bd_ref):
        j = 2 * g + ph
        x_cur, x_nxt = (xbuf0, xbuf1) if ph == 0 else (xbuf1, xbuf0)
        o_cur, o_prv = (obuf0, obuf1) if ph == 0 else (obuf1, obuf0)

        def ffn_chunk(n):
            cg = slice(n * cols_sub, (n + 1) * cols_sub)
            cu = slice(D_FF + n * cols_sub, D_FF + (n + 1) * cols_sub)
            xb = xb16[...]
            gate = jnp.minimum(_dot(xb, wgu_ref[0, :, cg]) + bgu_ref[0, :, cg], SWIGLU_LIMIT)
            up = jnp.clip(_dot(xb, wgu_ref[0, :, cu]) + bgu_ref[0, :, cu], -SWIGLU_LIMIT, SWIGLU_LIMIT)
            act = (up + 1.0) * gate * _sigmoid(SWIGLU_ALPHA * gate)
            o_cur[:, 0, :] += _dot(act.astype(BF16), wd_ref[0, cg, :])

        @pl.when(j < n_used)
        def _():
            wait_tile_gather(x_cur, gsem.at[ph])

            @pl.when(j >= 1)
            def _():
                wait_tile_scatter(o_cur, ssem.at[ph])

            for r in range(tm):
                gather_row(src_next_ref, r, x_nxt, gsem.at[1 - ph], priority=r % 2)
                scatter_row(dst_prev_ref, r, o_prv, ssem.at[1 - ph], priority=r % 2)
            xb16[...] = x_cur[:, 0, :].astype(BF16)
            o_cur[:, 0, :] = jnp.broadcast_to(bd_ref[0], (tm, D_MODEL))

        @pl.when(j + 1 <= n_used)
        def _():
            for n in range(n_sub):
                ffn_chunk(n)

        @pl.when(j == n_used)
        def _():
            def body(r, carry):
                scatter_row(dst_prev_ref, r, o_prv, ssem.at[1 - ph])
                return carry
            lax.fori_loop(0, tm, body, 0)
            wait_tile_scatter(o_prv, ssem.at[1 - ph])
            wait_tile_scatter(o_cur, ssem.at[ph])
            wait_tile_gather(x_cur, gsem.at[ph])

    phase(0, src_a_ref, dst_a_ref, wgu_a_ref, bgu_a_ref, wd_a_ref, bd_a_ref)
    phase(1, src_b_ref, dst_b_ref, wgu_b_ref, bgu_b_ref, wd_b_ref, bd_b_ref)


def _experts(h2, src_tok, dst_row, tile_expert, n_used, wgu_b, bgu, wd_b, bd, n_slots, n_dump):
    D = h2.shape[-1]
    tm = EXPERT_TILE
    n_tiles = tile_expert.shape[0]
    src3 = src_tok.reshape(n_tiles + 2, 1, tm)
    dst3 = dst_row.reshape(n_tiles + 2, 1, tm)
    meta = jnp.concatenate([n_used.reshape(1), tile_expert]).astype(jnp.int32)
    smem_row = lambda f: pl.BlockSpec((1, 1, tm), f, memory_space=pltpu.SMEM)
    e_of = lambda j, m: m[1 + jnp.minimum(j, n_tiles - 1)]
    weights = lambda ph: [
        pl.BlockSpec((1, D, 2 * D_FF), lambda g, m: (e_of(2 * g + ph, m), 0, 0)),
        pl.BlockSpec((1, 1, 2 * D_FF), lambda g, m: (e_of(2 * g + ph, m), 0, 0)),
        pl.BlockSpec((1, D_FF, D), lambda g, m: (e_of(2 * g + ph, m), 0, 0)),
        pl.BlockSpec((1, 1, D), lambda g, m: (e_of(2 * g + ph, m), 0, 0))]
    grid_spec = pltpu.PrefetchScalarGridSpec(
        num_scalar_prefetch=1,
        grid=((n_tiles + 2) // 2,),
        in_specs=[smem_row(lambda g, m: (0, 0, 0)),
                  smem_row(lambda g, m: (2 * g + 1, 0, 0)),
                  smem_row(lambda g, m: (jnp.minimum(2 * g + 2, n_tiles + 1), 0, 0)),
                  smem_row(lambda g, m: (2 * g, 0, 0)),
                  smem_row(lambda g, m: (2 * g + 1, 0, 0)),
                  pl.BlockSpec(memory_space=pl.ANY)] + weights(0) + weights(1),
        out_specs=pl.BlockSpec(memory_space=pl.ANY),
        scratch_shapes=[pltpu.VMEM((tm, 1, D), F32), pltpu.VMEM((tm, 1, D), F32),
                        pltpu.VMEM((tm, 1, D), F32), pltpu.VMEM((tm, 1, D), F32),
                        pltpu.VMEM((tm, D), BF16),
                        pltpu.SemaphoreType.DMA((2,)), pltpu.SemaphoreType.DMA((2,))],
    )
    bgu3 = bgu.reshape(N_EXPERTS, 1, 2 * D_FF)
    bd3 = bd.reshape(N_EXPERTS, 1, D)
    return pl.pallas_call(
        functools.partial(_experts_kernel, n_slots=n_slots),
        grid_spec=grid_spec,
        out_shape=jax.ShapeDtypeStruct((n_slots + n_dump, 1, D), F32),
        compiler_params=_params("arbitrary"),
        name="experts",
    )(meta, src3, src3, src3, dst3, dst3, h2, wgu_b, bgu3, wd_b, bd3, wgu_b, bgu3, wd_b, bd3)


def _route(top_idx_t):
    T = top_idx_t.shape[1]
    tm = EXPERT_TILE
    M = T * TOP_K
    slot_expert = top_idx_t.reshape(M)
    order = jnp.argsort(slot_expert).astype(jnp.int32)
    counts = jnp.bincount(slot_expert, length=N_EXPERTS)
    padded = (counts + tm - 1) // tm * tm
    pad_end = jnp.cumsum(padded)
    pad_start = pad_end - padded
    start = jnp.cumsum(counts) - counts
    n_tiles = M // tm + N_EXPERTS
    tile_start = jnp.arange(n_tiles) * tm
    tile_expert = jnp.minimum(jnp.sum(pad_end[None, :] <= tile_start[:, None], axis=1), N_EXPERTS - 1)
    tile_valid = jnp.clip(counts[tile_expert] - (tile_start - pad_start[tile_expert]), 0, tm)
    n_used = pad_end[-1] // tm
    r = jnp.arange(tm)[None, :]
    sorted_pos = (start[tile_expert] + tile_start - pad_start[tile_expert])[:, None] + r
    slot = order[jnp.clip(sorted_pos, 0, M - 1)]
    is_real = r < tile_valid[:, None]
    src_tok = jnp.where(is_real, slot % T, 0).astype(jnp.int32)
    dump = M + tm + tile_expert[:, None] * tm + (r - tile_valid[:, None])
    dst_row = jnp.where(is_real, slot, dump).astype(jnp.int32)
    lead = (M + r).astype(jnp.int32)
    zeros = jnp.zeros((1, tm), jnp.int32)
    src_ext = jnp.concatenate([src_tok, zeros, zeros], axis=0)
    dst_ext = jnp.concatenate([lead, dst_row, zeros], axis=0)
    n_dump = (N_EXPERTS + 1) * tm
    return src_ext, dst_ext, tile_expert.astype(jnp.int32), n_used.astype(jnp.int32), n_dump


def _combine_kernel(x1_ref, y0_ref, y1_ref, y2_ref, y3_ref, w_ref, gate_ref, o_ref):
    w = w_ref[...]
    acc = w[:, 0:1] * y0_ref[:, 0, :]
    for j, y_ref in enumerate((y1_ref, y2_ref, y3_ref), start=1):
        acc = acc + w[:, j:j + 1] * y_ref[:, 0, :]
    o_ref[...] = x1_ref[...] + gate_ref[0] * acc


def _combine(x1, y_slots, weights, gate2, S):
    T, D = x1.shape
    tm = 256
    per_b = S // tm
    n_t = T // tm
    plane = lambda k: pl.BlockSpec((tm, 1, D), lambda i: (k * n_t + i, 0, 0))
    return pl.pallas_call(
        _combine_kernel,
        grid=(n_t,),
        in_specs=[pl.BlockSpec((tm, D), lambda i: (i, 0))] + [plane(k) for k in range(TOP_K)]
                 + [pl.BlockSpec((tm, LANES), lambda i: (i, 0)),
                    pl.BlockSpec((1, 1, D), lambda i: (i // per_b, 0, 0))],
        out_specs=pl.BlockSpec((tm, D), lambda i: (i, 0)),
        out_shape=jax.ShapeDtypeStruct((T, D), F32),
        compiler_params=_params("arbitrary"),
        name="combine",
    )(x1, y_slots, y_slots, y_slots, y_slots, weights, gate2)


def _pad_cols(w, n):
    return jnp.pad(w, ((0, 0), (0, n - w.shape[1])))


def _pad_rows(w, n):
    return jnp.pad(w, ((0, n - w.shape[0]), (0, 0)))


def _layer(x, c, w_ada, b_ada, norm1_g, w_in, rwkv_mu, rwkv_w0, rwkv_w_up, rwkv_a0, rwkv_a_up,
           rwkv_g_up, rwkv_k_k, rwkv_k_a, rwkv_r_k, rwkv_ln_g, rwkv_ln_b, q_norm_g, k_norm_g,
           w_out, norm2_g, w_router, b_router, w_gate_up, b_gate_up, w_down, b_down):
    B, S, D = x.shape
    T = B * S
    W = RWKV_WIDTH
    x2 = x.reshape(T, D)

    mods = _ada(c, w_ada, b_ada)
    shift1, scale1, gate1, shift2, scale2, gate2 = [
        mods[:, j * D:(j + 1) * D].reshape(B, 1, D) for j in range(6)]

    pieces = [(w_in[:, :XW_OFF], XW_OFF),
              (w_in[:, XW_OFF:XW_OFF + DECAY_LORA], LANES),
              (w_in[:, XW_OFF + DECAY_LORA:XW_OFF + DECAY_LORA + AAA_LORA], LANES),
              (w_in[:, XW_OFF + DECAY_LORA + AAA_LORA:RWKV_PROJ], 2 * LANES),
              (w_in[:, RWKV_PROJ:], MOBA_PROJ)]
    w_in_b = jnp.concatenate([_pad_cols(w, n) for w, n in pieces], axis=1).astype(BF16)
    mu_pieces = [(rwkv_mu[None, :XW_OFF], XW_OFF),
                 (rwkv_mu[None, XW_OFF:XW_OFF + DECAY_LORA], LANES),
                 (rwkv_mu[None, XW_OFF + DECAY_LORA:XW_OFF + DECAY_LORA + AAA_LORA], LANES),
                 (rwkv_mu[None, XW_OFF + DECAY_LORA + AAA_LORA:], 2 * LANES)]
    mu = jnp.concatenate([_pad_cols(m, n) for m, n in mu_pieces], axis=1)

    p_rwkv, p_moba = _in_proj(x2, norm1_g, scale1, shift1, w_in_b, S)

    row = lambda a: a.reshape(1, W)
    prm = (mu, row(rwkv_w0), row(rwkv_a0), row(rwkv_k_k), row(rwkv_k_a), row(rwkv_r_k),
           row(rwkv_ln_g), row(rwkv_ln_b), _pad_rows(rwkv_w_up, LANES), _pad_rows(rwkv_a_up, LANES),
           _pad_rows(rwkv_g_up, 2 * LANES))
    y_rwkv = _rwkv(p_rwkv, prm, B, S)
    y_moba = _moba(p_moba, q_norm_g, k_norm_g, B, S)

    x1, h2, top_idx_t, weights = _out_proj(y_rwkv.reshape(T, W), y_moba.reshape(T, MOBA_WIDTH), x2,
                                           gate1, norm2_g, scale2, shift2, w_out.astype(BF16),
                                           w_router, b_router, S)

    src_tok, dst_row, tile_expert, n_used, n_dump = _route(top_idx_t)
    y_slots = _experts(h2, src_tok, dst_row, tile_expert, n_used, w_gate_up.astype(BF16), b_gate_up,
                       w_down.astype(BF16), b_down, T * TOP_K, n_dump)
    out = _combine(x1, y_slots, weights, gate2, S)
    return out.reshape(B, S, D)


def kernel(x, c, w_ada, b_ada, norm1_g, w_in, rwkv_mu, rwkv_w0, rwkv_w_up, rwkv_a0, rwkv_a_up, rwkv_g_up, rwkv_k_k, rwkv_k_a, rwkv_r_k, rwkv_ln_g, rwkv_ln_b, q_norm_g, k_norm_g, w_out, norm2_g, w_router, b_router, w_gate_up, b_gate_up, w_down, b_down):
    for l in range(w_ada.shape[0]):
        x = _layer(x, c, w_ada[l], b_ada[l], norm1_g[l], w_in[l], rwkv_mu[l], rwkv_w0[l],
                   rwkv_w_up[l], rwkv_a0[l], rwkv_a_up[l], rwkv_g_up[l], rwkv_k_k[l], rwkv_k_a[l],
                   rwkv_r_k[l], rwkv_ln_g[l], rwkv_ln_b[l], q_norm_g[l], k_norm_g[l], w_out[l],
                   norm2_g[l], w_router[l], b_router[l], w_gate_up[l], b_gate_up[l], w_down[l],
                   b_down[l])
    return x
```

```python
import functools

import jax
import jax.numpy as jnp
from jax import lax
from jax.experimental import pallas as pl
from jax.experimental.pallas import tpu as pltpu

F32 = jnp.float32
BF16 = jnp.bfloat16

D_MODEL = 1024
HEAD_DIM = 64
RWKV_WIDTH = 512
MOBA_WIDTH = 512
RWKV_HEADS = RWKV_WIDTH // HEAD_DIM
MOBA_HEADS = MOBA_WIDTH // HEAD_DIM
DECAY_LORA = 64
AAA_LORA = 64
GATE_LORA = 160
RWKV_LN_EPS = 64e-5
RWKV_PROJ = 3 * RWKV_WIDTH + DECAY_LORA + AAA_LORA + GATE_LORA
MOBA_PROJ = 3 * MOBA_WIDTH
MOBA_BLOCK = 256
MOBA_TOPK = 3
N_EXPERTS = 32
TOP_K = 4
D_FF = D_MODEL
SWIGLU_LIMIT = 7.0
SWIGLU_ALPHA = 1.702
NORM_EPS = 1e-6

LANES = 128
SUBLANES = 8
XW_OFF = 3 * RWKV_WIDTH
XA_OFF = XW_OFF + LANES
XG_OFF = XA_OFF + LANES
RWKV_COLS = XG_OFF + 2 * LANES
CHUNK = 64
EXPERT_TILE = 512
EXPERT_SUBSTEPS = 4
VMEM_LIMIT = 56 * 1024 * 1024

NN = (((1,), (0,)), ((), ()))
NT = (((1,), (1,)), ((), ()))
TN = (((0,), (0,)), ((), ()))


def _dot(a, b, dims=NN):
    return lax.dot_general(a, b, dims, preferred_element_type=F32)


def _split(a):
    hi = a.astype(BF16)
    lo = (a - hi.astype(F32)).astype(BF16)
    return hi, lo


def _mm(a, b, dims=NN, passes=1):
    if passes == 1:
        return _dot(a.astype(BF16), b.astype(BF16), dims)
    a_hi, a_lo = _split(a)
    b_hi, b_lo = _split(b)
    return _dot(a_hi, b_hi, dims) + (_dot(a_hi, b_lo, dims) + _dot(a_lo, b_hi, dims))


def _split3(a):
    hi = a.astype(BF16)
    r = a - hi.astype(F32)
    mid = r.astype(BF16)
    lo = (r - mid.astype(F32)).astype(BF16)
    return hi, mid, lo


def _mm_exact_rhs(a, b_bf16, dims=NN):
    hi, mid, lo = _split3(a)
    return _dot(hi, b_bf16, dims) + (_dot(mid, b_bf16, dims) + _dot(lo, b_bf16, dims))


def _mm_exact_lhs(a_bf16, b, dims=NN):
    hi, mid, lo = _split3(b)
    return _dot(a_bf16, hi, dims) + (_dot(a_bf16, mid, dims) + _dot(a_bf16, lo, dims))


def _iota2(shape, dim):
    return lax.broadcasted_iota(jnp.int32, shape, dim)


def _group_ones(n, group):
    return (_iota2((n, n), 0) // group == _iota2((n, n), 1) // group).astype(BF16)


def _sigmoid(x):
    return 1.0 / (1.0 + jnp.exp(-x))


def _params(*sem):
    return pltpu.CompilerParams(dimension_semantics=sem, vmem_limit_bytes=VMEM_LIMIT)


def _ada_kernel(c_ref, w_ref, b_ref, o_ref):
    c = c_ref[...]
    o_ref[...] = _mm(c * _sigmoid(c), w_ref[...], passes=3) + b_ref[...]


def _ada(c, w_ada, b_ada):
    B, D = c.shape
    n_out = w_ada.shape[1]
    tn = 1024
    return pl.pallas_call(
        _ada_kernel,
        grid=(n_out // tn,),
        in_specs=[pl.BlockSpec((B, D), lambda j: (0, 0)),
                  pl.BlockSpec((D, tn), lambda j: (0, j)),
                  pl.BlockSpec((1, tn), lambda j: (0, j))],
        out_specs=pl.BlockSpec((B, tn), lambda j: (0, j)),
        out_shape=jax.ShapeDtypeStruct((B, n_out), F32),
        compiler_params=_params("arbitrary"),
        name="ada",
    )(c, w_ada, b_ada.reshape(1, n_out))


def _rms_modulate(x, g, scale, shift):
    y = x * lax.rsqrt(jnp.mean(x * x, axis=-1, keepdims=True) + NORM_EPS)
    return (y * g) * (1.0 + scale) + shift


def _in_proj_kernel(x_ref, g_ref, scale_ref, shift_ref, w_ref, pr_ref, pm_ref):
    h = _rms_modulate(x_ref[...], g_ref[...], scale_ref[0], shift_ref[0])
    proj = _dot(h.astype(BF16), w_ref[...])
    pr_ref[...] = proj[:, :RWKV_COLS]
    pm_ref[...] = proj[:, RWKV_COLS:]


def _in_proj(x2, norm_g, scale, shift, w_in_b, S):
    T, D = x2.shape
    tm = 256
    per_b = S // tm
    n_cols = w_in_b.shape[1]
    return pl.pallas_call(
        _in_proj_kernel,
        grid=(T // tm,),
        in_specs=[pl.BlockSpec((tm, D), lambda i: (i, 0)),
                  pl.BlockSpec((1, D), lambda i: (0, 0)),
                  pl.BlockSpec((1, 1, D), lambda i: (i // per_b, 0, 0)),
                  pl.BlockSpec((1, 1, D), lambda i: (i // per_b, 0, 0)),
                  pl.BlockSpec((D, n_cols), lambda i: (0, 0))],
        out_specs=[pl.BlockSpec((tm, RWKV_COLS), lambda i: (i, 0)),
                   pl.BlockSpec((tm, MOBA_PROJ), lambda i: (i, 0))],
        out_shape=[jax.ShapeDtypeStruct((T, RWKV_COLS), F32),
                   jax.ShapeDtypeStruct((T, MOBA_PROJ), F32)],
        compiler_params=_params("arbitrary"),
        name="in_proj",
    )(x2, norm_g.reshape(1, D), scale, shift, w_in_b)


RWKV_TILE = 256


def _unit_lower_inverses(Ls):
    n = Ls[0].shape[0]
    r = _iota2((n, n), 0)
    c = _iota2((n, n), 1)
    eye = (r == c).astype(F32)
    in_block = r // 8 == c // 8
    b16 = lambda xs: [x.astype(BF16) for x in xs]
    Ld = [jnp.where(in_block, L, 0.0) for L in Ls]
    Ld_b = b16(Ld)
    Nb_b = [(L - d).astype(BF16) for L, d in zip(Ls, Ld)]
    L2 = [_dot(d, d) for d in Ld_b]
    L2_b = b16(L2)
    L4_b = b16([_dot(x, x) for x in L2_b])
    T0 = [eye + d + l2 + _dot(db, l2b) for d, l2, db, l2b in zip(Ld, L2, Ld_b, L2_b)]
    T0 = [t + _dot(t.astype(BF16), l4b) for t, l4b in zip(T0, L4_b)]
    T0_b = b16(T0)
    M1_b = b16([_dot(t, nb) for t, nb in zip(T0_b, Nb_b)])
    M2_b = b16([_dot(m, m) for m in M1_b])
    M4_b = b16([_dot(m, m) for m in M2_b])
    X = [t + _dot(m4, tb) for t, m4, tb in zip(T0, M4_b, T0_b)]
    X = [x + _dot(m2, x.astype(BF16)) for x, m2 in zip(X, M2_b)]
    return [x + _dot(m1, x.astype(BF16)) for x, m1 in zip(X, M1_b)]


def _rwkv_kernel(p_ref, mu_ref, w0_ref, a0_ref, kk_ref, ka_ref, rk_ref, lng_ref, lnb_ref,
                 wup_ref, aup_ref, gup_ref, y_ref, carry_ref, state_ref, *, ts):
    t = pl.program_id(1)
    W = RWKV_WIDTH

    @pl.when(t == 0)
    def _():
        carry_ref[...] = jnp.zeros_like(carry_ref)
        state_ref[...] = jnp.zeros_like(state_ref)

    p = p_ref[0]
    row = _iota2((ts, 1), 0)
    prev = jnp.where(row == 0, carry_ref[...], pltpu.roll(p, 1, 0))
    carry_ref[...] = p[ts - 1:ts, :]
    pm = p + (prev - p) * mu_ref[...]
    r = pm[:, 0:W]
    k = pm[:, W:2 * W]
    v = pm[:, 2 * W:3 * W]
    xw = pm[:, XW_OFF:XA_OFF]
    xa = pm[:, XA_OFF:XG_OFF]
    xg = pm[:, XG_OFF:RWKV_COLS]

    z = -(w0_ref[...] + _mm(jnp.tanh(xw), wup_ref[...], passes=3))
    softplus = jnp.maximum(z, 0.0) + jnp.log(1.0 + jnp.exp(-jnp.abs(z)))
    logd = -jnp.exp(-softplus - 0.5)
    alpha = _sigmoid(a0_ref[...] + _mm(xa, aup_ref[...], passes=3))
    gate = _mm(_sigmoid(xg), gup_ref[...])

    head_ones = _group_ones(W, HEAD_DIM)
    kk = k * kk_ref[...]
    kk_norm = jnp.sqrt(_mm_exact_rhs(kk * kk, head_ones))
    kk = kk / jnp.maximum(kk_norm, 1e-12)
    kmod = k * (1.0 + (alpha - 1.0) * ka_ref[...])
    bonus = _mm_exact_rhs(r * kmod * rk_ref[...], head_ones) * v

    tr = _iota2((ts, ts), 0)
    tc = _iota2((ts, ts), 1)
    cum = ((tr // CHUNK == tc // CHUNK) & (tc <= tr)).astype(BF16)
    logp = _mm_exact_lhs(cum, logd)
    inv_p = jnp.exp(-logp)
    a_t = -kk * jnp.exp(logp - logd)
    b_t = kk * alpha * inv_p
    k_t = kmod * inv_p
    r_t = r * jnp.exp(logp)

    n2 = 2 * CHUNK
    n_chunks = ts // CHUNK
    n_pairs = RWKV_HEADS // 2
    inst = [(ci, pi) for ci in range(n_chunks) for pi in range(n_pairs)]
    lane = _iota2((1, LANES), 1)
    m0 = lane < HEAD_DIM
    sr = _iota2((n2, n2), 0)
    sc = _iota2((n2, n2), 1)
    strict_lower = sc < sr
    incl_lower = sc <= sr

    def stacked(x, ci, pi):
        xt = x[ci * CHUNK:(ci + 1) * CHUNK, pi * LANES:(pi + 1) * LANES]
        return jnp.concatenate([jnp.where(m0, xt, 0.0), jnp.where(m0, 0.0, xt)], axis=0)

    pcs = [jnp.exp(logp[(ci + 1) * CHUNK - 1:(ci + 1) * CHUNK, pi * LANES:(pi + 1) * LANES])
           for ci, pi in inst]
    r_s = [stacked(r_t, ci, pi) for ci, pi in inst]
    a_b = [stacked(a_t, ci, pi).astype(BF16) for ci, pi in inst]
    r_b = [x.astype(BF16) for x in r_s]
    b_s = [stacked(b_t, ci, pi) for ci, pi in inst]
    k_s = [stacked(k_t, ci, pi) for ci, pi in inst]
    v_b = [stacked(v, ci, pi).astype(BF16) for ci, pi in inst]
    ar_b = [jnp.concatenate([a, rr], axis=0) for a, rr in zip(a_b, r_b)]
    bk_b = [jnp.concatenate([b.astype(BF16), kx.astype(BF16)], axis=0) for b, kx in zip(b_s, k_s)]
    gram = [_dot(ar, bk, NT) for ar, bk in zip(ar_b, bk_b)]
    l_ab = [jnp.where(strict_lower, g[:n2, :n2], 0.0) for g in gram]
    l_ak_b = [jnp.where(strict_lower, g[:n2, n2:], 0.0).astype(BF16) for g in gram]
    m_rb_b = [jnp.where(incl_lower, g[n2:, :n2], 0.0).astype(BF16) for g in gram]
    m_rk_b = [jnp.where(incl_lower, g[n2:, n2:], 0.0).astype(BF16) for g in gram]
    t_b = [x.astype(BF16) for x in _unit_lower_inverses(l_ab)]
    lakv_b = [_dot(l, vv).astype(BF16) for l, vv in zip(l_ak_b, v_b)]
    wu_b = [_dot(tb, jnp.concatenate([a, lv], axis=1)).astype(BF16)
            for tb, a, lv in zip(t_b, a_b, lakv_b)]
    mwu = [_dot(m, wu) for m, wu in zip(m_rb_b, wu_b)]
    q_b = [(rs + x[:, :LANES]).astype(BF16) for rs, x in zip(r_s, mwu)]
    y0 = [x[:, LANES:] + _dot(m, vv) for x, m, vv in zip(mwu, m_rk_b, v_b)]
    bp_b = [(b * pc).astype(BF16) for b, pc in zip(b_s, pcs)]
    kp_b = [(kx * pc).astype(BF16) for kx, pc in zip(k_s, pcs)]
    gh = [_dot(wu, bp, TN) for wu, bp in zip(wu_b, bp_b)]
    g_b = [x[:LANES].astype(BF16) for x in gh]
    h = [x[LANES:] + _dot(vv, kp, TN) for x, vv, kp in zip(gh, v_b, kp_b)]

    y_chunks = []
    for ci in range(n_chunks):
        ids = [ci * n_pairs + pi for pi in range(n_pairs)]
        s0 = [state_ref[pi] for pi in range(n_pairs)]
        s0_b = [s.astype(BF16) for s in s0]
        y_s = [_dot(q_b[n], sb, NT) + y0[n] for n, sb in zip(ids, s0_b)]
        s1 = [s * pcs[n] + _dot(sb, g_b[n]) + h[n] for n, s, sb in zip(ids, s0, s0_b)]
        for pi in range(n_pairs):
            state_ref[pi] = s1[pi]
        y_chunks.append(jnp.concatenate([x[:CHUNK] + x[CHUNK:] for x in y_s], axis=1))
    y = jnp.concatenate(y_chunks, axis=0) if n_chunks > 1 else y_chunks[0]

    mean = _mm_exact_rhs(y, head_ones) * (1.0 / HEAD_DIM)
    yc = y - mean
    var = _mm_exact_rhs(yc * yc, head_ones) * (1.0 / HEAD_DIM)
    yn = yc * lax.rsqrt(var + RWKV_LN_EPS) * lng_ref[...] + lnb_ref[...]
    y_ref[0] = (yn + bonus) * gate


def _rwkv(p_rwkv, prm, B, S, ts=RWKV_TILE):
    W = RWKV_WIDTH
    vec = lambda n: pl.BlockSpec((1, n), lambda b, t: (0, 0))
    mat = lambda m, n: pl.BlockSpec((m, n), lambda b, t: (0, 0))
    return pl.pallas_call(
        functools.partial(_rwkv_kernel, ts=ts),
        grid=(B, S // ts),
        in_specs=[pl.BlockSpec((1, ts, RWKV_COLS), lambda b, t: (b, t, 0)),
                  vec(RWKV_COLS)] + [vec(W)] * 7 + [mat(LANES, W), mat(LANES, W), mat(2 * LANES, W)],
        out_specs=pl.BlockSpec((1, ts, W), lambda b, t: (b, t, 0)),
        out_shape=jax.ShapeDtypeStruct((B, S, W), F32),
        scratch_shapes=[pltpu.VMEM((1, RWKV_COLS), F32),
                        pltpu.VMEM((RWKV_HEADS // 2, LANES, LANES), F32)],
        compiler_params=_params("arbitrary", "arbitrary"),
        name="rwkv",
    )(p_rwkv.reshape(B, S, RWKV_COLS), *prm)


def _moba_kernel(slopes_ref, q_ref, k_ref, v_ref, qg_ref, kg_ref, o_ref, *, S):
    pair = pl.program_id(1)
    NB = S // MOBA_BLOCK
    BLK = MOBA_BLOCK
    n_sel = min(MOBA_TOPK, NB)
    scale = HEAD_DIM ** -0.5
    head_ones = _group_ones(LANES, HEAD_DIM)
    lane = _iota2((1, LANES), 1)

    def head_norm(x, g):
        ss = _mm_exact_rhs(x * x, head_ones)
        return x * lax.rsqrt(ss * (1.0 / HEAD_DIM) + NORM_EPS) * g

    qn = head_norm(q_ref[0], qg_ref[...])
    kn = head_norm(k_ref[0], kg_ref[...])
    kmean = jnp.mean(kn.reshape(NB, BLK, LANES), axis=1)
    q_t = qn.T
    v_tb = v_ref[0].T.astype(BF16)

    blk_of_q = _iota2((1, S), 1) // BLK
    nidx = _iota2((NB, 1), 0)
    valid = nidx < blk_of_q
    q_blk = (blk_of_q * BLK).astype(F32)
    q_loc = (_iota2((1, S), 1) % BLK).astype(F32)
    k_blk = (_iota2((S, 1), 0) // BLK * BLK).astype(F32)
    k_loc = (_iota2((S, 1), 0) % BLK).astype(F32)
    q_lane = _iota2((LANES, 1), 0)
    causal = _iota2((BLK, BLK), 0) <= _iota2((BLK, BLK), 1)

    out_rows = []
    for h in range(2):
        hmask = (lane // HEAD_DIM) == h
        slope = slopes_ref[pair * 2 + h]
        spare = (1 - h) * HEAD_DIM
        k_aug = jnp.where(hmask, kn, 0.0)
        k_aug = jnp.where(lane == spare, slope * k_blk, k_aug)
        k_aug = jnp.where(lane == spare + 1, slope * k_loc, k_aug)
        k_aug = jnp.where((lane == spare + 2) | (lane == spare + 3), 1.0, k_aug)
        q_aug = q_t * scale
        q_aug = jnp.where((q_lane == spare) | (q_lane == spare + 1), 1.0, q_aug)
        q_aug = jnp.where(q_lane == spare + 2, -slope * q_blk, q_aug)
        q_aug = jnp.where(q_lane == spare + 3, -slope * q_loc, q_aug)
        q_hb = q_aug.astype(BF16)
        gate = _mm(jnp.where(hmask, kmean, 0.0), q_t, passes=3)
        gate = jnp.where(valid, gate, -jnp.inf)
        rank = jnp.zeros((NB, S), jnp.int32)
        for m in range(NB):
            gm = gate[m:m + 1, :]
            ahead = (gm > gate) | ((gm == gate) & (m < nidx))
            rank = rank + ahead.astype(jnp.int32)
        sel = valid & (rank < n_sel)
        k_hb = k_aug.astype(BF16)
        v_h = v_tb[h * HEAD_DIM:(h + 1) * HEAD_DIM, :]
        out_blocks = []
        for i in range(NB):
            qs = slice(i * BLK, (i + 1) * BLK)
            tiles = []
            m_run = None
            for n in range(i + 1):
                ks = slice(n * BLK, (n + 1) * BLK)
                s = _dot(k_hb[ks, :], q_hb[:, qs])
                if n < i:
                    s = jnp.where(sel[n:n + 1, qs], s, -jnp.inf)
                else:
                    s = jnp.where(causal, s, -jnp.inf)
                tiles.append(s)
                mx = jnp.max(s, axis=0, keepdims=True)
                m_run = mx if m_run is None else jnp.maximum(m_run, mx)
            l_run = jnp.zeros((1, BLK), F32)
            acc = jnp.zeros((HEAD_DIM, BLK), F32)
            for n in range(i + 1):
                ks = slice(n * BLK, (n + 1) * BLK)
                pt = jnp.exp(tiles[n] - m_run)
                l_run = l_run + jnp.sum(pt, axis=0, keepdims=True)
                acc = acc + _dot(v_h[:, ks], pt.astype(BF16))
            out_blocks.append(acc / l_run)
        out_rows.append(jnp.concatenate(out_blocks, axis=1))
    o_ref[0] = jnp.concatenate(out_rows, axis=0).T


def _moba(p_moba, q_norm_g, k_norm_g, B, S):
    pairs = MOBA_HEADS // 2
    col = lambda off: pl.BlockSpec((1, S, LANES), lambda b, p: (b, 0, off + p))
    gain = pl.BlockSpec((1, LANES), lambda b, p: (0, 0))
    tile2 = lambda g: jnp.concatenate([g, g]).reshape(1, LANES)
    p3 = p_moba.reshape(B, S, MOBA_PROJ)
    slopes = jnp.exp2(-8.0 * (jnp.arange(MOBA_HEADS, dtype=F32) + 1.0) / MOBA_HEADS)
    return pl.pallas_call(
        functools.partial(_moba_kernel, S=S),
        grid=(B, pairs),
        in_specs=[pl.BlockSpec(memory_space=pltpu.SMEM), col(0), col(pairs), col(2 * pairs),
                  gain, gain],
        out_specs=pl.BlockSpec((1, S, LANES), lambda b, p: (b, 0, p)),
        out_shape=jax.ShapeDtypeStruct((B, S, MOBA_WIDTH), F32),
        compiler_params=_params("arbitrary", "arbitrary"),
        name="moba",
    )(slopes, p3, p3, p3, tile2(q_norm_g), tile2(k_norm_g))


def _out_proj_kernel(yr_ref, ym_ref, x_ref, gate_ref, g_ref, scale_ref, shift_ref, w_ref,
                     wr_ref, br_ref, x1_ref, h2_ref, idx_ref, wgt_ref):
    W = RWKV_WIDTH
    mix = (_dot(yr_ref[...].astype(BF16), w_ref[0:W, :])
           + _dot(ym_ref[...].astype(BF16), w_ref[W:, :]))
    x1 = x_ref[...] + gate_ref[0] * mix
    x1_ref[...] = x1
    h2 = _rms_modulate(x1, g_ref[...], scale_ref[0], shift_ref[0])
    h2_ref[:, 0, :] = h2
    logits_t = (_mm(h2, wr_ref[...], passes=3) + br_ref[...]).T[:N_EXPERTS, :]
    tm = logits_t.shape[1]
    eidx = _iota2((N_EXPERTS, 1), 0)
    vals, idxs = [], []
    for _ in range(TOP_K):
        m = jnp.max(logits_t, axis=0, keepdims=True)
        idx = jnp.min(jnp.where(logits_t == m, eidx, N_EXPERTS), axis=0, keepdims=True)
        vals.append(m)
        idxs.append(idx)
        logits_t = jnp.where(eidx == idx, -jnp.inf, logits_t)
    idx_ref[...] = jnp.concatenate(idxs, axis=0)
    e = [jnp.exp(v - vals[0]) for v in vals]
    total = e[0] + e[1] + e[2] + e[3]
    wgt_t = jnp.concatenate([x / total for x in e] + [jnp.zeros((LANES - TOP_K, tm), F32)], axis=0)
    wgt_ref[...] = wgt_t.T


def _out_proj(y_rwkv, y_moba, x2, gate1, norm_g, scale, shift, w_out_b, w_router, b_router, S):
    T, D = x2.shape
    tm = 256
    per_b = S // tm
    rows = lambda n: pl.BlockSpec((tm, n), lambda i: (i, 0))
    mod = pl.BlockSpec((1, 1, D), lambda i: (i // per_b, 0, 0))
    full = lambda m, n: pl.BlockSpec((m, n), lambda i: (0, 0))
    wr = _pad_cols(w_router, LANES)
    br = jnp.concatenate([b_router, jnp.full((LANES - N_EXPERTS,), -jnp.inf, F32)]).reshape(1, LANES)
    return pl.pallas_call(
        _out_proj_kernel,
        grid=(T // tm,),
        in_specs=[rows(RWKV_WIDTH), rows(MOBA_WIDTH), rows(D), mod, full(1, D), mod, mod,
                  full(D, D), full(D, LANES), full(1, LANES)],
        out_specs=[rows(D), pl.BlockSpec((tm, 1, D), lambda i: (i, 0, 0)),
                   pl.BlockSpec((TOP_K, tm), lambda i: (0, i)), rows(LANES)],
        out_shape=[jax.ShapeDtypeStruct((T, D), F32), jax.ShapeDtypeStruct((T, 1, D), F32),
                   jax.ShapeDtypeStruct((TOP_K, T), jnp.int32), jax.ShapeDtypeStruct((T, LANES), F32)],
        compiler_params=_params("arbitrary"),
        name="out_proj",
    )(y_rwkv, y_moba, x2, gate1, norm_g.reshape(1, D), scale, shift, w_out_b, wr, br)


def _experts_kernel(meta_ref, src_first_ref, src_next_ref, dst_prev_ref, h_hbm, wgu_ref, bgu_ref,
                    wd_ref, bd_ref, y_hbm, xbuf0, xbuf1, obuf0, obuf1, xb16, wgu16, wd16, gsem, ssem,
                    *, n_slots, n_tiles):
    j = pl.program_id(0)
    n_used = meta_ref[0]
    tm = EXPERT_TILE
    n_sub = EXPERT_SUBSTEPS
    cols_sub = D_FF // n_sub
    n_dump = y_hbm.shape[0] - n_slots
    expert = lambda i: meta_ref[1 + jnp.clip(i, 0, n_tiles - 1)]

    def gather_row(src_ref, r, dst_buf, sem, priority=0):
        pltpu.make_async_copy(h_hbm.at[src_ref[0, 0, r]], dst_buf.at[pl.ds(r, 1)], sem).start(
            priority=priority)

    def scatter_row(dst_ref, r, src_buf, sem, priority=0):
        pltpu.make_async_copy(src_buf.at[pl.ds(r, 1)], y_hbm.at[dst_ref[0, 0, r]], sem).start(
            priority=priority)

    def wait_tile_gather(buf, sem):
        pltpu.make_async_copy(h_hbm.at[pl.ds(0, tm), 0], buf, sem).wait()

    def wait_tile_scatter(buf, sem):
        pltpu.make_async_copy(buf, y_hbm.at[pl.ds(0, tm), 0], sem).wait()

    @pl.when(j == 0)
    def _():
        def body(r, carry):
            gather_row(src_first_ref, r, xbuf0, gsem.at[0])
            return carry
        lax.fori_loop(0, tm, body, 0)
        obuf1[...] = jnp.zeros_like(obuf1)
        for d in range(n_dump // tm):
            cp = pltpu.make_async_copy(obuf1, y_hbm.at[pl.ds(n_slots + d * tm, tm), 0], ssem.at[0])
            cp.start()
            cp.wait()

    @pl.when((j < n_used) & ((j == 0) | (expert(j) != expert(j - 1))))
    def _():
        wgu16[...] = wgu_ref[0].astype(BF16)
        wd16[...] = wd_ref[0].astype(BF16)

    def ffn_chunk(n, o_cur):
        cg = slice(n * cols_sub, (n + 1) * cols_sub)
        cu = slice(D_FF + n * cols_sub, D_FF + (n + 1) * cols_sub)
        xb = xb16[...]
        gate = jnp.minimum(_dot(xb, wgu16[:, cg]) + bgu_ref[0, :, cg], SWIGLU_LIMIT)
        up = jnp.clip(_dot(xb, wgu16[:, cu]) + bgu_ref[0, :, cu], -SWIGLU_LIMIT, SWIGLU_LIMIT)
        act = (up + 1.0) * gate * _sigmoid(SWIGLU_ALPHA * gate)
        o_cur[...] += _dot(act.astype(BF16), wd16[cg, :])

    def tile_step(ph):
        x_cur, x_nxt = (xbuf0, xbuf1) if ph == 0 else (xbuf1, xbuf0)
        o_cur, o_prv = (obuf0, obuf1) if ph == 0 else (obuf1, obuf0)
        mine = j % 2 == ph

        @pl.when(mine & (j < n_used))
        def _():
            wait_tile_gather(x_cur, gsem.at[ph])

            @pl.when(j >= 1)
            def _():
                wait_tile_scatter(o_cur, ssem.at[ph])

            for r in range(tm):
                gather_row(src_next_ref, r, x_nxt, gsem.at[1 - ph], priority=r % 2)
                scatter_row(dst_prev_ref, r, o_prv, ssem.at[1 - ph], priority=r % 2)
            xb16[...] = x_cur[...].astype(BF16)
            o_cur[...] = jnp.broadcast_to(bd_ref[0], (tm, D_MODEL))

        @pl.when(mine & (j + 1 <= n_used))
        def _():
            for n in range(n_sub):
                ffn_chunk(n, o_cur)

        @pl.when(mine & (j == n_used))
        def _():
            def body(r, carry):
                scatter_row(dst_prev_ref, r, o_prv, ssem.at[1 - ph])
                return carry
            lax.fori_loop(0, tm, body, 0)
            wait_tile_scatter(o_prv, ssem.at[1 - ph])
            wait_tile_scatter(o_cur, ssem.at[ph])
            wait_tile_gather(x_cur, gsem.at[ph])

    tile_step(0)
    tile_step(1)


def _experts(h2, src_tok, dst_row, tile_expert, n_used, wgu, bgu, wd, bd, n_slots, n_dump):
    D = h2.shape[-1]
    tm = EXPERT_TILE
    n_tiles = tile_expert.shape[0]
    src3 = src_tok.reshape(n_tiles + 2, 1, tm)
    dst3 = dst_row.reshape(n_tiles + 2, 1, tm)
    meta = jnp.concatenate([n_used.reshape(1), tile_expert]).astype(jnp.int32)
    smem_row = lambda f: pl.BlockSpec((1, 1, tm), f, memory_space=pltpu.SMEM)
    e_of = lambda j, m: m[1 + jnp.minimum(j, n_tiles - 1)]
    grid_spec = pltpu.PrefetchScalarGridSpec(
        num_scalar_prefetch=1,
        grid=(n_tiles + 2,),
        in_specs=[smem_row(lambda j, m: (0, 0, 0)),
                  smem_row(lambda j, m: (jnp.minimum(j + 1, n_tiles + 1), 0, 0)),
                  smem_row(lambda j, m: (j, 0, 0)),
                  pl.BlockSpec(memory_space=pl.ANY),
                  pl.BlockSpec((1, D, 2 * D_FF), lambda j, m: (e_of(j, m), 0, 0)),
                  pl.BlockSpec((1, 1, 2 * D_FF), lambda j, m: (e_of(j, m), 0, 0)),
                  pl.BlockSpec((1, D_FF, D), lambda j, m: (e_of(j, m), 0, 0)),
                  pl.BlockSpec((1, 1, D), lambda j, m: (e_of(j, m), 0, 0))],
        out_specs=pl.BlockSpec(memory_space=pl.ANY),
        scratch_shapes=[pltpu.VMEM((tm, D), F32), pltpu.VMEM((tm, D), F32),
                        pltpu.VMEM((tm, D), F32), pltpu.VMEM((tm, D), F32),
                        pltpu.VMEM((tm, D), BF16),
                        pltpu.VMEM((D, 2 * D_FF), BF16), pltpu.VMEM((D_FF, D), BF16),
                        pltpu.SemaphoreType.DMA((2,)), pltpu.SemaphoreType.DMA((2,))],
    )
    return pl.pallas_call(
        functools.partial(_experts_kernel, n_slots=n_slots, n_tiles=n_tiles),
        grid_spec=grid_spec,
        out_shape=jax.ShapeDtypeStruct((n_slots + n_dump, 1, D), F32),
        compiler_params=_params("arbitrary"),
        name="experts",
    )(meta, src3, src3, dst3, h2, wgu, bgu.reshape(N_EXPERTS, 1, 2 * D_FF), wd,
      bd.reshape(N_EXPERTS, 1, D))


def _route(top_idx_t):
    T = top_idx_t.shape[1]
    tm = EXPERT_TILE
    M = T * TOP_K
    slot_expert = top_idx_t.reshape(M)
    order = jnp.argsort(slot_expert).astype(jnp.int32)
    counts = jnp.bincount(slot_expert, length=N_EXPERTS)
    padded = (counts + tm - 1) // tm * tm
    pad_end = jnp.cumsum(padded)
    pad_start = pad_end - padded
    start = jnp.cumsum(counts) - counts
    n_tiles = M // tm + N_EXPERTS
    tile_start = jnp.arange(n_tiles) * tm
    tile_expert = jnp.minimum(jnp.sum(pad_end[None, :] <= tile_start[:, None], axis=1), N_EXPERTS - 1)
    tile_valid = jnp.clip(counts[tile_expert] - (tile_start - pad_start[tile_expert]), 0, tm)
    n_used = pad_end[-1] // tm
    r = jnp.arange(tm)[None, :]
    sorted_pos = (start[tile_expert] + tile_start - pad_start[tile_expert])[:, None] + r
    slot = order[jnp.clip(sorted_pos, 0, M - 1)]
    is_real = r < tile_valid[:, None]
    src_tok = jnp.where(is_real, slot % T, 0).astype(jnp.int32)
    dump = M + tm + tile_expert[:, None] * tm + (r - tile_valid[:, None])
    dst_row = jnp.where(is_real, slot, dump).astype(jnp.int32)
    lead = (M + r).astype(jnp.int32)
    zeros = jnp.zeros((1, tm), jnp.int32)
    src_ext = jnp.concatenate([src_tok, zeros, zeros], axis=0)
    dst_ext = jnp.concatenate([lead, dst_row, zeros], axis=0)
    n_dump = (N_EXPERTS + 1) * tm
    return src_ext, dst_ext, tile_expert.astype(jnp.int32), n_used.astype(jnp.int32), n_dump


def _combine_kernel(x1_ref, y0_ref, y1_ref, y2_ref, y3_ref, w_ref, gate_ref, o_ref):
    w = w_ref[...]
    acc = w[:, 0:1] * y0_ref[:, 0, :]
    for j, y_ref in enumerate((y1_ref, y2_ref, y3_ref), start=1):
        acc = acc + w[:, j:j + 1] * y_ref[:, 0, :]
    o_ref[...] = x1_ref[...] + gate_ref[0] * acc


def _combine(x1, y_slots, weights, gate2, S):
    T, D = x1.shape
    tm = 256
    per_b = S // tm
    n_t = T // tm
    plane = lambda k: pl.BlockSpec((tm, 1, D), lambda i: (k * n_t + i, 0, 0))
    return pl.pallas_call(
        _combine_kernel,
        grid=(n_t,),
        in_specs=[pl.BlockSpec((tm, D), lambda i: (i, 0))] + [plane(k) for k in range(TOP_K)]
                 + [pl.BlockSpec((tm, LANES), lambda i: (i, 0)),
                    pl.BlockSpec((1, 1, D), lambda i: (i // per_b, 0, 0))],
        out_specs=pl.BlockSpec((tm, D), lambda i: (i, 0)),
        out_shape=jax.ShapeDtypeStruct((T, D), F32),
        compiler_params=_params("arbitrary"),
        name="combine",
    )(x1, y_slots, y_slots, y_slots, y_slots, weights, gate2)


def _pad_cols(w, n):
    return jnp.pad(w, ((0, 0), (0, n - w.shape[1])))


def _pad_rows(w, n):
    return jnp.pad(w, ((0, n - w.shape[0]), (0, 0)))


def _layer(x, c, w_ada, b_ada, norm1_g, w_in, rwkv_mu, rwkv_w0, rwkv_w_up, rwkv_a0, rwkv_a_up,
           rwkv_g_up, rwkv_k_k, rwkv_k_a, rwkv_r_k, rwkv_ln_g, rwkv_ln_b, q_norm_g, k_norm_g,
           w_out, norm2_g, w_router, b_router, w_gate_up, b_gate_up, w_down, b_down):
    B, S, D = x.shape
    T = B * S
    W = RWKV_WIDTH
    x2 = x.reshape(T, D)

    mods = _ada(c, w_ada, b_ada)
    shift1, scale1, gate1, shift2, scale2, gate2 = [
        mods[:, j * D:(j + 1) * D].reshape(B, 1, D) for j in range(6)]

    pieces = [(w_in[:, :XW_OFF], XW_OFF),
              (w_in[:, XW_OFF:XW_OFF + DECAY_LORA], LANES),
              (w_in[:, XW_OFF + DECAY_LORA:XW_OFF + DECAY_LORA + AAA_LORA], LANES),
              (w_in[:, XW_OFF + DECAY_LORA + AAA_LORA:RWKV_PROJ], 2 * LANES),
              (w_in[:, RWKV_PROJ:], MOBA_PROJ)]
    w_in_b = jnp.concatenate([_pad_cols(w, n) for w, n in pieces], axis=1).astype(BF16)
    mu_pieces = [(rwkv_mu[None, :XW_OFF], XW_OFF),
                 (rwkv_mu[None, XW_OFF:XW_OFF + DECAY_LORA], LANES),
                 (rwkv_mu[None, XW_OFF + DECAY_LORA:XW_OFF + DECAY_LORA + AAA_LORA], LANES),
                 (rwkv_mu[None, XW_OFF + DECAY_LORA + AAA_LORA:], 2 * LANES)]
    mu = jnp.concatenate([_pad_cols(m, n) for m, n in mu_pieces], axis=1)

    p_rwkv, p_moba = _in_proj(x2, norm1_g, scale1, shift1, w_in_b, S)

    row = lambda a: a.reshape(1, W)
    prm = (mu, row(rwkv_w0), row(rwkv_a0), row(rwkv_k_k), row(rwkv_k_a), row(rwkv_r_k),
           row(rwkv_ln_g), row(rwkv_ln_b), _pad_rows(rwkv_w_up, LANES), _pad_rows(rwkv_a_up, LANES),
           _pad_rows(rwkv_g_up, 2 * LANES))
    y_rwkv = _rwkv(p_rwkv, prm, B, S)
    y_moba = _moba(p_moba, q_norm_g, k_norm_g, B, S)

    x1, h2, top_idx_t, weights = _out_proj(y_rwkv.reshape(T, W), y_moba.reshape(T, MOBA_WIDTH), x2,
                                           gate1, norm2_g, scale2, shift2, w_out.astype(BF16),
                                           w_router, b_router, S)

    src_tok, dst_row, tile_expert, n_used, n_dump = _route(top_idx_t)
    y_slots = _experts(h2, src_tok, dst_row, tile_expert, n_used, w_gate_up, b_gate_up, w_down, b_down,
                       T * TOP_K, n_dump)
    out = _combine(x1, y_slots, weights, gate2, S)
    return out.reshape(B, S, D)


def kernel(x, c, w_ada, b_ada, norm1_g, w_in, rwkv_mu, rwkv_w0, rwkv_w_up, rwkv_a0, rwkv_a_up, rwkv_g_up, rwkv_k_k, rwkv_k_a, rwkv_r_k, rwkv_ln_g, rwkv_ln_b, q_norm_g, k_norm_g, w_out, norm2_g, w_router, b_router, w_gate_up, b_gate_up, w_down, b_down):
    for l in range(w_ada.shape[0]):
        x = _layer(x, c, w_ada[l], b_ada[l], norm1_g[l], w_in[l], rwkv_mu[l], rwkv_w0[l],
                   rwkv_w_up[l], rwkv_a0[l], rwkv_a_up[l], rwkv_g_up[l], rwkv_k_k[l], rwkv_k_a[l],
                   rwkv_r_k[l], rwkv_ln_g[l], rwkv_ln_b[l], q_norm_g[l], k_norm_g[l], w_out[l],
                   norm2_g[l], w_router[l], b_router[l], w_gate_up[l], b_gate_up[l], w_down[l],
                   b_down[l])
    return x
```

```python
import functools

import jax
import jax.numpy as jnp
from jax import lax
from jax.experimental import pallas as pl
from jax.experimental.pallas import tpu as pltpu

F32 = jnp.float32
BF16 = jnp.bfloat16

D_MODEL = 1024
HEAD_DIM = 64
RWKV_WIDTH = 512
MOBA_WIDTH = 512
RWKV_HEADS = RWKV_WIDTH // HEAD_DIM
MOBA_HEADS = MOBA_WIDTH // HEAD_DIM
DECAY_LORA = 64
AAA_LORA = 64
GATE_LORA = 160
RWKV_LN_EPS = 64e-5
RWKV_PROJ = 3 * RWKV_WIDTH + DECAY_LORA + AAA_LORA + GATE_LORA
MOBA_PROJ = 3 * MOBA_WIDTH
MOBA_BLOCK = 256
MOBA_TOPK = 3
N_EXPERTS = 32
TOP_K = 4
D_FF = D_MODEL
SWIGLU_LIMIT = 7.0
SWIGLU_ALPHA = 1.702
NORM_EPS = 1e-6

LANES = 128
SUBLANES = 8
XW_OFF = 3 * RWKV_WIDTH
XA_OFF = XW_OFF + LANES
XG_OFF = XA_OFF + LANES
RWKV_COLS = XG_OFF + 2 * LANES
CHUNK = 64
EXPERT_TILE = 512
EXPERT_SUBSTEPS = 4
VMEM_LIMIT = 56 * 1024 * 1024

NN = (((1,), (0,)), ((), ()))
NT = (((1,), (1,)), ((), ()))
TN = (((0,), (0,)), ((), ()))


def _dot(a, b, dims=NN):
    return lax.dot_general(a, b, dims, preferred_element_type=F32)


def _split(a):
    hi = a.astype(BF16)
    lo = (a - hi.astype(F32)).astype(BF16)
    return hi, lo


def _mm(a, b, dims=NN, passes=1):
    if passes == 1:
        return _dot(a.astype(BF16), b.astype(BF16), dims)
    a_hi, a_lo = _split(a)
    b_hi, b_lo = _split(b)
    return _dot(a_hi, b_hi, dims) + (_dot(a_hi, b_lo, dims) + _dot(a_lo, b_hi, dims))


def _split3(a):
    hi = a.astype(BF16)
    r = a - hi.astype(F32)
    mid = r.astype(BF16)
    lo = (r - mid.astype(F32)).astype(BF16)
    return hi, mid, lo


def _mm_exact_rhs(a, b_bf16, dims=NN):
    hi, mid, lo = _split3(a)
    return _dot(hi, b_bf16, dims) + (_dot(mid, b_bf16, dims) + _dot(lo, b_bf16, dims))


def _mm_exact_lhs(a_bf16, b, dims=NN):
    hi, mid, lo = _split3(b)
    return _dot(a_bf16, hi, dims) + (_dot(a_bf16, mid, dims) + _dot(a_bf16, lo, dims))


def _iota2(shape, dim):
    return lax.broadcasted_iota(jnp.int32, shape, dim)


def _group_ones(n, group):
    return (_iota2((n, n), 0) // group == _iota2((n, n), 1) // group).astype(BF16)


def _sigmoid(x):
    return 1.0 / (1.0 + jnp.exp(-x))


def _params(*sem):
    return pltpu.CompilerParams(dimension_semantics=sem, vmem_limit_bytes=VMEM_LIMIT)


def _ada_kernel(c_ref, w_ref, b_ref, o_ref):
    c = c_ref[...]
    o_ref[...] = _mm(c * _sigmoid(c), w_ref[...], passes=3) + b_ref[...]


def _ada(c, w_ada, b_ada):
    B, D = c.shape
    n_out = w_ada.shape[1]
    tn = 1024
    return pl.pallas_call(
        _ada_kernel,
        grid=(n_out // tn,),
        in_specs=[pl.BlockSpec((B, D), lambda j: (0, 0)),
                  pl.BlockSpec((D, tn), lambda j: (0, j)),
                  pl.BlockSpec((1, tn), lambda j: (0, j))],
        out_specs=pl.BlockSpec((B, tn), lambda j: (0, j)),
        out_shape=jax.ShapeDtypeStruct((B, n_out), F32),
        compiler_params=_params("arbitrary"),
        name="ada",
    )(c, w_ada, b_ada.reshape(1, n_out))


def _rms_modulate(x, g, scale, shift):
    y = x * lax.rsqrt(jnp.mean(x * x, axis=-1, keepdims=True) + NORM_EPS)
    return (y * g) * (1.0 + scale) + shift


def _in_proj_kernel(x_ref, g_ref, scale_ref, shift_ref, w_ref, pr_ref, pm_ref):
    h = _rms_modulate(x_ref[...], g_ref[...], scale_ref[0], shift_ref[0])
    proj = _dot(h.astype(BF16), w_ref[...])
    pr_ref[...] = proj[:, :RWKV_COLS]
    pm_ref[...] = proj[:, RWKV_COLS:]


def _in_proj(x2, norm_g, scale, shift, w_in_b, S):
    T, D = x2.shape
    tm = 256
    per_b = S // tm
    n_cols = w_in_b.shape[1]
    return pl.pallas_call(
        _in_proj_kernel,
        grid=(T // tm,),
        in_specs=[pl.BlockSpec((tm, D), lambda i: (i, 0)),
                  pl.BlockSpec((1, D), lambda i: (0, 0)),
                  pl.BlockSpec((1, 1, D), lambda i: (i // per_b, 0, 0)),
                  pl.BlockSpec((1, 1, D), lambda i: (i // per_b, 0, 0)),
                  pl.BlockSpec((D, n_cols), lambda i: (0, 0))],
        out_specs=[pl.BlockSpec((tm, RWKV_COLS), lambda i: (i, 0)),
                   pl.BlockSpec((tm, MOBA_PROJ), lambda i: (i, 0))],
        out_shape=[jax.ShapeDtypeStruct((T, RWKV_COLS), F32),
                   jax.ShapeDtypeStruct((T, MOBA_PROJ), F32)],
        compiler_params=_params("arbitrary"),
        name="in_proj",
    )(x2, norm_g.reshape(1, D), scale, shift, w_in_b)


RWKV_TILE = 256


def _unit_lower_inverses(Ls):
    n = Ls[0].shape[0]
    r = _iota2((n, n), 0)
    c = _iota2((n, n), 1)
    eye = (r == c).astype(F32)
    in_block = r // 8 == c // 8
    b16 = lambda xs: [x.astype(BF16) for x in xs]
    Ld = [jnp.where(in_block, L, 0.0) for L in Ls]
    Ld_b = b16(Ld)
    Nb_b = [(L - d).astype(BF16) for L, d in zip(Ls, Ld)]
    L2 = [_dot(d, d) for d in Ld_b]
    L2_b = b16(L2)
    L4_b = b16([_dot(x, x) for x in L2_b])
    T0 = [eye + d + l2 + _dot(db, l2b) for d, l2, db, l2b in zip(Ld, L2, Ld_b, L2_b)]
    T0 = [t + _dot(t.astype(BF16), l4b) for t, l4b in zip(T0, L4_b)]
    T0_b = b16(T0)
    M1_b = b16([_dot(t, nb) for t, nb in zip(T0_b, Nb_b)])
    M2_b = b16([_dot(m, m) for m in M1_b])
    M4_b = b16([_dot(m, m) for m in M2_b])
    X = [t + _dot(m4, tb) for t, m4, tb in zip(T0, M4_b, T0_b)]
    X = [x + _dot(m2, x.astype(BF16)) for x, m2 in zip(X, M2_b)]
    return [x + _dot(m1, x.astype(BF16)) for x, m1 in zip(X, M1_b)]


def _rwkv_kernel(p_ref, mu_ref, w0_ref, a0_ref, kk_ref, ka_ref, rk_ref, lng_ref, lnb_ref,
                 wup_ref, aup_ref, gup_ref, y_ref, carry_ref, state_ref, *, ts):
    t = pl.program_id(1)
    W = RWKV_WIDTH

    @pl.when(t == 0)
    def _():
        carry_ref[...] = jnp.zeros_like(carry_ref)
        state_ref[...] = jnp.zeros_like(state_ref)

    p = p_ref[0]
    row = _iota2((ts, 1), 0)
    prev = jnp.where(row == 0, carry_ref[...], pltpu.roll(p, 1, 0))
    carry_ref[...] = p[ts - 1:ts, :]
    pm = p + (prev - p) * mu_ref[...]
    r = pm[:, 0:W]
    k = pm[:, W:2 * W]
    v = pm[:, 2 * W:3 * W]
    xw = pm[:, XW_OFF:XA_OFF]
    xa = pm[:, XA_OFF:XG_OFF]
    xg = pm[:, XG_OFF:RWKV_COLS]

    z = -(w0_ref[...] + _mm(jnp.tanh(xw), wup_ref[...], passes=3))
    softplus = jnp.maximum(z, 0.0) + jnp.log(1.0 + jnp.exp(-jnp.abs(z)))
    logd = -jnp.exp(-softplus - 0.5)
    alpha = _sigmoid(a0_ref[...] + _mm(xa, aup_ref[...], passes=3))
    gate = _mm(_sigmoid(xg), gup_ref[...])

    head_ones = _group_ones(W, HEAD_DIM)
    kk = k * kk_ref[...]
    kk_norm = jnp.sqrt(_mm_exact_rhs(kk * kk, head_ones))
    kk = kk / jnp.maximum(kk_norm, 1e-12)
    kmod = k * (1.0 + (alpha - 1.0) * ka_ref[...])
    bonus = _mm_exact_rhs(r * kmod * rk_ref[...], head_ones) * v

    tr = _iota2((ts, ts), 0)
    tc = _iota2((ts, ts), 1)
    cum = ((tr // CHUNK == tc // CHUNK) & (tc <= tr)).astype(BF16)
    logp = _mm_exact_lhs(cum, logd)
    inv_p = jnp.exp(-logp)
    a_t = -kk * jnp.exp(logp - logd)
    b_t = kk * alpha * inv_p
    k_t = kmod * inv_p
    r_t = r * jnp.exp(logp)

    n2 = 2 * CHUNK
    n_chunks = ts // CHUNK
    n_pairs = RWKV_HEADS // 2
    inst = [(ci, pi) for ci in range(n_chunks) for pi in range(n_pairs)]
    lane = _iota2((1, LANES), 1)
    m0 = lane < HEAD_DIM
    sr = _iota2((n2, n2), 0)
    sc = _iota2((n2, n2), 1)
    strict_lower = sc < sr
    incl_lower = sc <= sr

    def stacked(x, ci, pi):
        xt = x[ci * CHUNK:(ci + 1) * CHUNK, pi * LANES:(pi + 1) * LANES]
        return jnp.concatenate([jnp.where(m0, xt, 0.0), jnp.where(m0, 0.0, xt)], axis=0)

    pcs = [jnp.exp(logp[(ci + 1) * CHUNK - 1:(ci + 1) * CHUNK, pi * LANES:(pi + 1) * LANES])
           for ci, pi in inst]
    r_s = [stacked(r_t, ci, pi) for ci, pi in inst]
    a_b = [stacked(a_t, ci, pi).astype(BF16) for ci, pi in inst]
    r_b = [x.astype(BF16) for x in r_s]
    b_s = [stacked(b_t, ci, pi) for ci, pi in inst]
    k_s = [stacked(k_t, ci, pi) for ci, pi in inst]
    v_b = [stacked(v, ci, pi).astype(BF16) for ci, pi in inst]
    ar_b = [jnp.concatenate([a, rr], axis=0) for a, rr in zip(a_b, r_b)]
    bk_b = [jnp.concatenate([b.astype(BF16), kx.astype(BF16)], axis=0) for b, kx in zip(b_s, k_s)]
    gram = [_dot(ar, bk, NT) for ar, bk in zip(ar_b, bk_b)]
    l_ab = [jnp.where(strict_lower, g[:n2, :n2], 0.0) for g in gram]
    l_ak_b = [jnp.where(strict_lower, g[:n2, n2:], 0.0).astype(BF16) for g in gram]
    m_rb_b = [jnp.where(incl_lower, g[n2:, :n2], 0.0).astype(BF16) for g in gram]
    m_rk_b = [jnp.where(incl_lower, g[n2:, n2:], 0.0).astype(BF16) for g in gram]
    t_b = [x.astype(BF16) for x in _unit_lower_inverses(l_ab)]
    lakv_b = [_dot(l, vv).astype(BF16) for l, vv in zip(l_ak_b, v_b)]
    wu_b = [_dot(tb, jnp.concatenate([a, lv], axis=1)).astype(BF16)
            for tb, a, lv in zip(t_b, a_b, lakv_b)]
    mwu = [_dot(m, wu) for m, wu in zip(m_rb_b, wu_b)]
    q_b = [(rs + x[:, :LANES]).astype(BF16) for rs, x in zip(r_s, mwu)]
    y0 = [x[:, LANES:] + _dot(m, vv) for x, m, vv in zip(mwu, m_rk_b, v_b)]
    bp_b = [(b * pc).astype(BF16) for b, pc in zip(b_s, pcs)]
    kp_b = [(kx * pc).astype(BF16) for kx, pc in zip(k_s, pcs)]
    gh = [_dot(wu, bp, TN) for wu, bp in zip(wu_b, bp_b)]
    g_b = [x[:LANES].astype(BF16) for x in gh]
    h = [x[LANES:] + _dot(vv, kp, TN) for x, vv, kp in zip(gh, v_b, kp_b)]

    y_chunks = []
    for ci in range(n_chunks):
        ids = [ci * n_pairs + pi for pi in range(n_pairs)]
        s0 = [state_ref[pi] for pi in range(n_pairs)]
        s0_b = [s.astype(BF16) for s in s0]
        y_s = [_dot(q_b[n], sb, NT) + y0[n] for n, sb in zip(ids, s0_b)]
        s1 = [s * pcs[n] + _dot(sb, g_b[n]) + h[n] for n, s, sb in zip(ids, s0, s0_b)]
        for pi in range(n_pairs):
            state_ref[pi] = s1[pi]
        y_chunks.append(jnp.concatenate([x[:CHUNK] + x[CHUNK:] for x in y_s], axis=1))
    y = jnp.concatenate(y_chunks, axis=0) if n_chunks > 1 else y_chunks[0]

    mean = _mm_exact_rhs(y, head_ones) * (1.0 / HEAD_DIM)
    yc = y - mean
    var = _mm_exact_rhs(yc * yc, head_ones) * (1.0 / HEAD_DIM)
    yn = yc * lax.rsqrt(var + RWKV_LN_EPS) * lng_ref[...] + lnb_ref[...]
    y_ref[0] = (yn + bonus) * gate


def _rwkv(p_rwkv, prm, B, S, ts=RWKV_TILE):
    W = RWKV_WIDTH
    vec = lambda n: pl.BlockSpec((1, n), lambda b, t: (0, 0))
    mat = lambda m, n: pl.BlockSpec((m, n), lambda b, t: (0, 0))
    return pl.pallas_call(
        functools.partial(_rwkv_kernel, ts=ts),
        grid=(B, S // ts),
        in_specs=[pl.BlockSpec((1, ts, RWKV_COLS), lambda b, t: (b, t, 0)),
                  vec(RWKV_COLS)] + [vec(W)] * 7 + [mat(LANES, W), mat(LANES, W), mat(2 * LANES, W)],
        out_specs=pl.BlockSpec((1, ts, W), lambda b, t: (b, t, 0)),
        out_shape=jax.ShapeDtypeStruct((B, S, W), F32),
        scratch_shapes=[pltpu.VMEM((1, RWKV_COLS), F32),
                        pltpu.VMEM((RWKV_HEADS // 2, LANES, LANES), F32)],
        compiler_params=_params("arbitrary", "arbitrary"),
        name="rwkv",
    )(p_rwkv.reshape(B, S, RWKV_COLS), *prm)


def _moba_kernel(slopes_ref, q_ref, k_ref, v_ref, qg_ref, kg_ref, o_ref, *, S):
    pair = pl.program_id(1)
    NB = S // MOBA_BLOCK
    BLK = MOBA_BLOCK
    n_sel = min(MOBA_TOPK, NB)
    scale = HEAD_DIM ** -0.5
    head_ones = _group_ones(LANES, HEAD_DIM)
    lane = _iota2((1, LANES), 1)

    def head_norm(x, g):
        ss = _mm_exact_rhs(x * x, head_ones)
        return x * lax.rsqrt(ss * (1.0 / HEAD_DIM) + NORM_EPS) * g

    qn = head_norm(q_ref[0], qg_ref[...])
    kn = head_norm(k_ref[0], kg_ref[...])
    kmean = jnp.mean(kn.reshape(NB, BLK, LANES), axis=1)
    q_t = qn.T
    v_tb = v_ref[0].T.astype(BF16)

    blk_of_q = _iota2((1, S), 1) // BLK
    nidx = _iota2((NB, 1), 0)
    valid = nidx < blk_of_q
    q_blk = (blk_of_q * BLK).astype(F32)
    q_loc = (_iota2((1, S), 1) % BLK).astype(F32)
    k_blk = (_iota2((S, 1), 0) // BLK * BLK).astype(F32)
    k_loc = (_iota2((S, 1), 0) % BLK).astype(F32)
    q_lane = _iota2((LANES, 1), 0)
    causal = _iota2((BLK, BLK), 0) <= _iota2((BLK, BLK), 1)

    out_rows = []
    for h in range(2):
        hmask = (lane // HEAD_DIM) == h
        slope = slopes_ref[pair * 2 + h]
        spare = (1 - h) * HEAD_DIM
        k_aug = jnp.where(hmask, kn, 0.0)
        k_aug = jnp.where(lane == spare, slope * k_blk, k_aug)
        k_aug = jnp.where(lane == spare + 1, slope * k_loc, k_aug)
        k_aug = jnp.where((lane == spare + 2) | (lane == spare + 3), 1.0, k_aug)
        q_aug = q_t * scale
        q_aug = jnp.where((q_lane == spare) | (q_lane == spare + 1), 1.0, q_aug)
        q_aug = jnp.where(q_lane == spare + 2, -slope * q_blk, q_aug)
        q_aug = jnp.where(q_lane == spare + 3, -slope * q_loc, q_aug)
        q_hb = q_aug.astype(BF16)
        gate = _mm(jnp.where(hmask, kmean, 0.0), q_t, passes=3)
        gate = jnp.where(valid, gate, -jnp.inf)
        rank = jnp.zeros((NB, S), jnp.int32)
        for m in range(NB):
            gm = gate[m:m + 1, :]
            ahead = (gm > gate) | ((gm == gate) & (m < nidx))
            rank = rank + ahead.astype(jnp.int32)
        sel = valid & (rank < n_sel)
        k_hb = k_aug.astype(BF16)
        v_h = v_tb[h * HEAD_DIM:(h + 1) * HEAD_DIM, :]
        out_blocks = []
        for i in range(NB):
            qs = slice(i * BLK, (i + 1) * BLK)
            tiles = []
            m_run = None
            for n in range(i + 1):
                ks = slice(n * BLK, (n + 1) * BLK)
                s = _dot(k_hb[ks, :], q_hb[:, qs])
                if n < i:
                    s = jnp.where(sel[n:n + 1, qs], s, -jnp.inf)
                else:
                    s = jnp.where(causal, s, -jnp.inf)
                tiles.append(s)
                mx = jnp.max(s, axis=0, keepdims=True)
                m_run = mx if m_run is None else jnp.maximum(m_run, mx)
            l_run = jnp.zeros((1, BLK), F32)
            acc = jnp.zeros((HEAD_DIM, BLK), F32)
            for n in range(i + 1):
                ks = slice(n * BLK, (n + 1) * BLK)
                pt = jnp.exp(tiles[n] - m_run)
                l_run = l_run + jnp.sum(pt, axis=0, keepdims=True)
                acc = acc + _dot(v_h[:, ks], pt.astype(BF16))
            out_blocks.append(acc / l_run)
        out_rows.append(jnp.concatenate(out_blocks, axis=1))
    o_ref[0] = jnp.concatenate(out_rows, axis=0).T


def _moba(p_moba, q_norm_g, k_norm_g, B, S):
    pairs = MOBA_HEADS // 2
    col = lambda off: pl.BlockSpec((1, S, LANES), lambda b, p: (b, 0, off + p))
    gain = pl.BlockSpec((1, LANES), lambda b, p: (0, 0))
    tile2 = lambda g: jnp.concatenate([g, g]).reshape(1, LANES)
    p3 = p_moba.reshape(B, S, MOBA_PROJ)
    slopes = jnp.exp2(-8.0 * (jnp.arange(MOBA_HEADS, dtype=F32) + 1.0) / MOBA_HEADS)
    return pl.pallas_call(
        functools.partial(_moba_kernel, S=S),
        grid=(B, pairs),
        in_specs=[pl.BlockSpec(memory_space=pltpu.SMEM), col(0), col(pairs), col(2 * pairs),
                  gain, gain],
        out_specs=pl.BlockSpec((1, S, LANES), lambda b, p: (b, 0, p)),
        out_shape=jax.ShapeDtypeStruct((B, S, MOBA_WIDTH), F32),
        compiler_params=_params("arbitrary", "arbitrary"),
        name="moba",
    )(slopes, p3, p3, p3, tile2(q_norm_g), tile2(k_norm_g))


def _out_proj_kernel(yr_ref, ym_ref, x_ref, gate_ref, g_ref, scale_ref, shift_ref, w_ref,
                     wr_ref, br_ref, x1_ref, h2_ref, idx_ref, wgt_ref):
    W = RWKV_WIDTH
    mix = (_dot(yr_ref[...].astype(BF16), w_ref[0:W, :])
           + _dot(ym_ref[...].astype(BF16), w_ref[W:, :]))
    x1 = x_ref[...] + gate_ref[0] * mix
    x1_ref[...] = x1
    h2 = _rms_modulate(x1, g_ref[...], scale_ref[0], shift_ref[0])
    h2_ref[:, 0, :] = h2
    logits_t = (_mm(h2, wr_ref[...], passes=3) + br_ref[...]).T[:N_EXPERTS, :]
    tm = logits_t.shape[1]
    eidx = _iota2((N_EXPERTS, 1), 0)
    vals, idxs = [], []
    for _ in range(TOP_K):
        m = jnp.max(logits_t, axis=0, keepdims=True)
        idx = jnp.min(jnp.where(logits_t == m, eidx, N_EXPERTS), axis=0, keepdims=True)
        vals.append(m)
        idxs.append(idx)
        logits_t = jnp.where(eidx == idx, -jnp.inf, logits_t)
    idx_ref[...] = jnp.concatenate(idxs, axis=0)
    e = [jnp.exp(v - vals[0]) for v in vals]
    total = e[0] + e[1] + e[2] + e[3]
    wgt_t = jnp.concatenate([x / total for x in e] + [jnp.zeros((LANES - TOP_K, tm), F32)], axis=0)
    wgt_ref[...] = wgt_t.T


def _out_proj(y_rwkv, y_moba, x2, gate1, norm_g, scale, shift, w_out_b, w_router, b_router, S):
    T, D = x2.shape
    tm = 256
    per_b = S // tm
    rows = lambda n: pl.BlockSpec((tm, n), lambda i: (i, 0))
    mod = pl.BlockSpec((1, 1, D), lambda i: (i // per_b, 0, 0))
    full = lambda m, n: pl.BlockSpec((m, n), lambda i: (0, 0))
    wr = _pad_cols(w_router, LANES)
    br = jnp.concatenate([b_router, jnp.full((LANES - N_EXPERTS,), -jnp.inf, F32)]).reshape(1, LANES)
    return pl.pallas_call(
        _out_proj_kernel,
        grid=(T // tm,),
        in_specs=[rows(RWKV_WIDTH), rows(MOBA_WIDTH), rows(D), mod, full(1, D), mod, mod,
                  full(D, D), full(D, LANES), full(1, LANES)],
        out_specs=[rows(D), pl.BlockSpec((tm, 1, D), lambda i: (i, 0, 0)),
                   pl.BlockSpec((TOP_K, tm), lambda i: (0, i)), rows(LANES)],
        out_shape=[jax.ShapeDtypeStruct((T, D), F32), jax.ShapeDtypeStruct((T, 1, D), F32),
                   jax.ShapeDtypeStruct((TOP_K, T), jnp.int32), jax.ShapeDtypeStruct((T, LANES), F32)],
        compiler_params=_params("arbitrary"),
        name="out_proj",
    )(y_rwkv, y_moba, x2, gate1, norm_g.reshape(1, D), scale, shift, w_out_b, wr, br)


def _experts_kernel(meta_ref, src_first_ref, src_next_ref, h_hbm, wgu_ref, bgu_ref, wd_ref, bd_ref,
                    y_hbm, xbuf0, xbuf1, obuf0, obuf1, xb16, wgu16, wd16, gsem, wsem,
                    *, n_tiles):
    j = pl.program_id(0)
    n_used = meta_ref[0]
    tm = EXPERT_TILE
    n_sub = EXPERT_SUBSTEPS
    cols_sub = D_FF // n_sub
    expert = lambda i: meta_ref[1 + jnp.clip(i, 0, n_tiles - 1)]

    def gather_row(src_ref, r, dst_buf, sem, priority=0):
        pltpu.make_async_copy(h_hbm.at[src_ref[0, 0, r]], dst_buf.at[pl.ds(r, 1)], sem).start(
            priority=priority)

    def wait_tile_gather(buf, sem):
        pltpu.make_async_copy(h_hbm.at[pl.ds(0, tm), 0], buf, sem).wait()

    def write_back(buf, tile, sem):
        return pltpu.make_async_copy(buf, y_hbm.at[pl.ds(tile * tm, tm), 0], sem)

    @pl.when(j == 0)
    def _():
        def body(r, carry):
            gather_row(src_first_ref, r, xbuf0, gsem.at[0])
            return carry
        lax.fori_loop(0, tm, body, 0)

    @pl.when((j < n_used) & ((j == 0) | (expert(j) != expert(j - 1))))
    def _():
        wgu16[...] = wgu_ref[0].astype(BF16)
        wd16[...] = wd_ref[0].astype(BF16)

    def ffn_chunk(n, o_cur):
        cg = slice(n * cols_sub, (n + 1) * cols_sub)
        cu = slice(D_FF + n * cols_sub, D_FF + (n + 1) * cols_sub)
        xb = xb16[...]
        gate = jnp.minimum(_dot(xb, wgu16[:, cg]) + bgu_ref[0, :, cg], SWIGLU_LIMIT)
        up = jnp.clip(_dot(xb, wgu16[:, cu]) + bgu_ref[0, :, cu], -SWIGLU_LIMIT, SWIGLU_LIMIT)
        act = (up + 1.0) * gate * _sigmoid(SWIGLU_ALPHA * gate)
        o_cur[...] += _dot(act.astype(BF16), wd16[cg, :])

    def tile_step(ph):
        x_cur, x_nxt = (xbuf0, xbuf1) if ph == 0 else (xbuf1, xbuf0)
        o_cur, o_prv = (obuf0, obuf1) if ph == 0 else (obuf1, obuf0)
        mine = j % 2 == ph

        @pl.when(mine & (j < n_used))
        def _():
            wait_tile_gather(x_cur, gsem.at[ph])

            @pl.when(j >= 2)
            def _():
                write_back(o_cur, j - 2, wsem.at[ph]).wait()

            for r in range(tm):
                gather_row(src_next_ref, r, x_nxt, gsem.at[1 - ph], priority=r % 2)
            xb16[...] = x_cur[...].astype(BF16)
            o_cur[...] = jnp.broadcast_to(bd_ref[0], (tm, D_MODEL))

        @pl.when(mine & (j + 1 <= n_used))
        def _():
            for n in range(n_sub):
                ffn_chunk(n, o_cur)
            write_back(o_cur, j, wsem.at[ph]).start()

        @pl.when(mine & (j == n_used))
        def _():
            wait_tile_gather(x_cur, gsem.at[ph])
            write_back(o_prv, j - 1, wsem.at[1 - ph]).wait()

            @pl.when(j >= 2)
            def _():
                write_back(o_cur, j - 2, wsem.at[ph]).wait()

        @pl.when(mine & (j >= n_used) & (j < n_tiles))
        def _():
            o_cur[...] = jnp.zeros_like(o_cur)
            fill = write_back(o_cur, j, wsem.at[ph])
            fill.start()
            fill.wait()

    tile_step(0)
    tile_step(1)


def _experts(h2, src_tok, tile_expert, n_used, wgu, bgu, wd, bd):
    D = h2.shape[-1]
    tm = EXPERT_TILE
    n_tiles = tile_expert.shape[0]
    src3 = src_tok.reshape(n_tiles + 1, 1, tm)
    meta = jnp.concatenate([n_used.reshape(1), tile_expert]).astype(jnp.int32)
    smem_row = lambda f: pl.BlockSpec((1, 1, tm), f, memory_space=pltpu.SMEM)
    e_of = lambda j, m: m[1 + jnp.minimum(j, n_tiles - 1)]
    grid_spec = pltpu.PrefetchScalarGridSpec(
        num_scalar_prefetch=1,
        grid=(n_tiles + 1,),
        in_specs=[smem_row(lambda j, m: (0, 0, 0)),
                  smem_row(lambda j, m: (jnp.minimum(j + 1, n_tiles), 0, 0)),
                  pl.BlockSpec(memory_space=pl.ANY),
                  pl.BlockSpec((1, D, 2 * D_FF), lambda j, m: (e_of(j, m), 0, 0)),
                  pl.BlockSpec((1, 1, 2 * D_FF), lambda j, m: (e_of(j, m), 0, 0)),
                  pl.BlockSpec((1, D_FF, D), lambda j, m: (e_of(j, m), 0, 0)),
                  pl.BlockSpec((1, 1, D), lambda j, m: (e_of(j, m), 0, 0))],
        out_specs=pl.BlockSpec(memory_space=pl.ANY),
        scratch_shapes=[pltpu.VMEM((tm, D), F32), pltpu.VMEM((tm, D), F32),
                        pltpu.VMEM((tm, D), F32), pltpu.VMEM((tm, D), F32),
                        pltpu.VMEM((tm, D), BF16),
                        pltpu.VMEM((D, 2 * D_FF), BF16), pltpu.VMEM((D_FF, D), BF16),
                        pltpu.SemaphoreType.DMA((2,)), pltpu.SemaphoreType.DMA((2,))],
    )
    return pl.pallas_call(
        functools.partial(_experts_kernel, n_tiles=n_tiles),
        grid_spec=grid_spec,
        out_shape=jax.ShapeDtypeStruct((n_tiles * tm, 1, D), F32),
        compiler_params=_params("arbitrary"),
        name="experts",
    )(meta, src3, src3, h2, wgu, bgu.reshape(N_EXPERTS, 1, 2 * D_FF), wd, bd.reshape(N_EXPERTS, 1, D))


def _route(top_idx_t):
    T = top_idx_t.shape[1]
    tm = EXPERT_TILE
    M = T * TOP_K
    slot_expert = top_idx_t.reshape(M)
    order = jnp.argsort(slot_expert).astype(jnp.int32)
    rank = jnp.argsort(order).astype(jnp.int32)
    counts = jnp.bincount(slot_expert, length=N_EXPERTS)
    padded = (counts + tm - 1) // tm * tm
    pad_end = jnp.cumsum(padded)
    pad_start = pad_end - padded
    start = jnp.cumsum(counts) - counts
    n_tiles = M // tm + N_EXPERTS
    tile_start = jnp.arange(n_tiles) * tm
    tile_expert = jnp.minimum(jnp.sum(pad_end[None, :] <= tile_start[:, None], axis=1), N_EXPERTS - 1)
    tile_valid = jnp.clip(counts[tile_expert] - (tile_start - pad_start[tile_expert]), 0, tm)
    n_used = pad_end[-1] // tm
    r = jnp.arange(tm)[None, :]
    sorted_pos = (start[tile_expert] + tile_start - pad_start[tile_expert])[:, None] + r
    slot = order[jnp.clip(sorted_pos, 0, M - 1)]
    src_tok = jnp.where(r < tile_valid[:, None], slot % T, 0).astype(jnp.int32)
    src_ext = jnp.concatenate([src_tok, jnp.zeros((1, tm), jnp.int32)], axis=0)
    slot_row = (pad_start[slot_expert] + rank - start[slot_expert]).astype(jnp.int32)
    return src_ext, tile_expert.astype(jnp.int32), n_used.astype(jnp.int32), slot_row


COMBINE_TILE = 256


def _combine_kernel(row_first_ref, row_next_ref, x1_ref, w_ref, gate_ref, y_hbm, o_ref,
                    ybuf0, ybuf1, sem, *, n_steps):
    i = pl.program_id(0)
    tm = COMBINE_TILE
    n_rows = TOP_K * tm

    def fetch_row(row_ref, r, buf, s, priority=0):
        pltpu.make_async_copy(y_hbm.at[row_ref[0, 0, r]], buf.at[pl.ds(r, 1)], s).start(priority=priority)

    @pl.when(i == 0)
    def _():
        def body(r, carry):
            fetch_row(row_first_ref, r, ybuf0, sem.at[0])
            return carry
        lax.fori_loop(0, n_rows, body, 0)

    def step(ph):
        cur, nxt = (ybuf0, ybuf1) if ph == 0 else (ybuf1, ybuf0)
        mine = i % 2 == ph

        @pl.when(mine & (i + 1 < n_steps))
        def _():
            for r in range(n_rows):
                fetch_row(row_next_ref, r, nxt, sem.at[1 - ph], priority=r % 2)

        @pl.when(mine)
        def _():
            pltpu.make_async_copy(y_hbm.at[pl.ds(0, n_rows), 0], cur, sem.at[ph]).wait()
            w = w_ref[...]
            acc = w[:, 0:1] * cur[0:tm, :]
            for k in range(1, TOP_K):
                acc = acc + w[:, k:k + 1] * cur[k * tm:(k + 1) * tm, :]
            o_ref[...] = x1_ref[...] + gate_ref[0] * acc

    step(0)
    step(1)


def _combine(x1, y_sorted, slot_row, weights, gate2, S):
    T, D = x1.shape
    tm = COMBINE_TILE
    per_b = S // tm
    n_steps = T // tm
    n_rows = TOP_K * tm
    rows = slot_row.reshape(TOP_K, n_steps, tm).transpose(1, 0, 2).reshape(n_steps, 1, n_rows)
    smem_row = lambda f: pl.BlockSpec((1, 1, n_rows), f, memory_space=pltpu.SMEM)
    return pl.pallas_call(
        functools.partial(_combine_kernel, n_steps=n_steps),
        grid=(n_steps,),
        in_specs=[smem_row(lambda i: (0, 0, 0)),
                  smem_row(lambda i: (jnp.minimum(i + 1, n_steps - 1), 0, 0)),
                  pl.BlockSpec((tm, D), lambda i: (i, 0)),
                  pl.BlockSpec((tm, LANES), lambda i: (i, 0)),
                  pl.BlockSpec((1, 1, D), lambda i: (i // per_b, 0, 0)),
                  pl.BlockSpec(memory_space=pl.ANY)],
        out_specs=pl.BlockSpec((tm, D), lambda i: (i, 0)),
        out_shape=jax.ShapeDtypeStruct((T, D), F32),
        scratch_shapes=[pltpu.VMEM((n_rows, D), F32), pltpu.VMEM((n_rows, D), F32),
                        pltpu.SemaphoreType.DMA((2,))],
        compiler_params=_params("arbitrary"),
        name="combine",
    )(rows, rows, x1, weights, gate2, y_sorted)


def _pad_cols(w, n):
    return jnp.pad(w, ((0, 0), (0, n - w.shape[1])))


def _pad_rows(w, n):
    return jnp.pad(w, ((0, n - w.shape[0]), (0, 0)))


def _layer(x, c, w_ada, b_ada, norm1_g, w_in, rwkv_mu, rwkv_w0, rwkv_w_up, rwkv_a0, rwkv_a_up,
           rwkv_g_up, rwkv_k_k, rwkv_k_a, rwkv_r_k, rwkv_ln_g, rwkv_ln_b, q_norm_g, k_norm_g,
           w_out, norm2_g, w_router, b_router, w_gate_up, b_gate_up, w_down, b_down):
    B, S, D = x.shape
    T = B * S
    W = RWKV_WIDTH
    x2 = x.reshape(T, D)

    mods = _ada(c, w_ada, b_ada)
    shift1, scale1, gate1, shift2, scale2, gate2 = [
        mods[:, j * D:(j + 1) * D].reshape(B, 1, D) for j in range(6)]

    pieces = [(w_in[:, :XW_OFF], XW_OFF),
              (w_in[:, XW_OFF:XW_OFF + DECAY_LORA], LANES),
              (w_in[:, XW_OFF + DECAY_LORA:XW_OFF + DECAY_LORA + AAA_LORA], LANES),
              (w_in[:, XW_OFF + DECAY_LORA + AAA_LORA:RWKV_PROJ], 2 * LANES),
              (w_in[:, RWKV_PROJ:], MOBA_PROJ)]
    w_in_b = jnp.concatenate([_pad_cols(w, n) for w, n in pieces], axis=1).astype(BF16)
    mu_pieces = [(rwkv_mu[None, :XW_OFF], XW_OFF),
                 (rwkv_mu[None, XW_OFF:XW_OFF + DECAY_LORA], LANES),
                 (rwkv_mu[None, XW_OFF + DECAY_LORA:XW_OFF + DECAY_LORA + AAA_LORA], LANES),
                 (rwkv_mu[None, XW_OFF + DECAY_LORA + AAA_LORA:], 2 * LANES)]
    mu = jnp.concatenate([_pad_cols(m, n) for m, n in mu_pieces], axis=1)

    p_rwkv, p_moba = _in_proj(x2, norm1_g, scale1, shift1, w_in_b, S)

    row = lambda a: a.reshape(1, W)
    prm = (mu, row(rwkv_w0), row(rwkv_a0), row(rwkv_k_k), row(rwkv_k_a), row(rwkv_r_k),
           row(rwkv_ln_g), row(rwkv_ln_b), _pad_rows(rwkv_w_up, LANES), _pad_rows(rwkv_a_up, LANES),
           _pad_rows(rwkv_g_up, 2 * LANES))
    y_rwkv = _rwkv(p_rwkv, prm, B, S)
    y_moba = _moba(p_moba, q_norm_g, k_norm_g, B, S)

    x1, h2, top_idx_t, weights = _out_proj(y_rwkv.reshape(T, W), y_moba.reshape(T, MOBA_WIDTH), x2,
                                           gate1, norm2_g, scale2, shift2, w_out.astype(BF16),
                                           w_router, b_router, S)

    src_tok, tile_expert, n_used, slot_row = _route(top_idx_t)
    y_sorted = _experts(h2, src_tok, tile_expert, n_used, w_gate_up, b_gate_up, w_down, b_down)
    out = _combine(x1, y_sorted, slot_row, weights, gate2, S)
    return out.reshape(B, S, D)


def kernel(x, c, w_ada, b_ada, norm1_g, w_in, rwkv_mu, rwkv_w0, rwkv_w_up, rwkv_a0, rwkv_a_up, rwkv_g_up, rwkv_k_k, rwkv_k_a, rwkv_r_k, rwkv_ln_g, rwkv_ln_b, q_norm_g, k_norm_g, w_out, norm2_g, w_router, b_router, w_gate_up, b_gate_up, w_down, b_down):
    for l in range(w_ada.shape[0]):
        x = _layer(x, c, w_ada[l], b_ada[l], norm1_g[l], w_in[l], rwkv_mu[l], rwkv_w0[l],
                   rwkv_w_up[l], rwkv_a0[l], rwkv_a_up[l], rwkv_g_up[l], rwkv_k_k[l], rwkv_k_a[l],
                   rwkv_r_k[l], rwkv_ln_g[l], rwkv_ln_b[l], q_norm_g[l], k_norm_g[l], w_out[l],
                   norm2_g[l], w_router[l], b_router[l], w_gate_up[l], b_gate_up[l], w_down[l],
                   b_down[l])
    return x
```

```python
import functools

import jax
import jax.numpy as jnp
from jax import lax
from jax.experimental import pallas as pl
from jax.experimental.pallas import tpu as pltpu

F32 = jnp.float32
BF16 = jnp.bfloat16

D_MODEL = 1024
HEAD_DIM = 64
RWKV_WIDTH = 512
MOBA_WIDTH = 512
RWKV_HEADS = RWKV_WIDTH // HEAD_DIM
MOBA_HEADS = MOBA_WIDTH // HEAD_DIM
DECAY_LORA = 64
AAA_LORA = 64
GATE_LORA = 160
RWKV_LN_EPS = 64e-5
RWKV_PROJ = 3 * RWKV_WIDTH + DECAY_LORA + AAA_LORA + GATE_LORA
MOBA_PROJ = 3 * MOBA_WIDTH
MOBA_BLOCK = 256
MOBA_TOPK = 3
N_EXPERTS = 32
TOP_K = 4
D_FF = D_MODEL
SWIGLU_LIMIT = 7.0
SWIGLU_ALPHA = 1.702
NORM_EPS = 1e-6

LANES = 128
SUBLANES = 8
XW_OFF = 3 * RWKV_WIDTH
XA_OFF = XW_OFF + LANES
XG_OFF = XA_OFF + LANES
RWKV_COLS = XG_OFF + 2 * LANES
CHUNK = 64
EXPERT_TILE = 512
EXPERT_SUBSTEPS = 4
VMEM_LIMIT = 56 * 1024 * 1024

NN = (((1,), (0,)), ((), ()))
NT = (((1,), (1,)), ((), ()))
TN = (((0,), (0,)), ((), ()))


def _dot(a, b, dims=NN):
    return lax.dot_general(a, b, dims, preferred_element_type=F32)


def _split(a):
    hi = a.astype(BF16)
    lo = (a - hi.astype(F32)).astype(BF16)
    return hi, lo


def _mm(a, b, dims=NN, passes=1):
    if passes == 1:
        return _dot(a.astype(BF16), b.astype(BF16), dims)
    a_hi, a_lo = _split(a)
    b_hi, b_lo = _split(b)
    return _dot(a_hi, b_hi, dims) + (_dot(a_hi, b_lo, dims) + _dot(a_lo, b_hi, dims))


def _split3(a):
    hi = a.astype(BF16)
    r = a - hi.astype(F32)
    mid = r.astype(BF16)
    lo = (r - mid.astype(F32)).astype(BF16)
    return hi, mid, lo


def _mm_exact_rhs(a, b_bf16, dims=NN):
    hi, mid, lo = _split3(a)
    return _dot(hi, b_bf16, dims) + (_dot(mid, b_bf16, dims) + _dot(lo, b_bf16, dims))


def _mm_exact_lhs(a_bf16, b, dims=NN):
    hi, mid, lo = _split3(b)
    return _dot(a_bf16, hi, dims) + (_dot(a_bf16, mid, dims) + _dot(a_bf16, lo, dims))


def _iota2(shape, dim):
    return lax.broadcasted_iota(jnp.int32, shape, dim)


def _group_ones(n, group):
    return (_iota2((n, n), 0) // group == _iota2((n, n), 1) // group).astype(BF16)


def _sigmoid(x):
    return 1.0 / (1.0 + jnp.exp(-x))


def _params(*sem):
    return pltpu.CompilerParams(dimension_semantics=sem, vmem_limit_bytes=VMEM_LIMIT)


def _ada_kernel(c_ref, w_ref, b_ref, o_ref):
    c = c_ref[...]
    o_ref[...] = _mm(c * _sigmoid(c), w_ref[...], passes=3) + b_ref[...]


def _ada(c, w_ada, b_ada):
    B, D = c.shape
    n_out = w_ada.shape[1]
    tn = 1024
    return pl.pallas_call(
        _ada_kernel,
        grid=(n_out // tn,),
        in_specs=[pl.BlockSpec((B, D), lambda j: (0, 0)),
                  pl.BlockSpec((D, tn), lambda j: (0, j)),
                  pl.BlockSpec((1, tn), lambda j: (0, j))],
        out_specs=pl.BlockSpec((B, tn), lambda j: (0, j)),
        out_shape=jax.ShapeDtypeStruct((B, n_out), F32),
        compiler_params=_params("arbitrary"),
        name="ada",
    )(c, w_ada, b_ada.reshape(1, n_out))


def _rms_modulate(x, g, scale, shift):
    y = x * lax.rsqrt(jnp.mean(x * x, axis=-1, keepdims=True) + NORM_EPS)
    return (y * g) * (1.0 + scale) + shift


def _in_proj_kernel(x_ref, g_ref, scale_ref, shift_ref, w_ref, pr_ref, pm_ref):
    h = _rms_modulate(x_ref[...], g_ref[...], scale_ref[0], shift_ref[0])
    proj = _dot(h.astype(BF16), w_ref[...])
    pr_ref[...] = proj[:, :RWKV_COLS]
    pm_ref[...] = proj[:, RWKV_COLS:]


def _in_proj(x2, norm_g, scale, shift, w_in_b, S):
    T, D = x2.shape
    tm = 256
    per_b = S // tm
    n_cols = w_in_b.shape[1]
    return pl.pallas_call(
        _in_proj_kernel,
        grid=(T // tm,),
        in_specs=[pl.BlockSpec((tm, D), lambda i: (i, 0)),
                  pl.BlockSpec((1, D), lambda i: (0, 0)),
                  pl.BlockSpec((1, 1, D), lambda i: (i // per_b, 0, 0)),
                  pl.BlockSpec((1, 1, D), lambda i: (i // per_b, 0, 0)),
                  pl.BlockSpec((D, n_cols), lambda i: (0, 0))],
        out_specs=[pl.BlockSpec((tm, RWKV_COLS), lambda i: (i, 0)),
                   pl.BlockSpec((tm, MOBA_PROJ), lambda i: (i, 0))],
        out_shape=[jax.ShapeDtypeStruct((T, RWKV_COLS), F32),
                   jax.ShapeDtypeStruct((T, MOBA_PROJ), F32)],
        compiler_params=_params("arbitrary"),
        name="in_proj",
    )(x2, norm_g.reshape(1, D), scale, shift, w_in_b)


RWKV_TILE = 256


def _unit_lower_inverses(Ls):
    n = Ls[0].shape[0]
    r = _iota2((n, n), 0)
    c = _iota2((n, n), 1)
    eye = (r == c).astype(F32)
    in_block = r // 8 == c // 8
    b16 = lambda xs: [x.astype(BF16) for x in xs]
    Ld = [jnp.where(in_block, L, 0.0) for L in Ls]
    Ld_b = b16(Ld)
    Nb_b = [(L - d).astype(BF16) for L, d in zip(Ls, Ld)]
    L2 = [_dot(d, d) for d in Ld_b]
    L2_b = b16(L2)
    L4_b = b16([_dot(x, x) for x in L2_b])
    T0 = [eye + d + l2 + _dot(db, l2b) for d, l2, db, l2b in zip(Ld, L2, Ld_b, L2_b)]
    T0 = [t + _dot(t.astype(BF16), l4b) for t, l4b in zip(T0, L4_b)]
    T0_b = b16(T0)
    M1_b = b16([_dot(t, nb) for t, nb in zip(T0_b, Nb_b)])
    M2_b = b16([_dot(m, m) for m in M1_b])
    M4_b = b16([_dot(m, m) for m in M2_b])
    X = [t + _dot(m4, tb) for t, m4, tb in zip(T0, M4_b, T0_b)]
    X = [x + _dot(m2, x.astype(BF16)) for x, m2 in zip(X, M2_b)]
    return [x + _dot(m1, x.astype(BF16)) for x, m1 in zip(X, M1_b)]


def _rwkv_kernel(p_ref, mu_ref, w0_ref, a0_ref, kk_ref, ka_ref, rk_ref, lng_ref, lnb_ref,
                 wup_ref, aup_ref, gup_ref, y_ref, carry_ref, state_ref, *, ts):
    t = pl.program_id(1)
    W = RWKV_WIDTH

    @pl.when(t == 0)
    def _():
        carry_ref[...] = jnp.zeros_like(carry_ref)
        state_ref[...] = jnp.zeros_like(state_ref)

    p = p_ref[0]
    row = _iota2((ts, 1), 0)
    prev = jnp.where(row == 0, carry_ref[...], pltpu.roll(p, 1, 0))
    carry_ref[...] = p[ts - 1:ts, :]
    pm = p + (prev - p) * mu_ref[...]
    r = pm[:, 0:W]
    k = pm[:, W:2 * W]
    v = pm[:, 2 * W:3 * W]
    xw = pm[:, XW_OFF:XA_OFF]
    xa = pm[:, XA_OFF:XG_OFF]
    xg = pm[:, XG_OFF:RWKV_COLS]

    z = -(w0_ref[...] + _mm(jnp.tanh(xw), wup_ref[...], passes=3))
    softplus = jnp.maximum(z, 0.0) + jnp.log(1.0 + jnp.exp(-jnp.abs(z)))
    logd = -jnp.exp(-softplus - 0.5)
    alpha = _sigmoid(a0_ref[...] + _mm(xa, aup_ref[...], passes=3))
    gate = _mm(_sigmoid(xg), gup_ref[...])

    head_ones = _group_ones(W, HEAD_DIM)
    kk = k * kk_ref[...]
    kk_norm = jnp.sqrt(_mm_exact_rhs(kk * kk, head_ones))
    kk = kk / jnp.maximum(kk_norm, 1e-12)
    kmod = k * (1.0 + (alpha - 1.0) * ka_ref[...])
    bonus = _mm_exact_rhs(r * kmod * rk_ref[...], head_ones) * v

    tr = _iota2((ts, ts), 0)
    tc = _iota2((ts, ts), 1)
    cum = ((tr // CHUNK == tc // CHUNK) & (tc <= tr)).astype(BF16)
    logp = _mm_exact_lhs(cum, logd)
    inv_p = jnp.exp(-logp)
    a_t = -kk * jnp.exp(logp - logd)
    b_t = kk * alpha * inv_p
    k_t = kmod * inv_p
    r_t = r * jnp.exp(logp)

    n2 = 2 * CHUNK
    n_chunks = ts // CHUNK
    n_pairs = RWKV_HEADS // 2
    inst = [(ci, pi) for ci in range(n_chunks) for pi in range(n_pairs)]
    lane = _iota2((1, LANES), 1)
    m0 = lane < HEAD_DIM
    sr = _iota2((n2, n2), 0)
    sc = _iota2((n2, n2), 1)
    strict_lower = sc < sr
    incl_lower = sc <= sr

    def stacked(x, ci, pi):
        xt = x[ci * CHUNK:(ci + 1) * CHUNK, pi * LANES:(pi + 1) * LANES]
        return jnp.concatenate([jnp.where(m0, xt, 0.0), jnp.where(m0, 0.0, xt)], axis=0)

    pcs = [jnp.exp(logp[(ci + 1) * CHUNK - 1:(ci + 1) * CHUNK, pi * LANES:(pi + 1) * LANES])
           for ci, pi in inst]
    r_s = [stacked(r_t, ci, pi) for ci, pi in inst]
    a_b = [stacked(a_t, ci, pi).astype(BF16) for ci, pi in inst]
    r_b = [x.astype(BF16) for x in r_s]
    b_s = [stacked(b_t, ci, pi) for ci, pi in inst]
    k_s = [stacked(k_t, ci, pi) for ci, pi in inst]
    v_b = [stacked(v, ci, pi).astype(BF16) for ci, pi in inst]
    ar_b = [jnp.concatenate([a, rr], axis=0) for a, rr in zip(a_b, r_b)]
    bk_b = [jnp.concatenate([b.astype(BF16), kx.astype(BF16)], axis=0) for b, kx in zip(b_s, k_s)]
    gram = [_dot(ar, bk, NT) for ar, bk in zip(ar_b, bk_b)]
    l_ab = [jnp.where(strict_lower, g[:n2, :n2], 0.0) for g in gram]
    l_ak_b = [jnp.where(strict_lower, g[:n2, n2:], 0.0).astype(BF16) for g in gram]
    m_rb_b = [jnp.where(incl_lower, g[n2:, :n2], 0.0).astype(BF16) for g in gram]
    m_rk_b = [jnp.where(incl_lower, g[n2:, n2:], 0.0).astype(BF16) for g in gram]
    t_b = [x.astype(BF16) for x in _unit_lower_inverses(l_ab)]
    lakv_b = [_dot(l, vv).astype(BF16) for l, vv in zip(l_ak_b, v_b)]
    wu_b = [_dot(tb, jnp.concatenate([a, lv], axis=1)).astype(BF16)
            for tb, a, lv in zip(t_b, a_b, lakv_b)]
    mwu = [_dot(m, wu) for m, wu in zip(m_rb_b, wu_b)]
    q_b = [(rs + x[:, :LANES]).astype(BF16) for rs, x in zip(r_s, mwu)]
    y0 = [x[:, LANES:] + _dot(m, vv) for x, m, vv in zip(mwu, m_rk_b, v_b)]
    bp_b = [(b * pc).astype(BF16) for b, pc in zip(b_s, pcs)]
    kp_b = [(kx * pc).astype(BF16) for kx, pc in zip(k_s, pcs)]
    gh = [_dot(wu, bp, TN) for wu, bp in zip(wu_b, bp_b)]
    g_b = [x[:LANES].astype(BF16) for x in gh]
    h = [x[LANES:] + _dot(vv, kp, TN) for x, vv, kp in zip(gh, v_b, kp_b)]

    y_chunks = []
    for ci in range(n_chunks):
        ids = [ci * n_pairs + pi for pi in range(n_pairs)]
        s0 = [state_ref[pi] for pi in range(n_pairs)]
        s0_b = [s.astype(BF16) for s in s0]
        y_s = [_dot(q_b[n], sb, NT) + y0[n] for n, sb in zip(ids, s0_b)]
        s1 = [s * pcs[n] + _dot(sb, g_b[n]) + h[n] for n, s, sb in zip(ids, s0, s0_b)]
        for pi in range(n_pairs):
            state_ref[pi] = s1[pi]
        y_chunks.append(jnp.concatenate([x[:CHUNK] + x[CHUNK:] for x in y_s], axis=1))
    y = jnp.concatenate(y_chunks, axis=0) if n_chunks > 1 else y_chunks[0]

    mean = _mm_exact_rhs(y, head_ones) * (1.0 / HEAD_DIM)
    yc = y - mean
    var = _mm_exact_rhs(yc * yc, head_ones) * (1.0 / HEAD_DIM)
    yn = yc * lax.rsqrt(var + RWKV_LN_EPS) * lng_ref[...] + lnb_ref[...]
    y_ref[0] = (yn + bonus) * gate


def _rwkv(p_rwkv, prm, B, S, ts=RWKV_TILE):
    W = RWKV_WIDTH
    vec = lambda n: pl.BlockSpec((1, n), lambda b, t: (0, 0))
    mat = lambda m, n: pl.BlockSpec((m, n), lambda b, t: (0, 0))
    return pl.pallas_call(
        functools.partial(_rwkv_kernel, ts=ts),
        grid=(B, S // ts),
        in_specs=[pl.BlockSpec((1, ts, RWKV_COLS), lambda b, t: (b, t, 0)),
                  vec(RWKV_COLS)] + [vec(W)] * 7 + [mat(LANES, W), mat(LANES, W), mat(2 * LANES, W)],
        out_specs=pl.BlockSpec((1, ts, W), lambda b, t: (b, t, 0)),
        out_shape=jax.ShapeDtypeStruct((B, S, W), F32),
        scratch_shapes=[pltpu.VMEM((1, RWKV_COLS), F32),
                        pltpu.VMEM((RWKV_HEADS // 2, LANES, LANES), F32)],
        compiler_params=_params("arbitrary", "arbitrary"),
        name="rwkv",
    )(p_rwkv.reshape(B, S, RWKV_COLS), *prm)


def _moba_kernel(slopes_ref, q_ref, k_ref, v_ref, qg_ref, kg_ref, o_ref, *, S):
    pair = pl.program_id(1)
    NB = S // MOBA_BLOCK
    BLK = MOBA_BLOCK
    n_sel = min(MOBA_TOPK, NB)
    scale = HEAD_DIM ** -0.5
    head_ones = _group_ones(LANES, HEAD_DIM)
    lane = _iota2((1, LANES), 1)

    def head_norm(x, g):
        ss = _mm_exact_rhs(x * x, head_ones)
        return x * lax.rsqrt(ss * (1.0 / HEAD_DIM) + NORM_EPS) * g

    qn = head_norm(q_ref[0], qg_ref[...])
    kn = head_norm(k_ref[0], kg_ref[...])
    kmean = jnp.mean(kn.reshape(NB, BLK, LANES), axis=1)
    q_t = qn.T
    v_tb = v_ref[0].T.astype(BF16)

    blk_of_q = _iota2((1, S), 1) // BLK
    nidx = _iota2((NB, 1), 0)
    valid = nidx < blk_of_q
    q_blk = (blk_of_q * BLK).astype(F32)
    q_loc = (_iota2((1, S), 1) % BLK).astype(F32)
    k_blk = (_iota2((S, 1), 0) // BLK * BLK).astype(F32)
    k_loc = (_iota2((S, 1), 0) % BLK).astype(F32)
    q_lane = _iota2((LANES, 1), 0)
    causal = _iota2((BLK, BLK), 0) <= _iota2((BLK, BLK), 1)

    out_rows = []
    for h in range(2):
        hmask = (lane // HEAD_DIM) == h
        slope = slopes_ref[pair * 2 + h]
        spare = (1 - h) * HEAD_DIM
        k_aug = jnp.where(hmask, kn, 0.0)
        k_aug = jnp.where(lane == spare, slope * k_blk, k_aug)
        k_aug = jnp.where(lane == spare + 1, slope * k_loc, k_aug)
        k_aug = jnp.where((lane == spare + 2) | (lane == spare + 3), 1.0, k_aug)
        q_aug = q_t * scale
        q_aug = jnp.where((q_lane == spare) | (q_lane == spare + 1), 1.0, q_aug)
        q_aug = jnp.where(q_lane == spare + 2, -slope * q_blk, q_aug)
        q_aug = jnp.where(q_lane == spare + 3, -slope * q_loc, q_aug)
        q_hb = q_aug.astype(BF16)
        gate = _mm(jnp.where(hmask, kmean, 0.0), q_t, passes=3)
        gate = jnp.where(valid, gate, -jnp.inf)
        rank = jnp.zeros((NB, S), jnp.int32)
        for m in range(NB):
            gm = gate[m:m + 1, :]
            ahead = (gm > gate) | ((gm == gate) & (m < nidx))
            rank = rank + ahead.astype(jnp.int32)
        sel = valid & (rank < n_sel)
        k_hb = k_aug.astype(BF16)
        v_h = v_tb[h * HEAD_DIM:(h + 1) * HEAD_DIM, :]
        out_blocks = []
        for i in range(NB):
            qs = slice(i * BLK, (i + 1) * BLK)
            tiles = []
            m_run = None
            for n in range(i + 1):
                ks = slice(n * BLK, (n + 1) * BLK)
                s = _dot(k_hb[ks, :], q_hb[:, qs])
                if n < i:
                    s = jnp.where(sel[n:n + 1, qs], s, -jnp.inf)
                else:
                    s = jnp.where(causal, s, -jnp.inf)
                tiles.append(s)
                mx = jnp.max(s, axis=0, keepdims=True)
                m_run = mx if m_run is None else jnp.maximum(m_run, mx)
            l_run = jnp.zeros((1, BLK), F32)
            acc = jnp.zeros((HEAD_DIM, BLK), F32)
            for n in range(i + 1):
                ks = slice(n * BLK, (n + 1) * BLK)
                pt = jnp.exp(tiles[n] - m_run)
                l_run = l_run + jnp.sum(pt, axis=0, keepdims=True)
                acc = acc + _dot(v_h[:, ks], pt.astype(BF16))
            out_blocks.append(acc / l_run)
        out_rows.append(jnp.concatenate(out_blocks, axis=1))
    o_ref[0] = jnp.concatenate(out_rows, axis=0).T


def _moba(p_moba, q_norm_g, k_norm_g, B, S):
    pairs = MOBA_HEADS // 2
    col = lambda off: pl.BlockSpec((1, S, LANES), lambda b, p: (b, 0, off + p))
    gain = pl.BlockSpec((1, LANES), lambda b, p: (0, 0))
    tile2 = lambda g: jnp.concatenate([g, g]).reshape(1, LANES)
    p3 = p_moba.reshape(B, S, MOBA_PROJ)
    slopes = jnp.exp2(-8.0 * (jnp.arange(MOBA_HEADS, dtype=F32) + 1.0) / MOBA_HEADS)
    return pl.pallas_call(
        functools.partial(_moba_kernel, S=S),
        grid=(B, pairs),
        in_specs=[pl.BlockSpec(memory_space=pltpu.SMEM), col(0), col(pairs), col(2 * pairs),
                  gain, gain],
        out_specs=pl.BlockSpec((1, S, LANES), lambda b, p: (b, 0, p)),
        out_shape=jax.ShapeDtypeStruct((B, S, MOBA_WIDTH), F32),
        compiler_params=_params("arbitrary", "arbitrary"),
        name="moba",
    )(slopes, p3, p3, p3, tile2(q_norm_g), tile2(k_norm_g))


def _out_proj_kernel(yr_ref, ym_ref, x_ref, gate_ref, g_ref, scale_ref, shift_ref, w_ref,
                     wr_ref, br_ref, x1_ref, h2_ref, idx_ref, wgt_ref):
    W = RWKV_WIDTH
    mix = (_dot(yr_ref[...].astype(BF16), w_ref[0:W, :])
           + _dot(ym_ref[...].astype(BF16), w_ref[W:, :]))
    x1 = x_ref[...] + gate_ref[0] * mix
    x1_ref[...] = x1
    h2 = _rms_modulate(x1, g_ref[...], scale_ref[0], shift_ref[0])
    h2_ref[:, 0, :] = h2
    logits_t = (_mm(h2, wr_ref[...], passes=3) + br_ref[...]).T[:N_EXPERTS, :]
    tm = logits_t.shape[1]
    eidx = _iota2((N_EXPERTS, 1), 0)
    vals, idxs = [], []
    for _ in range(TOP_K):
        m = jnp.max(logits_t, axis=0, keepdims=True)
        idx = jnp.min(jnp.where(logits_t == m, eidx, N_EXPERTS), axis=0, keepdims=True)
        vals.append(m)
        idxs.append(idx)
        logits_t = jnp.where(eidx == idx, -jnp.inf, logits_t)
    idx_ref[...] = jnp.concatenate(idxs, axis=0)
    e = [jnp.exp(v - vals[0]) for v in vals]
    total = e[0] + e[1] + e[2] + e[3]
    wgt_t = jnp.concatenate([x / total for x in e] + [jnp.zeros((LANES - TOP_K, tm), F32)], axis=0)
    wgt_ref[...] = wgt_t.T


def _out_proj(y_rwkv, y_moba, x2, gate1, norm_g, scale, shift, w_out_b, w_router, b_router, S):
    T, D = x2.shape
    tm = 256
    per_b = S // tm
    rows = lambda n: pl.BlockSpec((tm, n), lambda i: (i, 0))
    mod = pl.BlockSpec((1, 1, D), lambda i: (i // per_b, 0, 0))
    full = lambda m, n: pl.BlockSpec((m, n), lambda i: (0, 0))
    wr = _pad_cols(w_router, LANES)
    br = jnp.concatenate([b_router, jnp.full((LANES - N_EXPERTS,), -jnp.inf, F32)]).reshape(1, LANES)
    return pl.pallas_call(
        _out_proj_kernel,
        grid=(T // tm,),
        in_specs=[rows(RWKV_WIDTH), rows(MOBA_WIDTH), rows(D), mod, full(1, D), mod, mod,
                  full(D, D), full(D, LANES), full(1, LANES)],
        out_specs=[rows(D), pl.BlockSpec((tm, 1, D), lambda i: (i, 0, 0)),
                   pl.BlockSpec((TOP_K, tm), lambda i: (0, i)), rows(LANES)],
        out_shape=[jax.ShapeDtypeStruct((T, D), F32), jax.ShapeDtypeStruct((T, 1, D), F32),
                   jax.ShapeDtypeStruct((TOP_K, T), jnp.int32), jax.ShapeDtypeStruct((T, LANES), F32)],
        compiler_params=_params("arbitrary"),
        name="out_proj",
    )(y_rwkv, y_moba, x2, gate1, norm_g.reshape(1, D), scale, shift, w_out_b, wr, br)


def _experts_kernel(meta_ref, src_first_ref, src_next_ref, h_hbm, wgu_ref, bgu_ref, wd_ref, bd_ref,
                    y_hbm, xbuf0, xbuf1, obuf0, obuf1, xb16, wgu16, wd16, gsem, wsem,
                    *, n_tiles):
    j = pl.program_id(0)
    n_used = meta_ref[0]
    tm = EXPERT_TILE
    n_sub = EXPERT_SUBSTEPS
    cols_sub = D_FF // n_sub
    expert = lambda i: meta_ref[1 + jnp.clip(i, 0, n_tiles - 1)]

    def gather_row(src_ref, r, dst_buf, sem, priority=0):
        pltpu.make_async_copy(h_hbm.at[src_ref[0, 0, r]], dst_buf.at[pl.ds(r, 1)], sem).start(
            priority=priority)

    def wait_tile_gather(buf, sem):
        pltpu.make_async_copy(h_hbm.at[pl.ds(0, tm), 0], buf, sem).wait()

    def write_back(buf, tile, sem):
        return pltpu.make_async_copy(buf, y_hbm.at[pl.ds(tile * tm, tm), 0], sem)

    @pl.when(j == 0)
    def _():
        def body(r, carry):
            gather_row(src_first_ref, r, xbuf0, gsem.at[0])
            return carry
        lax.fori_loop(0, tm, body, 0)

    @pl.when((j < n_used) & ((j == 0) | (expert(j) != expert(j - 1))))
    def _():
        wgu16[...] = wgu_ref[0].astype(BF16)
        wd16[...] = wd_ref[0].astype(BF16)

    def ffn_chunk(n, o_cur):
        cg = slice(n * cols_sub, (n + 1) * cols_sub)
        cu = slice(D_FF + n * cols_sub, D_FF + (n + 1) * cols_sub)
        xb = xb16[...]
        gate = jnp.minimum(_dot(xb, wgu16[:, cg]) + bgu_ref[0, :, cg], SWIGLU_LIMIT)
        up = jnp.clip(_dot(xb, wgu16[:, cu]) + bgu_ref[0, :, cu], -SWIGLU_LIMIT, SWIGLU_LIMIT)
        act = (up + 1.0) * gate * _sigmoid(SWIGLU_ALPHA * gate)
        o_cur[...] += _dot(act.astype(BF16), wd16[cg, :])

    def tile_step(ph):
        x_cur, x_nxt = (xbuf0, xbuf1) if ph == 0 else (xbuf1, xbuf0)
        o_cur, o_prv = (obuf0, obuf1) if ph == 0 else (obuf1, obuf0)
        mine = j % 2 == ph

        @pl.when(mine & (j < n_used))
        def _():
            wait_tile_gather(x_cur, gsem.at[ph])

            @pl.when(j >= 2)
            def _():
                write_back(o_cur, j - 2, wsem.at[ph]).wait()

            xb16[...] = x_cur[...].astype(BF16)
            o_cur[...] = jnp.broadcast_to(bd_ref[0], (tm, D_MODEL))

        rows_sub = tm // n_sub
        for n in range(n_sub):
            @pl.when(mine & (j + 2 * n + 1 <= n_used + 2 * n))
            def _():
                for r in range(n * rows_sub, (n + 1) * rows_sub):
                    gather_row(src_next_ref, r, x_nxt, gsem.at[1 - ph], priority=r % 2)

            @pl.when(mine & (j + 2 * n + 2 <= n_used + 2 * n + 1))
            def _():
                ffn_chunk(n, o_cur)
                if n == n_sub - 1:
                    write_back(o_cur, j, wsem.at[ph]).start()

        @pl.when(mine & (j == n_used))
        def _():
            wait_tile_gather(x_cur, gsem.at[ph])
            write_back(o_prv, j - 1, wsem.at[1 - ph]).wait()

            @pl.when(j >= 2)
            def _():
                write_back(o_cur, j - 2, wsem.at[ph]).wait()

        @pl.when(mine & (j >= n_used) & (j < n_tiles))
        def _():
            o_cur[...] = jnp.zeros_like(o_cur)
            fill = write_back(o_cur, j, wsem.at[ph])
            fill.start()
            fill.wait()

    tile_step(0)
    tile_step(1)


def _experts(h2, src_tok, tile_expert, n_used, wgu, bgu, wd, bd):
    D = h2.shape[-1]
    tm = EXPERT_TILE
    n_tiles = tile_expert.shape[0]
    src3 = src_tok.reshape(n_tiles + 1, 1, tm)
    meta = jnp.concatenate([n_used.reshape(1), tile_expert]).astype(jnp.int32)
    smem_row = lambda f: pl.BlockSpec((1, 1, tm), f, memory_space=pltpu.SMEM)
    e_of = lambda j, m: m[1 + jnp.minimum(j, n_tiles - 1)]
    grid_spec = pltpu.PrefetchScalarGridSpec(
        num_scalar_prefetch=1,
        grid=(n_tiles + 1,),
        in_specs=[smem_row(lambda j, m: (0, 0, 0)),
                  smem_row(lambda j, m: (jnp.minimum(j + 1, n_tiles), 0, 0)),
                  pl.BlockSpec(memory_space=pl.ANY),
                  pl.BlockSpec((1, D, 2 * D_FF), lambda j, m: (e_of(j, m), 0, 0)),
                  pl.BlockSpec((1, 1, 2 * D_FF), lambda j, m: (e_of(j, m), 0, 0)),
                  pl.BlockSpec((1, D_FF, D), lambda j, m: (e_of(j, m), 0, 0)),
                  pl.BlockSpec((1, 1, D), lambda j, m: (e_of(j, m), 0, 0))],
        out_specs=pl.BlockSpec(memory_space=pl.ANY),
        scratch_shapes=[pltpu.VMEM((tm, D), F32), pltpu.VMEM((tm, D), F32),
                        pltpu.VMEM((tm, D), F32), pltpu.VMEM((tm, D), F32),
                        pltpu.VMEM((tm, D), BF16),
                        pltpu.VMEM((D, 2 * D_FF), BF16), pltpu.VMEM((D_FF, D), BF16),
                        pltpu.SemaphoreType.DMA((2,)), pltpu.SemaphoreType.DMA((2,))],
    )
    return pl.pallas_call(
        functools.partial(_experts_kernel, n_tiles=n_tiles),
        grid_spec=grid_spec,
        out_shape=jax.ShapeDtypeStruct((n_tiles * tm, 1, D), F32),
        compiler_params=_params("arbitrary"),
        name="experts",
    )(meta, src3, src3, h2, wgu, bgu.reshape(N_EXPERTS, 1, 2 * D_FF), wd, bd.reshape(N_EXPERTS, 1, D))


def _route(top_idx_t):
    T = top_idx_t.shape[1]
    tm = EXPERT_TILE
    M = T * TOP_K
    slot_expert = top_idx_t.reshape(M)
    order = jnp.argsort(slot_expert).astype(jnp.int32)
    rank = jnp.argsort(order).astype(jnp.int32)
    counts = jnp.bincount(slot_expert, length=N_EXPERTS)
    padded = (counts + tm - 1) // tm * tm
    pad_end = jnp.cumsum(padded)
    pad_start = pad_end - padded
    start = jnp.cumsum(counts) - counts
    n_tiles = M // tm + N_EXPERTS
    tile_start = jnp.arange(n_tiles) * tm
    tile_expert = jnp.minimum(jnp.sum(pad_end[None, :] <= tile_start[:, None], axis=1), N_EXPERTS - 1)
    tile_valid = jnp.clip(counts[tile_expert] - (tile_start - pad_start[tile_expert]), 0, tm)
    n_used = pad_end[-1] // tm
    r = jnp.arange(tm)[None, :]
    sorted_pos = (start[tile_expert] + tile_start - pad_start[tile_expert])[:, None] + r
    slot = order[jnp.clip(sorted_pos, 0, M - 1)]
    src_tok = jnp.where(r < tile_valid[:, None], slot % T, 0).astype(jnp.int32)
    src_ext = jnp.concatenate([src_tok, jnp.zeros((1, tm), jnp.int32)], axis=0)
    slot_row = (pad_start[slot_expert] + rank - start[slot_expert]).astype(jnp.int32)
    return src_ext, tile_expert.astype(jnp.int32), n_used.astype(jnp.int32), slot_row


COMBINE_TILE = 256


def _combine_kernel(row_first_ref, row_next_ref, x1_ref, w_ref, gate_ref, y_hbm, o_ref,
                    ybuf0, ybuf1, sem, *, n_steps):
    i = pl.program_id(0)
    tm = COMBINE_TILE
    n_rows = TOP_K * tm

    def fetch_row(row_ref, r, buf, s, priority=0):
        pltpu.make_async_copy(y_hbm.at[row_ref[0, 0, r]], buf.at[pl.ds(r, 1)], s).start(priority=priority)

    @pl.when(i == 0)
    def _():
        def body(r, carry):
            fetch_row(row_first_ref, r, ybuf0, sem.at[0])
            return carry
        lax.fori_loop(0, n_rows, body, 0)

    def step(ph):
        cur, nxt = (ybuf0, ybuf1) if ph == 0 else (ybuf1, ybuf0)
        mine = i % 2 == ph

        @pl.when(mine & (i + 1 < n_steps))
        def _():
            for r in range(n_rows):
                fetch_row(row_next_ref, r, nxt, sem.at[1 - ph], priority=r % 2)

        @pl.when(mine)
        def _():
            pltpu.make_async_copy(y_hbm.at[pl.ds(0, n_rows), 0], cur, sem.at[ph]).wait()
            w = w_ref[...]
            acc = w[:, 0:1] * cur[0:tm, :]
            for k in range(1, TOP_K):
                acc = acc + w[:, k:k + 1] * cur[k * tm:(k + 1) * tm, :]
            o_ref[...] = x1_ref[...] + gate_ref[0] * acc

    step(0)
    step(1)


def _combine(x1, y_sorted, slot_row, weights, gate2, S):
    T, D = x1.shape
    tm = COMBINE_TILE
    per_b = S // tm
    n_steps = T // tm
    n_rows = TOP_K * tm
    rows = slot_row.reshape(TOP_K, n_steps, tm).transpose(1, 0, 2).reshape(n_steps, 1, n_rows)
    smem_row = lambda f: pl.BlockSpec((1, 1, n_rows), f, memory_space=pltpu.SMEM)
    return pl.pallas_call(
        functools.partial(_combine_kernel, n_steps=n_steps),
        grid=(n_steps,),
        in_specs=[smem_row(lambda i: (0, 0, 0)),
                  smem_row(lambda i: (jnp.minimum(i + 1, n_steps - 1), 0, 0)),
                  pl.BlockSpec((tm, D), lambda i: (i, 0)),
                  pl.BlockSpec((tm, LANES), lambda i: (i, 0)),
                  pl.BlockSpec((1, 1, D), lambda i: (i // per_b, 0, 0)),
                  pl.BlockSpec(memory_space=pl.ANY)],
        out_specs=pl.BlockSpec((tm, D), lambda i: (i, 0)),
        out_shape=jax.ShapeDtypeStruct((T, D), F32),
        scratch_shapes=[pltpu.VMEM((n_rows, D), F32), pltpu.VMEM((n_rows, D), F32),
                        pltpu.SemaphoreType.DMA((2,))],
        compiler_params=_params("arbitrary"),
        name="combine",
    )(rows, rows, x1, weights, gate2, y_sorted)


def _pad_cols(w, n):
    return jnp.pad(w, ((0, 0), (0, n - w.shape[1])))


def _pad_rows(w, n):
    return jnp.pad(w, ((0, n - w.shape[0]), (0, 0)))


def _layer(x, c, w_ada, b_ada, norm1_g, w_in, rwkv_mu, rwkv_w0, rwkv_w_up, rwkv_a0, rwkv_a_up,
           rwkv_g_up, rwkv_k_k, rwkv_k_a, rwkv_r_k, rwkv_ln_g, rwkv_ln_b, q_norm_g, k_norm_g,
           w_out, norm2_g, w_router, b_router, w_gate_up, b_gate_up, w_down, b_down):
    B, S, D = x.shape
    T = B * S
    W = RWKV_WIDTH
    x2 = x.reshape(T, D)

    mods = _ada(c, w_ada, b_ada)
    shift1, scale1, gate1, shift2, scale2, gate2 = [
        mods[:, j * D:(j + 1) * D].reshape(B, 1, D) for j in range(6)]

    pieces = [(w_in[:, :XW_OFF], XW_OFF),
              (w_in[:, XW_OFF:XW_OFF + DECAY_LORA], LANES),
              (w_in[:, XW_OFF + DECAY_LORA:XW_OFF + DECAY_LORA + AAA_LORA], LANES),
              (w_in[:, XW_OFF + DECAY_LORA + AAA_LORA:RWKV_PROJ], 2 * LANES),
              (w_in[:, RWKV_PROJ:], MOBA_PROJ)]
    w_in_b = jnp.concatenate([_pad_cols(w, n) for w, n in pieces], axis=1).astype(BF16)
    mu_pieces = [(rwkv_mu[None, :XW_OFF], XW_OFF),
                 (rwkv_mu[None, XW_OFF:XW_OFF + DECAY_LORA], LANES),
                 (rwkv_mu[None, XW_OFF + DECAY_LORA:XW_OFF + DECAY_LORA + AAA_LORA], LANES),
                 (rwkv_mu[None, XW_OFF + DECAY_LORA + AAA_LORA:], 2 * LANES)]
    mu = jnp.concatenate([_pad_cols(m, n) for m, n in mu_pieces], axis=1)

    p_rwkv, p_moba = _in_proj(x2, norm1_g, scale1, shift1, w_in_b, S)

    row = lambda a: a.reshape(1, W)
    prm = (mu, row(rwkv_w0), row(rwkv_a0), row(rwkv_k_k), row(rwkv_k_a), row(rwkv_r_k),
           row(rwkv_ln_g), row(rwkv_ln_b), _pad_rows(rwkv_w_up, LANES), _pad_rows(rwkv_a_up, LANES),
           _pad_rows(rwkv_g_up, 2 * LANES))
    y_rwkv = _rwkv(p_rwkv, prm, B, S)
    y_moba = _moba(p_moba, q_norm_g, k_norm_g, B, S)

    x1, h2, top_idx_t, weights = _out_proj(y_rwkv.reshape(T, W), y_moba.reshape(T, MOBA_WIDTH), x2,
                                           gate1, norm2_g, scale2, shift2, w_out.astype(BF16),
                                           w_router, b_router, S)

    src_tok, tile_expert, n_used, slot_row = _route(top_idx_t)
    y_sorted = _experts(h2, src_tok, tile_expert, n_used, w_gate_up, b_gate_up, w_down, b_down)
    out = _combine(x1, y_sorted, slot_row, weights, gate2, S)
    return out.reshape(B, S, D)


def kernel(x, c, w_ada, b_ada, norm1_g, w_in, rwkv_mu, rwkv_w0, rwkv_w_up, rwkv_a0, rwkv_a_up, rwkv_g_up, rwkv_k_k, rwkv_k_a, rwkv_r_k, rwkv_ln_g, rwkv_ln_b, q_norm_g, k_norm_g, w_out, norm2_g, w_router, b_router, w_gate_up, b_gate_up, w_down, b_down):
    for l in range(w_ada.shape[0]):
        x = _layer(x, c, w_ada[l], b_ada[l], norm1_g[l], w_in[l], rwkv_mu[l], rwkv_w0[l],
                   rwkv_w_up[l], rwkv_a0[l], rwkv_a_up[l], rwkv_g_up[l], rwkv_k_k[l], rwkv_k_a[l],
                   rwkv_r_k[l], rwkv_ln_g[l], rwkv_ln_b[l], q_norm_g[l], k_norm_g[l], w_out[l],
                   norm2_g[l], w_router[l], b_router[l], w_gate_up[l], b_gate_up[l], w_down[l],
                   b_down[l])
    return x
```

```python
import functools

import jax
import jax.numpy as jnp
from jax import lax
from jax.experimental import pallas as pl
from jax.experimental.pallas import tpu as pltpu

F32 = jnp.float32
BF16 = jnp.bfloat16

D_MODEL = 1024
HEAD_DIM = 64
RWKV_WIDTH = 512
MOBA_WIDTH = 512
RWKV_HEADS = RWKV_WIDTH // HEAD_DIM
MOBA_HEADS = MOBA_WIDTH // HEAD_DIM
DECAY_LORA = 64
AAA_LORA = 64
GATE_LORA = 160
RWKV_LN_EPS = 64e-5
RWKV_PROJ = 3 * RWKV_WIDTH + DECAY_LORA + AAA_LORA + GATE_LORA
MOBA_PROJ = 3 * MOBA_WIDTH
MOBA_BLOCK = 256
MOBA_TOPK = 3
N_EXPERTS = 32
TOP_K = 4
D_FF = D_MODEL
SWIGLU_LIMIT = 7.0
SWIGLU_ALPHA = 1.702
NORM_EPS = 1e-6

LANES = 128
SUBLANES = 8
XW_OFF = 3 * RWKV_WIDTH
XA_OFF = XW_OFF + LANES
XG_OFF = XA_OFF + LANES
RWKV_COLS = XG_OFF + 2 * LANES
CHUNK = 64
EXPERT_TILE = 512
EXPERT_SUBSTEPS = 4
VMEM_LIMIT = 56 * 1024 * 1024

NN = (((1,), (0,)), ((), ()))
NT = (((1,), (1,)), ((), ()))
TN = (((0,), (0,)), ((), ()))


def _dot(a, b, dims=NN):
    return lax.dot_general(a, b, dims, preferred_element_type=F32)


def _split(a):
    hi = a.astype(BF16)
    lo = (a - hi.astype(F32)).astype(BF16)
    return hi, lo


def _mm(a, b, dims=NN, passes=1):
    if passes == 1:
        return _dot(a.astype(BF16), b.astype(BF16), dims)
    a_hi, a_lo = _split(a)
    b_hi, b_lo = _split(b)
    return _dot(a_hi, b_hi, dims) + (_dot(a_hi, b_lo, dims) + _dot(a_lo, b_hi, dims))


def _split3(a):
    hi = a.astype(BF16)
    r = a - hi.astype(F32)
    mid = r.astype(BF16)
    lo = (r - mid.astype(F32)).astype(BF16)
    return hi, mid, lo


def _mm_exact_rhs(a, b_bf16, dims=NN):
    hi, mid, lo = _split3(a)
    return _dot(hi, b_bf16, dims) + (_dot(mid, b_bf16, dims) + _dot(lo, b_bf16, dims))


def _mm_exact_lhs(a_bf16, b, dims=NN):
    hi, mid, lo = _split3(b)
    return _dot(a_bf16, hi, dims) + (_dot(a_bf16, mid, dims) + _dot(a_bf16, lo, dims))


def _iota2(shape, dim):
    return lax.broadcasted_iota(jnp.int32, shape, dim)


def _group_ones(n, group):
    return (_iota2((n, n), 0) // group == _iota2((n, n), 1) // group).astype(BF16)


def _sigmoid(x):
    return 1.0 / (1.0 + jnp.exp(-x))


def _params(*sem):
    return pltpu.CompilerParams(dimension_semantics=sem, vmem_limit_bytes=VMEM_LIMIT)


def _ada_kernel(c_ref, w_ref, b_ref, o_ref):
    c = c_ref[...]
    o_ref[...] = _mm(c * _sigmoid(c), w_ref[...], passes=3) + b_ref[...]


def _ada(c, w_ada, b_ada):
    B, D = c.shape
    n_out = w_ada.shape[1]
    tn = 1024
    return pl.pallas_call(
        _ada_kernel,
        grid=(n_out // tn,),
        in_specs=[pl.BlockSpec((B, D), lambda j: (0, 0)),
                  pl.BlockSpec((D, tn), lambda j: (0, j)),
                  pl.BlockSpec((1, tn), lambda j: (0, j))],
        out_specs=pl.BlockSpec((B, tn), lambda j: (0, j)),
        out_shape=jax.ShapeDtypeStruct((B, n_out), F32),
        compiler_params=_params("arbitrary"),
        name="ada",
    )(c, w_ada, b_ada.reshape(1, n_out))


def _rms_modulate(x, g, scale, shift):
    y = x * lax.rsqrt(jnp.mean(x * x, axis=-1, keepdims=True) + NORM_EPS)
    return (y * g) * (1.0 + scale) + shift


def _in_proj_kernel(x_ref, g_ref, scale_ref, shift_ref, w_ref, pr_ref, pm_ref):
    h = _rms_modulate(x_ref[...], g_ref[...], scale_ref[0], shift_ref[0])
    proj = _dot(h.astype(BF16), w_ref[...])
    pr_ref[...] = proj[:, :RWKV_COLS]
    pm_ref[...] = proj[:, RWKV_COLS:]


def _in_proj(x2, norm_g, scale, shift, w_in_b, S):
    T, D = x2.shape
    tm = 256
    per_b = S // tm
    n_cols = w_in_b.shape[1]
    return pl.pallas_call(
        _in_proj_kernel,
        grid=(T // tm,),
        in_specs=[pl.BlockSpec((tm, D), lambda i: (i, 0)),
                  pl.BlockSpec((1, D), lambda i: (0, 0)),
                  pl.BlockSpec((1, 1, D), lambda i: (i // per_b, 0, 0)),
                  pl.BlockSpec((1, 1, D), lambda i: (i // per_b, 0, 0)),
                  pl.BlockSpec((D, n_cols), lambda i: (0, 0))],
        out_specs=[pl.BlockSpec((tm, RWKV_COLS), lambda i: (i, 0)),
                   pl.BlockSpec((tm, MOBA_PROJ), lambda i: (i, 0))],
        out_shape=[jax.ShapeDtypeStruct((T, RWKV_COLS), F32),
                   jax.ShapeDtypeStruct((T, MOBA_PROJ), F32)],
        compiler_params=_params("arbitrary"),
        name="in_proj",
    )(x2, norm_g.reshape(1, D), scale, shift, w_in_b)


RWKV_TILE = 256


def _unit_lower_inverses(Ls):
    n = Ls[0].shape[0]
    r = _iota2((n, n), 0)
    c = _iota2((n, n), 1)
    eye = (r == c).astype(F32)
    in_block = r // 8 == c // 8
    b16 = lambda xs: [x.astype(BF16) for x in xs]
    Ld = [jnp.where(in_block, L, 0.0) for L in Ls]
    Ld_b = b16(Ld)
    Nb_b = [(L - d).astype(BF16) for L, d in zip(Ls, Ld)]
    L2 = [_dot(d, d) for d in Ld_b]
    L2_b = b16(L2)
    L4_b = b16([_dot(x, x) for x in L2_b])
    T0 = [eye + d + l2 + _dot(db, l2b) for d, l2, db, l2b in zip(Ld, L2, Ld_b, L2_b)]
    T0 = [t + _dot(t.astype(BF16), l4b) for t, l4b in zip(T0, L4_b)]
    T0_b = b16(T0)
    M1_b = b16([_dot(t, nb) for t, nb in zip(T0_b, Nb_b)])
    M2_b = b16([_dot(m, m) for m in M1_b])
    M4_b = b16([_dot(m, m) for m in M2_b])
    X = [t + _dot(m4, tb) for t, m4, tb in zip(T0, M4_b, T0_b)]
    X = [x + _dot(m2, x.astype(BF16)) for x, m2 in zip(X, M2_b)]
    return [x + _dot(m1, x.astype(BF16)) for x, m1 in zip(X, M1_b)]


def _rwkv_kernel(p_ref, mu_ref, w0_ref, a0_ref, kk_ref, ka_ref, rk_ref, lng_ref, lnb_ref,
                 wup_ref, aup_ref, gup_ref, y_ref, carry_ref, state_ref, *, ts):
    t = pl.program_id(1)
    W = RWKV_WIDTH

    @pl.when(t == 0)
    def _():
        carry_ref[...] = jnp.zeros_like(carry_ref)
        state_ref[...] = jnp.zeros_like(state_ref)

    p = p_ref[0]
    row = _iota2((ts, 1), 0)
    prev = jnp.where(row == 0, carry_ref[...], pltpu.roll(p, 1, 0))
    carry_ref[...] = p[ts - 1:ts, :]
    pm = p + (prev - p) * mu_ref[...]
    r = pm[:, 0:W]
    k = pm[:, W:2 * W]
    v = pm[:, 2 * W:3 * W]
    xw = pm[:, XW_OFF:XA_OFF]
    xa = pm[:, XA_OFF:XG_OFF]
    xg = pm[:, XG_OFF:RWKV_COLS]

    z = -(w0_ref[...] + _mm(jnp.tanh(xw), wup_ref[...], passes=3))
    softplus = jnp.maximum(z, 0.0) + jnp.log(1.0 + jnp.exp(-jnp.abs(z)))
    logd = -jnp.exp(-softplus - 0.5)
    alpha = _sigmoid(a0_ref[...] + _mm(xa, aup_ref[...], passes=3))
    gate = _mm(_sigmoid(xg), gup_ref[...])

    head_ones = _group_ones(W, HEAD_DIM)
    kk = k * kk_ref[...]
    kk_norm = jnp.sqrt(_mm_exact_rhs(kk * kk, head_ones))
    kk = kk / jnp.maximum(kk_norm, 1e-12)
    kmod = k * (1.0 + (alpha - 1.0) * ka_ref[...])
    bonus = _mm_exact_rhs(r * kmod * rk_ref[...], head_ones) * v

    tr = _iota2((ts, ts), 0)
    tc = _iota2((ts, ts), 1)
    cum = ((tr // CHUNK == tc // CHUNK) & (tc <= tr)).astype(BF16)
    logp = _mm_exact_lhs(cum, logd)
    inv_p = jnp.exp(-logp)
    a_t = -kk * jnp.exp(logp - logd)
    b_t = kk * alpha * inv_p
    k_t = kmod * inv_p
    r_t = r * jnp.exp(logp)

    n2 = 2 * CHUNK
    n_chunks = ts // CHUNK
    n_pairs = RWKV_HEADS // 2
    inst = [(ci, pi) for ci in range(n_chunks) for pi in range(n_pairs)]
    lane = _iota2((1, LANES), 1)
    m0 = lane < HEAD_DIM
    sr = _iota2((n2, n2), 0)
    sc = _iota2((n2, n2), 1)
    strict_lower = sc < sr
    incl_lower = sc <= sr

    def stacked(x, ci, pi):
        xt = x[ci * CHUNK:(ci + 1) * CHUNK, pi * LANES:(pi + 1) * LANES]
        return jnp.concatenate([jnp.where(m0, xt, 0.0), jnp.where(m0, 0.0, xt)], axis=0)

    pcs = [jnp.exp(logp[(ci + 1) * CHUNK - 1:(ci + 1) * CHUNK, pi * LANES:(pi + 1) * LANES])
           for ci, pi in inst]
    r_s = [stacked(r_t, ci, pi) for ci, pi in inst]
    a_b = [stacked(a_t, ci, pi).astype(BF16) for ci, pi in inst]
    r_b = [x.astype(BF16) for x in r_s]
    b_s = [stacked(b_t, ci, pi) for ci, pi in inst]
    k_s = [stacked(k_t, ci, pi) for ci, pi in inst]
    v_b = [stacked(v, ci, pi).astype(BF16) for ci, pi in inst]
    ar_b = [jnp.concatenate([a, rr], axis=0) for a, rr in zip(a_b, r_b)]
    bk_b = [jnp.concatenate([b.astype(BF16), kx.astype(BF16)], axis=0) for b, kx in zip(b_s, k_s)]
    gram = [_dot(ar, bk, NT) for ar, bk in zip(ar_b, bk_b)]
    l_ab = [jnp.where(strict_lower, g[:n2, :n2], 0.0) for g in gram]
    l_ak_b = [jnp.where(strict_lower, g[:n2, n2:], 0.0).astype(BF16) for g in gram]
    m_rb_b = [jnp.where(incl_lower, g[n2:, :n2], 0.0).astype(BF16) for g in gram]
    m_rk_b = [jnp.where(incl_lower, g[n2:, n2:], 0.0).astype(BF16) for g in gram]
    t_b = [x.astype(BF16) for x in _unit_lower_inverses(l_ab)]
    lakv_b = [_dot(l, vv).astype(BF16) for l, vv in zip(l_ak_b, v_b)]
    wu_b = [_dot(tb, jnp.concatenate([a, lv], axis=1)).astype(BF16)
            for tb, a, lv in zip(t_b, a_b, lakv_b)]
    mwu = [_dot(m, wu) for m, wu in zip(m_rb_b, wu_b)]
    q_b = [(rs + x[:, :LANES]).astype(BF16) for rs, x in zip(r_s, mwu)]
    y0 = [x[:, LANES:] + _dot(m, vv) for x, m, vv in zip(mwu, m_rk_b, v_b)]
    bp_b = [(b * pc).astype(BF16) for b, pc in zip(b_s, pcs)]
    kp_b = [(kx * pc).astype(BF16) for kx, pc in zip(k_s, pcs)]
    gh = [_dot(wu, bp, TN) for wu, bp in zip(wu_b, bp_b)]
    g_b = [x[:LANES].astype(BF16) for x in gh]
    h = [x[LANES:] + _dot(vv, kp, TN) for x, vv, kp in zip(gh, v_b, kp_b)]

    y_chunks = []
    for ci in range(n_chunks):
        ids = [ci * n_pairs + pi for pi in range(n_pairs)]
        s0 = [state_ref[pi] for pi in range(n_pairs)]
        s0_b = [s.astype(BF16) for s in s0]
        y_s = [_dot(q_b[n], sb, NT) + y0[n] for n, sb in zip(ids, s0_b)]
        s1 = [s * pcs[n] + _dot(sb, g_b[n]) + h[n] for n, s, sb in zip(ids, s0, s0_b)]
        for pi in range(n_pairs):
            state_ref[pi] = s1[pi]
        y_chunks.append(jnp.concatenate([x[:CHUNK] + x[CHUNK:] for x in y_s], axis=1))
    y = jnp.concatenate(y_chunks, axis=0) if n_chunks > 1 else y_chunks[0]

    mean = _mm_exact_rhs(y, head_ones) * (1.0 / HEAD_DIM)
    yc = y - mean
    var = _mm_exact_rhs(yc * yc, head_ones) * (1.0 / HEAD_DIM)
    yn = yc * lax.rsqrt(var + RWKV_LN_EPS) * lng_ref[...] + lnb_ref[...]
    y_ref[0] = (yn + bonus) * gate


def _rwkv(p_rwkv, prm, B, S, ts=RWKV_TILE):
    W = RWKV_WIDTH
    vec = lambda n: pl.BlockSpec((1, n), lambda b, t: (0, 0))
    mat = lambda m, n: pl.BlockSpec((m, n), lambda b, t: (0, 0))
    return pl.pallas_call(
        functools.partial(_rwkv_kernel, ts=ts),
        grid=(B, S // ts),
        in_specs=[pl.BlockSpec((1, ts, RWKV_COLS), lambda b, t: (b, t, 0)),
                  vec(RWKV_COLS)] + [vec(W)] * 7 + [mat(LANES, W), mat(LANES, W), mat(2 * LANES, W)],
        out_specs=pl.BlockSpec((1, ts, W), lambda b, t: (b, t, 0)),
        out_shape=jax.ShapeDtypeStruct((B, S, W), F32),
        scratch_shapes=[pltpu.VMEM((1, RWKV_COLS), F32),
                        pltpu.VMEM((RWKV_HEADS // 2, LANES, LANES), F32)],
        compiler_params=_params("arbitrary", "arbitrary"),
        name="rwkv",
    )(p_rwkv.reshape(B, S, RWKV_COLS), *prm)


def _moba_kernel(slopes_ref, q_ref, k_ref, v_ref, qg_ref, kg_ref, o_ref, *, S):
    pair = pl.program_id(1)
    NB = S // MOBA_BLOCK
    BLK = MOBA_BLOCK
    n_sel = min(MOBA_TOPK, NB)
    scale = HEAD_DIM ** -0.5
    head_ones = _group_ones(LANES, HEAD_DIM)
    lane = _iota2((1, LANES), 1)

    def head_norm(x, g):
        ss = _mm_exact_rhs(x * x, head_ones)
        return x * lax.rsqrt(ss * (1.0 / HEAD_DIM) + NORM_EPS) * g

    qn = head_norm(q_ref[0], qg_ref[...])
    kn = head_norm(k_ref[0], kg_ref[...])
    kmean = jnp.mean(kn.reshape(NB, BLK, LANES), axis=1)
    q_t = qn.T
    v_tb = v_ref[0].T.astype(BF16)

    blk_of_q = _iota2((1, S), 1) // BLK
    nidx = _iota2((NB, 1), 0)
    valid = nidx < blk_of_q
    q_blk = (blk_of_q * BLK).astype(F32)
    q_loc = (_iota2((1, S), 1) % BLK).astype(F32)
    k_blk = (_iota2((S, 1), 0) // BLK * BLK).astype(F32)
    k_loc = (_iota2((S, 1), 0) % BLK).astype(F32)
    q_lane = _iota2((LANES, 1), 0)
    causal = _iota2((BLK, BLK), 0) <= _iota2((BLK, BLK), 1)

    out_rows = []
    for h in range(2):
        hmask = (lane // HEAD_DIM) == h
        slope = slopes_ref[pair * 2 + h]
        spare = (1 - h) * HEAD_DIM
        k_aug = jnp.where(hmask, kn, 0.0)
        k_aug = jnp.where(lane == spare, slope * k_blk, k_aug)
        k_aug = jnp.where(lane == spare + 1, slope * k_loc, k_aug)
        k_aug = jnp.where((lane == spare + 2) | (lane == spare + 3), 1.0, k_aug)
        q_aug = q_t * scale
        q_aug = jnp.where((q_lane == spare) | (q_lane == spare + 1), 1.0, q_aug)
        q_aug = jnp.where(q_lane == spare + 2, -slope * q_blk, q_aug)
        q_aug = jnp.where(q_lane == spare + 3, -slope * q_loc, q_aug)
        q_hb = q_aug.astype(BF16)
        gate = _mm(jnp.where(hmask, kmean, 0.0), q_t, passes=3)
        gate = jnp.where(valid, gate, -jnp.inf)
        rank = jnp.zeros((NB, S), jnp.int32)
        for m in range(NB):
            gm = gate[m:m + 1, :]
            ahead = (gm > gate) | ((gm == gate) & (m < nidx))
            rank = rank + ahead.astype(jnp.int32)
        sel = valid & (rank < n_sel)
        k_hb = k_aug.astype(BF16)
        v_h = v_tb[h * HEAD_DIM:(h + 1) * HEAD_DIM, :]
        out_blocks = []
        for i in range(NB):
            qs = slice(i * BLK, (i + 1) * BLK)
            tiles = []
            m_run = None
            for n in range(i + 1):
                ks = slice(n * BLK, (n + 1) * BLK)
                s = _dot(k_hb[ks, :], q_hb[:, qs])
                if n < i:
                    s = jnp.where(sel[n:n + 1, qs], s, -jnp.inf)
                else:
                    s = jnp.where(causal, s, -jnp.inf)
                tiles.append(s)
                mx = jnp.max(s, axis=0, keepdims=True)
                m_run = mx if m_run is None else jnp.maximum(m_run, mx)
            l_run = jnp.zeros((1, BLK), F32)
            acc = jnp.zeros((HEAD_DIM, BLK), F32)
            for n in range(i + 1):
                ks = slice(n * BLK, (n + 1) * BLK)
                pt = jnp.exp(tiles[n] - m_run)
                l_run = l_run + jnp.sum(pt, axis=0, keepdims=True)
                acc = acc + _dot(v_h[:, ks], pt.astype(BF16))
            out_blocks.append(acc / l_run)
        out_rows.append(jnp.concatenate(out_blocks, axis=1))
    o_ref[0] = jnp.concatenate(out_rows, axis=0).T


def _moba(p_moba, q_norm_g, k_norm_g, B, S):
    pairs = MOBA_HEADS // 2
    col = lambda off: pl.BlockSpec((1, S, LANES), lambda b, p: (b, 0, off + p))
    gain = pl.BlockSpec((1, LANES), lambda b, p: (0, 0))
    tile2 = lambda g: jnp.concatenate([g, g]).reshape(1, LANES)
    p3 = p_moba.reshape(B, S, MOBA_PROJ)
    slopes = jnp.exp2(-8.0 * (jnp.arange(MOBA_HEADS, dtype=F32) + 1.0) / MOBA_HEADS)
    return pl.pallas_call(
        functools.partial(_moba_kernel, S=S),
        grid=(B, pairs),
        in_specs=[pl.BlockSpec(memory_space=pltpu.SMEM), col(0), col(pairs), col(2 * pairs),
                  gain, gain],
        out_specs=pl.BlockSpec((1, S, LANES), lambda b, p: (b, 0, p)),
        out_shape=jax.ShapeDtypeStruct((B, S, MOBA_WIDTH), F32),
        compiler_params=_params("arbitrary", "arbitrary"),
        name="moba",
    )(slopes, p3, p3, p3, tile2(q_norm_g), tile2(k_norm_g))


def _out_proj_kernel(yr_ref, ym_ref, x_ref, gate_ref, g_ref, scale_ref, shift_ref, w_ref,
                     wr_ref, br_ref, x1_ref, h2_ref, idx_ref, wgt_ref):
    W = RWKV_WIDTH
    mix = (_dot(yr_ref[...].astype(BF16), w_ref[0:W, :])
           + _dot(ym_ref[...].astype(BF16), w_ref[W:, :]))
    x1 = x_ref[...] + gate_ref[0] * mix
    x1_ref[...] = x1
    h2 = _rms_modulate(x1, g_ref[...], scale_ref[0], shift_ref[0])
    h2_ref[:, 0, :] = h2
    logits_t = (_mm(h2, wr_ref[...], passes=3) + br_ref[...]).T[:N_EXPERTS, :]
    tm = logits_t.shape[1]
    eidx = _iota2((N_EXPERTS, 1), 0)
    vals, idxs = [], []
    for _ in range(TOP_K):
        m = jnp.max(logits_t, axis=0, keepdims=True)
        idx = jnp.min(jnp.where(logits_t == m, eidx, N_EXPERTS), axis=0, keepdims=True)
        vals.append(m)
        idxs.append(idx)
        logits_t = jnp.where(eidx == idx, -jnp.inf, logits_t)
    idx_ref[...] = jnp.concatenate(idxs, axis=0)
    e = [jnp.exp(v - vals[0]) for v in vals]
    total = e[0] + e[1] + e[2] + e[3]
    wgt_t = jnp.concatenate([x / total for x in e] + [jnp.zeros((LANES - TOP_K, tm), F32)], axis=0)
    wgt_ref[...] = wgt_t.T


def _out_proj(y_rwkv, y_moba, x2, gate1, norm_g, scale, shift, w_out_b, w_router, b_router, S):
    T, D = x2.shape
    tm = 256
    per_b = S // tm
    rows = lambda n: pl.BlockSpec((tm, n), lambda i: (i, 0))
    mod = pl.BlockSpec((1, 1, D), lambda i: (i // per_b, 0, 0))
    full = lambda m, n: pl.BlockSpec((m, n), lambda i: (0, 0))
    wr = _pad_cols(w_router, LANES)
    br = jnp.concatenate([b_router, jnp.full((LANES - N_EXPERTS,), -jnp.inf, F32)]).reshape(1, LANES)
    return pl.pallas_call(
        _out_proj_kernel,
        grid=(T // tm,),
        in_specs=[rows(RWKV_WIDTH), rows(MOBA_WIDTH), rows(D), mod, full(1, D), mod, mod,
                  full(D, D), full(D, LANES), full(1, LANES)],
        out_specs=[rows(D), pl.BlockSpec((tm, 1, D), lambda i: (i, 0, 0)),
                   pl.BlockSpec((TOP_K, tm), lambda i: (0, i)), rows(LANES)],
        out_shape=[jax.ShapeDtypeStruct((T, D), F32), jax.ShapeDtypeStruct((T, 1, D), F32),
                   jax.ShapeDtypeStruct((TOP_K, T), jnp.int32), jax.ShapeDtypeStruct((T, LANES), F32)],
        compiler_params=_params("arbitrary"),
        name="out_proj",
    )(y_rwkv, y_moba, x2, gate1, norm_g.reshape(1, D), scale, shift, w_out_b, wr, br)


def _experts_kernel(meta_ref, src_first_ref, src_next_ref, h_hbm, wgu_ref, bgu_ref, wd_ref, bd_ref,
                    y_hbm, xbuf0, xbuf1, obuf0, obuf1, xb16, wgu16, wd16, gsem, wsem,
                    *, n_tiles):
    j = pl.program_id(0)
    n_used = meta_ref[0]
    tm = EXPERT_TILE
    n_sub = EXPERT_SUBSTEPS
    cols_sub = D_FF // n_sub
    expert = lambda i: meta_ref[1 + jnp.clip(i, 0, n_tiles - 1)]

    def gather_row(src_ref, r, dst_buf, sem, priority=0):
        pltpu.make_async_copy(h_hbm.at[src_ref[0, 0, r]], dst_buf.at[pl.ds(r, 1)], sem).start(
            priority=priority)

    def wait_tile_gather(buf, sem):
        pltpu.make_async_copy(h_hbm.at[pl.ds(0, tm), 0], buf, sem).wait()

    def write_back(buf, tile, sem):
        return pltpu.make_async_copy(buf, y_hbm.at[pl.ds(tile * tm, tm), 0], sem)

    @pl.when(j == 0)
    def _():
        def body(r, carry):
            gather_row(src_first_ref, r, xbuf0, gsem.at[0])
            return carry
        lax.fori_loop(0, tm, body, 0)

    @pl.when((j < n_used) & ((j == 0) | (expert(j) != expert(j - 1))))
    def _():
        wgu16[...] = wgu_ref[0].astype(BF16)
        wd16[...] = wd_ref[0].astype(BF16)

    def ffn_chunk(n, o_cur):
        cg = slice(n * cols_sub, (n + 1) * cols_sub)
        cu = slice(D_FF + n * cols_sub, D_FF + (n + 1) * cols_sub)
        xb = xb16[...]
        gate = jnp.minimum(_dot(xb, wgu16[:, cg]) + bgu_ref[0, :, cg], SWIGLU_LIMIT)
        up = jnp.clip(_dot(xb, wgu16[:, cu]) + bgu_ref[0, :, cu], -SWIGLU_LIMIT, SWIGLU_LIMIT)
        act = (up + 1.0) * gate * _sigmoid(SWIGLU_ALPHA * gate)
        o_cur[...] += _dot(act.astype(BF16), wd16[cg, :])

    def tile_step(ph):
        x_cur, x_nxt = (xbuf0, xbuf1) if ph == 0 else (xbuf1, xbuf0)
        o_cur, o_prv = (obuf0, obuf1) if ph == 0 else (obuf1, obuf0)
        mine = j % 2 == ph

        @pl.when(mine & (j < n_used))
        def _():
            wait_tile_gather(x_cur, gsem.at[ph])

            @pl.when(j >= 2)
            def _():
                write_back(o_cur, j - 2, wsem.at[ph]).wait()

            for r in range(tm):
                gather_row(src_next_ref, r, x_nxt, gsem.at[1 - ph], priority=1)
            xb16[...] = x_cur[...].astype(BF16)
            o_cur[...] = jnp.broadcast_to(bd_ref[0], (tm, D_MODEL))

        @pl.when(mine & (j + 1 <= n_used))
        def _():
            for n in range(n_sub):
                ffn_chunk(n, o_cur)
            write_back(o_cur, j, wsem.at[ph]).start()

        @pl.when(mine & (j == n_used))
        def _():
            wait_tile_gather(x_cur, gsem.at[ph])
            write_back(o_prv, j - 1, wsem.at[1 - ph]).wait()

            @pl.when(j >= 2)
            def _():
                write_back(o_cur, j - 2, wsem.at[ph]).wait()

        @pl.when(mine & (j >= n_used) & (j < n_tiles))
        def _():
            o_cur[...] = jnp.zeros_like(o_cur)
            fill = write_back(o_cur, j, wsem.at[ph])
            fill.start()
            fill.wait()

    tile_step(0)
    tile_step(1)


def _experts(h2, src_tok, tile_expert, n_used, wgu, bgu, wd, bd):
    D = h2.shape[-1]
    tm = EXPERT_TILE
    n_tiles = tile_expert.shape[0]
    src3 = src_tok.reshape(n_tiles + 1, 1, tm)
    meta = jnp.concatenate([n_used.reshape(1), tile_expert]).astype(jnp.int32)
    smem_row = lambda f: pl.BlockSpec((1, 1, tm), f, memory_space=pltpu.SMEM)
    e_of = lambda j, m: m[1 + jnp.minimum(j, n_tiles - 1)]
    grid_spec = pltpu.PrefetchScalarGridSpec(
        num_scalar_prefetch=1,
        grid=(n_tiles + 1,),
        in_specs=[smem_row(lambda j, m: (0, 0, 0)),
                  smem_row(lambda j, m: (jnp.minimum(j + 1, n_tiles), 0, 0)),
                  pl.BlockSpec(memory_space=pl.ANY),
                  pl.BlockSpec((1, D, 2 * D_FF), lambda j, m: (e_of(j, m), 0, 0)),
                  pl.BlockSpec((1, 1, 2 * D_FF), lambda j, m: (e_of(j, m), 0, 0)),
                  pl.BlockSpec((1, D_FF, D), lambda j, m: (e_of(j, m), 0, 0)),
                  pl.BlockSpec((1, 1, D), lambda j, m: (e_of(j, m), 0, 0))],
        out_specs=pl.BlockSpec(memory_space=pl.ANY),
        scratch_shapes=[pltpu.VMEM((tm, D), F32), pltpu.VMEM((tm, D), F32),
                        pltpu.VMEM((tm, D), F32), pltpu.VMEM((tm, D), F32),
                        pltpu.VMEM((tm, D), BF16),
                        pltpu.VMEM((D, 2 * D_FF), BF16), pltpu.VMEM((D_FF, D), BF16),
                        pltpu.SemaphoreType.DMA((2,)), pltpu.SemaphoreType.DMA((2,))],
    )
    return pl.pallas_call(
        functools.partial(_experts_kernel, n_tiles=n_tiles),
        grid_spec=grid_spec,
        out_shape=jax.ShapeDtypeStruct((n_tiles * tm, 1, D), F32),
        compiler_params=_params("arbitrary"),
        name="experts",
    )(meta, src3, src3, h2, wgu, bgu.reshape(N_EXPERTS, 1, 2 * D_FF), wd, bd.reshape(N_EXPERTS, 1, D))


def _route(top_idx_t):
    T = top_idx_t.shape[1]
    tm = EXPERT_TILE
    M = T * TOP_K
    slot_expert = top_idx_t.reshape(M)
    order = jnp.argsort(slot_expert).astype(jnp.int32)
    rank = jnp.argsort(order).astype(jnp.int32)
    counts = jnp.bincount(slot_expert, length=N_EXPERTS)
    padded = (counts + tm - 1) // tm * tm
    pad_end = jnp.cumsum(padded)
    pad_start = pad_end - padded
    start = jnp.cumsum(counts) - counts
    n_tiles = M // tm + N_EXPERTS
    tile_start = jnp.arange(n_tiles) * tm
    tile_expert = jnp.minimum(jnp.sum(pad_end[None, :] <= tile_start[:, None], axis=1), N_EXPERTS - 1)
    tile_valid = jnp.clip(counts[tile_expert] - (tile_start - pad_start[tile_expert]), 0, tm)
    n_used = pad_end[-1] // tm
    r = jnp.arange(tm)[None, :]
    sorted_pos = (start[tile_expert] + tile_start - pad_start[tile_expert])[:, None] + r
    slot = order[jnp.clip(sorted_pos, 0, M - 1)]
    src_tok = jnp.where(r < tile_valid[:, None], slot % T, 0).astype(jnp.int32)
    src_ext = jnp.concatenate([src_tok, jnp.zeros((1, tm), jnp.int32)], axis=0)
    slot_row = (pad_start[slot_expert] + rank - start[slot_expert]).astype(jnp.int32)
    return src_ext, tile_expert.astype(jnp.int32), n_used.astype(jnp.int32), slot_row


COMBINE_TILE = 512


def _combine_kernel(row_first_ref, row_next_ref, x1_ref, w_ref, gate_ref, y_hbm, o_ref,
                    ybuf0, ybuf1, sem, *, n_steps):
    i = pl.program_id(0)
    tm = COMBINE_TILE
    n_rows = TOP_K * tm

    def fetch_row(row_ref, r, buf, s, priority=0):
        pltpu.make_async_copy(y_hbm.at[row_ref[0, 0, r]], buf.at[pl.ds(r, 1)], s).start(priority=priority)

    @pl.when(i == 0)
    def _():
        def body(r, carry):
            fetch_row(row_first_ref, r, ybuf0, sem.at[0])
            return carry
        lax.fori_loop(0, n_rows, body, 0)

    def step(ph):
        cur, nxt = (ybuf0, ybuf1) if ph == 0 else (ybuf1, ybuf0)
        mine = i % 2 == ph

        @pl.when(mine & (i + 1 < n_steps))
        def _():
            for r in range(n_rows):
                fetch_row(row_next_ref, r, nxt, sem.at[1 - ph], priority=r % 2)

        @pl.when(mine)
        def _():
            pltpu.make_async_copy(y_hbm.at[pl.ds(0, n_rows), 0], cur, sem.at[ph]).wait()
            w = w_ref[...]
            acc = w[:, 0:1] * cur[0:tm, :]
            for k in range(1, TOP_K):
                acc = acc + w[:, k:k + 1] * cur[k * tm:(k + 1) * tm, :]
            o_ref[...] = x1_ref[...] + gate_ref[0] * acc

    step(0)
    step(1)


def _combine(x1, y_sorted, slot_row, weights, gate2, S):
    T, D = x1.shape
    tm = COMBINE_TILE
    per_b = S // tm
    n_steps = T // tm
    n_rows = TOP_K * tm
    rows = slot_row.reshape(TOP_K, n_steps, tm).transpose(1, 0, 2).reshape(n_steps, 1, n_rows)
    smem_row = lambda f: pl.BlockSpec((1, 1, n_rows), f, memory_space=pltpu.SMEM)
    return pl.pallas_call(
        functools.partial(_combine_kernel, n_steps=n_steps),
        grid=(n_steps,),
        in_specs=[smem_row(lambda i: (0, 0, 0)),
                  smem_row(lambda i: (jnp.minimum(i + 1, n_steps - 1), 0, 0)),
                  pl.BlockSpec((tm, D), lambda i: (i, 0)),
                  pl.BlockSpec((tm, LANES), lambda i: (i, 0)),
                  pl.BlockSpec((1, 1, D), lambda i: (i // per_b, 0, 0)),
                  pl.BlockSpec(memory_space=pl.ANY)],
        out_specs=pl.BlockSpec((tm, D), lambda i: (i, 0)),
        out_shape=jax.ShapeDtypeStruct((T, D), F32),
        scratch_shapes=[pltpu.VMEM((n_rows, D), F32), pltpu.VMEM((n_rows, D), F32),
                        pltpu.SemaphoreType.DMA((2,))],
        compiler_params=_params("arbitrary"),
        name="combine",
    )(rows, rows, x1, weights, gate2, y_sorted)


def _pad_cols(w, n):
    return jnp.pad(w, ((0, 0), (0, n - w.shape[1])))


def _pad_rows(w, n):
    return jnp.pad(w, ((0, n - w.shape[0]), (0, 0)))


def _layer(x, c, w_ada, b_ada, norm1_g, w_in, rwkv_mu, rwkv_w0, rwkv_w_up, rwkv_a0, rwkv_a_up,
           rwkv_g_up, rwkv_k_k, rwkv_k_a, rwkv_r_k, rwkv_ln_g, rwkv_ln_b, q_norm_g, k_norm_g,
           w_out, norm2_g, w_router, b_router, w_gate_up, b_gate_up, w_down, b_down):
    B, S, D = x.shape
    T = B * S
    W = RWKV_WIDTH
    x2 = x.reshape(T, D)

    mods = _ada(c, w_ada, b_ada)
    shift1, scale1, gate1, shift2, scale2, gate2 = [
        mods[:, j * D:(j + 1) * D].reshape(B, 1, D) for j in range(6)]

    pieces = [(w_in[:, :XW_OFF], XW_OFF),
              (w_in[:, XW_OFF:XW_OFF + DECAY_LORA], LANES),
              (w_in[:, XW_OFF + DECAY_LORA:XW_OFF + DECAY_LORA + AAA_LORA], LANES),
              (w_in[:, XW_OFF + DECAY_LORA + AAA_LORA:RWKV_PROJ], 2 * LANES),
              (w_in[:, RWKV_PROJ:], MOBA_PROJ)]
    w_in_b = jnp.concatenate([_pad_cols(w, n) for w, n in pieces], axis=1).astype(BF16)
    mu_pieces = [(rwkv_mu[None, :XW_OFF], XW_OFF),
                 (rwkv_mu[None, XW_OFF:XW_OFF + DECAY_LORA], LANES),
                 (rwkv_mu[None, XW_OFF + DECAY_LORA:XW_OFF + DECAY_LORA + AAA_LORA], LANES),
                 (rwkv_mu[None, XW_OFF + DECAY_LORA + AAA_LORA:], 2 * LANES)]
    mu = jnp.concatenate([_pad_cols(m, n) for m, n in mu_pieces], axis=1)

    p_rwkv, p_moba = _in_proj(x2, norm1_g, scale1, shift1, w_in_b, S)

    row = lambda a: a.reshape(1, W)
    prm = (mu, row(rwkv_w0), row(rwkv_a0), row(rwkv_k_k), row(rwkv_k_a), row(rwkv_r_k),
           row(rwkv_ln_g), row(rwkv_ln_b), _pad_rows(rwkv_w_up, LANES), _pad_rows(rwkv_a_up, LANES),
           _pad_rows(rwkv_g_up, 2 * LANES))
    y_rwkv = _rwkv(p_rwkv, prm, B, S)
    y_moba = _moba(p_moba, q_norm_g, k_norm_g, B, S)

    x1, h2, top_idx_t, weights = _out_proj(y_rwkv.reshape(T, W), y_moba.reshape(T, MOBA_WIDTH), x2,
                                           gate1, norm2_g, scale2, shift2, w_out.astype(BF16),
                                           w_router, b_router, S)

    src_tok, tile_expert, n_used, slot_row = _route(top_idx_t)
    y_sorted = _experts(h2, src_tok, tile_expert, n_used, w_gate_up, b_gate_up, w_down, b_down)
    out = _combine(x1, y_sorted, slot_row, weights, gate2, S)
    return out.reshape(B, S, D)


def kernel(x, c, w_ada, b_ada, norm1_g, w_in, rwkv_mu, rwkv_w0, rwkv_w_up, rwkv_a0, rwkv_a_up, rwkv_g_up, rwkv_k_k, rwkv_k_a, rwkv_r_k, rwkv_ln_g, rwkv_ln_b, q_norm_g, k_norm_g, w_out, norm2_g, w_router, b_router, w_gate_up, b_gate_up, w_down, b_down):
    for l in range(w_ada.shape[0]):
        x = _layer(x, c, w_ada[l], b_ada[l], norm1_g[l], w_in[l], rwkv_mu[l], rwkv_w0[l],
                   rwkv_w_up[l], rwkv_a0[l], rwkv_a_up[l], rwkv_g_up[l], rwkv_k_k[l], rwkv_k_a[l],
                   rwkv_r_k[l], rwkv_ln_g[l], rwkv_ln_b[l], q_norm_g[l], k_norm_g[l], w_out[l],
                   norm2_g[l], w_router[l], b_router[l], w_gate_up[l], b_gate_up[l], w_down[l],
                   b_down[l])
    return x
```

```python
import functools

import jax
import jax.numpy as jnp
from jax import lax
from jax.experimental import pallas as pl
from jax.experimental.pallas import tpu as pltpu

F32 = jnp.float32
BF16 = jnp.bfloat16

D_MODEL = 1024
HEAD_DIM = 64
RWKV_WIDTH = 512
MOBA_WIDTH = 512
RWKV_HEADS = RWKV_WIDTH // HEAD_DIM
MOBA_HEADS = MOBA_WIDTH // HEAD_DIM
DECAY_LORA = 64
AAA_LORA = 64
GATE_LORA = 160
RWKV_LN_EPS = 64e-5
RWKV_PROJ = 3 * RWKV_WIDTH + DECAY_LORA + AAA_LORA + GATE_LORA
MOBA_PROJ = 3 * MOBA_WIDTH
MOBA_BLOCK = 256
MOBA_TOPK = 3
N_EXPERTS = 32
TOP_K = 4
D_FF = D_MODEL
SWIGLU_LIMIT = 7.0
SWIGLU_ALPHA = 1.702
NORM_EPS = 1e-6

LANES = 128
SUBLANES = 8
XW_OFF = 3 * RWKV_WIDTH
XA_OFF = XW_OFF + LANES
XG_OFF = XA_OFF + LANES
RWKV_COLS = XG_OFF + 2 * LANES
CHUNK = 64
EXPERT_TILE = 1024
EXPERT_SUBSTEPS = 4
VMEM_LIMIT = 56 * 1024 * 1024

NN = (((1,), (0,)), ((), ()))
NT = (((1,), (1,)), ((), ()))
TN = (((0,), (0,)), ((), ()))


def _dot(a, b, dims=NN):
    return lax.dot_general(a, b, dims, preferred_element_type=F32)


def _split(a):
    hi = a.astype(BF16)
    lo = (a - hi.astype(F32)).astype(BF16)
    return hi, lo


def _mm(a, b, dims=NN, passes=1):
    if passes == 1:
        return _dot(a.astype(BF16), b.astype(BF16), dims)
    a_hi, a_lo = _split(a)
    b_hi, b_lo = _split(b)
    return _dot(a_hi, b_hi, dims) + (_dot(a_hi, b_lo, dims) + _dot(a_lo, b_hi, dims))


def _split3(a):
    hi = a.astype(BF16)
    r = a - hi.astype(F32)
    mid = r.astype(BF16)
    lo = (r - mid.astype(F32)).astype(BF16)
    return hi, mid, lo


def _mm_exact_rhs(a, b_bf16, dims=NN):
    hi, mid, lo = _split3(a)
    return _dot(hi, b_bf16, dims) + (_dot(mid, b_bf16, dims) + _dot(lo, b_bf16, dims))


def _mm_exact_lhs(a_bf16, b, dims=NN):
    hi, mid, lo = _split3(b)
    return _dot(a_bf16, hi, dims) + (_dot(a_bf16, mid, dims) + _dot(a_bf16, lo, dims))


def _iota2(shape, dim):
    return lax.broadcasted_iota(jnp.int32, shape, dim)


def _group_ones(n, group):
    return (_iota2((n, n), 0) // group == _iota2((n, n), 1) // group).astype(BF16)


def _sigmoid(x):
    return 1.0 / (1.0 + jnp.exp(-x))


def _params(*sem):
    return pltpu.CompilerParams(dimension_semantics=sem, vmem_limit_bytes=VMEM_LIMIT)


def _ada_kernel(c_ref, w_ref, b_ref, o_ref):
    c = c_ref[...]
    o_ref[...] = _mm(c * _sigmoid(c), w_ref[...], passes=3) + b_ref[...]


def _ada(c, w_ada, b_ada):
    B, D = c.shape
    n_out = w_ada.shape[1]
    tn = 1024
    return pl.pallas_call(
        _ada_kernel,
        grid=(n_out // tn,),
        in_specs=[pl.BlockSpec((B, D), lambda j: (0, 0)),
                  pl.BlockSpec((D, tn), lambda j: (0, j)),
                  pl.BlockSpec((1, tn), lambda j: (0, j))],
        out_specs=pl.BlockSpec((B, tn), lambda j: (0, j)),
        out_shape=jax.ShapeDtypeStruct((B, n_out), F32),
        compiler_params=_params("arbitrary"),
        name="ada",
    )(c, w_ada, b_ada.reshape(1, n_out))


def _rms_modulate(x, g, scale, shift):
    y = x * lax.rsqrt(jnp.mean(x * x, axis=-1, keepdims=True) + NORM_EPS)
    return (y * g) * (1.0 + scale) + shift


def _in_proj_kernel(x_ref, g_ref, scale_ref, shift_ref, w_ref, pr_ref, pm_ref):
    h = _rms_modulate(x_ref[...], g_ref[...], scale_ref[0], shift_ref[0])
    proj = _dot(h.astype(BF16), w_ref[...])
    pr_ref[...] = proj[:, :RWKV_COLS]
    pm_ref[...] = proj[:, RWKV_COLS:]


def _in_proj(x2, norm_g, scale, shift, w_in_b, S):
    T, D = x2.shape
    tm = 256
    per_b = S // tm
    n_cols = w_in_b.shape[1]
    return pl.pallas_call(
        _in_proj_kernel,
        grid=(T // tm,),
        in_specs=[pl.BlockSpec((tm, D), lambda i: (i, 0)),
                  pl.BlockSpec((1, D), lambda i: (0, 0)),
                  pl.BlockSpec((1, 1, D), lambda i: (i // per_b, 0, 0)),
                  pl.BlockSpec((1, 1, D), lambda i: (i // per_b, 0, 0)),
                  pl.BlockSpec((D, n_cols), lambda i: (0, 0))],
        out_specs=[pl.BlockSpec((tm, RWKV_COLS), lambda i: (i, 0)),
                   pl.BlockSpec((tm, MOBA_PROJ), lambda i: (i, 0))],
        out_shape=[jax.ShapeDtypeStruct((T, RWKV_COLS), F32),
                   jax.ShapeDtypeStruct((T, MOBA_PROJ), F32)],
        compiler_params=_params("arbitrary"),
        name="in_proj",
    )(x2, norm_g.reshape(1, D), scale, shift, w_in_b)


RWKV_TILE = 256


def _unit_lower_inverses(Ls):
    n = Ls[0].shape[0]
    r = _iota2((n, n), 0)
    c = _iota2((n, n), 1)
    eye = (r == c).astype(F32)
    in_block = r // 8 == c // 8
    b16 = lambda xs: [x.astype(BF16) for x in xs]
    Ld = [jnp.where(in_block, L, 0.0) for L in Ls]
    Ld_b = b16(Ld)
    Nb_b = [(L - d).astype(BF16) for L, d in zip(Ls, Ld)]
    L2 = [_dot(d, d) for d in Ld_b]
    L2_b = b16(L2)
    L4_b = b16([_dot(x, x) for x in L2_b])
    T0 = [eye + d + l2 + _dot(db, l2b) for d, l2, db, l2b in zip(Ld, L2, Ld_b, L2_b)]
    T0 = [t + _dot(t.astype(BF16), l4b) for t, l4b in zip(T0, L4_b)]
    T0_b = b16(T0)
    M1_b = b16([_dot(t, nb) for t, nb in zip(T0_b, Nb_b)])
    M2_b = b16([_dot(m, m) for m in M1_b])
    M4_b = b16([_dot(m, m) for m in M2_b])
    X = [t + _dot(m4, tb) for t, m4, tb in zip(T0, M4_b, T0_b)]
    X = [x + _dot(m2, x.astype(BF16)) for x, m2 in zip(X, M2_b)]
    return [x + _dot(m1, x.astype(BF16)) for x, m1 in zip(X, M1_b)]


def _rwkv_kernel(p_ref, mu_ref, w0_ref, a0_ref, kk_ref, ka_ref, rk_ref, lng_ref, lnb_ref,
                 wup_ref, aup_ref, gup_ref, y_ref, carry_ref, state_ref, *, ts):
    t = pl.program_id(1)
    W = RWKV_WIDTH

    @pl.when(t == 0)
    def _():
        carry_ref[...] = jnp.zeros_like(carry_ref)
        state_ref[...] = jnp.zeros_like(state_ref)

    p = p_ref[0]
    row = _iota2((ts, 1), 0)
    prev = jnp.where(row == 0, carry_ref[...], pltpu.roll(p, 1, 0))
    carry_ref[...] = p[ts - 1:ts, :]
    pm = p + (prev - p) * mu_ref[...]
    r = pm[:, 0:W]
    k = pm[:, W:2 * W]
    v = pm[:, 2 * W:3 * W]
    xw = pm[:, XW_OFF:XA_OFF]
    xa = pm[:, XA_OFF:XG_OFF]
    xg = pm[:, XG_OFF:RWKV_COLS]

    z = -(w0_ref[...] + _mm(jnp.tanh(xw), wup_ref[...], passes=3))
    softplus = jnp.maximum(z, 0.0) + jnp.log(1.0 + jnp.exp(-jnp.abs(z)))
    logd = -jnp.exp(-softplus - 0.5)
    alpha = _sigmoid(a0_ref[...] + _mm(xa, aup_ref[...], passes=3))
    gate = _mm(_sigmoid(xg), gup_ref[...])

    head_ones = _group_ones(W, HEAD_DIM)
    kk = k * kk_ref[...]
    kk_norm = jnp.sqrt(_mm_exact_rhs(kk * kk, head_ones))
    kk = kk / jnp.maximum(kk_norm, 1e-12)
    kmod = k * (1.0 + (alpha - 1.0) * ka_ref[...])
    bonus = _mm_exact_rhs(r * kmod * rk_ref[...], head_ones) * v

    tr = _iota2((ts, ts), 0)
    tc = _iota2((ts, ts), 1)
    cum = ((tr // CHUNK == tc // CHUNK) & (tc <= tr)).astype(BF16)
    logp = _mm_exact_lhs(cum, logd)
    inv_p = jnp.exp(-logp)
    a_t = -kk * jnp.exp(logp - logd)
    b_t = kk * alpha * inv_p
    k_t = kmod * inv_p
    r_t = r * jnp.exp(logp)

    n2 = 2 * CHUNK
    n_chunks = ts // CHUNK
    n_pairs = RWKV_HEADS // 2
    inst = [(ci, pi) for ci in range(n_chunks) for pi in range(n_pairs)]
    lane = _iota2((1, LANES), 1)
    m0 = lane < HEAD_DIM
    sr = _iota2((n2, n2), 0)
    sc = _iota2((n2, n2), 1)
    strict_lower = sc < sr
    incl_lower = sc <= sr

    def stacked(x, ci, pi):
        xt = x[ci * CHUNK:(ci + 1) * CHUNK, pi * LANES:(pi + 1) * LANES]
        return jnp.concatenate([jnp.where(m0, xt, 0.0), jnp.where(m0, 0.0, xt)], axis=0)

    pcs = [jnp.exp(logp[(ci + 1) * CHUNK - 1:(ci + 1) * CHUNK, pi * LANES:(pi + 1) * LANES])
           for ci, pi in inst]
    r_s = [stacked(r_t, ci, pi) for ci, pi in inst]
    a_b = [stacked(a_t, ci, pi).astype(BF16) for ci, pi in inst]
    r_b = [x.astype(BF16) for x in r_s]
    b_s = [stacked(b_t, ci, pi) for ci, pi in inst]
    k_s = [stacked(k_t, ci, pi) for ci, pi in inst]
    v_b = [stacked(v, ci, pi).astype(BF16) for ci, pi in inst]
    ar_b = [jnp.concatenate([a, rr], axis=0) for a, rr in zip(a_b, r_b)]
    bk_b = [jnp.concatenate([b.astype(BF16), kx.astype(BF16)], axis=0) for b, kx in zip(b_s, k_s)]
    gram = [_dot(ar, bk, NT) for ar, bk in zip(ar_b, bk_b)]
    l_ab = [jnp.where(strict_lower, g[:n2, :n2], 0.0) for g in gram]
    l_ak_b = [jnp.where(strict_lower, g[:n2, n2:], 0.0).astype(BF16) for g in gram]
    m_rb_b = [jnp.where(incl_lower, g[n2:, :n2], 0.0).astype(BF16) for g in gram]
    m_rk_b = [jnp.where(incl_lower, g[n2:, n2:], 0.0).astype(BF16) for g in gram]
    t_b = [x.astype(BF16) for x in _unit_lower_inverses(l_ab)]
    lakv_b = [_dot(l, vv).astype(BF16) for l, vv in zip(l_ak_b, v_b)]
    wu_b = [_dot(tb, jnp.concatenate([a, lv], axis=1)).astype(BF16)
            for tb, a, lv in zip(t_b, a_b, lakv_b)]
    mwu = [_dot(m, wu) for m, wu in zip(m_rb_b, wu_b)]
    q_b = [(rs + x[:, :LANES]).astype(BF16) for rs, x in zip(r_s, mwu)]
    y0 = [x[:, LANES:] + _dot(m, vv) for x, m, vv in zip(mwu, m_rk_b, v_b)]
    bp_b = [(b * pc).astype(BF16) for b, pc in zip(b_s, pcs)]
    kp_b = [(kx * pc).astype(BF16) for kx, pc in zip(k_s, pcs)]
    gh = [_dot(wu, bp, TN) for wu, bp in zip(wu_b, bp_b)]
    g_b = [x[:LANES].astype(BF16) for x in gh]
    h = [x[LANES:] + _dot(vv, kp, TN) for x, vv, kp in zip(gh, v_b, kp_b)]

    y_chunks = []
    for ci in range(n_chunks):
        ids = [ci * n_pairs + pi for pi in range(n_pairs)]
        s0 = [state_ref[pi] for pi in range(n_pairs)]
        s0_b = [s.astype(BF16) for s in s0]
        y_s = [_dot(q_b[n], sb, NT) + y0[n] for n, sb in zip(ids, s0_b)]
        s1 = [s * pcs[n] + _dot(sb, g_b[n]) + h[n] for n, s, sb in zip(ids, s0, s0_b)]
        for pi in range(n_pairs):
            state_ref[pi] = s1[pi]
        y_chunks.append(jnp.concatenate([x[:CHUNK] + x[CHUNK:] for x in y_s], axis=1))
    y = jnp.concatenate(y_chunks, axis=0) if n_chunks > 1 else y_chunks[0]

    mean = _mm_exact_rhs(y, head_ones) * (1.0 / HEAD_DIM)
    yc = y - mean
    var = _mm_exact_rhs(yc * yc, head_ones) * (1.0 / HEAD_DIM)
    yn = yc * lax.rsqrt(var + RWKV_LN_EPS) * lng_ref[...] + lnb_ref[...]
    y_ref[0] = (yn + bonus) * gate


def _rwkv(p_rwkv, prm, B, S, ts=RWKV_TILE):
    W = RWKV_WIDTH
    vec = lambda n: pl.BlockSpec((1, n), lambda b, t: (0, 0))
    mat = lambda m, n: pl.BlockSpec((m, n), lambda b, t: (0, 0))
    return pl.pallas_call(
        functools.partial(_rwkv_kernel, ts=ts),
        grid=(B, S // ts),
        in_specs=[pl.BlockSpec((1, ts, RWKV_COLS), lambda b, t: (b, t, 0)),
                  vec(RWKV_COLS)] + [vec(W)] * 7 + [mat(LANES, W), mat(LANES, W), mat(2 * LANES, W)],
        out_specs=pl.BlockSpec((1, ts, W), lambda b, t: (b, t, 0)),
        out_shape=jax.ShapeDtypeStruct((B, S, W), F32),
        scratch_shapes=[pltpu.VMEM((1, RWKV_COLS), F32),
                        pltpu.VMEM((RWKV_HEADS // 2, LANES, LANES), F32)],
        compiler_params=_params("arbitrary", "arbitrary"),
        name="rwkv",
    )(p_rwkv.reshape(B, S, RWKV_COLS), *prm)


def _moba_kernel(slopes_ref, q_ref, k_ref, v_ref, qg_ref, kg_ref, o_ref, *, S):
    pair = pl.program_id(1)
    NB = S // MOBA_BLOCK
    BLK = MOBA_BLOCK
    n_sel = min(MOBA_TOPK, NB)
    scale = HEAD_DIM ** -0.5
    head_ones = _group_ones(LANES, HEAD_DIM)
    lane = _iota2((1, LANES), 1)

    def head_norm(x, g):
        ss = _mm_exact_rhs(x * x, head_ones)
        return x * lax.rsqrt(ss * (1.0 / HEAD_DIM) + NORM_EPS) * g

    qn = head_norm(q_ref[0], qg_ref[...])
    kn = head_norm(k_ref[0], kg_ref[...])
    kmean = jnp.mean(kn.reshape(NB, BLK, LANES), axis=1)
    q_t = qn.T
    v_tb = v_ref[0].T.astype(BF16)

    blk_of_q = _iota2((1, S), 1) // BLK
    nidx = _iota2((NB, 1), 0)
    valid = nidx < blk_of_q
    q_blk = (blk_of_q * BLK).astype(F32)
    q_loc = (_iota2((1, S), 1) % BLK).astype(F32)
    k_blk = (_iota2((S, 1), 0) // BLK * BLK).astype(F32)
    k_loc = (_iota2((S, 1), 0) % BLK).astype(F32)
    q_lane = _iota2((LANES, 1), 0)
    causal = _iota2((BLK, BLK), 0) <= _iota2((BLK, BLK), 1)

    out_rows = []
    for h in range(2):
        hmask = (lane // HEAD_DIM) == h
        slope = slopes_ref[pair * 2 + h]
        spare = (1 - h) * HEAD_DIM
        k_aug = jnp.where(hmask, kn, 0.0)
        k_aug = jnp.where(lane == spare, slope * k_blk, k_aug)
        k_aug = jnp.where(lane == spare + 1, slope * k_loc, k_aug)
        k_aug = jnp.where((lane == spare + 2) | (lane == spare + 3), 1.0, k_aug)
        q_aug = q_t * scale
        q_aug = jnp.where((q_lane == spare) | (q_lane == spare + 1), 1.0, q_aug)
        q_aug = jnp.where(q_lane == spare + 2, -slope * q_blk, q_aug)
        q_aug = jnp.where(q_lane == spare + 3, -slope * q_loc, q_aug)
        q_hb = q_aug.astype(BF16)
        gate = _mm(jnp.where(hmask, kmean, 0.0), q_t, passes=3)
        gate = jnp.where(valid, gate, -jnp.inf)
        rank = jnp.zeros((NB, S), jnp.int32)
        for m in range(NB):
            gm = gate[m:m + 1, :]
            ahead = (gm > gate) | ((gm == gate) & (m < nidx))
            rank = rank + ahead.astype(jnp.int32)
        sel = valid & (rank < n_sel)
        k_hb = k_aug.astype(BF16)
        v_h = v_tb[h * HEAD_DIM:(h + 1) * HEAD_DIM, :]
        out_blocks = []
        for i in range(NB):
            qs = slice(i * BLK, (i + 1) * BLK)
            tiles = []
            m_run = None
            for n in range(i + 1):
                ks = slice(n * BLK, (n + 1) * BLK)
                s = _dot(k_hb[ks, :], q_hb[:, qs])
                if n < i:
                    s = jnp.where(sel[n:n + 1, qs], s, -jnp.inf)
                else:
                    s = jnp.where(causal, s, -jnp.inf)
                tiles.append(s)
                mx = jnp.max(s, axis=0, keepdims=True)
                m_run = mx if m_run is None else jnp.maximum(m_run, mx)
            l_run = jnp.zeros((1, BLK), F32)
            acc = jnp.zeros((HEAD_DIM, BLK), F32)
            for n in range(i + 1):
                ks = slice(n * BLK, (n + 1) * BLK)
                pt = jnp.exp(tiles[n] - m_run)
                l_run = l_run + jnp.sum(pt, axis=0, keepdims=True)
                acc = acc + _dot(v_h[:, ks], pt.astype(BF16))
            out_blocks.append(acc / l_run)
        out_rows.append(jnp.concatenate(out_blocks, axis=1))
    o_ref[0] = jnp.concatenate(out_rows, axis=0).T


def _moba(p_moba, q_norm_g, k_norm_g, B, S):
    pairs = MOBA_HEADS // 2
    col = lambda off: pl.BlockSpec((1, S, LANES), lambda b, p: (b, 0, off + p))
    gain = pl.BlockSpec((1, LANES), lambda b, p: (0, 0))
    tile2 = lambda g: jnp.concatenate([g, g]).reshape(1, LANES)
    p3 = p_moba.reshape(B, S, MOBA_PROJ)
    slopes = jnp.exp2(-8.0 * (jnp.arange(MOBA_HEADS, dtype=F32) + 1.0) / MOBA_HEADS)
    return pl.pallas_call(
        functools.partial(_moba_kernel, S=S),
        grid=(B, pairs),
        in_specs=[pl.BlockSpec(memory_space=pltpu.SMEM), col(0), col(pairs), col(2 * pairs),
                  gain, gain],
        out_specs=pl.BlockSpec((1, S, LANES), lambda b, p: (b, 0, p)),
        out_shape=jax.ShapeDtypeStruct((B, S, MOBA_WIDTH), F32),
        compiler_params=_params("arbitrary", "arbitrary"),
        name="moba",
    )(slopes, p3, p3, p3, tile2(q_norm_g), tile2(k_norm_g))


def _out_proj_kernel(yr_ref, ym_ref, x_ref, gate_ref, g_ref, scale_ref, shift_ref, w_ref,
                     wr_ref, br_ref, x1_ref, h2_ref, idx_ref, wgt_ref):
    W = RWKV_WIDTH
    mix = (_dot(yr_ref[...].astype(BF16), w_ref[0:W, :])
           + _dot(ym_ref[...].astype(BF16), w_ref[W:, :]))
    x1 = x_ref[...] + gate_ref[0] * mix
    x1_ref[...] = x1
    h2 = _rms_modulate(x1, g_ref[...], scale_ref[0], shift_ref[0])
    h2_ref[:, 0, :] = h2
    logits_t = (_mm(h2, wr_ref[...], passes=3) + br_ref[...]).T[:N_EXPERTS, :]
    tm = logits_t.shape[1]
    eidx = _iota2((N_EXPERTS, 1), 0)
    vals, idxs = [], []
    for _ in range(TOP_K):
        m = jnp.max(logits_t, axis=0, keepdims=True)
        idx = jnp.min(jnp.where(logits_t == m, eidx, N_EXPERTS), axis=0, keepdims=True)
        vals.append(m)
        idxs.append(idx)
        logits_t = jnp.where(eidx == idx, -jnp.inf, logits_t)
    idx_ref[...] = jnp.concatenate(idxs, axis=0)
    e = [jnp.exp(v - vals[0]) for v in vals]
    total = e[0] + e[1] + e[2] + e[3]
    wgt_t = jnp.concatenate([x / total for x in e] + [jnp.zeros((LANES - TOP_K, tm), F32)], axis=0)
    wgt_ref[...] = wgt_t.T


def _out_proj(y_rwkv, y_moba, x2, gate1, norm_g, scale, shift, w_out_b, w_router, b_router, S):
    T, D = x2.shape
    tm = 256
    per_b = S // tm
    rows = lambda n: pl.BlockSpec((tm, n), lambda i: (i, 0))
    mod = pl.BlockSpec((1, 1, D), lambda i: (i // per_b, 0, 0))
    full = lambda m, n: pl.BlockSpec((m, n), lambda i: (0, 0))
    wr = _pad_cols(w_router, LANES)
    br = jnp.concatenate([b_router, jnp.full((LANES - N_EXPERTS,), -jnp.inf, F32)]).reshape(1, LANES)
    return pl.pallas_call(
        _out_proj_kernel,
        grid=(T // tm,),
        in_specs=[rows(RWKV_WIDTH), rows(MOBA_WIDTH), rows(D), mod, full(1, D), mod, mod,
                  full(D, D), full(D, LANES), full(1, LANES)],
        out_specs=[rows(D), pl.BlockSpec((tm, 1, D), lambda i: (i, 0, 0)),
                   pl.BlockSpec((TOP_K, tm), lambda i: (0, i)), rows(LANES)],
        out_shape=[jax.ShapeDtypeStruct((T, D), F32), jax.ShapeDtypeStruct((T, 1, D), F32),
                   jax.ShapeDtypeStruct((TOP_K, T), jnp.int32), jax.ShapeDtypeStruct((T, LANES), F32)],
        compiler_params=_params("arbitrary"),
        name="out_proj",
    )(y_rwkv, y_moba, x2, gate1, norm_g.reshape(1, D), scale, shift, w_out_b, wr, br)


def _experts_kernel(meta_ref, src_first_ref, src_next_ref, h_hbm, wgu_ref, bgu_ref, wd_ref, bd_ref,
                    y_hbm, xbuf0, xbuf1, obuf0, obuf1, xb16, wgu16, wd16, gsem, wsem,
                    *, n_tiles):
    j = pl.program_id(0)
    n_used = meta_ref[0]
    tm = EXPERT_TILE
    n_sub = EXPERT_SUBSTEPS
    cols_sub = D_FF // n_sub
    expert = lambda i: meta_ref[1 + jnp.clip(i, 0, n_tiles - 1)]

    def gather_row(src_ref, r, dst_buf, sem, priority=0):
        pltpu.make_async_copy(h_hbm.at[src_ref[0, 0, r]], dst_buf.at[pl.ds(r, 1)], sem).start(
            priority=priority)

    def wait_tile_gather(buf, sem):
        pltpu.make_async_copy(h_hbm.at[pl.ds(0, tm), 0], buf, sem).wait()

    def write_back(buf, tile, sem):
        return pltpu.make_async_copy(buf, y_hbm.at[pl.ds(tile * tm, tm), 0], sem)

    @pl.when(j == 0)
    def _():
        def body(r, carry):
            gather_row(src_first_ref, r, xbuf0, gsem.at[0])
            return carry
        lax.fori_loop(0, tm, body, 0)

    @pl.when((j < n_used) & ((j == 0) | (expert(j) != expert(j - 1))))
    def _():
        wgu16[...] = wgu_ref[0].astype(BF16)
        wd16[...] = wd_ref[0].astype(BF16)

    def ffn_chunk(n, o_cur):
        cg = slice(n * cols_sub, (n + 1) * cols_sub)
        cu = slice(D_FF + n * cols_sub, D_FF + (n + 1) * cols_sub)
        xb = xb16[...]
        gate = jnp.minimum(_dot(xb, wgu16[:, cg]) + bgu_ref[0, :, cg], SWIGLU_LIMIT)
        up = jnp.clip(_dot(xb, wgu16[:, cu]) + bgu_ref[0, :, cu], -SWIGLU_LIMIT, SWIGLU_LIMIT)
        act = (up + 1.0) * gate * _sigmoid(SWIGLU_ALPHA * gate)
        o_cur[...] += _dot(act.astype(BF16), wd16[cg, :])

    def tile_step(ph):
        x_cur, x_nxt = (xbuf0, xbuf1) if ph == 0 else (xbuf1, xbuf0)
        o_cur, o_prv = (obuf0, obuf1) if ph == 0 else (obuf1, obuf0)
        mine = j % 2 == ph

        @pl.when(mine & (j < n_used))
        def _():
            wait_tile_gather(x_cur, gsem.at[ph])

            @pl.when(j >= 2)
            def _():
                write_back(o_cur, j - 2, wsem.at[ph]).wait()

            for r in range(tm):
                gather_row(src_next_ref, r, x_nxt, gsem.at[1 - ph], priority=r % 2)
            xb16[...] = x_cur[...].astype(BF16)
            o_cur[...] = jnp.broadcast_to(bd_ref[0], (tm, D_MODEL))

        @pl.when(mine & (j + 1 <= n_used))
        def _():
            for n in range(n_sub):
                ffn_chunk(n, o_cur)
            write_back(o_cur, j, wsem.at[ph]).start()

        @pl.when(mine & (j == n_used))
        def _():
            wait_tile_gather(x_cur, gsem.at[ph])
            write_back(o_prv, j - 1, wsem.at[1 - ph]).wait()

            @pl.when(j >= 2)
            def _():
                write_back(o_cur, j - 2, wsem.at[ph]).wait()

        @pl.when(mine & (j >= n_used) & (j < n_tiles))
        def _():
            o_cur[...] = jnp.zeros_like(o_cur)
            fill = write_back(o_cur, j, wsem.at[ph])
            fill.start()
            fill.wait()

    tile_step(0)
    tile_step(1)


def _experts(h2, src_tok, tile_expert, n_used, wgu, bgu, wd, bd):
    D = h2.shape[-1]
    tm = EXPERT_TILE
    n_tiles = tile_expert.shape[0]
    src3 = src_tok.reshape(n_tiles + 1, 1, tm)
    meta = jnp.concatenate([n_used.reshape(1), tile_expert]).astype(jnp.int32)
    smem_row = lambda f: pl.BlockSpec((1, 1, tm), f, memory_space=pltpu.SMEM)
    e_of = lambda j, m: m[1 + jnp.minimum(j, n_tiles - 1)]
    grid_spec = pltpu.PrefetchScalarGridSpec(
        num_scalar_prefetch=1,
        grid=(n_tiles + 1,),
        in_specs=[smem_row(lambda j, m: (0, 0, 0)),
                  smem_row(lambda j, m: (jnp.minimum(j + 1, n_tiles), 0, 0)),
                  pl.BlockSpec(memory_space=pl.ANY),
                  pl.BlockSpec((1, D, 2 * D_FF), lambda j, m: (e_of(j, m), 0, 0)),
                  pl.BlockSpec((1, 1, 2 * D_FF), lambda j, m: (e_of(j, m), 0, 0)),
                  pl.BlockSpec((1, D_FF, D), lambda j, m: (e_of(j, m), 0, 0)),
                  pl.BlockSpec((1, 1, D), lambda j, m: (e_of(j, m), 0, 0))],
        out_specs=pl.BlockSpec(memory_space=pl.ANY),
        scratch_shapes=[pltpu.VMEM((tm, D), F32), pltpu.VMEM((tm, D), F32),
                        pltpu.VMEM((tm, D), F32), pltpu.VMEM((tm, D), F32),
                        pltpu.VMEM((tm, D), BF16),
                        pltpu.VMEM((D, 2 * D_FF), BF16), pltpu.VMEM((D_FF, D), BF16),
                        pltpu.SemaphoreType.DMA((2,)), pltpu.SemaphoreType.DMA((2,))],
    )
    return pl.pallas_call(
        functools.partial(_experts_kernel, n_tiles=n_tiles),
        grid_spec=grid_spec,
        out_shape=jax.ShapeDtypeStruct((n_tiles * tm, 1, D), F32),
        compiler_params=_params("arbitrary"),
        name="experts",
    )(meta, src3, src3, h2, wgu, bgu.reshape(N_EXPERTS, 1, 2 * D_FF), wd, bd.reshape(N_EXPERTS, 1, D))


def _route(top_idx_t):
    T = top_idx_t.shape[1]
    tm = EXPERT_TILE
    M = T * TOP_K
    slot_expert = top_idx_t.reshape(M)
    order = jnp.argsort(slot_expert).astype(jnp.int32)
    rank = jnp.argsort(order).astype(jnp.int32)
    counts = jnp.bincount(slot_expert, length=N_EXPERTS)
    padded = (counts + tm - 1) // tm * tm
    pad_end = jnp.cumsum(padded)
    pad_start = pad_end - padded
    start = jnp.cumsum(counts) - counts
    n_tiles = M // tm + N_EXPERTS
    tile_start = jnp.arange(n_tiles) * tm
    tile_expert = jnp.minimum(jnp.sum(pad_end[None, :] <= tile_start[:, None], axis=1), N_EXPERTS - 1)
    tile_valid = jnp.clip(counts[tile_expert] - (tile_start - pad_start[tile_expert]), 0, tm)
    n_used = pad_end[-1] // tm
    r = jnp.arange(tm)[None, :]
    sorted_pos = (start[tile_expert] + tile_start - pad_start[tile_expert])[:, None] + r
    slot = order[jnp.clip(sorted_pos, 0, M - 1)]
    src_tok = jnp.where(r < tile_valid[:, None], slot % T, 0).astype(jnp.int32)
    src_ext = jnp.concatenate([src_tok, jnp.zeros((1, tm), jnp.int32)], axis=0)
    slot_row = (pad_start[slot_expert] + rank - start[slot_expert]).astype(jnp.int32)
    return src_ext, tile_expert.astype(jnp.int32), n_used.astype(jnp.int32), slot_row


COMBINE_TILE = 256


def _combine_kernel(row_first_ref, row_next_ref, x1_ref, w_ref, gate_ref, y_hbm, o_ref,
                    ybuf0, ybuf1, sem, *, n_steps):
    i = pl.program_id(0)
    tm = COMBINE_TILE
    n_rows = TOP_K * tm

    def fetch_row(row_ref, r, buf, s, priority=0):
        pltpu.make_async_copy(y_hbm.at[row_ref[0, 0, r]], buf.at[pl.ds(r, 1)], s).start(priority=priority)

    @pl.when(i == 0)
    def _():
        def body(r, carry):
            fetch_row(row_first_ref, r, ybuf0, sem.at[0])
            return carry
        lax.fori_loop(0, n_rows, body, 0)

    def step(ph):
        cur, nxt = (ybuf0, ybuf1) if ph == 0 else (ybuf1, ybuf0)
        mine = i % 2 == ph

        @pl.when(mine & (i + 1 < n_steps))
        def _():
            for r in range(n_rows):
                fetch_row(row_next_ref, r, nxt, sem.at[1 - ph], priority=r % 2)

        @pl.when(mine)
        def _():
            pltpu.make_async_copy(y_hbm.at[pl.ds(0, n_rows), 0], cur, sem.at[ph]).wait()
            w = w_ref[...]
            acc = w[:, 0:1] * cur[0:tm, :]
            for k in range(1, TOP_K):
                acc = acc + w[:, k:k + 1] * cur[k * tm:(k + 1) * tm, :]
            o_ref[...] = x1_ref[...] + gate_ref[0] * acc

    step(0)
    step(1)


def _combine(x1, y_sorted, slot_row, weights, gate2, S):
    T, D = x1.shape
    tm = COMBINE_TILE
    per_b = S // tm
    n_steps = T // tm
    n_rows = TOP_K * tm
    rows = slot_row.reshape(TOP_K, n_steps, tm).transpose(1, 0, 2).reshape(n_steps, 1, n_rows)
    smem_row = lambda f: pl.BlockSpec((1, 1, n_rows), f, memory_space=pltpu.SMEM)
    return pl.pallas_call(
        functools.partial(_combine_kernel, n_steps=n_steps),
        grid=(n_steps,),
        in_specs=[smem_row(lambda i: (0, 0, 0)),
                  smem_row(lambda i: (jnp.minimum(i + 1, n_steps - 1), 0, 0)),
                  pl.BlockSpec((tm, D), lambda i: (i, 0)),
                  pl.BlockSpec((tm, LANES), lambda i: (i, 0)),
                  pl.BlockSpec((1, 1, D), lambda i: (i // per_b, 0, 0)),
                  pl.BlockSpec(memory_space=pl.ANY)],
        out_specs=pl.BlockSpec((tm, D), lambda i: (i, 0)),
        out_shape=jax.ShapeDtypeStruct((T, D), F32),
        scratch_shapes=[pltpu.VMEM((n_rows, D), F32), pltpu.VMEM((n_rows, D), F32),
                        pltpu.SemaphoreType.DMA((2,))],
        compiler_params=_params("arbitrary"),
        name="combine",
    )(rows, rows, x1, weights, gate2, y_sorted)


def _pad_cols(w, n):
    return jnp.pad(w, ((0, 0), (0, n - w.shape[1])))


def _pad_rows(w, n):
    return jnp.pad(w, ((0, n - w.shape[0]), (0, 0)))


def _layer(x, c, w_ada, b_ada, norm1_g, w_in, rwkv_mu, rwkv_w0, rwkv_w_up, rwkv_a0, rwkv_a_up,
           rwkv_g_up, rwkv_k_k, rwkv_k_a, rwkv_r_k, rwkv_ln_g, rwkv_ln_b, q_norm_g, k_norm_g,
           w_out, norm2_g, w_router, b_router, w_gate_up, b_gate_up, w_down, b_down):
    B, S, D = x.shape
    T = B * S
    W = RWKV_WIDTH
    x2 = x.reshape(T, D)

    mods = _ada(c, w_ada, b_ada)
    shift1, scale1, gate1, shift2, scale2, gate2 = [
        mods[:, j * D:(j + 1) * D].reshape(B, 1, D) for j in range(6)]

    pieces = [(w_in[:, :XW_OFF], XW_OFF),
              (w_in[:, XW_OFF:XW_OFF + DECAY_LORA], LANES),
              (w_in[:, XW_OFF + DECAY_LORA:XW_OFF + DECAY_LORA + AAA_LORA], LANES),
              (w_in[:, XW_OFF + DECAY_LORA + AAA_LORA:RWKV_PROJ], 2 * LANES),
              (w_in[:, RWKV_PROJ:], MOBA_PROJ)]
    w_in_b = jnp.concatenate([_pad_cols(w, n) for w, n in pieces], axis=1).astype(BF16)
    mu_pieces = [(rwkv_mu[None, :XW_OFF], XW_OFF),
                 (rwkv_mu[None, XW_OFF:XW_OFF + DECAY_LORA], LANES),
                 (rwkv_mu[None, XW_OFF + DECAY_LORA:XW_OFF + DECAY_LORA + AAA_LORA], LANES),
                 (rwkv_mu[None, XW_OFF + DECAY_LORA + AAA_LORA:], 2 * LANES)]
    mu = jnp.concatenate([_pad_cols(m, n) for m, n in mu_pieces], axis=1)

    p_rwkv, p_moba = _in_proj(x2, norm1_g, scale1, shift1, w_in_b, S)

    row = lambda a: a.reshape(1, W)
    prm = (mu, row(rwkv_w0), row(rwkv_a0), row(rwkv_k_k), row(rwkv_k_a), row(rwkv_r_k),
           row(rwkv_ln_g), row(rwkv_ln_b), _pad_rows(rwkv_w_up, LANES), _pad_rows(rwkv_a_up, LANES),
           _pad_rows(rwkv_g_up, 2 * LANES))
    y_rwkv = _rwkv(p_rwkv, prm, B, S)
    y_moba = _moba(p_moba, q_norm_g, k_norm_g, B, S)

    x1, h2, top_idx_t, weights = _out_proj(y_rwkv.reshape(T, W), y_moba.reshape(T, MOBA_WIDTH), x2,
                                           gate1, norm2_g, scale2, shift2, w_out.astype(BF16),
                                           w_router, b_router, S)

    src_tok, tile_expert, n_used, slot_row = _route(top_idx_t)
    y_sorted = _experts(h2, src_tok, tile_expert, n_used, w_gate_up, b_gate_up, w_down, b_down)
    out = _combine(x1, y_sorted, slot_row, weights, gate2, S)
    return out.reshape(B, S, D)


def kernel(x, c, w_ada, b_ada, norm1_g, w_in, rwkv_mu, rwkv_w0, rwkv_w_up, rwkv_a0, rwkv_a_up, rwkv_g_up, rwkv_k_k, rwkv_k_a, rwkv_r_k, rwkv_ln_g, rwkv_ln_b, q_norm_g, k_norm_g, w_out, norm2_g, w_router, b_router, w_gate_up, b_gate_up, w_down, b_down):
    for l in range(w_ada.shape[0]):
        x = _layer(x, c, w_ada[l], b_ada[l], norm1_g[l], w_in[l], rwkv_mu[l], rwkv_w0[l],
                   rwkv_w_up[l], rwkv_a0[l], rwkv_a_up[l], rwkv_g_up[l], rwkv_k_k[l], rwkv_k_a[l],
                   rwkv_r_k[l], rwkv_ln_g[l], rwkv_ln_b[l], q_norm_g[l], k_norm_g[l], w_out[l],
                   norm2_g[l], w_router[l], b_router[l], w_gate_up[l], b_gate_up[l], w_down[l],
                   b_down[l])
    return x
```

```python
import functools

import jax
import jax.numpy as jnp
from jax import lax
from jax.experimental import pallas as pl
from jax.experimental.pallas import tpu as pltpu

F32 = jnp.float32
BF16 = jnp.bfloat16

D_MODEL = 1024
HEAD_DIM = 64
RWKV_WIDTH = 512
MOBA_WIDTH = 512
RWKV_HEADS = RWKV_WIDTH // HEAD_DIM
MOBA_HEADS = MOBA_WIDTH // HEAD_DIM
DECAY_LORA = 64
AAA_LORA = 64
GATE_LORA = 160
RWKV_LN_EPS = 64e-5
RWKV_PROJ = 3 * RWKV_WIDTH + DECAY_LORA + AAA_LORA + GATE_LORA
MOBA_PROJ = 3 * MOBA_WIDTH
MOBA_BLOCK = 256
MOBA_TOPK = 3
N_EXPERTS = 32
TOP_K = 4
D_FF = D_MODEL
SWIGLU_LIMIT = 7.0
SWIGLU_ALPHA = 1.702
NORM_EPS = 1e-6

LANES = 128
SUBLANES = 8
XW_OFF = 3 * RWKV_WIDTH
XA_OFF = XW_OFF + LANES
XG_OFF = XA_OFF + LANES
RWKV_COLS = XG_OFF + 2 * LANES
CHUNK = 64
EXPERT_TILE = 256
EXPERT_SUBSTEPS = 4
VMEM_LIMIT = 56 * 1024 * 1024

NN = (((1,), (0,)), ((), ()))
NT = (((1,), (1,)), ((), ()))
TN = (((0,), (0,)), ((), ()))


def _dot(a, b, dims=NN):
    return lax.dot_general(a, b, dims, preferred_element_type=F32)


def _split(a):
    hi = a.astype(BF16)
    lo = (a - hi.astype(F32)).astype(BF16)
    return hi, lo


def _mm(a, b, dims=NN, passes=1):
    if passes == 1:
        return _dot(a.astype(BF16), b.astype(BF16), dims)
    a_hi, a_lo = _split(a)
    b_hi, b_lo = _split(b)
    return _dot(a_hi, b_hi, dims) + (_dot(a_hi, b_lo, dims) + _dot(a_lo, b_hi, dims))


def _split3(a):
    hi = a.astype(BF16)
    r = a - hi.astype(F32)
    mid = r.astype(BF16)
    lo = (r - mid.astype(F32)).astype(BF16)
    return hi, mid, lo


def _mm_exact_rhs(a, b_bf16, dims=NN):
    hi, mid, lo = _split3(a)
    return _dot(hi, b_bf16, dims) + (_dot(mid, b_bf16, dims) + _dot(lo, b_bf16, dims))


def _mm_exact_lhs(a_bf16, b, dims=NN):
    hi, mid, lo = _split3(b)
    return _dot(a_bf16, hi, dims) + (_dot(a_bf16, mid, dims) + _dot(a_bf16, lo, dims))


def _iota2(shape, dim):
    return lax.broadcasted_iota(jnp.int32, shape, dim)


def _group_ones(n, group):
    return (_iota2((n, n), 0) // group == _iota2((n, n), 1) // group).astype(BF16)


def _sigmoid(x):
    return 1.0 / (1.0 + jnp.exp(-x))


def _params(*sem):
    return pltpu.CompilerParams(dimension_semantics=sem, vmem_limit_bytes=VMEM_LIMIT)


def _ada_kernel(c_ref, w_ref, b_ref, o_ref):
    c = c_ref[...]
    o_ref[...] = _mm(c * _sigmoid(c), w_ref[...], passes=3) + b_ref[...]


def _ada(c, w_ada, b_ada):
    B, D = c.shape
    n_out = w_ada.shape[1]
    tn = 1024
    return pl.pallas_call(
        _ada_kernel,
        grid=(n_out // tn,),
        in_specs=[pl.BlockSpec((B, D), lambda j: (0, 0)),
                  pl.BlockSpec((D, tn), lambda j: (0, j)),
                  pl.BlockSpec((1, tn), lambda j: (0, j))],
        out_specs=pl.BlockSpec((B, tn), lambda j: (0, j)),
        out_shape=jax.ShapeDtypeStruct((B, n_out), F32),
        compiler_params=_params("arbitrary"),
        name="ada",
    )(c, w_ada, b_ada.reshape(1, n_out))


def _rms_modulate(x, g, scale, shift):
    y = x * lax.rsqrt(jnp.mean(x * x, axis=-1, keepdims=True) + NORM_EPS)
    return (y * g) * (1.0 + scale) + shift


def _in_proj_kernel(x_ref, g_ref, scale_ref, shift_ref, w_ref, pr_ref, pm_ref):
    h = _rms_modulate(x_ref[...], g_ref[...], scale_ref[0], shift_ref[0])
    proj = _dot(h.astype(BF16), w_ref[...])
    pr_ref[...] = proj[:, :RWKV_COLS]
    pm_ref[...] = proj[:, RWKV_COLS:]


def _in_proj(x2, norm_g, scale, shift, w_in_b, S):
    T, D = x2.shape
    tm = 256
    per_b = S // tm
    n_cols = w_in_b.shape[1]
    return pl.pallas_call(
        _in_proj_kernel,
        grid=(T // tm,),
        in_specs=[pl.BlockSpec((tm, D), lambda i: (i, 0)),
                  pl.BlockSpec((1, D), lambda i: (0, 0)),
                  pl.BlockSpec((1, 1, D), lambda i: (i // per_b, 0, 0)),
                  pl.BlockSpec((1, 1, D), lambda i: (i // per_b, 0, 0)),
                  pl.BlockSpec((D, n_cols), lambda i: (0, 0))],
        out_specs=[pl.BlockSpec((tm, RWKV_COLS), lambda i: (i, 0)),
                   pl.BlockSpec((tm, MOBA_PROJ), lambda i: (i, 0))],
        out_shape=[jax.ShapeDtypeStruct((T, RWKV_COLS), F32),
                   jax.ShapeDtypeStruct((T, MOBA_PROJ), F32)],
        compiler_params=_params("arbitrary"),
        name="in_proj",
    )(x2, norm_g.reshape(1, D), scale, shift, w_in_b)


RWKV_TILE = 256


def _unit_lower_inverses(Ls):
    n = Ls[0].shape[0]
    r = _iota2((n, n), 0)
    c = _iota2((n, n), 1)
    eye = (r == c).astype(F32)
    in_block = r // 8 == c // 8
    b16 = lambda xs: [x.astype(BF16) for x in xs]
    Ld = [jnp.where(in_block, L, 0.0) for L in Ls]
    Ld_b = b16(Ld)
    Nb_b = [(L - d).astype(BF16) for L, d in zip(Ls, Ld)]
    L2 = [_dot(d, d) for d in Ld_b]
    L2_b = b16(L2)
    L4_b = b16([_dot(x, x) for x in L2_b])
    T0 = [eye + d + l2 + _dot(db, l2b) for d, l2, db, l2b in zip(Ld, L2, Ld_b, L2_b)]
    T0 = [t + _dot(t.astype(BF16), l4b) for t, l4b in zip(T0, L4_b)]
    T0_b = b16(T0)
    M1_b = b16([_dot(t, nb) for t, nb in zip(T0_b, Nb_b)])
    M2_b = b16([_dot(m, m) for m in M1_b])
    M4_b = b16([_dot(m, m) for m in M2_b])
    X = [t + _dot(m4, tb) for t, m4, tb in zip(T0, M4_b, T0_b)]
    X = [x + _dot(m2, x.astype(BF16)) for x, m2 in zip(X, M2_b)]
    return [x + _dot(m1, x.astype(BF16)) for x, m1 in zip(X, M1_b)]


def _rwkv_kernel(p_ref, mu_ref, w0_ref, a0_ref, kk_ref, ka_ref, rk_ref, lng_ref, lnb_ref,
                 wup_ref, aup_ref, gup_ref, y_ref, carry_ref, state_ref, *, ts):
    t = pl.program_id(1)
    W = RWKV_WIDTH

    @pl.when(t == 0)
    def _():
        carry_ref[...] = jnp.zeros_like(carry_ref)
        state_ref[...] = jnp.zeros_like(state_ref)

    p = p_ref[0]
    row = _iota2((ts, 1), 0)
    prev = jnp.where(row == 0, carry_ref[...], pltpu.roll(p, 1, 0))
    carry_ref[...] = p[ts - 1:ts, :]
    pm = p + (prev - p) * mu_ref[...]
    r = pm[:, 0:W]
    k = pm[:, W:2 * W]
    v = pm[:, 2 * W:3 * W]
    xw = pm[:, XW_OFF:XA_OFF]
    xa = pm[:, XA_OFF:XG_OFF]
    xg = pm[:, XG_OFF:RWKV_COLS]

    z = -(w0_ref[...] + _mm(jnp.tanh(xw), wup_ref[...], passes=3))
    softplus = jnp.maximum(z, 0.0) + jnp.log(1.0 + jnp.exp(-jnp.abs(z)))
    logd = -jnp.exp(-softplus - 0.5)
    alpha = _sigmoid(a0_ref[...] + _mm(xa, aup_ref[...], passes=3))
    gate = _mm(_sigmoid(xg), gup_ref[...])

    head_ones = _group_ones(W, HEAD_DIM)
    kk = k * kk_ref[...]
    kk_norm = jnp.sqrt(_mm_exact_rhs(kk * kk, head_ones))
    kk = kk / jnp.maximum(kk_norm, 1e-12)
    kmod = k * (1.0 + (alpha - 1.0) * ka_ref[...])
    bonus = _mm_exact_rhs(r * kmod * rk_ref[...], head_ones) * v

    tr = _iota2((ts, ts), 0)
    tc = _iota2((ts, ts), 1)
    cum = ((tr // CHUNK == tc // CHUNK) & (tc <= tr)).astype(BF16)
    logp = _mm_exact_lhs(cum, logd)
    inv_p = jnp.exp(-logp)
    a_t = -kk * jnp.exp(logp - logd)
    b_t = kk * alpha * inv_p
    k_t = kmod * inv_p
    r_t = r * jnp.exp(logp)

    n2 = 2 * CHUNK
    n_chunks = ts // CHUNK
    n_pairs = RWKV_HEADS // 2
    inst = [(ci, pi) for ci in range(n_chunks) for pi in range(n_pairs)]
    lane = _iota2((1, LANES), 1)
    m0 = lane < HEAD_DIM
    sr = _iota2((n2, n2), 0)
    sc = _iota2((n2, n2), 1)
    strict_lower = sc < sr
    incl_lower = sc <= sr

    def stacked(x, ci, pi):
        xt = x[ci * CHUNK:(ci + 1) * CHUNK, pi * LANES:(pi + 1) * LANES]
        return jnp.concatenate([jnp.where(m0, xt, 0.0), jnp.where(m0, 0.0, xt)], axis=0)

    pcs = [jnp.exp(logp[(ci + 1) * CHUNK - 1:(ci + 1) * CHUNK, pi * LANES:(pi + 1) * LANES])
           for ci, pi in inst]
    r_s = [stacked(r_t, ci, pi) for ci, pi in inst]
    a_b = [stacked(a_t, ci, pi).astype(BF16) for ci, pi in inst]
    r_b = [x.astype(BF16) for x in r_s]
    b_s = [stacked(b_t, ci, pi) for ci, pi in inst]
    k_s = [stacked(k_t, ci, pi) for ci, pi in inst]
    v_b = [stacked(v, ci, pi).astype(BF16) for ci, pi in inst]
    ar_b = [jnp.concatenate([a, rr], axis=0) for a, rr in zip(a_b, r_b)]
    bk_b = [jnp.concatenate([b.astype(BF16), kx.astype(BF16)], axis=0) for b, kx in zip(b_s, k_s)]
    gram = [_dot(ar, bk, NT) for ar, bk in zip(ar_b, bk_b)]
    l_ab = [jnp.where(strict_lower, g[:n2, :n2], 0.0) for g in gram]
    l_ak_b = [jnp.where(strict_lower, g[:n2, n2:], 0.0).astype(BF16) for g in gram]
    m_rb_b = [jnp.where(incl_lower, g[n2:, :n2], 0.0).astype(BF16) for g in gram]
    m_rk_b = [jnp.where(incl_lower, g[n2:, n2:], 0.0).astype(BF16) for g in gram]
    t_b = [x.astype(BF16) for x in _unit_lower_inverses(l_ab)]
    lakv_b = [_dot(l, vv).astype(BF16) for l, vv in zip(l_ak_b, v_b)]
    wu_b = [_dot(tb, jnp.concatenate([a, lv], axis=1)).astype(BF16)
            for tb, a, lv in zip(t_b, a_b, lakv_b)]
    mwu = [_dot(m, wu) for m, wu in zip(m_rb_b, wu_b)]
    q_b = [(rs + x[:, :LANES]).astype(BF16) for rs, x in zip(r_s, mwu)]
    y0 = [x[:, LANES:] + _dot(m, vv) for x, m, vv in zip(mwu, m_rk_b, v_b)]
    bp_b = [(b * pc).astype(BF16) for b, pc in zip(b_s, pcs)]
    kp_b = [(kx * pc).astype(BF16) for kx, pc in zip(k_s, pcs)]
    gh = [_dot(wu, bp, TN) for wu, bp in zip(wu_b, bp_b)]
    g_b = [x[:LANES].astype(BF16) for x in gh]
    h = [x[LANES:] + _dot(vv, kp, TN) for x, vv, kp in zip(gh, v_b, kp_b)]

    y_chunks = []
    for ci in range(n_chunks):
        ids = [ci * n_pairs + pi for pi in range(n_pairs)]
        s0 = [state_ref[pi] for pi in range(n_pairs)]
        s0_b = [s.astype(BF16) for s in s0]
        y_s = [_dot(q_b[n], sb, NT) + y0[n] for n, sb in zip(ids, s0_b)]
        s1 = [s * pcs[n] + _dot(sb, g_b[n]) + h[n] for n, s, sb in zip(ids, s0, s0_b)]
        for pi in range(n_pairs):
            state_ref[pi] = s1[pi]
        y_chunks.append(jnp.concatenate([x[:CHUNK] + x[CHUNK:] for x in y_s], axis=1))
    y = jnp.concatenate(y_chunks, axis=0) if n_chunks > 1 else y_chunks[0]

    mean = _mm_exact_rhs(y, head_ones) * (1.0 / HEAD_DIM)
    yc = y - mean
    var = _mm_exact_rhs(yc * yc, head_ones) * (1.0 / HEAD_DIM)
    yn = yc * lax.rsqrt(var + RWKV_LN_EPS) * lng_ref[...] + lnb_ref[...]
    y_ref[0] = (yn + bonus) * gate


def _rwkv(p_rwkv, prm, B, S, ts=RWKV_TILE):
    W = RWKV_WIDTH
    vec = lambda n: pl.BlockSpec((1, n), lambda b, t: (0, 0))
    mat = lambda m, n: pl.BlockSpec((m, n), lambda b, t: (0, 0))
    return pl.pallas_call(
        functools.partial(_rwkv_kernel, ts=ts),
        grid=(B, S // ts),
        in_specs=[pl.BlockSpec((1, ts, RWKV_COLS), lambda b, t: (b, t, 0)),
                  vec(RWKV_COLS)] + [vec(W)] * 7 + [mat(LANES, W), mat(LANES, W), mat(2 * LANES, W)],
        out_specs=pl.BlockSpec((1, ts, W), lambda b, t: (b, t, 0)),
        out_shape=jax.ShapeDtypeStruct((B, S, W), F32),
        scratch_shapes=[pltpu.VMEM((1, RWKV_COLS), F32),
                        pltpu.VMEM((RWKV_HEADS // 2, LANES, LANES), F32)],
        compiler_params=_params("arbitrary", "arbitrary"),
        name="rwkv",
    )(p_rwkv.reshape(B, S, RWKV_COLS), *prm)


def _moba_kernel(slopes_ref, q_ref, k_ref, v_ref, qg_ref, kg_ref, o_ref, *, S):
    pair = pl.program_id(1)
    NB = S // MOBA_BLOCK
    BLK = MOBA_BLOCK
    n_sel = min(MOBA_TOPK, NB)
    scale = HEAD_DIM ** -0.5
    head_ones = _group_ones(LANES, HEAD_DIM)
    lane = _iota2((1, LANES), 1)

    def head_norm(x, g):
        ss = _mm_exact_rhs(x * x, head_ones)
        return x * lax.rsqrt(ss * (1.0 / HEAD_DIM) + NORM_EPS) * g

    qn = head_norm(q_ref[0], qg_ref[...])
    kn = head_norm(k_ref[0], kg_ref[...])
    kmean = jnp.mean(kn.reshape(NB, BLK, LANES), axis=1)
    q_t = qn.T
    v_tb = v_ref[0].T.astype(BF16)

    blk_of_q = _iota2((1, S), 1) // BLK
    nidx = _iota2((NB, 1), 0)
    valid = nidx < blk_of_q
    q_blk = (blk_of_q * BLK).astype(F32)
    q_loc = (_iota2((1, S), 1) % BLK).astype(F32)
    k_blk = (_iota2((S, 1), 0) // BLK * BLK).astype(F32)
    k_loc = (_iota2((S, 1), 0) % BLK).astype(F32)
    q_lane = _iota2((LANES, 1), 0)
    causal = _iota2((BLK, BLK), 0) <= _iota2((BLK, BLK), 1)

    out_rows = []
    for h in range(2):
        hmask = (lane // HEAD_DIM) == h
        slope = slopes_ref[pair * 2 + h]
        spare = (1 - h) * HEAD_DIM
        k_aug = jnp.where(hmask, kn, 0.0)
        k_aug = jnp.where(lane == spare, slope * k_blk, k_aug)
        k_aug = jnp.where(lane == spare + 1, slope * k_loc, k_aug)
        k_aug = jnp.where((lane == spare + 2) | (lane == spare + 3), 1.0, k_aug)
        q_aug = q_t * scale
        q_aug = jnp.where((q_lane == spare) | (q_lane == spare + 1), 1.0, q_aug)
        q_aug = jnp.where(q_lane == spare + 2, -slope * q_blk, q_aug)
        q_aug = jnp.where(q_lane == spare + 3, -slope * q_loc, q_aug)
        q_hb = q_aug.astype(BF16)
        gate = _mm(jnp.where(hmask, kmean, 0.0), q_t, passes=3)
        gate = jnp.where(valid, gate, -jnp.inf)
        rank = jnp.zeros((NB, S), jnp.int32)
        for m in range(NB):
            gm = gate[m:m + 1, :]
            ahead = (gm > gate) | ((gm == gate) & (m < nidx))
            rank = rank + ahead.astype(jnp.int32)
        sel = valid & (rank < n_sel)
        k_hb = k_aug.astype(BF16)
        v_h = v_tb[h * HEAD_DIM:(h + 1) * HEAD_DIM, :]
        out_blocks = []
        for i in range(NB):
            qs = slice(i * BLK, (i + 1) * BLK)
            tiles = []
            m_run = None
            for n in range(i + 1):
                ks = slice(n * BLK, (n + 1) * BLK)
                s = _dot(k_hb[ks, :], q_hb[:, qs])
                if n < i:
                    s = jnp.where(sel[n:n + 1, qs], s, -jnp.inf)
                else:
                    s = jnp.where(causal, s, -jnp.inf)
                tiles.append(s)
                mx = jnp.max(s, axis=0, keepdims=True)
                m_run = mx if m_run is None else jnp.maximum(m_run, mx)
            l_run = jnp.zeros((1, BLK), F32)
            acc = jnp.zeros((HEAD_DIM, BLK), F32)
            for n in range(i + 1):
                ks = slice(n * BLK, (n + 1) * BLK)
                pt = jnp.exp(tiles[n] - m_run)
                l_run = l_run + jnp.sum(pt, axis=0, keepdims=True)
                acc = acc + _dot(v_h[:, ks], pt.astype(BF16))
            out_blocks.append(acc / l_run)
        out_rows.append(jnp.concatenate(out_blocks, axis=1))
    o_ref[0] = jnp.concatenate(out_rows, axis=0).T


def _moba(p_moba, q_norm_g, k_norm_g, B, S):
    pairs = MOBA_HEADS // 2
    col = lambda off: pl.BlockSpec((1, S, LANES), lambda b, p: (b, 0, off + p))
    gain = pl.BlockSpec((1, LANES), lambda b, p: (0, 0))
    tile2 = lambda g: jnp.concatenate([g, g]).reshape(1, LANES)
    p3 = p_moba.reshape(B, S, MOBA_PROJ)
    slopes = jnp.exp2(-8.0 * (jnp.arange(MOBA_HEADS, dtype=F32) + 1.0) / MOBA_HEADS)
    return pl.pallas_call(
        functools.partial(_moba_kernel, S=S),
        grid=(B, pairs),
        in_specs=[pl.BlockSpec(memory_space=pltpu.SMEM), col(0), col(pairs), col(2 * pairs),
                  gain, gain],
        out_specs=pl.BlockSpec((1, S, LANES), lambda b, p: (b, 0, p)),
        out_shape=jax.ShapeDtypeStruct((B, S, MOBA_WIDTH), F32),
        compiler_params=_params("arbitrary", "arbitrary"),
        name="moba",
    )(slopes, p3, p3, p3, tile2(q_norm_g), tile2(k_norm_g))


def _out_proj_kernel(yr_ref, ym_ref, x_ref, gate_ref, g_ref, scale_ref, shift_ref, w_ref,
                     wr_ref, br_ref, x1_ref, h2_ref, idx_ref, wgt_ref):
    W = RWKV_WIDTH
    mix = (_dot(yr_ref[...].astype(BF16), w_ref[0:W, :])
           + _dot(ym_ref[...].astype(BF16), w_ref[W:, :]))
    x1 = x_ref[...] + gate_ref[0] * mix
    x1_ref[...] = x1
    h2 = _rms_modulate(x1, g_ref[...], scale_ref[0], shift_ref[0])
    h2_ref[:, 0, :] = h2
    logits_t = (_mm(h2, wr_ref[...], passes=3) + br_ref[...]).T[:N_EXPERTS, :]
    tm = logits_t.shape[1]
    eidx = _iota2((N_EXPERTS, 1), 0)
    vals, idxs = [], []
    for _ in range(TOP_K):
        m = jnp.max(logits_t, axis=0, keepdims=True)
        idx = jnp.min(jnp.where(logits_t == m, eidx, N_EXPERTS), axis=0, keepdims=True)
        vals.append(m)
        idxs.append(idx)
        logits_t = jnp.where(eidx == idx, -jnp.inf, logits_t)
    idx_ref[...] = jnp.concatenate(idxs, axis=0)
    e = [jnp.exp(v - vals[0]) for v in vals]
    total = e[0] + e[1] + e[2] + e[3]
    wgt_t = jnp.concatenate([x / total for x in e] + [jnp.zeros((LANES - TOP_K, tm), F32)], axis=0)
    wgt_ref[...] = wgt_t.T


def _out_proj(y_rwkv, y_moba, x2, gate1, norm_g, scale, shift, w_out_b, w_router, b_router, S):
    T, D = x2.shape
    tm = 256
    per_b = S // tm
    rows = lambda n: pl.BlockSpec((tm, n), lambda i: (i, 0))
    mod = pl.BlockSpec((1, 1, D), lambda i: (i // per_b, 0, 0))
    full = lambda m, n: pl.BlockSpec((m, n), lambda i: (0, 0))
    wr = _pad_cols(w_router, LANES)
    br = jnp.concatenate([b_router, jnp.full((LANES - N_EXPERTS,), -jnp.inf, F32)]).reshape(1, LANES)
    return pl.pallas_call(
        _out_proj_kernel,
        grid=(T // tm,),
        in_specs=[rows(RWKV_WIDTH), rows(MOBA_WIDTH), rows(D), mod, full(1, D), mod, mod,
                  full(D, D), full(D, LANES), full(1, LANES)],
        out_specs=[rows(D), pl.BlockSpec((tm, 1, D), lambda i: (i, 0, 0)),
                   pl.BlockSpec((TOP_K, tm), lambda i: (0, i)), rows(LANES)],
        out_shape=[jax.ShapeDtypeStruct((T, D), F32), jax.ShapeDtypeStruct((T, 1, D), F32),
                   jax.ShapeDtypeStruct((TOP_K, T), jnp.int32), jax.ShapeDtypeStruct((T, LANES), F32)],
        compiler_params=_params("arbitrary"),
        name="out_proj",
    )(y_rwkv, y_moba, x2, gate1, norm_g.reshape(1, D), scale, shift, w_out_b, wr, br)


def _experts_kernel(meta_ref, src_first_ref, src_next_ref, h_hbm, wgu_ref, bgu_ref, wd_ref, bd_ref,
                    y_hbm, xbuf0, xbuf1, obuf0, obuf1, xb16, wgu16, wd16, gsem, wsem,
                    *, n_tiles):
    j = pl.program_id(0)
    n_used = meta_ref[0]
    tm = EXPERT_TILE
    n_sub = EXPERT_SUBSTEPS
    cols_sub = D_FF // n_sub
    expert = lambda i: meta_ref[1 + jnp.clip(i, 0, n_tiles - 1)]

    def gather_row(src_ref, r, dst_buf, sem, priority=0):
        pltpu.make_async_copy(h_hbm.at[src_ref[0, 0, r]], dst_buf.at[pl.ds(r, 1)], sem).start(
            priority=priority)

    def wait_tile_gather(buf, sem):
        pltpu.make_async_copy(h_hbm.at[pl.ds(0, tm), 0], buf, sem).wait()

    def write_back(buf, tile, sem):
        return pltpu.make_async_copy(buf, y_hbm.at[pl.ds(tile * tm, tm), 0], sem)

    @pl.when(j == 0)
    def _():
        def body(r, carry):
            gather_row(src_first_ref, r, xbuf0, gsem.at[0])
            return carry
        lax.fori_loop(0, tm, body, 0)

    @pl.when((j < n_used) & ((j == 0) | (expert(j) != expert(j - 1))))
    def _():
        wgu16[...] = wgu_ref[0].astype(BF16)
        wd16[...] = wd_ref[0].astype(BF16)

    def ffn_chunk(n, o_cur):
        cg = slice(n * cols_sub, (n + 1) * cols_sub)
        cu = slice(D_FF + n * cols_sub, D_FF + (n + 1) * cols_sub)
        xb = xb16[...]
        gate = jnp.minimum(_dot(xb, wgu16[:, cg]) + bgu_ref[0, :, cg], SWIGLU_LIMIT)
        up = jnp.clip(_dot(xb, wgu16[:, cu]) + bgu_ref[0, :, cu], -SWIGLU_LIMIT, SWIGLU_LIMIT)
        act = (up + 1.0) * gate * _sigmoid(SWIGLU_ALPHA * gate)
        o_cur[...] += _dot(act.astype(BF16), wd16[cg, :])

    def tile_step(ph):
        x_cur, x_nxt = (xbuf0, xbuf1) if ph == 0 else (xbuf1, xbuf0)
        o_cur, o_prv = (obuf0, obuf1) if ph == 0 else (obuf1, obuf0)
        mine = j % 2 == ph

        @pl.when(mine & (j < n_used))
        def _():
            wait_tile_gather(x_cur, gsem.at[ph])

            @pl.when(j >= 2)
            def _():
                write_back(o_cur, j - 2, wsem.at[ph]).wait()

            for r in range(tm):
                gather_row(src_next_ref, r, x_nxt, gsem.at[1 - ph], priority=r % 2)
            xb16[...] = x_cur[...].astype(BF16)
            o_cur[...] = jnp.broadcast_to(bd_ref[0], (tm, D_MODEL))

        @pl.when(mine & (j + 1 <= n_used))
        def _():
            for n in range(n_sub):
                ffn_chunk(n, o_cur)
            write_back(o_cur, j, wsem.at[ph]).start()

        @pl.when(mine & (j == n_used))
        def _():
            wait_tile_gather(x_cur, gsem.at[ph])
            write_back(o_prv, j - 1, wsem.at[1 - ph]).wait()

            @pl.when(j >= 2)
            def _():
                write_back(o_cur, j - 2, wsem.at[ph]).wait()

        @pl.when(mine & (j >= n_used) & (j < n_tiles))
        def _():
            o_cur[...] = jnp.zeros_like(o_cur)
            fill = write_back(o_cur, j, wsem.at[ph])
            fill.start()
            fill.wait()

    tile_step(0)
    tile_step(1)


def _experts(h2, src_tok, tile_expert, n_used, wgu, bgu, wd, bd):
    D = h2.shape[-1]
    tm = EXPERT_TILE
    n_tiles = tile_expert.shape[0]
    src3 = src_tok.reshape(n_tiles + 1, 1, tm)
    meta = jnp.concatenate([n_used.reshape(1), tile_expert]).astype(jnp.int32)
    smem_row = lambda f: pl.BlockSpec((1, 1, tm), f, memory_space=pltpu.SMEM)
    e_of = lambda j, m: m[1 + jnp.minimum(j, n_tiles - 1)]
    grid_spec = pltpu.PrefetchScalarGridSpec(
        num_scalar_prefetch=1,
        grid=(n_tiles + 1,),
        in_specs=[smem_row(lambda j, m: (0, 0, 0)),
                  smem_row(lambda j, m: (jnp.minimum(j + 1, n_tiles), 0, 0)),
                  pl.BlockSpec(memory_space=pl.ANY),
                  pl.BlockSpec((1, D, 2 * D_FF), lambda j, m: (e_of(j, m), 0, 0)),
                  pl.BlockSpec((1, 1, 2 * D_FF), lambda j, m: (e_of(j, m), 0, 0)),
                  pl.BlockSpec((1, D_FF, D), lambda j, m: (e_of(j, m), 0, 0)),
                  pl.BlockSpec((1, 1, D), lambda j, m: (e_of(j, m), 0, 0))],
        out_specs=pl.BlockSpec(memory_space=pl.ANY),
        scratch_shapes=[pltpu.VMEM((tm, D), F32), pltpu.VMEM((tm, D), F32),
                        pltpu.VMEM((tm, D), F32), pltpu.VMEM((tm, D), F32),
                        pltpu.VMEM((tm, D), BF16),
                        pltpu.VMEM((D, 2 * D_FF), BF16), pltpu.VMEM((D_FF, D), BF16),
                        pltpu.SemaphoreType.DMA((2,)), pltpu.SemaphoreType.DMA((2,))],
    )
    return pl.pallas_call(
        functools.partial(_experts_kernel, n_tiles=n_tiles),
        grid_spec=grid_spec,
        out_shape=jax.ShapeDtypeStruct((n_tiles * tm, 1, D), F32),
        compiler_params=_params("arbitrary"),
        name="experts",
    )(meta, src3, src3, h2, wgu, bgu.reshape(N_EXPERTS, 1, 2 * D_FF), wd, bd.reshape(N_EXPERTS, 1, D))


def _route(top_idx_t):
    T = top_idx_t.shape[1]
    tm = EXPERT_TILE
    M = T * TOP_K
    slot_expert = top_idx_t.reshape(M)
    order = jnp.argsort(slot_expert).astype(jnp.int32)
    rank = jnp.argsort(order).astype(jnp.int32)
    counts = jnp.bincount(slot_expert, length=N_EXPERTS)
    padded = (counts + tm - 1) // tm * tm
    pad_end = jnp.cumsum(padded)
    pad_start = pad_end - padded
    start = jnp.cumsum(counts) - counts
    n_tiles = M // tm + N_EXPERTS
    tile_start = jnp.arange(n_tiles) * tm
    tile_expert = jnp.minimum(jnp.sum(pad_end[None, :] <= tile_start[:, None], axis=1), N_EXPERTS - 1)
    tile_valid = jnp.clip(counts[tile_expert] - (tile_start - pad_start[tile_expert]), 0, tm)
    n_used = pad_end[-1] // tm
    r = jnp.arange(tm)[None, :]
    sorted_pos = (start[tile_expert] + tile_start - pad_start[tile_expert])[:, None] + r
    slot = order[jnp.clip(sorted_pos, 0, M - 1)]
    src_tok = jnp.where(r < tile_valid[:, None], slot % T, 0).astype(jnp.int32)
    src_ext = jnp.concatenate([src_tok, jnp.zeros((1, tm), jnp.int32)], axis=0)
    slot_row = (pad_start[slot_expert] + rank - start[slot_expert]).astype(jnp.int32)
    return src_ext, tile_expert.astype(jnp.int32), n_used.astype(jnp.int32), slot_row


COMBINE_TILE = 256


def _combine_kernel(row_first_ref, row_next_ref, x1_ref, w_ref, gate_ref, y_hbm, o_ref,
                    ybuf0, ybuf1, sem, *, n_steps):
    i = pl.program_id(0)
    tm = COMBINE_TILE
    n_rows = TOP_K * tm

    def fetch_row(row_ref, r, buf, s, priority=0):
        pltpu.make_async_copy(y_hbm.at[row_ref[0, 0, r]], buf.at[pl.ds(r, 1)], s).start(priority=priority)

    @pl.when(i == 0)
    def _():
        def body(r, carry):
            fetch_row(row_first_ref, r, ybuf0, sem.at[0])
            return carry
        lax.fori_loop(0, n_rows, body, 0)

    def step(ph):
        cur, nxt = (ybuf0, ybuf1) if ph == 0 else (ybuf1, ybuf0)
        mine = i % 2 == ph

        @pl.when(mine & (i + 1 < n_steps))
        def _():
            for r in range(n_rows):
                fetch_row(row_next_ref, r, nxt, sem.at[1 - ph], priority=r % 2)

        @pl.when(mine)
        def _():
            pltpu.make_async_copy(y_hbm.at[pl.ds(0, n_rows), 0], cur, sem.at[ph]).wait()
            w = w_ref[...]
            acc = w[:, 0:1] * cur[0:tm, :]
            for k in range(1, TOP_K):
                acc = acc + w[:, k:k + 1] * cur[k * tm:(k + 1) * tm, :]
            o_ref[...] = x1_ref[...] + gate_ref[0] * acc

    step(0)
    step(1)


def _combine(x1, y_sorted, slot_row, weights, gate2, S):
    T, D = x1.shape
    tm = COMBINE_TILE
    per_b = S // tm
    n_steps = T // tm
    n_rows = TOP_K * tm
    rows = slot_row.reshape(TOP_K, n_steps, tm).transpose(1, 0, 2).reshape(n_steps, 1, n_rows)
    smem_row = lambda f: pl.BlockSpec((1, 1, n_rows), f, memory_space=pltpu.SMEM)
    return pl.pallas_call(
        functools.partial(_combine_kernel, n_steps=n_steps),
        grid=(n_steps,),
        in_specs=[smem_row(lambda i: (0, 0, 0)),
                  smem_row(lambda i: (jnp.minimum(i + 1, n_steps - 1), 0, 0)),
                  pl.BlockSpec((tm, D), lambda i: (i, 0)),
                  pl.BlockSpec((tm, LANES), lambda i: (i, 0)),
                  pl.BlockSpec((1, 1, D), lambda i: (i // per_b, 0, 0)),
                  pl.BlockSpec(memory_space=pl.ANY)],
        out_specs=pl.BlockSpec((tm, D), lambda i: (i, 0)),
        out_shape=jax.ShapeDtypeStruct((T, D), F32),
        scratch_shapes=[pltpu.VMEM((n_rows, D), F32), pltpu.VMEM((n_rows, D), F32),
                        pltpu.SemaphoreType.DMA((2,))],
        compiler_params=_params("arbitrary"),
        name="combine",
    )(rows, rows, x1, weights, gate2, y_sorted)


def _pad_cols(w, n):
    return jnp.pad(w, ((0, 0), (0, n - w.shape[1])))


def _pad_rows(w, n):
    return jnp.pad(w, ((0, n - w.shape[0]), (0, 0)))


def _layer(x, c, w_ada, b_ada, norm1_g, w_in, rwkv_mu, rwkv_w0, rwkv_w_up, rwkv_a0, rwkv_a_up,
           rwkv_g_up, rwkv_k_k, rwkv_k_a, rwkv_r_k, rwkv_ln_g, rwkv_ln_b, q_norm_g, k_norm_g,
           w_out, norm2_g, w_router, b_router, w_gate_up, b_gate_up, w_down, b_down):
    B, S, D = x.shape
    T = B * S
    W = RWKV_WIDTH
    x2 = x.reshape(T, D)

    mods = _ada(c, w_ada, b_ada)
    shift1, scale1, gate1, shift2, scale2, gate2 = [
        mods[:, j * D:(j + 1) * D].reshape(B, 1, D) for j in range(6)]

    pieces = [(w_in[:, :XW_OFF], XW_OFF),
              (w_in[:, XW_OFF:XW_OFF + DECAY_LORA], LANES),
              (w_in[:, XW_OFF + DECAY_LORA:XW_OFF + DECAY_LORA + AAA_LORA], LANES),
              (w_in[:, XW_OFF + DECAY_LORA + AAA_LORA:RWKV_PROJ], 2 * LANES),
              (w_in[:, RWKV_PROJ:], MOBA_PROJ)]
    w_in_b = jnp.concatenate([_pad_cols(w, n) for w, n in pieces], axis=1).astype(BF16)
    mu_pieces = [(rwkv_mu[None, :XW_OFF], XW_OFF),
                 (rwkv_mu[None, XW_OFF:XW_OFF + DECAY_LORA], LANES),
                 (rwkv_mu[None, XW_OFF + DECAY_LORA:XW_OFF + DECAY_LORA + AAA_LORA], LANES),
                 (rwkv_mu[None, XW_OFF + DECAY_LORA + AAA_LORA:], 2 * LANES)]
    mu = jnp.concatenate([_pad_cols(m, n) for m, n in mu_pieces], axis=1)

    p_rwkv, p_moba = _in_proj(x2, norm1_g, scale1, shift1, w_in_b, S)

    row = lambda a: a.reshape(1, W)
    prm = (mu, row(rwkv_w0), row(rwkv_a0), row(rwkv_k_k), row(rwkv_k_a), row(rwkv_r_k),
           row(rwkv_ln_g), row(rwkv_ln_b), _pad_rows(rwkv_w_up, LANES), _pad_rows(rwkv_a_up, LANES),
           _pad_rows(rwkv_g_up, 2 * LANES))
    y_rwkv = _rwkv(p_rwkv, prm, B, S)
    y_moba = _moba(p_moba, q_norm_g, k_norm_g, B, S)

    x1, h2, top_idx_t, weights = _out_proj(y_rwkv.reshape(T, W), y_moba.reshape(T, MOBA_WIDTH), x2,
                                           gate1, norm2_g, scale2, shift2, w_out.astype(BF16),
                                           w_router, b_router, S)

    src_tok, tile_expert, n_used, slot_row = _route(top_idx_t)
    y_sorted = _experts(h2, src_tok, tile_expert, n_used, w_gate_up, b_gate_up, w_down, b_down)
    out = _combine(x1, y_sorted, slot_row, weights, gate2, S)
    return out.reshape(B, S, D)


def kernel(x, c, w_ada, b_ada, norm1_g, w_in, rwkv_mu, rwkv_w0, rwkv_w_up, rwkv_a0, rwkv_a_up, rwkv_g_up, rwkv_k_k, rwkv_k_a, rwkv_r_k, rwkv_ln_g, rwkv_ln_b, q_norm_g, k_norm_g, w_out, norm2_g, w_router, b_router, w_gate_up, b_gate_up, w_down, b_down):
    for l in range(w_ada.shape[0]):
        x = _layer(x, c, w_ada[l], b_ada[l], norm1_g[l], w_in[l], rwkv_mu[l], rwkv_w0[l],
                   rwkv_w_up[l], rwkv_a0[l], rwkv_a_up[l], rwkv_g_up[l], rwkv_k_k[l], rwkv_k_a[l],
                   rwkv_r_k[l], rwkv_ln_g[l], rwkv_ln_b[l], q_norm_g[l], k_norm_g[l], w_out[l],
                   norm2_g[l], w_router[l], b_router[l], w_gate_up[l], b_gate_up[l], w_down[l],
                   b_down[l])
    return x
```

```python
import functools

import jax
import jax.numpy as jnp
from jax import lax
from jax.experimental import pallas as pl
from jax.experimental.pallas import tpu as pltpu

F32 = jnp.float32
BF16 = jnp.bfloat16

D_MODEL = 1024
HEAD_DIM = 64
RWKV_WIDTH = 512
MOBA_WIDTH = 512
RWKV_HEADS = RWKV_WIDTH // HEAD_DIM
MOBA_HEADS = MOBA_WIDTH // HEAD_DIM
DECAY_LORA = 64
AAA_LORA = 64
GATE_LORA = 160
RWKV_LN_EPS = 64e-5
RWKV_PROJ = 3 * RWKV_WIDTH + DECAY_LORA + AAA_LORA + GATE_LORA
MOBA_PROJ = 3 * MOBA_WIDTH
MOBA_BLOCK = 256
MOBA_TOPK = 3
N_EXPERTS = 32
TOP_K = 4
D_FF = D_MODEL
SWIGLU_LIMIT = 7.0
SWIGLU_ALPHA = 1.702
NORM_EPS = 1e-6

LANES = 128
SUBLANES = 8
XW_OFF = 3 * RWKV_WIDTH
XA_OFF = XW_OFF + LANES
XG_OFF = XA_OFF + LANES
RWKV_COLS = XG_OFF + 2 * LANES
CHUNK = 64
EXPERT_TILE = 512
EXPERT_SUBSTEPS = 2
VMEM_LIMIT = 56 * 1024 * 1024

NN = (((1,), (0,)), ((), ()))
NT = (((1,), (1,)), ((), ()))
TN = (((0,), (0,)), ((), ()))


def _dot(a, b, dims=NN):
    return lax.dot_general(a, b, dims, preferred_element_type=F32)


def _split(a):
    hi = a.astype(BF16)
    lo = (a - hi.astype(F32)).astype(BF16)
    return hi, lo


def _mm(a, b, dims=NN, passes=1):
    if passes == 1:
        return _dot(a.astype(BF16), b.astype(BF16), dims)
    a_hi, a_lo = _split(a)
    b_hi, b_lo = _split(b)
    return _dot(a_hi, b_hi, dims) + (_dot(a_hi, b_lo, dims) + _dot(a_lo, b_hi, dims))


def _split3(a):
    hi = a.astype(BF16)
    r = a - hi.astype(F32)
    mid = r.astype(BF16)
    lo = (r - mid.astype(F32)).astype(BF16)
    return hi, mid, lo


def _mm_exact_rhs(a, b_bf16, dims=NN):
    hi, mid, lo = _split3(a)
    return _dot(hi, b_bf16, dims) + (_dot(mid, b_bf16, dims) + _dot(lo, b_bf16, dims))


def _mm_exact_lhs(a_bf16, b, dims=NN):
    hi, mid, lo = _split3(b)
    return _dot(a_bf16, hi, dims) + (_dot(a_bf16, mid, dims) + _dot(a_bf16, lo, dims))


def _iota2(shape, dim):
    return lax.broadcasted_iota(jnp.int32, shape, dim)


def _group_ones(n, group):
    return (_iota2((n, n), 0) // group == _iota2((n, n), 1) // group).astype(BF16)


def _sigmoid(x):
    return 1.0 / (1.0 + jnp.exp(-x))


def _params(*sem):
    return pltpu.CompilerParams(dimension_semantics=sem, vmem_limit_bytes=VMEM_LIMIT)


def _ada_kernel(c_ref, w_ref, b_ref, o_ref):
    c = c_ref[...]
    o_ref[...] = _mm(c * _sigmoid(c), w_ref[...], passes=3) + b_ref[...]


def _ada(c, w_ada, b_ada):
    B, D = c.shape
    n_out = w_ada.shape[1]
    tn = 1024
    return pl.pallas_call(
        _ada_kernel,
        grid=(n_out // tn,),
        in_specs=[pl.BlockSpec((B, D), lambda j: (0, 0)),
                  pl.BlockSpec((D, tn), lambda j: (0, j)),
                  pl.BlockSpec((1, tn), lambda j: (0, j))],
        out_specs=pl.BlockSpec((B, tn), lambda j: (0, j)),
        out_shape=jax.ShapeDtypeStruct((B, n_out), F32),
        compiler_params=_params("arbitrary"),
        name="ada",
    )(c, w_ada, b_ada.reshape(1, n_out))


def _rms_modulate(x, g, scale, shift):
    y = x * lax.rsqrt(jnp.mean(x * x, axis=-1, keepdims=True) + NORM_EPS)
    return (y * g) * (1.0 + scale) + shift


def _in_proj_kernel(x_ref, g_ref, scale_ref, shift_ref, w_ref, pr_ref, pm_ref):
    h = _rms_modulate(x_ref[...], g_ref[...], scale_ref[0], shift_ref[0])
    proj = _dot(h.astype(BF16), w_ref[...])
    pr_ref[...] = proj[:, :RWKV_COLS]
    pm_ref[...] = proj[:, RWKV_COLS:]


def _in_proj(x2, norm_g, scale, shift, w_in_b, S):
    T, D = x2.shape
    tm = 256
    per_b = S // tm
    n_cols = w_in_b.shape[1]
    return pl.pallas_call(
        _in_proj_kernel,
        grid=(T // tm,),
        in_specs=[pl.BlockSpec((tm, D), lambda i: (i, 0)),
                  pl.BlockSpec((1, D), lambda i: (0, 0)),
                  pl.BlockSpec((1, 1, D), lambda i: (i // per_b, 0, 0)),
                  pl.BlockSpec((1, 1, D), lambda i: (i // per_b, 0, 0)),
                  pl.BlockSpec((D, n_cols), lambda i: (0, 0))],
        out_specs=[pl.BlockSpec((tm, RWKV_COLS), lambda i: (i, 0)),
                   pl.BlockSpec((tm, MOBA_PROJ), lambda i: (i, 0))],
        out_shape=[jax.ShapeDtypeStruct((T, RWKV_COLS), F32),
                   jax.ShapeDtypeStruct((T, MOBA_PROJ), F32)],
        compiler_params=_params("arbitrary"),
        name="in_proj",
    )(x2, norm_g.reshape(1, D), scale, shift, w_in_b)


RWKV_TILE = 256


def _unit_lower_inverses(Ls):
    n = Ls[0].shape[0]
    r = _iota2((n, n), 0)
    c = _iota2((n, n), 1)
    eye = (r == c).astype(F32)
    in_block = r // 8 == c // 8
    b16 = lambda xs: [x.astype(BF16) for x in xs]
    Ld = [jnp.where(in_block, L, 0.0) for L in Ls]
    Ld_b = b16(Ld)
    Nb_b = [(L - d).astype(BF16) for L, d in zip(Ls, Ld)]
    L2 = [_dot(d, d) for d in Ld_b]
    L2_b = b16(L2)
    L4_b = b16([_dot(x, x) for x in L2_b])
    T0 = [eye + d + l2 + _dot(db, l2b) for d, l2, db, l2b in zip(Ld, L2, Ld_b, L2_b)]
    T0 = [t + _dot(t.astype(BF16), l4b) for t, l4b in zip(T0, L4_b)]
    T0_b = b16(T0)
    M1_b = b16([_dot(t, nb) for t, nb in zip(T0_b, Nb_b)])
    M2_b = b16([_dot(m, m) for m in M1_b])
    M4_b = b16([_dot(m, m) for m in M2_b])
    X = [t + _dot(m4, tb) for t, m4, tb in zip(T0, M4_b, T0_b)]
    X = [x + _dot(m2, x.astype(BF16)) for x, m2 in zip(X, M2_b)]
    return [x + _dot(m1, x.astype(BF16)) for x, m1 in zip(X, M1_b)]


def _rwkv_kernel(p_ref, mu_ref, w0_ref, a0_ref, kk_ref, ka_ref, rk_ref, lng_ref, lnb_ref,
                 wup_ref, aup_ref, gup_ref, y_ref, carry_ref, state_ref, *, ts):
    t = pl.program_id(1)
    W = RWKV_WIDTH

    @pl.when(t == 0)
    def _():
        carry_ref[...] = jnp.zeros_like(carry_ref)
        state_ref[...] = jnp.zeros_like(state_ref)

    p = p_ref[0]
    row = _iota2((ts, 1), 0)
    prev = jnp.where(row == 0, carry_ref[...], pltpu.roll(p, 1, 0))
    carry_ref[...] = p[ts - 1:ts, :]
    pm = p + (prev - p) * mu_ref[...]
    r = pm[:, 0:W]
    k = pm[:, W:2 * W]
    v = pm[:, 2 * W:3 * W]
    xw = pm[:, XW_OFF:XA_OFF]
    xa = pm[:, XA_OFF:XG_OFF]
    xg = pm[:, XG_OFF:RWKV_COLS]

    z = -(w0_ref[...] + _mm(jnp.tanh(xw), wup_ref[...], passes=3))
    softplus = jnp.maximum(z, 0.0) + jnp.log(1.0 + jnp.exp(-jnp.abs(z)))
    logd = -jnp.exp(-softplus - 0.5)
    alpha = _sigmoid(a0_ref[...] + _mm(xa, aup_ref[...], passes=3))
    gate = _mm(_sigmoid(xg), gup_ref[...])

    head_ones = _group_ones(W, HEAD_DIM)
    kk = k * kk_ref[...]
    kk_norm = jnp.sqrt(_mm_exact_rhs(kk * kk, head_ones))
    kk = kk / jnp.maximum(kk_norm, 1e-12)
    kmod = k * (1.0 + (alpha - 1.0) * ka_ref[...])
    bonus = _mm_exact_rhs(r * kmod * rk_ref[...], head_ones) * v

    tr = _iota2((ts, ts), 0)
    tc = _iota2((ts, ts), 1)
    cum = ((tr // CHUNK == tc // CHUNK) & (tc <= tr)).astype(BF16)
    logp = _mm_exact_lhs(cum, logd)
    inv_p = jnp.exp(-logp)
    a_t = -kk * jnp.exp(logp - logd)
    b_t = kk * alpha * inv_p
    k_t = kmod * inv_p
    r_t = r * jnp.exp(logp)

    n2 = 2 * CHUNK
    n_chunks = ts // CHUNK
    n_pairs = RWKV_HEADS // 2
    inst = [(ci, pi) for ci in range(n_chunks) for pi in range(n_pairs)]
    lane = _iota2((1, LANES), 1)
    m0 = lane < HEAD_DIM
    sr = _iota2((n2, n2), 0)
    sc = _iota2((n2, n2), 1)
    strict_lower = sc < sr
    incl_lower = sc <= sr

    def stacked(x, ci, pi):
        xt = x[ci * CHUNK:(ci + 1) * CHUNK, pi * LANES:(pi + 1) * LANES]
        return jnp.concatenate([jnp.where(m0, xt, 0.0), jnp.where(m0, 0.0, xt)], axis=0)

    pcs = [jnp.exp(logp[(ci + 1) * CHUNK - 1:(ci + 1) * CHUNK, pi * LANES:(pi + 1) * LANES])
           for ci, pi in inst]
    r_s = [stacked(r_t, ci, pi) for ci, pi in inst]
    a_b = [stacked(a_t, ci, pi).astype(BF16) for ci, pi in inst]
    r_b = [x.astype(BF16) for x in r_s]
    b_s = [stacked(b_t, ci, pi) for ci, pi in inst]
    k_s = [stacked(k_t, ci, pi) for ci, pi in inst]
    v_b = [stacked(v, ci, pi).astype(BF16) for ci, pi in inst]
    ar_b = [jnp.concatenate([a, rr], axis=0) for a, rr in zip(a_b, r_b)]
    bk_b = [jnp.concatenate([b.astype(BF16), kx.astype(BF16)], axis=0) for b, kx in zip(b_s, k_s)]
    gram = [_dot(ar, bk, NT) for ar, bk in zip(ar_b, bk_b)]
    l_ab = [jnp.where(strict_lower, g[:n2, :n2], 0.0) for g in gram]
    l_ak_b = [jnp.where(strict_lower, g[:n2, n2:], 0.0).astype(BF16) for g in gram]
    m_rb_b = [jnp.where(incl_lower, g[n2:, :n2], 0.0).astype(BF16) for g in gram]
    m_rk_b = [jnp.where(incl_lower, g[n2:, n2:], 0.0).astype(BF16) for g in gram]
    t_b = [x.astype(BF16) for x in _unit_lower_inverses(l_ab)]
    lakv_b = [_dot(l, vv).astype(BF16) for l, vv in zip(l_ak_b, v_b)]
    wu_b = [_dot(tb, jnp.concatenate([a, lv], axis=1)).astype(BF16)
            for tb, a, lv in zip(t_b, a_b, lakv_b)]
    mwu = [_dot(m, wu) for m, wu in zip(m_rb_b, wu_b)]
    q_b = [(rs + x[:, :LANES]).astype(BF16) for rs, x in zip(r_s, mwu)]
    y0 = [x[:, LANES:] + _dot(m, vv) for x, m, vv in zip(mwu, m_rk_b, v_b)]
    bp_b = [(b * pc).astype(BF16) for b, pc in zip(b_s, pcs)]
    kp_b = [(kx * pc).astype(BF16) for kx, pc in zip(k_s, pcs)]
    gh = [_dot(wu, bp, TN) for wu, bp in zip(wu_b, bp_b)]
    g_b = [x[:LANES].astype(BF16) for x in gh]
    h = [x[LANES:] + _dot(vv, kp, TN) for x, vv, kp in zip(gh, v_b, kp_b)]

    y_chunks = []
    for ci in range(n_chunks):
        ids = [ci * n_pairs + pi for pi in range(n_pairs)]
        s0 = [state_ref[pi] for pi in range(n_pairs)]
        s0_b = [s.astype(BF16) for s in s0]
        y_s = [_dot(q_b[n], sb, NT) + y0[n] for n, sb in zip(ids, s0_b)]
        s1 = [s * pcs[n] + _dot(sb, g_b[n]) + h[n] for n, s, sb in zip(ids, s0, s0_b)]
        for pi in range(n_pairs):
            state_ref[pi] = s1[pi]
        y_chunks.append(jnp.concatenate([x[:CHUNK] + x[CHUNK:] for x in y_s], axis=1))
    y = jnp.concatenate(y_chunks, axis=0) if n_chunks > 1 else y_chunks[0]

    mean = _mm_exact_rhs(y, head_ones) * (1.0 / HEAD_DIM)
    yc = y - mean
    var = _mm_exact_rhs(yc * yc, head_ones) * (1.0 / HEAD_DIM)
    yn = yc * lax.rsqrt(var + RWKV_LN_EPS) * lng_ref[...] + lnb_ref[...]
    y_ref[0] = (yn + bonus) * gate


def _rwkv(p_rwkv, prm, B, S, ts=RWKV_TILE):
    W = RWKV_WIDTH
    vec = lambda n: pl.BlockSpec((1, n), lambda b, t: (0, 0))
    mat = lambda m, n: pl.BlockSpec((m, n), lambda b, t: (0, 0))
    return pl.pallas_call(
        functools.partial(_rwkv_kernel, ts=ts),
        grid=(B, S // ts),
        in_specs=[pl.BlockSpec((1, ts, RWKV_COLS), lambda b, t: (b, t, 0)),
                  vec(RWKV_COLS)] + [vec(W)] * 7 + [mat(LANES, W), mat(LANES, W), mat(2 * LANES, W)],
        out_specs=pl.BlockSpec((1, ts, W), lambda b, t: (b, t, 0)),
        out_shape=jax.ShapeDtypeStruct((B, S, W), F32),
        scratch_shapes=[pltpu.VMEM((1, RWKV_COLS), F32),
                        pltpu.VMEM((RWKV_HEADS // 2, LANES, LANES), F32)],
        compiler_params=_params("arbitrary", "arbitrary"),
        name="rwkv",
    )(p_rwkv.reshape(B, S, RWKV_COLS), *prm)


def _moba_kernel(slopes_ref, q_ref, k_ref, v_ref, qg_ref, kg_ref, o_ref, *, S):
    pair = pl.program_id(1)
    NB = S // MOBA_BLOCK
    BLK = MOBA_BLOCK
    n_sel = min(MOBA_TOPK, NB)
    scale = HEAD_DIM ** -0.5
    head_ones = _group_ones(LANES, HEAD_DIM)
    lane = _iota2((1, LANES), 1)

    def head_norm(x, g):
        ss = _mm_exact_rhs(x * x, head_ones)
        return x * lax.rsqrt(ss * (1.0 / HEAD_DIM) + NORM_EPS) * g

    qn = head_norm(q_ref[0], qg_ref[...])
    kn = head_norm(k_ref[0], kg_ref[...])
    kmean = jnp.mean(kn.reshape(NB, BLK, LANES), axis=1)
    q_t = qn.T
    v_tb = v_ref[0].T.astype(BF16)

    blk_of_q = _iota2((1, S), 1) // BLK
    nidx = _iota2((NB, 1), 0)
    valid = nidx < blk_of_q
    q_blk = (blk_of_q * BLK).astype(F32)
    q_loc = (_iota2((1, S), 1) % BLK).astype(F32)
    k_blk = (_iota2((S, 1), 0) // BLK * BLK).astype(F32)
    k_loc = (_iota2((S, 1), 0) % BLK).astype(F32)
    q_lane = _iota2((LANES, 1), 0)
    causal = _iota2((BLK, BLK), 0) <= _iota2((BLK, BLK), 1)

    out_rows = []
    for h in range(2):
        hmask = (lane // HEAD_DIM) == h
        slope = slopes_ref[pair * 2 + h]
        spare = (1 - h) * HEAD_DIM
        k_aug = jnp.where(hmask, kn, 0.0)
        k_aug = jnp.where(lane == spare, slope * k_blk, k_aug)
        k_aug = jnp.where(lane == spare + 1, slope * k_loc, k_aug)
        k_aug = jnp.where((lane == spare + 2) | (lane == spare + 3), 1.0, k_aug)
        q_aug = q_t * scale
        q_aug = jnp.where((q_lane == spare) | (q_lane == spare + 1), 1.0, q_aug)
        q_aug = jnp.where(q_lane == spare + 2, -slope * q_blk, q_aug)
        q_aug = jnp.where(q_lane == spare + 3, -slope * q_loc, q_aug)
        q_hb = q_aug.astype(BF16)
        gate = _mm(jnp.where(hmask, kmean, 0.0), q_t, passes=3)
        gate = jnp.where(valid, gate, -jnp.inf)
        rank = jnp.zeros((NB, S), jnp.int32)
        for m in range(NB):
            gm = gate[m:m + 1, :]
            ahead = (gm > gate) | ((gm == gate) & (m < nidx))
            rank = rank + ahead.astype(jnp.int32)
        sel = valid & (rank < n_sel)
        k_hb = k_aug.astype(BF16)
        v_h = v_tb[h * HEAD_DIM:(h + 1) * HEAD_DIM, :]
        out_blocks = []
        for i in range(NB):
            qs = slice(i * BLK, (i + 1) * BLK)
            tiles = []
            m_run = None
            for n in range(i + 1):
                ks = slice(n * BLK, (n + 1) * BLK)
                s = _dot(k_hb[ks, :], q_hb[:, qs])
                if n < i:
                    s = jnp.where(sel[n:n + 1, qs], s, -jnp.inf)
                else:
                    s = jnp.where(causal, s, -jnp.inf)
                tiles.append(s)
                mx = jnp.max(s, axis=0, keepdims=True)
                m_run = mx if m_run is None else jnp.maximum(m_run, mx)
            l_run = jnp.zeros((1, BLK), F32)
            acc = jnp.zeros((HEAD_DIM, BLK), F32)
            for n in range(i + 1):
                ks = slice(n * BLK, (n + 1) * BLK)
                pt = jnp.exp(tiles[n] - m_run)
                l_run = l_run + jnp.sum(pt, axis=0, keepdims=True)
                acc = acc + _dot(v_h[:, ks], pt.astype(BF16))
            out_blocks.append(acc / l_run)
        out_rows.append(jnp.concatenate(out_blocks, axis=1))
    o_ref[0] = jnp.concatenate(out_rows, axis=0).T


def _moba(p_moba, q_norm_g, k_norm_g, B, S):
    pairs = MOBA_HEADS // 2
    col = lambda off: pl.BlockSpec((1, S, LANES), lambda b, p: (b, 0, off + p))
    gain = pl.BlockSpec((1, LANES), lambda b, p: (0, 0))
    tile2 = lambda g: jnp.concatenate([g, g]).reshape(1, LANES)
    p3 = p_moba.reshape(B, S, MOBA_PROJ)
    slopes = jnp.exp2(-8.0 * (jnp.arange(MOBA_HEADS, dtype=F32) + 1.0) / MOBA_HEADS)
    return pl.pallas_call(
        functools.partial(_moba_kernel, S=S),
        grid=(B, pairs),
        in_specs=[pl.BlockSpec(memory_space=pltpu.SMEM), col(0), col(pairs), col(2 * pairs),
                  gain, gain],
        out_specs=pl.BlockSpec((1, S, LANES), lambda b, p: (b, 0, p)),
        out_shape=jax.ShapeDtypeStruct((B, S, MOBA_WIDTH), F32),
        compiler_params=_params("arbitrary", "arbitrary"),
        name="moba",
    )(slopes, p3, p3, p3, tile2(q_norm_g), tile2(k_norm_g))


def _out_proj_kernel(yr_ref, ym_ref, x_ref, gate_ref, g_ref, scale_ref, shift_ref, w_ref,
                     wr_ref, br_ref, x1_ref, h2_ref, idx_ref, wgt_ref):
    W = RWKV_WIDTH
    mix = (_dot(yr_ref[...].astype(BF16), w_ref[0:W, :])
           + _dot(ym_ref[...].astype(BF16), w_ref[W:, :]))
    x1 = x_ref[...] + gate_ref[0] * mix
    x1_ref[...] = x1
    h2 = _rms_modulate(x1, g_ref[...], scale_ref[0], shift_ref[0])
    h2_ref[:, 0, :] = h2
    logits_t = (_mm(h2, wr_ref[...], passes=3) + br_ref[...]).T[:N_EXPERTS, :]
    tm = logits_t.shape[1]
    eidx = _iota2((N_EXPERTS, 1), 0)
    vals, idxs = [], []
    for _ in range(TOP_K):
        m = jnp.max(logits_t, axis=0, keepdims=True)
        idx = jnp.min(jnp.where(logits_t == m, eidx, N_EXPERTS), axis=0, keepdims=True)
        vals.append(m)
        idxs.append(idx)
        logits_t = jnp.where(eidx == idx, -jnp.inf, logits_t)
    idx_ref[...] = jnp.concatenate(idxs, axis=0)
    e = [jnp.exp(v - vals[0]) for v in vals]
    total = e[0] + e[1] + e[2] + e[3]
    wgt_t = jnp.concatenate([x / total for x in e] + [jnp.zeros((LANES - TOP_K, tm), F32)], axis=0)
    wgt_ref[...] = wgt_t.T


def _out_proj(y_rwkv, y_moba, x2, gate1, norm_g, scale, shift, w_out_b, w_router, b_router, S):
    T, D = x2.shape
    tm = 256
    per_b = S // tm
    rows = lambda n: pl.BlockSpec((tm, n), lambda i: (i, 0))
    mod = pl.BlockSpec((1, 1, D), lambda i: (i // per_b, 0, 0))
    full = lambda m, n: pl.BlockSpec((m, n), lambda i: (0, 0))
    wr = _pad_cols(w_router, LANES)
    br = jnp.concatenate([b_router, jnp.full((LANES - N_EXPERTS,), -jnp.inf, F32)]).reshape(1, LANES)
    return pl.pallas_call(
        _out_proj_kernel,
        grid=(T // tm,),
        in_specs=[rows(RWKV_WIDTH), rows(MOBA_WIDTH), rows(D), mod, full(1, D), mod, mod,
                  full(D, D), full(D, LANES), full(1, LANES)],
        out_specs=[rows(D), pl.BlockSpec((tm, 1, D), lambda i: (i, 0, 0)),
                   pl.BlockSpec((TOP_K, tm), lambda i: (0, i)), rows(LANES)],
        out_shape=[jax.ShapeDtypeStruct((T, D), F32), jax.ShapeDtypeStruct((T, 1, D), F32),
                   jax.ShapeDtypeStruct((TOP_K, T), jnp.int32), jax.ShapeDtypeStruct((T, LANES), F32)],
        compiler_params=_params("arbitrary"),
        name="out_proj",
    )(y_rwkv, y_moba, x2, gate1, norm_g.reshape(1, D), scale, shift, w_out_b, wr, br)


def _experts_kernel(meta_ref, src_first_ref, src_next_ref, h_hbm, wgu_ref, bgu_ref, wd_ref, bd_ref,
                    y_hbm, xbuf0, xbuf1, obuf0, obuf1, xb16, wgu16, wd16, gsem, wsem,
                    *, n_tiles):
    j = pl.program_id(0)
    n_used = meta_ref[0]
    tm = EXPERT_TILE
    n_sub = EXPERT_SUBSTEPS
    cols_sub = D_FF // n_sub
    expert = lambda i: meta_ref[1 + jnp.clip(i, 0, n_tiles - 1)]

    def gather_row(src_ref, r, dst_buf, sem, priority=0):
        pltpu.make_async_copy(h_hbm.at[src_ref[0, 0, r]], dst_buf.at[pl.ds(r, 1)], sem).start(
            priority=priority)

    def wait_tile_gather(buf, sem):
        pltpu.make_async_copy(h_hbm.at[pl.ds(0, tm), 0], buf, sem).wait()

    def write_back(buf, tile, sem):
        return pltpu.make_async_copy(buf, y_hbm.at[pl.ds(tile * tm, tm), 0], sem)

    @pl.when(j == 0)
    def _():
        def body(r, carry):
            gather_row(src_first_ref, r, xbuf0, gsem.at[0])
            return carry
        lax.fori_loop(0, tm, body, 0)

    @pl.when((j < n_used) & ((j == 0) | (expert(j) != expert(j - 1))))
    def _():
        wgu16[...] = wgu_ref[0].astype(BF16)
        wd16[...] = wd_ref[0].astype(BF16)

    def ffn_chunk(n, o_cur):
        cg = slice(n * cols_sub, (n + 1) * cols_sub)
        cu = slice(D_FF + n * cols_sub, D_FF + (n + 1) * cols_sub)
        xb = xb16[...]
        gate = jnp.minimum(_dot(xb, wgu16[:, cg]) + bgu_ref[0, :, cg], SWIGLU_LIMIT)
        up = jnp.clip(_dot(xb, wgu16[:, cu]) + bgu_ref[0, :, cu], -SWIGLU_LIMIT, SWIGLU_LIMIT)
        act = (up + 1.0) * gate * _sigmoid(SWIGLU_ALPHA * gate)
        o_cur[...] += _dot(act.astype(BF16), wd16[cg, :])

    def tile_step(ph):
        x_cur, x_nxt = (xbuf0, xbuf1) if ph == 0 else (xbuf1, xbuf0)
        o_cur, o_prv = (obuf0, obuf1) if ph == 0 else (obuf1, obuf0)
        mine = j % 2 == ph

        @pl.when(mine & (j < n_used))
        def _():
            wait_tile_gather(x_cur, gsem.at[ph])

            @pl.when(j >= 2)
            def _():
                write_back(o_cur, j - 2, wsem.at[ph]).wait()

            for r in range(tm):
                gather_row(src_next_ref, r, x_nxt, gsem.at[1 - ph], priority=r % 2)
            xb16[...] = x_cur[...].astype(BF16)
            o_cur[...] = jnp.broadcast_to(bd_ref[0], (tm, D_MODEL))

        @pl.when(mine & (j + 1 <= n_used))
        def _():
            for n in range(n_sub):
                ffn_chunk(n, o_cur)
            write_back(o_cur, j, wsem.at[ph]).start()

        @pl.when(mine & (j == n_used))
        def _():
            wait_tile_gather(x_cur, gsem.at[ph])
            write_back(o_prv, j - 1, wsem.at[1 - ph]).wait()

            @pl.when(j >= 2)
            def _():
                write_back(o_cur, j - 2, wsem.at[ph]).wait()

        @pl.when(mine & (j >= n_used) & (j < n_tiles))
        def _():
            o_cur[...] = jnp.zeros_like(o_cur)
            fill = write_back(o_cur, j, wsem.at[ph])
            fill.start()
            fill.wait()

    tile_step(0)
    tile_step(1)


def _experts(h2, src_tok, tile_expert, n_used, wgu, bgu, wd, bd):
    D = h2.shape[-1]
    tm = EXPERT_TILE
    n_tiles = tile_expert.shape[0]
    src3 = src_tok.reshape(n_tiles + 1, 1, tm)
    meta = jnp.concatenate([n_used.reshape(1), tile_expert]).astype(jnp.int32)
    smem_row = lambda f: pl.BlockSpec((1, 1, tm), f, memory_space=pltpu.SMEM)
    e_of = lambda j, m: m[1 + jnp.minimum(j, n_tiles - 1)]
    grid_spec = pltpu.PrefetchScalarGridSpec(
        num_scalar_prefetch=1,
        grid=(n_tiles + 1,),
        in_specs=[smem_row(lambda j, m: (0, 0, 0)),
                  smem_row(lambda j, m: (jnp.minimum(j + 1, n_tiles), 0, 0)),
                  pl.BlockSpec(memory_space=pl.ANY),
                  pl.BlockSpec((1, D, 2 * D_FF), lambda j, m: (e_of(j, m), 0, 0)),
                  pl.BlockSpec((1, 1, 2 * D_FF), lambda j, m: (e_of(j, m), 0, 0)),
                  pl.BlockSpec((1, D_FF, D), lambda j, m: (e_of(j, m), 0, 0)),
                  pl.BlockSpec((1, 1, D), lambda j, m: (e_of(j, m), 0, 0))],
        out_specs=pl.BlockSpec(memory_space=pl.ANY),
        scratch_shapes=[pltpu.VMEM((tm, D), F32), pltpu.VMEM((tm, D), F32),
                        pltpu.VMEM((tm, D), F32), pltpu.VMEM((tm, D), F32),
                        pltpu.VMEM((tm, D), BF16),
                        pltpu.VMEM((D, 2 * D_FF), BF16), pltpu.VMEM((D_FF, D), BF16),
                        pltpu.SemaphoreType.DMA((2,)), pltpu.SemaphoreType.DMA((2,))],
    )
    return pl.pallas_call(
        functools.partial(_experts_kernel, n_tiles=n_tiles),
        grid_spec=grid_spec,
        out_shape=jax.ShapeDtypeStruct((n_tiles * tm, 1, D), F32),
        compiler_params=_params("arbitrary"),
        name="experts",
    )(meta, src3, src3, h2, wgu, bgu.reshape(N_EXPERTS, 1, 2 * D_FF), wd, bd.reshape(N_EXPERTS, 1, D))


def _route(top_idx_t):
    T = top_idx_t.shape[1]
    tm = EXPERT_TILE
    M = T * TOP_K
    slot_expert = top_idx_t.reshape(M)
    order = jnp.argsort(slot_expert).astype(jnp.int32)
    rank = jnp.argsort(order).astype(jnp.int32)
    counts = jnp.bincount(slot_expert, length=N_EXPERTS)
    padded = (counts + tm - 1) // tm * tm
    pad_end = jnp.cumsum(padded)
    pad_start = pad_end - padded
    start = jnp.cumsum(counts) - counts
    n_tiles = M // tm + N_EXPERTS
    tile_start = jnp.arange(n_tiles) * tm
    tile_expert = jnp.minimum(jnp.sum(pad_end[None, :] <= tile_start[:, None], axis=1), N_EXPERTS - 1)
    tile_valid = jnp.clip(counts[tile_expert] - (tile_start - pad_start[tile_expert]), 0, tm)
    n_used = pad_end[-1] // tm
    r = jnp.arange(tm)[None, :]
    sorted_pos = (start[tile_expert] + tile_start - pad_start[tile_expert])[:, None] + r
    slot = order[jnp.clip(sorted_pos, 0, M - 1)]
    src_tok = jnp.where(r < tile_valid[:, None], slot % T, 0).astype(jnp.int32)
    src_ext = jnp.concatenate([src_tok, jnp.zeros((1, tm), jnp.int32)], axis=0)
    slot_row = (pad_start[slot_expert] + rank - start[slot_expert]).astype(jnp.int32)
    return src_ext, tile_expert.astype(jnp.int32), n_used.astype(jnp.int32), slot_row


COMBINE_TILE = 256


def _combine_kernel(row_first_ref, row_next_ref, x1_ref, w_ref, gate_ref, y_hbm, o_ref,
                    ybuf0, ybuf1, sem, *, n_steps):
    i = pl.program_id(0)
    tm = COMBINE_TILE
    n_rows = TOP_K * tm

    def fetch_row(row_ref, r, buf, s, priority=0):
        pltpu.make_async_copy(y_hbm.at[row_ref[0, 0, r]], buf.at[pl.ds(r, 1)], s).start(priority=priority)

    @pl.when(i == 0)
    def _():
        def body(r, carry):
            fetch_row(row_first_ref, r, ybuf0, sem.at[0])
            return carry
        lax.fori_loop(0, n_rows, body, 0)

    def step(ph):
        cur, nxt = (ybuf0, ybuf1) if ph == 0 else (ybuf1, ybuf0)
        mine = i % 2 == ph

        @pl.when(mine & (i + 1 < n_steps))
        def _():
            for r in range(n_rows):
                fetch_row(row_next_ref, r, nxt, sem.at[1 - ph], priority=r % 2)

        @pl.when(mine)
        def _():
            pltpu.make_async_copy(y_hbm.at[pl.ds(0, n_rows), 0], cur, sem.at[ph]).wait()
            w = w_ref[...]
            acc = w[:, 0:1] * cur[0:tm, :]
            for k in range(1, TOP_K):
                acc = acc + w[:, k:k + 1] * cur[k * tm:(k + 1) * tm, :]
            o_ref[...] = x1_ref[...] + gate_ref[0] * acc

    step(0)
    step(1)


def _combine(x1, y_sorted, slot_row, weights, gate2, S):
    T, D = x1.shape
    tm = COMBINE_TILE
    per_b = S // tm
    n_steps = T // tm
    n_rows = TOP_K * tm
    rows = slot_row.reshape(TOP_K, n_steps, tm).transpose(1, 0, 2).reshape(n_steps, 1, n_rows)
    smem_row = lambda f: pl.BlockSpec((1, 1, n_rows), f, memory_space=pltpu.SMEM)
    return pl.pallas_call(
        functools.partial(_combine_kernel, n_steps=n_steps),
        grid=(n_steps,),
        in_specs=[smem_row(lambda i: (0, 0, 0)),
                  smem_row(lambda i: (jnp.minimum(i + 1, n_steps - 1), 0, 0)),
                  pl.BlockSpec((tm, D), lambda i: (i, 0)),
                  pl.BlockSpec((tm, LANES), lambda i: (i, 0)),
                  pl.BlockSpec((1, 1, D), lambda i: (i // per_b, 0, 0)),
                  pl.BlockSpec(memory_space=pl.ANY)],
        out_specs=pl.BlockSpec((tm, D), lambda i: (i, 0)),
        out_shape=jax.ShapeDtypeStruct((T, D), F32),
        scratch_shapes=[pltpu.VMEM((n_rows, D), F32), pltpu.VMEM((n_rows, D), F32),
                        pltpu.SemaphoreType.DMA((2,))],
        compiler_params=_params("arbitrary"),
        name="combine",
    )(rows, rows, x1, weights, gate2, y_sorted)


def _pad_cols(w, n):
    return jnp.pad(w, ((0, 0), (0, n - w.shape[1])))


def _pad_rows(w, n):
    return jnp.pad(w, ((0, n - w.shape[0]), (0, 0)))


def _layer(x, c, w_ada, b_ada, norm1_g, w_in, rwkv_mu, rwkv_w0, rwkv_w_up, rwkv_a0, rwkv_a_up,
           rwkv_g_up, rwkv_k_k, rwkv_k_a, rwkv_r_k, rwkv_ln_g, rwkv_ln_b, q_norm_g, k_norm_g,
           w_out, norm2_g, w_router, b_router, w_gate_up, b_gate_up, w_down, b_down):
    B, S, D = x.shape
    T = B * S
    W = RWKV_WIDTH
    x2 = x.reshape(T, D)

    mods = _ada(c, w_ada, b_ada)
    shift1, scale1, gate1, shift2, scale2, gate2 = [
        mods[:, j * D:(j + 1) * D].reshape(B, 1, D) for j in range(6)]

    pieces = [(w_in[:, :XW_OFF], XW_OFF),
              (w_in[:, XW_OFF:XW_OFF + DECAY_LORA], LANES),
              (w_in[:, XW_OFF + DECAY_LORA:XW_OFF + DECAY_LORA + AAA_LORA], LANES),
              (w_in[:, XW_OFF + DECAY_LORA + AAA_LORA:RWKV_PROJ], 2 * LANES),
              (w_in[:, RWKV_PROJ:], MOBA_PROJ)]
    w_in_b = jnp.concatenate([_pad_cols(w, n) for w, n in pieces], axis=1).astype(BF16)
    mu_pieces = [(rwkv_mu[None, :XW_OFF], XW_OFF),
                 (rwkv_mu[None, XW_OFF:XW_OFF + DECAY_LORA], LANES),
                 (rwkv_mu[None, XW_OFF + DECAY_LORA:XW_OFF + DECAY_LORA + AAA_LORA], LANES),
                 (rwkv_mu[None, XW_OFF + DECAY_LORA + AAA_LORA:], 2 * LANES)]
    mu = jnp.concatenate([_pad_cols(m, n) for m, n in mu_pieces], axis=1)

    p_rwkv, p_moba = _in_proj(x2, norm1_g, scale1, shift1, w_in_b, S)

    row = lambda a: a.reshape(1, W)
    prm = (mu, row(rwkv_w0), row(rwkv_a0), row(rwkv_k_k), row(rwkv_k_a), row(rwkv_r_k),
           row(rwkv_ln_g), row(rwkv_ln_b), _pad_rows(rwkv_w_up, LANES), _pad_rows(rwkv_a_up, LANES),
           _pad_rows(rwkv_g_up, 2 * LANES))
    y_rwkv = _rwkv(p_rwkv, prm, B, S)
    y_moba = _moba(p_moba, q_norm_g, k_norm_g, B, S)

    x1, h2, top_idx_t, weights = _out_proj(y_rwkv.reshape(T, W), y_moba.reshape(T, MOBA_WIDTH), x2,
                                           gate1, norm2_g, scale2, shift2, w_out.astype(BF16),
                                           w_router, b_router, S)

    src_tok, tile_expert, n_used, slot_row = _route(top_idx_t)
    y_sorted = _experts(h2, src_tok, tile_expert, n_used, w_gate_up, b_gate_up, w_down, b_down)
    out = _combine(x1, y_sorted, slot_row, weights, gate2, S)
    return out.reshape(B, S, D)


def kernel(x, c, w_ada, b_ada, norm1_g, w_in, rwkv_mu, rwkv_w0, rwkv_w_up, rwkv_a0, rwkv_a_up, rwkv_g_up, rwkv_k_k, rwkv_k_a, rwkv_r_k, rwkv_ln_g, rwkv_ln_b, q_norm_g, k_norm_g, w_out, norm2_g, w_router, b_router, w_gate_up, b_gate_up, w_down, b_down):
    for l in range(w_ada.shape[0]):
        x = _layer(x, c, w_ada[l], b_ada[l], norm1_g[l], w_in[l], rwkv_mu[l], rwkv_w0[l],
                   rwkv_w_up[l], rwkv_a0[l], rwkv_a_up[l], rwkv_g_up[l], rwkv_k_k[l], rwkv_k_a[l],
                   rwkv_r_k[l], rwkv_ln_g[l], rwkv_ln_b[l], q_norm_g[l], k_norm_g[l], w_out[l],
                   norm2_g[l], w_router[l], b_router[l], w_gate_up[l], b_gate_up[l], w_down[l],
                   b_down[l])
    return x
```

```python
import functools

import jax
import jax.numpy as jnp
from jax import lax
from jax.experimental import pallas as pl
from jax.experimental.pallas import tpu as pltpu

F32 = jnp.float32
BF16 = jnp.bfloat16

D_MODEL = 1024
HEAD_DIM = 64
RWKV_WIDTH = 512
MOBA_WIDTH = 512
RWKV_HEADS = RWKV_WIDTH // HEAD_DIM
MOBA_HEADS = MOBA_WIDTH // HEAD_DIM
DECAY_LORA = 64
AAA_LORA = 64
GATE_LORA = 160
RWKV_LN_EPS = 64e-5
RWKV_PROJ = 3 * RWKV_WIDTH + DECAY_LORA + AAA_LORA + GATE_LORA
MOBA_PROJ = 3 * MOBA_WIDTH
MOBA_BLOCK = 256
MOBA_TOPK = 3
N_EXPERTS = 32
TOP_K = 4
D_FF = D_MODEL
SWIGLU_LIMIT = 7.0
SWIGLU_ALPHA = 1.702
NORM_EPS = 1e-6

LANES = 128
SUBLANES = 8
XW_OFF = 3 * RWKV_WIDTH
XA_OFF = XW_OFF + LANES
XG_OFF = XA_OFF + LANES
RWKV_COLS = XG_OFF + 2 * LANES
CHUNK = 64
EXPERT_TILE = 512
EXPERT_SUBSTEPS = 1
VMEM_LIMIT = 56 * 1024 * 1024

NN = (((1,), (0,)), ((), ()))
NT = (((1,), (1,)), ((), ()))
TN = (((0,), (0,)), ((), ()))


def _dot(a, b, dims=NN):
    return lax.dot_general(a, b, dims, preferred_element_type=F32)


def _split(a):
    hi = a.astype(BF16)
    lo = (a - hi.astype(F32)).astype(BF16)
    return hi, lo


def _mm(a, b, dims=NN, passes=1):
    if passes == 1:
        return _dot(a.astype(BF16), b.astype(BF16), dims)
    a_hi, a_lo = _split(a)
    b_hi, b_lo = _split(b)
    return _dot(a_hi, b_hi, dims) + (_dot(a_hi, b_lo, dims) + _dot(a_lo, b_hi, dims))


def _split3(a):
    hi = a.astype(BF16)
    r = a - hi.astype(F32)
    mid = r.astype(BF16)
    lo = (r - mid.astype(F32)).astype(BF16)
    return hi, mid, lo


def _mm_exact_rhs(a, b_bf16, dims=NN):
    hi, mid, lo = _split3(a)
    return _dot(hi, b_bf16, dims) + (_dot(mid, b_bf16, dims) + _dot(lo, b_bf16, dims))


def _mm_exact_lhs(a_bf16, b, dims=NN):
    hi, mid, lo = _split3(b)
    return _dot(a_bf16, hi, dims) + (_dot(a_bf16, mid, dims) + _dot(a_bf16, lo, dims))


def _iota2(shape, dim):
    return lax.broadcasted_iota(jnp.int32, shape, dim)


def _group_ones(n, group):
    return (_iota2((n, n), 0) // group == _iota2((n, n), 1) // group).astype(BF16)


def _sigmoid(x):
    return 1.0 / (1.0 + jnp.exp(-x))


def _params(*sem):
    return pltpu.CompilerParams(dimension_semantics=sem, vmem_limit_bytes=VMEM_LIMIT)


def _ada_kernel(c_ref, w_ref, b_ref, o_ref):
    c = c_ref[...]
    o_ref[...] = _mm(c * _sigmoid(c), w_ref[...], passes=3) + b_ref[...]


def _ada(c, w_ada, b_ada):
    B, D = c.shape
    n_out = w_ada.shape[1]
    tn = 1024
    return pl.pallas_call(
        _ada_kernel,
        grid=(n_out // tn,),
        in_specs=[pl.BlockSpec((B, D), lambda j: (0, 0)),
                  pl.BlockSpec((D, tn), lambda j: (0, j)),
                  pl.BlockSpec((1, tn), lambda j: (0, j))],
        out_specs=pl.BlockSpec((B, tn), lambda j: (0, j)),
        out_shape=jax.ShapeDtypeStruct((B, n_out), F32),
        compiler_params=_params("arbitrary"),
        name="ada",
    )(c, w_ada, b_ada.reshape(1, n_out))


def _rms_modulate(x, g, scale, shift):
    y = x * lax.rsqrt(jnp.mean(x * x, axis=-1, keepdims=True) + NORM_EPS)
    return (y * g) * (1.0 + scale) + shift


def _in_proj_kernel(x_ref, g_ref, scale_ref, shift_ref, w_ref, pr_ref, pm_ref):
    h = _rms_modulate(x_ref[...], g_ref[...], scale_ref[0], shift_ref[0])
    proj = _dot(h.astype(BF16), w_ref[...])
    pr_ref[...] = proj[:, :RWKV_COLS]
    pm_ref[...] = proj[:, RWKV_COLS:]


def _in_proj(x2, norm_g, scale, shift, w_in_b, S):
    T, D = x2.shape
    tm = 256
    per_b = S // tm
    n_cols = w_in_b.shape[1]
    return pl.pallas_call(
        _in_proj_kernel,
        grid=(T // tm,),
        in_specs=[pl.BlockSpec((tm, D), lambda i: (i, 0)),
                  pl.BlockSpec((1, D), lambda i: (0, 0)),
                  pl.BlockSpec((1, 1, D), lambda i: (i // per_b, 0, 0)),
                  pl.BlockSpec((1, 1, D), lambda i: (i // per_b, 0, 0)),
                  pl.BlockSpec((D, n_cols), lambda i: (0, 0))],
        out_specs=[pl.BlockSpec((tm, RWKV_COLS), lambda i: (i, 0)),
                   pl.BlockSpec((tm, MOBA_PROJ), lambda i: (i, 0))],
        out_shape=[jax.ShapeDtypeStruct((T, RWKV_COLS), F32),
                   jax.ShapeDtypeStruct((T, MOBA_PROJ), F32)],
        compiler_params=_params("arbitrary"),
        name="in_proj",
    )(x2, norm_g.reshape(1, D), scale, shift, w_in_b)


RWKV_TILE = 256


def _unit_lower_inverses(Ls):
    n = Ls[0].shape[0]
    r = _iota2((n, n), 0)
    c = _iota2((n, n), 1)
    eye = (r == c).astype(F32)
    in_block = r // 8 == c // 8
    b16 = lambda xs: [x.astype(BF16) for x in xs]
    Ld = [jnp.where(in_block, L, 0.0) for L in Ls]
    Ld_b = b16(Ld)
    Nb_b = [(L - d).astype(BF16) for L, d in zip(Ls, Ld)]
    L2 = [_dot(d, d) for d in Ld_b]
    L2_b = b16(L2)
    L4_b = b16([_dot(x, x) for x in L2_b])
    T0 = [eye + d + l2 + _dot(db, l2b) for d, l2, db, l2b in zip(Ld, L2, Ld_b, L2_b)]
    T0 = [t + _dot(t.astype(BF16), l4b) for t, l4b in zip(T0, L4_b)]
    T0_b = b16(T0)
    M1_b = b16([_dot(t, nb) for t, nb in zip(T0_b, Nb_b)])
    M2_b = b16([_dot(m, m) for m in M1_b])
    M4_b = b16([_dot(m, m) for m in M2_b])
    X = [t + _dot(m4, tb) for t, m4, tb in zip(T0, M4_b, T0_b)]
    X = [x + _dot(m2, x.astype(BF16)) for x, m2 in zip(X, M2_b)]
    return [x + _dot(m1, x.astype(BF16)) for x, m1 in zip(X, M1_b)]


def _rwkv_kernel(p_ref, mu_ref, w0_ref, a0_ref, kk_ref, ka_ref, rk_ref, lng_ref, lnb_ref,
                 wup_ref, aup_ref, gup_ref, y_ref, carry_ref, state_ref, *, ts):
    t = pl.program_id(1)
    W = RWKV_WIDTH

    @pl.when(t == 0)
    def _():
        carry_ref[...] = jnp.zeros_like(carry_ref)
        state_ref[...] = jnp.zeros_like(state_ref)

    p = p_ref[0]
    row = _iota2((ts, 1), 0)
    prev = jnp.where(row == 0, carry_ref[...], pltpu.roll(p, 1, 0))
    carry_ref[...] = p[ts - 1:ts, :]
    pm = p + (prev - p) * mu_ref[...]
    r = pm[:, 0:W]
    k = pm[:, W:2 * W]
    v = pm[:, 2 * W:3 * W]
    xw = pm[:, XW_OFF:XA_OFF]
    xa = pm[:, XA_OFF:XG_OFF]
    xg = pm[:, XG_OFF:RWKV_COLS]

    z = -(w0_ref[...] + _mm(jnp.tanh(xw), wup_ref[...], passes=3))
    softplus = jnp.maximum(z, 0.0) + jnp.log(1.0 + jnp.exp(-jnp.abs(z)))
    logd = -jnp.exp(-softplus - 0.5)
    alpha = _sigmoid(a0_ref[...] + _mm(xa, aup_ref[...], passes=3))
    gate = _mm(_sigmoid(xg), gup_ref[...])

    head_ones = _group_ones(W, HEAD_DIM)
    kk = k * kk_ref[...]
    kk_norm = jnp.sqrt(_mm_exact_rhs(kk * kk, head_ones))
    kk = kk / jnp.maximum(kk_norm, 1e-12)
    kmod = k * (1.0 + (alpha - 1.0) * ka_ref[...])
    bonus = _mm_exact_rhs(r * kmod * rk_ref[...], head_ones) * v

    tr = _iota2((ts, ts), 0)
    tc = _iota2((ts, ts), 1)
    cum = ((tr // CHUNK == tc // CHUNK) & (tc <= tr)).astype(BF16)
    logp = _mm_exact_lhs(cum, logd)
    inv_p = jnp.exp(-logp)
    a_t = -kk * jnp.exp(logp - logd)
    b_t = kk * alpha * inv_p
    k_t = kmod * inv_p
    r_t = r * jnp.exp(logp)

    n2 = 2 * CHUNK
    n_chunks = ts // CHUNK
    n_pairs = RWKV_HEADS // 2
    inst = [(ci, pi) for ci in range(n_chunks) for pi in range(n_pairs)]
    lane = _iota2((1, LANES), 1)
    m0 = lane < HEAD_DIM
    sr = _iota2((n2, n2), 0)
    sc = _iota2((n2, n2), 1)
    strict_lower = sc < sr
    incl_lower = sc <= sr

    def stacked(x, ci, pi):
        xt = x[ci * CHUNK:(ci + 1) * CHUNK, pi * LANES:(pi + 1) * LANES]
        return jnp.concatenate([jnp.where(m0, xt, 0.0), jnp.where(m0, 0.0, xt)], axis=0)

    pcs = [jnp.exp(logp[(ci + 1) * CHUNK - 1:(ci + 1) * CHUNK, pi * LANES:(pi + 1) * LANES])
           for ci, pi in inst]
    r_s = [stacked(r_t, ci, pi) for ci, pi in inst]
    a_b = [stacked(a_t, ci, pi).astype(BF16) for ci, pi in inst]
    r_b = [x.astype(BF16) for x in r_s]
    b_s = [stacked(b_t, ci, pi) for ci, pi in inst]
    k_s = [stacked(k_t, ci, pi) for ci, pi in inst]
    v_b = [stacked(v, ci, pi).astype(BF16) for ci, pi in inst]
    ar_b = [jnp.concatenate([a, rr], axis=0) for a, rr in zip(a_b, r_b)]
    bk_b = [jnp.concatenate([b.astype(BF16), kx.astype(BF16)], axis=0) for b, kx in zip(b_s, k_s)]
    gram = [_dot(ar, bk, NT) for ar, bk in zip(ar_b, bk_b)]
    l_ab = [jnp.where(strict_lower, g[:n2, :n2], 0.0) for g in gram]
    l_ak_b = [jnp.where(strict_lower, g[:n2, n2:], 0.0).astype(BF16) for g in gram]
    m_rb_b = [jnp.where(incl_lower, g[n2:, :n2], 0.0).astype(BF16) for g in gram]
    m_rk_b = [jnp.where(incl_lower, g[n2:, n2:], 0.0).astype(BF16) for g in gram]
    t_b = [x.astype(BF16) for x in _unit_lower_inverses(l_ab)]
    lakv_b = [_dot(l, vv).astype(BF16) for l, vv in zip(l_ak_b, v_b)]
    wu_b = [_dot(tb, jnp.concatenate([a, lv], axis=1)).astype(BF16)
            for tb, a, lv in zip(t_b, a_b, lakv_b)]
    mwu = [_dot(m, wu) for m, wu in zip(m_rb_b, wu_b)]
    q_b = [(rs + x[:, :LANES]).astype(BF16) for rs, x in zip(r_s, mwu)]
    y0 = [x[:, LANES:] + _dot(m, vv) for x, m, vv in zip(mwu, m_rk_b, v_b)]
    bp_b = [(b * pc).astype(BF16) for b, pc in zip(b_s, pcs)]
    kp_b = [(kx * pc).astype(BF16) for kx, pc in zip(k_s, pcs)]
    gh = [_dot(wu, bp, TN) for wu, bp in zip(wu_b, bp_b)]
    g_b = [x[:LANES].astype(BF16) for x in gh]
    h = [x[LANES:] + _dot(vv, kp, TN) for x, vv, kp in zip(gh, v_b, kp_b)]

    y_chunks = []
    for ci in range(n_chunks):
        ids = [ci * n_pairs + pi for pi in range(n_pairs)]
        s0 = [state_ref[pi] for pi in range(n_pairs)]
        s0_b = [s.astype(BF16) for s in s0]
        y_s = [_dot(q_b[n], sb, NT) + y0[n] for n, sb in zip(ids, s0_b)]
        s1 = [s * pcs[n] + _dot(sb, g_b[n]) + h[n] for n, s, sb in zip(ids, s0, s0_b)]
        for pi in range(n_pairs):
            state_ref[pi] = s1[pi]
        y_chunks.append(jnp.concatenate([x[:CHUNK] + x[CHUNK:] for x in y_s], axis=1))
    y = jnp.concatenate(y_chunks, axis=0) if n_chunks > 1 else y_chunks[0]

    mean = _mm_exact_rhs(y, head_ones) * (1.0 / HEAD_DIM)
    yc = y - mean
    var = _mm_exact_rhs(yc * yc, head_ones) * (1.0 / HEAD_DIM)
    yn = yc * lax.rsqrt(var + RWKV_LN_EPS) * lng_ref[...] + lnb_ref[...]
    y_ref[0] = (yn + bonus) * gate


def _rwkv(p_rwkv, prm, B, S, ts=RWKV_TILE):
    W = RWKV_WIDTH
    vec = lambda n: pl.BlockSpec((1, n), lambda b, t: (0, 0))
    mat = lambda m, n: pl.BlockSpec((m, n), lambda b, t: (0, 0))
    return pl.pallas_call(
        functools.partial(_rwkv_kernel, ts=ts),
        grid=(B, S // ts),
        in_specs=[pl.BlockSpec((1, ts, RWKV_COLS), lambda b, t: (b, t, 0)),
                  vec(RWKV_COLS)] + [vec(W)] * 7 + [mat(LANES, W), mat(LANES, W), mat(2 * LANES, W)],
        out_specs=pl.BlockSpec((1, ts, W), lambda b, t: (b, t, 0)),
        out_shape=jax.ShapeDtypeStruct((B, S, W), F32),
        scratch_shapes=[pltpu.VMEM((1, RWKV_COLS), F32),
                        pltpu.VMEM((RWKV_HEADS // 2, LANES, LANES), F32)],
        compiler_params=_params("arbitrary", "arbitrary"),
        name="rwkv",
    )(p_rwkv.reshape(B, S, RWKV_COLS), *prm)


def _moba_kernel(slopes_ref, q_ref, k_ref, v_ref, qg_ref, kg_ref, o_ref, *, S):
    pair = pl.program_id(1)
    NB = S // MOBA_BLOCK
    BLK = MOBA_BLOCK
    n_sel = min(MOBA_TOPK, NB)
    scale = HEAD_DIM ** -0.5
    head_ones = _group_ones(LANES, HEAD_DIM)
    lane = _iota2((1, LANES), 1)

    def head_norm(x, g):
        ss = _mm_exact_rhs(x * x, head_ones)
        return x * lax.rsqrt(ss * (1.0 / HEAD_DIM) + NORM_EPS) * g

    qn = head_norm(q_ref[0], qg_ref[...])
    kn = head_norm(k_ref[0], kg_ref[...])
    kmean = jnp.mean(kn.reshape(NB, BLK, LANES), axis=1)
    q_t = qn.T
    v_tb = v_ref[0].T.astype(BF16)

    blk_of_q = _iota2((1, S), 1) // BLK
    nidx = _iota2((NB, 1), 0)
    valid = nidx < blk_of_q
    q_blk = (blk_of_q * BLK).astype(F32)
    q_loc = (_iota2((1, S), 1) % BLK).astype(F32)
    k_blk = (_iota2((S, 1), 0) // BLK * BLK).astype(F32)
    k_loc = (_iota2((S, 1), 0) % BLK).astype(F32)
    q_lane = _iota2((LANES, 1), 0)
    causal = _iota2((BLK, BLK), 0) <= _iota2((BLK, BLK), 1)

    out_rows = []
    for h in range(2):
        hmask = (lane // HEAD_DIM) == h
        slope = slopes_ref[pair * 2 + h]
        spare = (1 - h) * HEAD_DIM
        k_aug = jnp.where(hmask, kn, 0.0)
        k_aug = jnp.where(lane == spare, slope * k_blk, k_aug)
        k_aug = jnp.where(lane == spare + 1, slope * k_loc, k_aug)
        k_aug = jnp.where((lane == spare + 2) | (lane == spare + 3), 1.0, k_aug)
        q_aug = q_t * scale
        q_aug = jnp.where((q_lane == spare) | (q_lane == spare + 1), 1.0, q_aug)
        q_aug = jnp.where(q_lane == spare + 2, -slope * q_blk, q_aug)
        q_aug = jnp.where(q_lane == spare + 3, -slope * q_loc, q_aug)
        q_hb = q_aug.astype(BF16)
        gate = _mm(jnp.where(hmask, kmean, 0.0), q_t, passes=3)
        gate = jnp.where(valid, gate, -jnp.inf)
        rank = jnp.zeros((NB, S), jnp.int32)
        for m in range(NB):
            gm = gate[m:m + 1, :]
            ahead = (gm > gate) | ((gm == gate) & (m < nidx))
            rank = rank + ahead.astype(jnp.int32)
        sel = valid & (rank < n_sel)
        k_hb = k_aug.astype(BF16)
        v_h = v_tb[h * HEAD_DIM:(h + 1) * HEAD_DIM, :]
        out_blocks = []
        for i in range(NB):
            qs = slice(i * BLK, (i + 1) * BLK)
            tiles = []
            m_run = None
            for n in range(i + 1):
                ks = slice(n * BLK, (n + 1) * BLK)
                s = _dot(k_hb[ks, :], q_hb[:, qs])
                if n < i:
                    s = jnp.where(sel[n:n + 1, qs], s, -jnp.inf)
                else:
                    s = jnp.where(causal, s, -jnp.inf)
                tiles.append(s)
                mx = jnp.max(s, axis=0, keepdims=True)
                m_run = mx if m_run is None else jnp.maximum(m_run, mx)
            l_run = jnp.zeros((1, BLK), F32)
            acc = jnp.zeros((HEAD_DIM, BLK), F32)
            for n in range(i + 1):
                ks = slice(n * BLK, (n + 1) * BLK)
                pt = jnp.exp(tiles[n] - m_run)
                l_run = l_run + jnp.sum(pt, axis=0, keepdims=True)
                acc = acc + _dot(v_h[:, ks], pt.astype(BF16))
            out_blocks.append(acc / l_run)
        out_rows.append(jnp.concatenate(out_blocks, axis=1))
    o_ref[0] = jnp.concatenate(out_rows, axis=0).T


def _moba(p_moba, q_norm_g, k_norm_g, B, S):
    pairs = MOBA_HEADS // 2
    col = lambda off: pl.BlockSpec((1, S, LANES), lambda b, p: (b, 0, off + p))
    gain = pl.BlockSpec((1, LANES), lambda b, p: (0, 0))
    tile2 = lambda g: jnp.concatenate([g, g]).reshape(1, LANES)
    p3 = p_moba.reshape(B, S, MOBA_PROJ)
    slopes = jnp.exp2(-8.0 * (jnp.arange(MOBA_HEADS, dtype=F32) + 1.0) / MOBA_HEADS)
    return pl.pallas_call(
        functools.partial(_moba_kernel, S=S),
        grid=(B, pairs),
        in_specs=[pl.BlockSpec(memory_space=pltpu.SMEM), col(0), col(pairs), col(2 * pairs),
                  gain, gain],
        out_specs=pl.BlockSpec((1, S, LANES), lambda b, p: (b, 0, p)),
        out_shape=jax.ShapeDtypeStruct((B, S, MOBA_WIDTH), F32),
        compiler_params=_params("arbitrary", "arbitrary"),
        name="moba",
    )(slopes, p3, p3, p3, tile2(q_norm_g), tile2(k_norm_g))


def _out_proj_kernel(yr_ref, ym_ref, x_ref, gate_ref, g_ref, scale_ref, shift_ref, w_ref,
                     wr_ref, br_ref, x1_ref, h2_ref, idx_ref, wgt_ref):
    W = RWKV_WIDTH
    mix = (_dot(yr_ref[...].astype(BF16), w_ref[0:W, :])
           + _dot(ym_ref[...].astype(BF16), w_ref[W:, :]))
    x1 = x_ref[...] + gate_ref[0] * mix
    x1_ref[...] = x1
    h2 = _rms_modulate(x1, g_ref[...], scale_ref[0], shift_ref[0])
    h2_ref[:, 0, :] = h2
    logits_t = (_mm(h2, wr_ref[...], passes=3) + br_ref[...]).T[:N_EXPERTS, :]
    tm = logits_t.shape[1]
    eidx = _iota2((N_EXPERTS, 1), 0)
    vals, idxs = [], []
    for _ in range(TOP_K):
        m = jnp.max(logits_t, axis=0, keepdims=True)
        idx = jnp.min(jnp.where(logits_t == m, eidx, N_EXPERTS), axis=0, keepdims=True)
        vals.append(m)
        idxs.append(idx)
        logits_t = jnp.where(eidx == idx, -jnp.inf, logits_t)
    idx_ref[...] = jnp.concatenate(idxs, axis=0)
    e = [jnp.exp(v - vals[0]) for v in vals]
    total = e[0] + e[1] + e[2] + e[3]
    wgt_t = jnp.concatenate([x / total for x in e] + [jnp.zeros((LANES - TOP_K, tm), F32)], axis=0)
    wgt_ref[...] = wgt_t.T


def _out_proj(y_rwkv, y_moba, x2, gate1, norm_g, scale, shift, w_out_b, w_router, b_router, S):
    T, D = x2.shape
    tm = 256
    per_b = S // tm
    rows = lambda n: pl.BlockSpec((tm, n), lambda i: (i, 0))
    mod = pl.BlockSpec((1, 1, D), lambda i: (i // per_b, 0, 0))
    full = lambda m, n: pl.BlockSpec((m, n), lambda i: (0, 0))
    wr = _pad_cols(w_router, LANES)
    br = jnp.concatenate([b_router, jnp.full((LANES - N_EXPERTS,), -jnp.inf, F32)]).reshape(1, LANES)
    return pl.pallas_call(
        _out_proj_kernel,
        grid=(T // tm,),
        in_specs=[rows(RWKV_WIDTH), rows(MOBA_WIDTH), rows(D), mod, full(1, D), mod, mod,
                  full(D, D), full(D, LANES), full(1, LANES)],
        out_specs=[rows(D), pl.BlockSpec((tm, 1, D), lambda i: (i, 0, 0)),
                   pl.BlockSpec((TOP_K, tm), lambda i: (0, i)), rows(LANES)],
        out_shape=[jax.ShapeDtypeStruct((T, D), F32), jax.ShapeDtypeStruct((T, 1, D), F32),
                   jax.ShapeDtypeStruct((TOP_K, T), jnp.int32), jax.ShapeDtypeStruct((T, LANES), F32)],
        compiler_params=_params("arbitrary"),
        name="out_proj",
    )(y_rwkv, y_moba, x2, gate1, norm_g.reshape(1, D), scale, shift, w_out_b, wr, br)


def _experts_kernel(meta_ref, src_first_ref, src_next_ref, h_hbm, wgu_ref, bgu_ref, wd_ref, bd_ref,
                    y_hbm, xbuf0, xbuf1, obuf0, obuf1, xb16, wgu16, wd16, gsem, wsem,
                    *, n_tiles):
    j = pl.program_id(0)
    n_used = meta_ref[0]
    tm = EXPERT_TILE
    n_sub = EXPERT_SUBSTEPS
    cols_sub = D_FF // n_sub
    expert = lambda i: meta_ref[1 + jnp.clip(i, 0, n_tiles - 1)]

    def gather_row(src_ref, r, dst_buf, sem, priority=0):
        pltpu.make_async_copy(h_hbm.at[src_ref[0, 0, r]], dst_buf.at[pl.ds(r, 1)], sem).start(
            priority=priority)

    def wait_tile_gather(buf, sem):
        pltpu.make_async_copy(h_hbm.at[pl.ds(0, tm), 0], buf, sem).wait()

    def write_back(buf, tile, sem):
        return pltpu.make_async_copy(buf, y_hbm.at[pl.ds(tile * tm, tm), 0], sem)

    @pl.when(j == 0)
    def _():
        def body(r, carry):
            gather_row(src_first_ref, r, xbuf0, gsem.at[0])
            return carry
        lax.fori_loop(0, tm, body, 0)

    @pl.when((j < n_used) & ((j == 0) | (expert(j) != expert(j - 1))))
    def _():
        wgu16[...] = wgu_ref[0].astype(BF16)
        wd16[...] = wd_ref[0].astype(BF16)

    def ffn_chunk(n, o_cur):
        cg = slice(n * cols_sub, (n + 1) * cols_sub)
        cu = slice(D_FF + n * cols_sub, D_FF + (n + 1) * cols_sub)
        xb = xb16[...]
        gate = jnp.minimum(_dot(xb, wgu16[:, cg]) + bgu_ref[0, :, cg], SWIGLU_LIMIT)
        up = jnp.clip(_dot(xb, wgu16[:, cu]) + bgu_ref[0, :, cu], -SWIGLU_LIMIT, SWIGLU_LIMIT)
        act = (up + 1.0) * gate * _sigmoid(SWIGLU_ALPHA * gate)
        o_cur[...] += _dot(act.astype(BF16), wd16[cg, :])

    def tile_step(ph):
        x_cur, x_nxt = (xbuf0, xbuf1) if ph == 0 else (xbuf1, xbuf0)
        o_cur, o_prv = (obuf0, obuf1) if ph == 0 else (obuf1, obuf0)
        mine = j % 2 == ph

        @pl.when(mine & (j < n_used))
        def _():
            wait_tile_gather(x_cur, gsem.at[ph])

            @pl.when(j >= 2)
            def _():
                write_back(o_cur, j - 2, wsem.at[ph]).wait()

            for r in range(tm):
                gather_row(src_next_ref, r, x_nxt, gsem.at[1 - ph], priority=r % 2)
            xb16[...] = x_cur[...].astype(BF16)
            o_cur[...] = jnp.broadcast_to(bd_ref[0], (tm, D_MODEL))

        @pl.when(mine & (j + 1 <= n_used))
        def _():
            for n in range(n_sub):
                ffn_chunk(n, o_cur)
            write_back(o_cur, j, wsem.at[ph]).start()

        @pl.when(mine & (j == n_used))
        def _():
            wait_tile_gather(x_cur, gsem.at[ph])
            write_back(o_prv, j - 1, wsem.at[1 - ph]).wait()

            @pl.when(j >= 2)
            def _():
                write_back(o_cur, j - 2, wsem.at[ph]).wait()

        @pl.when(mine & (j >= n_used) & (j < n_tiles))
        def _():
            o_cur[...] = jnp.zeros_like(o_cur)
            fill = write_back(o_cur, j, wsem.at[ph])
            fill.start()
            fill.wait()

    tile_step(0)
    tile_step(1)


def _experts(h2, src_tok, tile_expert, n_used, wgu, bgu, wd, bd):
    D = h2.shape[-1]
    tm = EXPERT_TILE
    n_tiles = tile_expert.shape[0]
    src3 = src_tok.reshape(n_tiles + 1, 1, tm)
    meta = jnp.concatenate([n_used.reshape(1), tile_expert]).astype(jnp.int32)
    smem_row = lambda f: pl.BlockSpec((1, 1, tm), f, memory_space=pltpu.SMEM)
    e_of = lambda j, m: m[1 + jnp.minimum(j, n_tiles - 1)]
    grid_spec = pltpu.PrefetchScalarGridSpec(
        num_scalar_prefetch=1,
        grid=(n_tiles + 1,),
        in_specs=[smem_row(lambda j, m: (0, 0, 0)),
                  smem_row(lambda j, m: (jnp.minimum(j + 1, n_tiles), 0, 0)),
                  pl.BlockSpec(memory_space=pl.ANY),
                  pl.BlockSpec((1, D, 2 * D_FF), lambda j, m: (e_of(j, m), 0, 0)),
                  pl.BlockSpec((1, 1, 2 * D_FF), lambda j, m: (e_of(j, m), 0, 0)),
                  pl.BlockSpec((1, D_FF, D), lambda j, m: (e_of(j, m), 0, 0)),
                  pl.BlockSpec((1, 1, D), lambda j, m: (e_of(j, m), 0, 0))],
        out_specs=pl.BlockSpec(memory_space=pl.ANY),
        scratch_shapes=[pltpu.VMEM((tm, D), F32), pltpu.VMEM((tm, D), F32),
                        pltpu.VMEM((tm, D), F32), pltpu.VMEM((tm, D), F32),
                        pltpu.VMEM((tm, D), BF16),
                        pltpu.VMEM((D, 2 * D_FF), BF16), pltpu.VMEM((D_FF, D), BF16),
                        pltpu.SemaphoreType.DMA((2,)), pltpu.SemaphoreType.DMA((2,))],
    )
    return pl.pallas_call(
        functools.partial(_experts_kernel, n_tiles=n_tiles),
        grid_spec=grid_spec,
        out_shape=jax.ShapeDtypeStruct((n_tiles * tm, 1, D), F32),
        compiler_params=_params("arbitrary"),
        name="experts",
    )(meta, src3, src3, h2, wgu, bgu.reshape(N_EXPERTS, 1, 2 * D_FF), wd, bd.reshape(N_EXPERTS, 1, D))


def _route(top_idx_t):
    T = top_idx_t.shape[1]
    tm = EXPERT_TILE
    M = T * TOP_K
    slot_expert = top_idx_t.reshape(M)
    order = jnp.argsort(slot_expert).astype(jnp.int32)
    rank = jnp.argsort(order).astype(jnp.int32)
    counts = jnp.bincount(slot_expert, length=N_EXPERTS)
    padded = (counts + tm - 1) // tm * tm
    pad_end = jnp.cumsum(padded)
    pad_start = pad_end - padded
    start = jnp.cumsum(counts) - counts
    n_tiles = M // tm + N_EXPERTS
    tile_start = jnp.arange(n_tiles) * tm
    tile_expert = jnp.minimum(jnp.sum(pad_end[None, :] <= tile_start[:, None], axis=1), N_EXPERTS - 1)
    tile_valid = jnp.clip(counts[tile_expert] - (tile_start - pad_start[tile_expert]), 0, tm)
    n_used = pad_end[-1] // tm
    r = jnp.arange(tm)[None, :]
    sorted_pos = (start[tile_expert] + tile_start - pad_start[tile_expert])[:, None] + r
    slot = order[jnp.clip(sorted_pos, 0, M - 1)]
    src_tok = jnp.where(r < tile_valid[:, None], slot % T, 0).astype(jnp.int32)
    src_ext = jnp.concatenate([src_tok, jnp.zeros((1, tm), jnp.int32)], axis=0)
    slot_row = (pad_start[slot_expert] + rank - start[slot_expert]).astype(jnp.int32)
    return src_ext, tile_expert.astype(jnp.int32), n_used.astype(jnp.int32), slot_row


COMBINE_TILE = 256


def _combine_kernel(row_first_ref, row_next_ref, x1_ref, w_ref, gate_ref, y_hbm, o_ref,
                    ybuf0, ybuf1, sem, *, n_steps):
    i = pl.program_id(0)
    tm = COMBINE_TILE
    n_rows = TOP_K * tm

    def fetch_row(row_ref, r, buf, s, priority=0):
        pltpu.make_async_copy(y_hbm.at[row_ref[0, 0, r]], buf.at[pl.ds(r, 1)], s).start(priority=priority)

    @pl.when(i == 0)
    def _():
        def body(r, carry):
            fetch_row(row_first_ref, r, ybuf0, sem.at[0])
            return carry
        lax.fori_loop(0, n_rows, body, 0)

    def step(ph):
        cur, nxt = (ybuf0, ybuf1) if ph == 0 else (ybuf1, ybuf0)
        mine = i % 2 == ph

        @pl.when(mine & (i + 1 < n_steps))
        def _():
            for r in range(n_rows):
                fetch_row(row_next_ref, r, nxt, sem.at[1 - ph], priority=r % 2)

        @pl.when(mine)
        def _():
            pltpu.make_async_copy(y_hbm.at[pl.ds(0, n_rows), 0], cur, sem.at[ph]).wait()
            w = w_ref[...]
            acc = w[:, 0:1] * cur[0:tm, :]
            for k in range(1, TOP_K):
                acc = acc + w[:, k:k + 1] * cur[k * tm:(k + 1) * tm, :]
            o_ref[...] = x1_ref[...] + gate_ref[0] * acc

    step(0)
    step(1)


def _combine(x1, y_sorted, slot_row, weights, gate2, S):
    T, D = x1.shape
    tm = COMBINE_TILE
    per_b = S // tm
    n_steps = T // tm
    n_rows = TOP_K * tm
    rows = slot_row.reshape(TOP_K, n_steps, tm).transpose(1, 0, 2).reshape(n_steps, 1, n_rows)
    smem_row = lambda f: pl.BlockSpec((1, 1, n_rows), f, memory_space=pltpu.SMEM)
    return pl.pallas_call(
        functools.partial(_combine_kernel, n_steps=n_steps),
        grid=(n_steps,),
        in_specs=[smem_row(lambda i: (0, 0, 0)),
                  smem_row(lambda i: (jnp.minimum(i + 1, n_steps - 1), 0, 0)),
                  pl.BlockSpec((tm, D), lambda i: (i, 0)),
                  pl.BlockSpec((tm, LANES), lambda i: (i, 0)),
                  pl.BlockSpec((1, 1, D), lambda i: (i // per_b, 0, 0)),
                  pl.BlockSpec(memory_space=pl.ANY)],
        out_specs=pl.BlockSpec((tm, D), lambda i: (i, 0)),
        out_shape=jax.ShapeDtypeStruct((T, D), F32),
        scratch_shapes=[pltpu.VMEM((n_rows, D), F32), pltpu.VMEM((n_rows, D), F32),
                        pltpu.SemaphoreType.DMA((2,))],
        compiler_params=_params("arbitrary"),
        name="combine",
    )(rows, rows, x1, weights, gate2, y_sorted)


def _pad_cols(w, n):
    return jnp.pad(w, ((0, 0), (0, n - w.shape[1])))


def _pad_rows(w, n):
    return jnp.pad(w, ((0, n - w.shape[0]), (0, 0)))


def _layer(x, c, w_ada, b_ada, norm1_g, w_in, rwkv_mu, rwkv_w0, rwkv_w_up, rwkv_a0, rwkv_a_up,
           rwkv_g_up, rwkv_k_k, rwkv_k_a, rwkv_r_k, rwkv_ln_g, rwkv_ln_b, q_norm_g, k_norm_g,
           w_out, norm2_g, w_router, b_router, w_gate_up, b_gate_up, w_down, b_down):
    B, S, D = x.shape
    T = B * S
    W = RWKV_WIDTH
    x2 = x.reshape(T, D)

    mods = _ada(c, w_ada, b_ada)
    shift1, scale1, gate1, shift2, scale2, gate2 = [
        mods[:, j * D:(j + 1) * D].reshape(B, 1, D) for j in range(6)]

    pieces = [(w_in[:, :XW_OFF], XW_OFF),
              (w_in[:, XW_OFF:XW_OFF + DECAY_LORA], LANES),
              (w_in[:, XW_OFF + DECAY_LORA:XW_OFF + DECAY_LORA + AAA_LORA], LANES),
              (w_in[:, XW_OFF + DECAY_LORA + AAA_LORA:RWKV_PROJ], 2 * LANES),
              (w_in[:, RWKV_PROJ:], MOBA_PROJ)]
    w_in_b = jnp.concatenate([_pad_cols(w, n) for w, n in pieces], axis=1).astype(BF16)
    mu_pieces = [(rwkv_mu[None, :XW_OFF], XW_OFF),
                 (rwkv_mu[None, XW_OFF:XW_OFF + DECAY_LORA], LANES),
                 (rwkv_mu[None, XW_OFF + DECAY_LORA:XW_OFF + DECAY_LORA + AAA_LORA], LANES),
                 (rwkv_mu[None, XW_OFF + DECAY_LORA + AAA_LORA:], 2 * LANES)]
    mu = jnp.concatenate([_pad_cols(m, n) for m, n in mu_pieces], axis=1)

    p_rwkv, p_moba = _in_proj(x2, norm1_g, scale1, shift1, w_in_b, S)

    row = lambda a: a.reshape(1, W)
    prm = (mu, row(rwkv_w0), row(rwkv_a0), row(rwkv_k_k), row(rwkv_k_a), row(rwkv_r_k),
           row(rwkv_ln_g), row(rwkv_ln_b), _pad_rows(rwkv_w_up, LANES), _pad_rows(rwkv_a_up, LANES),
           _pad_rows(rwkv_g_up, 2 * LANES))
    y_rwkv = _rwkv(p_rwkv, prm, B, S)
    y_moba = _moba(p_moba, q_norm_g, k_norm_g, B, S)

    x1, h2, top_idx_t, weights = _out_proj(y_rwkv.reshape(T, W), y_moba.reshape(T, MOBA_WIDTH), x2,
                                           gate1, norm2_g, scale2, shift2, w_out.astype(BF16),
                                           w_router, b_router, S)

    src_tok, tile_expert, n_used, slot_row = _route(top_idx_t)
    y_sorted = _experts(h2, src_tok, tile_expert, n_used, w_gate_up, b_gate_up, w_down, b_down)
    out = _combine(x1, y_sorted, slot_row, weights, gate2, S)
    return out.reshape(B, S, D)


def kernel(x, c, w_ada, b_ada, norm1_g, w_in, rwkv_mu, rwkv_w0, rwkv_w_up, rwkv_a0, rwkv_a_up, rwkv_g_up, rwkv_k_k, rwkv_k_a, rwkv_r_k, rwkv_ln_g, rwkv_ln_b, q_norm_g, k_norm_g, w_out, norm2_g, w_router, b_router, w_gate_up, b_gate_up, w_down, b_down):
    for l in range(w_ada.shape[0]):
        x = _layer(x, c, w_ada[l], b_ada[l], norm1_g[l], w_in[l], rwkv_mu[l], rwkv_w0[l],
                   rwkv_w_up[l], rwkv_a0[l], rwkv_a_up[l], rwkv_g_up[l], rwkv_k_k[l], rwkv_k_a[l],
                   rwkv_r_k[l], rwkv_ln_g[l], rwkv_ln_b[l], q_norm_g[l], k_norm_g[l], w_out[l],
                   norm2_g[l], w_router[l], b_router[l], w_gate_up[l], b_gate_up[l], w_down[l],
                   b_down[l])
    return x
```

```python
import functools

import jax
import jax.numpy as jnp
from jax import lax
from jax.experimental import pallas as pl
from jax.experimental.pallas import tpu as pltpu

F32 = jnp.float32
BF16 = jnp.bfloat16

D_MODEL = 1024
HEAD_DIM = 64
RWKV_WIDTH = 512
MOBA_WIDTH = 512
RWKV_HEADS = RWKV_WIDTH // HEAD_DIM
MOBA_HEADS = MOBA_WIDTH // HEAD_DIM
DECAY_LORA = 64
AAA_LORA = 64
GATE_LORA = 160
RWKV_LN_EPS = 64e-5
RWKV_PROJ = 3 * RWKV_WIDTH + DECAY_LORA + AAA_LORA + GATE_LORA
MOBA_PROJ = 3 * MOBA_WIDTH
MOBA_BLOCK = 256
MOBA_TOPK = 3
N_EXPERTS = 32
TOP_K = 4
D_FF = D_MODEL
SWIGLU_LIMIT = 7.0
SWIGLU_ALPHA = 1.702
NORM_EPS = 1e-6

LANES = 128
SUBLANES = 8
XW_OFF = 3 * RWKV_WIDTH
XA_OFF = XW_OFF + LANES
XG_OFF = XA_OFF + LANES
RWKV_COLS = XG_OFF + 2 * LANES
CHUNK = 64
EXPERT_TILE = 512
EXPERT_SUBSTEPS = 2
VMEM_LIMIT = 56 * 1024 * 1024

NN = (((1,), (0,)), ((), ()))
NT = (((1,), (1,)), ((), ()))
TN = (((0,), (0,)), ((), ()))


def _dot(a, b, dims=NN):
    return lax.dot_general(a, b, dims, preferred_element_type=F32)


def _split(a):
    hi = a.astype(BF16)
    lo = (a - hi.astype(F32)).astype(BF16)
    return hi, lo


def _mm(a, b, dims=NN, passes=1):
    if passes == 1:
        return _dot(a.astype(BF16), b.astype(BF16), dims)
    a_hi, a_lo = _split(a)
    b_hi, b_lo = _split(b)
    return _dot(a_hi, b_hi, dims) + (_dot(a_hi, b_lo, dims) + _dot(a_lo, b_hi, dims))


def _split3(a):
    hi = a.astype(BF16)
    r = a - hi.astype(F32)
    mid = r.astype(BF16)
    lo = (r - mid.astype(F32)).astype(BF16)
    return hi, mid, lo


def _mm_exact_rhs(a, b_bf16, dims=NN):
    hi, mid, lo = _split3(a)
    return _dot(hi, b_bf16, dims) + (_dot(mid, b_bf16, dims) + _dot(lo, b_bf16, dims))


def _mm_exact_lhs(a_bf16, b, dims=NN):
    hi, mid, lo = _split3(b)
    return _dot(a_bf16, hi, dims) + (_dot(a_bf16, mid, dims) + _dot(a_bf16, lo, dims))


def _iota2(shape, dim):
    return lax.broadcasted_iota(jnp.int32, shape, dim)


def _group_ones(n, group):
    return (_iota2((n, n), 0) // group == _iota2((n, n), 1) // group).astype(BF16)


def _sigmoid(x):
    return 1.0 / (1.0 + jnp.exp(-x))


def _params(*sem):
    return pltpu.CompilerParams(dimension_semantics=sem, vmem_limit_bytes=VMEM_LIMIT)


def _ada_kernel(c_ref, w_ref, b_ref, o_ref):
    c = c_ref[...]
    o_ref[...] = _mm(c * _sigmoid(c), w_ref[...], passes=3) + b_ref[...]


def _ada(c, w_ada, b_ada):
    B, D = c.shape
    n_out = w_ada.shape[1]
    tn = 1024
    return pl.pallas_call(
        _ada_kernel,
        grid=(n_out // tn,),
        in_specs=[pl.BlockSpec((B, D), lambda j: (0, 0)),
                  pl.BlockSpec((D, tn), lambda j: (0, j)),
                  pl.BlockSpec((1, tn), lambda j: (0, j))],
        out_specs=pl.BlockSpec((B, tn), lambda j: (0, j)),
        out_shape=jax.ShapeDtypeStruct((B, n_out), F32),
        compiler_params=_params("arbitrary"),
        name="ada",
    )(c, w_ada, b_ada.reshape(1, n_out))


def _rms_modulate(x, g, scale, shift):
    y = x * lax.rsqrt(jnp.mean(x * x, axis=-1, keepdims=True) + NORM_EPS)
    return (y * g) * (1.0 + scale) + shift


def _in_proj_kernel(x_ref, g_ref, scale_ref, shift_ref, w_ref, pr_ref, pm_ref):
    h = _rms_modulate(x_ref[...], g_ref[...], scale_ref[0], shift_ref[0])
    proj = _dot(h.astype(BF16), w_ref[...])
    pr_ref[...] = proj[:, :RWKV_COLS]
    pm_ref[...] = proj[:, RWKV_COLS:]


def _in_proj(x2, norm_g, scale, shift, w_in_b, S):
    T, D = x2.shape
    tm = 256
    per_b = S // tm
    n_cols = w_in_b.shape[1]
    return pl.pallas_call(
        _in_proj_kernel,
        grid=(T // tm,),
        in_specs=[pl.BlockSpec((tm, D), lambda i: (i, 0)),
                  pl.BlockSpec((1, D), lambda i: (0, 0)),
                  pl.BlockSpec((1, 1, D), lambda i: (i // per_b, 0, 0)),
                  pl.BlockSpec((1, 1, D), lambda i: (i // per_b, 0, 0)),
                  pl.BlockSpec((D, n_cols), lambda i: (0, 0))],
        out_specs=[pl.BlockSpec((tm, RWKV_COLS), lambda i: (i, 0)),
                   pl.BlockSpec((tm, MOBA_PROJ), lambda i: (i, 0))],
        out_shape=[jax.ShapeDtypeStruct((T, RWKV_COLS), F32),
                   jax.ShapeDtypeStruct((T, MOBA_PROJ), F32)],
        compiler_params=_params("arbitrary"),
        name="in_proj",
    )(x2, norm_g.reshape(1, D), scale, shift, w_in_b)


RWKV_TILE = 256


def _unit_lower_inverses(Ls):
    n = Ls[0].shape[0]
    r = _iota2((n, n), 0)
    c = _iota2((n, n), 1)
    eye = (r == c).astype(F32)
    in_block = r // 8 == c // 8
    b16 = lambda xs: [x.astype(BF16) for x in xs]
    Ld = [jnp.where(in_block, L, 0.0) for L in Ls]
    Ld_b = b16(Ld)
    Nb_b = [(L - d).astype(BF16) for L, d in zip(Ls, Ld)]
    L2 = [_dot(d, d) for d in Ld_b]
    L2_b = b16(L2)
    L4_b = b16([_dot(x, x) for x in L2_b])
    T0 = [eye + d + l2 + _dot(db, l2b) for d, l2, db, l2b in zip(Ld, L2, Ld_b, L2_b)]
    T0 = [t + _dot(t.astype(BF16), l4b) for t, l4b in zip(T0, L4_b)]
    T0_b = b16(T0)
    M1_b = b16([_dot(t, nb) for t, nb in zip(T0_b, Nb_b)])
    M2_b = b16([_dot(m, m) for m in M1_b])
    M4_b = b16([_dot(m, m) for m in M2_b])
    X = [t + _dot(m4, tb) for t, m4, tb in zip(T0, M4_b, T0_b)]
    X = [x + _dot(m2, x.astype(BF16)) for x, m2 in zip(X, M2_b)]
    return [x + _dot(m1, x.astype(BF16)) for x, m1 in zip(X, M1_b)]


def _rwkv_kernel(p_ref, mu_ref, w0_ref, a0_ref, kk_ref, ka_ref, rk_ref, lng_ref, lnb_ref,
                 wup_ref, aup_ref, gup_ref, y_ref, carry_ref, state_ref, *, ts):
    t = pl.program_id(1)
    W = RWKV_WIDTH

    @pl.when(t == 0)
    def _():
        carry_ref[...] = jnp.zeros_like(carry_ref)
        state_ref[...] = jnp.zeros_like(state_ref)

    p = p_ref[0]
    row = _iota2((ts, 1), 0)
    prev = jnp.where(row == 0, carry_ref[...], pltpu.roll(p, 1, 0))
    carry_ref[...] = p[ts - 1:ts, :]
    pm = p + (prev - p) * mu_ref[...]
    r = pm[:, 0:W]
    k = pm[:, W:2 * W]
    v = pm[:, 2 * W:3 * W]
    xw = pm[:, XW_OFF:XA_OFF]
    xa = pm[:, XA_OFF:XG_OFF]
    xg = pm[:, XG_OFF:RWKV_COLS]

    z = -(w0_ref[...] + _mm(jnp.tanh(xw), wup_ref[...], passes=3))
    softplus = jnp.maximum(z, 0.0) + jnp.log(1.0 + jnp.exp(-jnp.abs(z)))
    logd = -jnp.exp(-softplus - 0.5)
    alpha = _sigmoid(a0_ref[...] + _mm(xa, aup_ref[...], passes=3))
    gate = _mm(_sigmoid(xg), gup_ref[...])

    head_ones = _group_ones(W, HEAD_DIM)
    kk = k * kk_ref[...]
    kk_norm = jnp.sqrt(_mm_exact_rhs(kk * kk, head_ones))
    kk = kk / jnp.maximum(kk_norm, 1e-12)
    kmod = k * (1.0 + (alpha - 1.0) * ka_ref[...])
    bonus = _mm_exact_rhs(r * kmod * rk_ref[...], head_ones) * v

    tr = _iota2((ts, ts), 0)
    tc = _iota2((ts, ts), 1)
    cum = ((tr // CHUNK == tc // CHUNK) & (tc <= tr)).astype(BF16)
    logp = _mm_exact_lhs(cum, logd)
    inv_p = jnp.exp(-logp)
    a_t = -kk * jnp.exp(logp - logd)
    b_t = kk * alpha * inv_p
    k_t = kmod * inv_p
    r_t = r * jnp.exp(logp)

    n2 = 2 * CHUNK
    n_chunks = ts // CHUNK
    n_pairs = RWKV_HEADS // 2
    inst = [(ci, pi) for ci in range(n_chunks) for pi in range(n_pairs)]
    lane = _iota2((1, LANES), 1)
    m0 = lane < HEAD_DIM
    sr = _iota2((n2, n2), 0)
    sc = _iota2((n2, n2), 1)
    strict_lower = sc < sr
    incl_lower = sc <= sr

    def stacked(x, ci, pi):
        xt = x[ci * CHUNK:(ci + 1) * CHUNK, pi * LANES:(pi + 1) * LANES]
        return jnp.concatenate([jnp.where(m0, xt, 0.0), jnp.where(m0, 0.0, xt)], axis=0)

    pcs = [jnp.exp(logp[(ci + 1) * CHUNK - 1:(ci + 1) * CHUNK, pi * LANES:(pi + 1) * LANES])
           for ci, pi in inst]
    r_s = [stacked(r_t, ci, pi) for ci, pi in inst]
    a_b = [stacked(a_t, ci, pi).astype(BF16) for ci, pi in inst]
    r_b = [x.astype(BF16) for x in r_s]
    b_s = [stacked(b_t, ci, pi) for ci, pi in inst]
    k_s = [stacked(k_t, ci, pi) for ci, pi in inst]
    v_b = [stacked(v, ci, pi).astype(BF16) for ci, pi in inst]
    ar_b = [jnp.concatenate([a, rr], axis=0) for a, rr in zip(a_b, r_b)]
    bk_b = [jnp.concatenate([b.astype(BF16), kx.astype(BF16)], axis=0) for b, kx in zip(b_s, k_s)]
    gram = [_dot(ar, bk, NT) for ar, bk in zip(ar_b, bk_b)]
    l_ab = [jnp.where(strict_lower, g[:n2, :n2], 0.0) for g in gram]
    l_ak_b = [jnp.where(strict_lower, g[:n2, n2:], 0.0).astype(BF16) for g in gram]
    m_rb_b = [jnp.where(incl_lower, g[n2:, :n2], 0.0).astype(BF16) for g in gram]
    m_rk_b = [jnp.where(incl_lower, g[n2:, n2:], 0.0).astype(BF16) for g in gram]
    t_b = [x.astype(BF16) for x in _unit_lower_inverses(l_ab)]
    lakv_b = [_dot(l, vv).astype(BF16) for l, vv in zip(l_ak_b, v_b)]
    wu_b = [_dot(tb, jnp.concatenate([a, lv], axis=1)).astype(BF16)
            for tb, a, lv in zip(t_b, a_b, lakv_b)]
    mwu = [_dot(m, wu) for m, wu in zip(m_rb_b, wu_b)]
    q_b = [(rs + x[:, :LANES]).astype(BF16) for rs, x in zip(r_s, mwu)]
    y0 = [x[:, LANES:] + _dot(m, vv) for x, m, vv in zip(mwu, m_rk_b, v_b)]
    bp_b = [(b * pc).astype(BF16) for b, pc in zip(b_s, pcs)]
    kp_b = [(kx * pc).astype(BF16) for kx, pc in zip(k_s, pcs)]
    gh = [_dot(wu, bp, TN) for wu, bp in zip(wu_b, bp_b)]
    g_b = [x[:LANES].astype(BF16) for x in gh]
    h = [x[LANES:] + _dot(vv, kp, TN) for x, vv, kp in zip(gh, v_b, kp_b)]

    y_chunks = []
    for ci in range(n_chunks):
        ids = [ci * n_pairs + pi for pi in range(n_pairs)]
        s0 = [state_ref[pi] for pi in range(n_pairs)]
        s0_b = [s.astype(BF16) for s in s0]
        y_s = [_dot(q_b[n], sb, NT) + y0[n] for n, sb in zip(ids, s0_b)]
        s1 = [s * pcs[n] + _dot(sb, g_b[n]) + h[n] for n, s, sb in zip(ids, s0, s0_b)]
        for pi in range(n_pairs):
            state_ref[pi] = s1[pi]
        y_chunks.append(jnp.concatenate([x[:CHUNK] + x[CHUNK:] for x in y_s], axis=1))
    y = jnp.concatenate(y_chunks, axis=0) if n_chunks > 1 else y_chunks[0]

    mean = _mm_exact_rhs(y, head_ones) * (1.0 / HEAD_DIM)
    yc = y - mean
    var = _mm_exact_rhs(yc * yc, head_ones) * (1.0 / HEAD_DIM)
    yn = yc * lax.rsqrt(var + RWKV_LN_EPS) * lng_ref[...] + lnb_ref[...]
    y_ref[0] = (yn + bonus) * gate


def _rwkv(p_rwkv, prm, B, S, ts=RWKV_TILE):
    W = RWKV_WIDTH
    vec = lambda n: pl.BlockSpec((1, n), lambda b, t: (0, 0))
    mat = lambda m, n: pl.BlockSpec((m, n), lambda b, t: (0, 0))
    return pl.pallas_call(
        functools.partial(_rwkv_kernel, ts=ts),
        grid=(B, S // ts),
        in_specs=[pl.BlockSpec((1, ts, RWKV_COLS), lambda b, t: (b, t, 0)),
                  vec(RWKV_COLS)] + [vec(W)] * 7 + [mat(LANES, W), mat(LANES, W), mat(2 * LANES, W)],
        out_specs=pl.BlockSpec((1, ts, W), lambda b, t: (b, t, 0)),
        out_shape=jax.ShapeDtypeStruct((B, S, W), F32),
        scratch_shapes=[pltpu.VMEM((1, RWKV_COLS), F32),
                        pltpu.VMEM((RWKV_HEADS // 2, LANES, LANES), F32)],
        compiler_params=_params("arbitrary", "arbitrary"),
        name="rwkv",
    )(p_rwkv.reshape(B, S, RWKV_COLS), *prm)


def _moba_kernel(slopes_ref, q_ref, k_ref, v_ref, qg_ref, kg_ref, o_ref, *, S):
    pair = pl.program_id(1)
    NB = S // MOBA_BLOCK
    BLK = MOBA_BLOCK
    n_sel = min(MOBA_TOPK, NB)
    scale = HEAD_DIM ** -0.5
    head_ones = _group_ones(LANES, HEAD_DIM)
    lane = _iota2((1, LANES), 1)

    def head_norm(x, g):
        ss = _mm_exact_rhs(x * x, head_ones)
        return x * lax.rsqrt(ss * (1.0 / HEAD_DIM) + NORM_EPS) * g

    qn = head_norm(q_ref[0], qg_ref[...])
    kn = head_norm(k_ref[0], kg_ref[...])
    kmean = jnp.mean(kn.reshape(NB, BLK, LANES), axis=1)
    q_t = qn.T
    v_tb = v_ref[0].T.astype(BF16)

    blk_of_q = _iota2((1, S), 1) // BLK
    nidx = _iota2((NB, 1), 0)
    valid = nidx < blk_of_q
    q_blk = (blk_of_q * BLK).astype(F32)
    q_loc = (_iota2((1, S), 1) % BLK).astype(F32)
    k_blk = (_iota2((S, 1), 0) // BLK * BLK).astype(F32)
    k_loc = (_iota2((S, 1), 0) % BLK).astype(F32)
    q_lane = _iota2((LANES, 1), 0)
    causal = _iota2((BLK, BLK), 0) <= _iota2((BLK, BLK), 1)

    out_rows = []
    for h in range(2):
        hmask = (lane // HEAD_DIM) == h
        slope = slopes_ref[pair * 2 + h]
        spare = (1 - h) * HEAD_DIM
        k_aug = jnp.where(hmask, kn, 0.0)
        k_aug = jnp.where(lane == spare, slope * k_blk, k_aug)
        k_aug = jnp.where(lane == spare + 1, slope * k_loc, k_aug)
        k_aug = jnp.where((lane == spare + 2) | (lane == spare + 3), 1.0, k_aug)
        q_aug = q_t * scale
        q_aug = jnp.where((q_lane == spare) | (q_lane == spare + 1), 1.0, q_aug)
        q_aug = jnp.where(q_lane == spare + 2, -slope * q_blk, q_aug)
        q_aug = jnp.where(q_lane == spare + 3, -slope * q_loc, q_aug)
        q_hb = q_aug.astype(BF16)
        gate = _mm(jnp.where(hmask, kmean, 0.0), q_t, passes=3)
        gate = jnp.where(valid, gate, -jnp.inf)
        rank = jnp.zeros((NB, S), jnp.int32)
        for m in range(NB):
            gm = gate[m:m + 1, :]
            ahead = (gm > gate) | ((gm == gate) & (m < nidx))
            rank = rank + ahead.astype(jnp.int32)
        sel = valid & (rank < n_sel)
        k_hb = k_aug.astype(BF16)
        v_h = v_tb[h * HEAD_DIM:(h + 1) * HEAD_DIM, :]
        out_blocks = []
        for i in range(NB):
            qs = slice(i * BLK, (i + 1) * BLK)
            m_run = l_run = acc = None
            for n in [i] + list(range(i)):
                ks = slice(n * BLK, (n + 1) * BLK)
                s = _dot(k_hb[ks, :], q_hb[:, qs])
                if n < i:
                    s = jnp.where(sel[n:n + 1, qs], s, -jnp.inf)
                else:
                    s = jnp.where(causal, s, -jnp.inf)
                mx = jnp.max(s, axis=0, keepdims=True)
                m_new = mx if m_run is None else jnp.maximum(m_run, mx)
                pt = jnp.exp(s - m_new)
                l_t = jnp.sum(pt, axis=0, keepdims=True)
                pv = _dot(v_h[:, ks], pt.astype(BF16))
                if m_run is None:
                    l_run, acc = l_t, pv
                else:
                    alpha = jnp.exp(m_run - m_new)
                    l_run = alpha * l_run + l_t
                    acc = alpha * acc + pv
                m_run = m_new
            out_blocks.append(acc / l_run)
        out_rows.append(jnp.concatenate(out_blocks, axis=1))
    o_ref[0] = jnp.concatenate(out_rows, axis=0).T


def _moba(p_moba, q_norm_g, k_norm_g, B, S):
    pairs = MOBA_HEADS // 2
    col = lambda off: pl.BlockSpec((1, S, LANES), lambda b, p: (b, 0, off + p))
    gain = pl.BlockSpec((1, LANES), lambda b, p: (0, 0))
    tile2 = lambda g: jnp.concatenate([g, g]).reshape(1, LANES)
    p3 = p_moba.reshape(B, S, MOBA_PROJ)
    slopes = jnp.exp2(-8.0 * (jnp.arange(MOBA_HEADS, dtype=F32) + 1.0) / MOBA_HEADS)
    return pl.pallas_call(
        functools.partial(_moba_kernel, S=S),
        grid=(B, pairs),
        in_specs=[pl.BlockSpec(memory_space=pltpu.SMEM), col(0), col(pairs), col(2 * pairs),
                  gain, gain],
        out_specs=pl.BlockSpec((1, S, LANES), lambda b, p: (b, 0, p)),
        out_shape=jax.ShapeDtypeStruct((B, S, MOBA_WIDTH), F32),
        compiler_params=_params("arbitrary", "arbitrary"),
        name="moba",
    )(slopes, p3, p3, p3, tile2(q_norm_g), tile2(k_norm_g))


def _out_proj_kernel(yr_ref, ym_ref, x_ref, gate_ref, g_ref, scale_ref, shift_ref, w_ref,
                     wr_ref, br_ref, x1_ref, h2_ref, idx_ref, wgt_ref):
    W = RWKV_WIDTH
    mix = (_dot(yr_ref[...].astype(BF16), w_ref[0:W, :])
           + _dot(ym_ref[...].astype(BF16), w_ref[W:, :]))
    x1 = x_ref[...] + gate_ref[0] * mix
    x1_ref[...] = x1
    h2 = _rms_modulate(x1, g_ref[...], scale_ref[0], shift_ref[0])
    h2_ref[:, 0, :] = h2
    logits_t = (_mm(h2, wr_ref[...], passes=3) + br_ref[...]).T[:N_EXPERTS, :]
    tm = logits_t.shape[1]
    eidx = _iota2((N_EXPERTS, 1), 0)
    vals, idxs = [], []
    for _ in range(TOP_K):
        m = jnp.max(logits_t, axis=0, keepdims=True)
        idx = jnp.min(jnp.where(logits_t == m, eidx, N_EXPERTS), axis=0, keepdims=True)
        vals.append(m)
        idxs.append(idx)
        logits_t = jnp.where(eidx == idx, -jnp.inf, logits_t)
    idx_ref[...] = jnp.concatenate(idxs, axis=0)
    e = [jnp.exp(v - vals[0]) for v in vals]
    total = e[0] + e[1] + e[2] + e[3]
    wgt_t = jnp.concatenate([x / total for x in e] + [jnp.zeros((LANES - TOP_K, tm), F32)], axis=0)
    wgt_ref[...] = wgt_t.T


def _out_proj(y_rwkv, y_moba, x2, gate1, norm_g, scale, shift, w_out_b, w_router, b_router, S):
    T, D = x2.shape
    tm = 256
    per_b = S // tm
    rows = lambda n: pl.BlockSpec((tm, n), lambda i: (i, 0))
    mod = pl.BlockSpec((1, 1, D), lambda i: (i // per_b, 0, 0))
    full = lambda m, n: pl.BlockSpec((m, n), lambda i: (0, 0))
    wr = _pad_cols(w_router, LANES)
    br = jnp.concatenate([b_router, jnp.full((LANES - N_EXPERTS,), -jnp.inf, F32)]).reshape(1, LANES)
    return pl.pallas_call(
        _out_proj_kernel,
        grid=(T // tm,),
        in_specs=[rows(RWKV_WIDTH), rows(MOBA_WIDTH), rows(D), mod, full(1, D), mod, mod,
                  full(D, D), full(D, LANES), full(1, LANES)],
        out_specs=[rows(D), pl.BlockSpec((tm, 1, D), lambda i: (i, 0, 0)),
                   pl.BlockSpec((TOP_K, tm), lambda i: (0, i)), rows(LANES)],
        out_shape=[jax.ShapeDtypeStruct((T, D), F32), jax.ShapeDtypeStruct((T, 1, D), F32),
                   jax.ShapeDtypeStruct((TOP_K, T), jnp.int32), jax.ShapeDtypeStruct((T, LANES), F32)],
        compiler_params=_params("arbitrary"),
        name="out_proj",
    )(y_rwkv, y_moba, x2, gate1, norm_g.reshape(1, D), scale, shift, w_out_b, wr, br)


def _experts_kernel(meta_ref, src_first_ref, src_next_ref, h_hbm, wgu_ref, bgu_ref, wd_ref, bd_ref,
                    y_hbm, xbuf0, xbuf1, obuf0, obuf1, xb16, wgu16, wd16, gsem, wsem,
                    *, n_tiles):
    j = pl.program_id(0)
    n_used = meta_ref[0]
    tm = EXPERT_TILE
    n_sub = EXPERT_SUBSTEPS
    cols_sub = D_FF // n_sub
    expert = lambda i: meta_ref[1 + jnp.clip(i, 0, n_tiles - 1)]

    def gather_row(src_ref, r, dst_buf, sem, priority=0):
        pltpu.make_async_copy(h_hbm.at[src_ref[0, 0, r]], dst_buf.at[pl.ds(r, 1)], sem).start(
            priority=priority)

    def wait_tile_gather(buf, sem):
        pltpu.make_async_copy(h_hbm.at[pl.ds(0, tm), 0], buf, sem).wait()

    def write_back(buf, tile, sem):
        return pltpu.make_async_copy(buf, y_hbm.at[pl.ds(tile * tm, tm), 0], sem)

    @pl.when(j == 0)
    def _():
        def body(r, carry):
            gather_row(src_first_ref, r, xbuf0, gsem.at[0])
            return carry
        lax.fori_loop(0, tm, body, 0)

    @pl.when((j < n_used) & ((j == 0) | (expert(j) != expert(j - 1))))
    def _():
        wgu16[...] = wgu_ref[0].astype(BF16)
        wd16[...] = wd_ref[0].astype(BF16)

    def ffn_chunk(n, o_cur):
        cg = slice(n * cols_sub, (n + 1) * cols_sub)
        cu = slice(D_FF + n * cols_sub, D_FF + (n + 1) * cols_sub)
        xb = xb16[...]
        gate = jnp.minimum(_dot(xb, wgu16[:, cg]) + bgu_ref[0, :, cg], SWIGLU_LIMIT)
        up = jnp.clip(_dot(xb, wgu16[:, cu]) + bgu_ref[0, :, cu], -SWIGLU_LIMIT, SWIGLU_LIMIT)
        act = (up + 1.0) * gate * _sigmoid(SWIGLU_ALPHA * gate)
        o_cur[...] += _dot(act.astype(BF16), wd16[cg, :])

    def tile_step(ph):
        x_cur, x_nxt = (xbuf0, xbuf1) if ph == 0 else (xbuf1, xbuf0)
        o_cur, o_prv = (obuf0, obuf1) if ph == 0 else (obuf1, obuf0)
        mine = j % 2 == ph

        @pl.when(mine & (j < n_used))
        def _():
            wait_tile_gather(x_cur, gsem.at[ph])

            @pl.when(j >= 2)
            def _():
                write_back(o_cur, j - 2, wsem.at[ph]).wait()

            for r in range(tm):
                gather_row(src_next_ref, r, x_nxt, gsem.at[1 - ph], priority=r % 2)
            xb16[...] = x_cur[...].astype(BF16)
            o_cur[...] = jnp.broadcast_to(bd_ref[0], (tm, D_MODEL))

        @pl.when(mine & (j + 1 <= n_used))
        def _():
            for n in range(n_sub):
                ffn_chunk(n, o_cur)
            write_back(o_cur, j, wsem.at[ph]).start()

        @pl.when(mine & (j == n_used))
        def _():
            wait_tile_gather(x_cur, gsem.at[ph])
            write_back(o_prv, j - 1, wsem.at[1 - ph]).wait()

            @pl.when(j >= 2)
            def _():
                write_back(o_cur, j - 2, wsem.at[ph]).wait()

        @pl.when(mine & (j >= n_used) & (j < n_tiles))
        def _():
            o_cur[...] = jnp.zeros_like(o_cur)
            fill = write_back(o_cur, j, wsem.at[ph])
            fill.start()
            fill.wait()

    tile_step(0)
    tile_step(1)


def _experts(h2, src_tok, tile_expert, n_used, wgu, bgu, wd, bd):
    D = h2.shape[-1]
    tm = EXPERT_TILE
    n_tiles = tile_expert.shape[0]
    src3 = src_tok.reshape(n_tiles + 1, 1, tm)
    meta = jnp.concatenate([n_used.reshape(1), tile_expert]).astype(jnp.int32)
    smem_row = lambda f: pl.BlockSpec((1, 1, tm), f, memory_space=pltpu.SMEM)
    e_of = lambda j, m: m[1 + jnp.minimum(j, n_tiles - 1)]
    grid_spec = pltpu.PrefetchScalarGridSpec(
        num_scalar_prefetch=1,
        grid=(n_tiles + 1,),
        in_specs=[smem_row(lambda j, m: (0, 0, 0)),
                  smem_row(lambda j, m: (jnp.minimum(j + 1, n_tiles), 0, 0)),
                  pl.BlockSpec(memory_space=pl.ANY),
                  pl.BlockSpec((1, D, 2 * D_FF), lambda j, m: (e_of(j, m), 0, 0)),
                  pl.BlockSpec((1, 1, 2 * D_FF), lambda j, m: (e_of(j, m), 0, 0)),
                  pl.BlockSpec((1, D_FF, D), lambda j, m: (e_of(j, m), 0, 0)),
                  pl.BlockSpec((1, 1, D), lambda j, m: (e_of(j, m), 0, 0))],
        out_specs=pl.BlockSpec(memory_space=pl.ANY),
        scratch_shapes=[pltpu.VMEM((tm, D), F32), pltpu.VMEM((tm, D), F32),
                        pltpu.VMEM((tm, D), F32), pltpu.VMEM((tm, D), F32),
                        pltpu.VMEM((tm, D), BF16),
                        pltpu.VMEM((D, 2 * D_FF), BF16), pltpu.VMEM((D_FF, D), BF16),
                        pltpu.SemaphoreType.DMA((2,)), pltpu.SemaphoreType.DMA((2,))],
    )
    return pl.pallas_call(
        functools.partial(_experts_kernel, n_tiles=n_tiles),
        grid_spec=grid_spec,
        out_shape=jax.ShapeDtypeStruct((n_tiles * tm, 1, D), F32),
        compiler_params=_params("arbitrary"),
        name="experts",
    )(meta, src3, src3, h2, wgu, bgu.reshape(N_EXPERTS, 1, 2 * D_FF), wd, bd.reshape(N_EXPERTS, 1, D))


def _route(top_idx_t):
    T = top_idx_t.shape[1]
    tm = EXPERT_TILE
    M = T * TOP_K
    slot_expert = top_idx_t.reshape(M)
    order = jnp.argsort(slot_expert).astype(jnp.int32)
    rank = jnp.argsort(order).astype(jnp.int32)
    counts = jnp.bincount(slot_expert, length=N_EXPERTS)
    padded = (counts + tm - 1) // tm * tm
    pad_end = jnp.cumsum(padded)
    pad_start = pad_end - padded
    start = jnp.cumsum(counts) - counts
    n_tiles = M // tm + N_EXPERTS
    tile_start = jnp.arange(n_tiles) * tm
    tile_expert = jnp.minimum(jnp.sum(pad_end[None, :] <= tile_start[:, None], axis=1), N_EXPERTS - 1)
    tile_valid = jnp.clip(counts[tile_expert] - (tile_start - pad_start[tile_expert]), 0, tm)
    n_used = pad_end[-1] // tm
    r = jnp.arange(tm)[None, :]
    sorted_pos = (start[tile_expert] + tile_start - pad_start[tile_expert])[:, None] + r
    slot = order[jnp.clip(sorted_pos, 0, M - 1)]
    src_tok = jnp.where(r < tile_valid[:, None], slot % T, 0).astype(jnp.int32)
    src_ext = jnp.concatenate([src_tok, jnp.zeros((1, tm), jnp.int32)], axis=0)
    slot_row = (pad_start[slot_expert] + rank - start[slot_expert]).astype(jnp.int32)
    return src_ext, tile_expert.astype(jnp.int32), n_used.astype(jnp.int32), slot_row


COMBINE_TILE = 256


def _combine_kernel(row_first_ref, row_next_ref, x1_ref, w_ref, gate_ref, y_hbm, o_ref,
                    ybuf0, ybuf1, sem, *, n_steps):
    i = pl.program_id(0)
    tm = COMBINE_TILE
    n_rows = TOP_K * tm

    def fetch_row(row_ref, r, buf, s, priority=0):
        pltpu.make_async_copy(y_hbm.at[row_ref[0, 0, r]], buf.at[pl.ds(r, 1)], s).start(priority=priority)

    @pl.when(i == 0)
    def _():
        def body(r, carry):
            fetch_row(row_first_ref, r, ybuf0, sem.at[0])
            return carry
        lax.fori_loop(0, n_rows, body, 0)

    def step(ph):
        cur, nxt = (ybuf0, ybuf1) if ph == 0 else (ybuf1, ybuf0)
        mine = i % 2 == ph

        @pl.when(mine & (i + 1 < n_steps))
        def _():
            for r in range(n_rows):
                fetch_row(row_next_ref, r, nxt, sem.at[1 - ph], priority=r % 2)

        @pl.when(mine)
        def _():
            pltpu.make_async_copy(y_hbm.at[pl.ds(0, n_rows), 0], cur, sem.at[ph]).wait()
            w = w_ref[...]
            acc = w[:, 0:1] * cur[0:tm, :]
            for k in range(1, TOP_K):
                acc = acc + w[:, k:k + 1] * cur[k * tm:(k + 1) * tm, :]
            o_ref[...] = x1_ref[...] + gate_ref[0] * acc

    step(0)
    step(1)


def _combine(x1, y_sorted, slot_row, weights, gate2, S):
    T, D = x1.shape
    tm = COMBINE_TILE
    per_b = S // tm
    n_steps = T // tm
    n_rows = TOP_K * tm
    rows = slot_row.reshape(TOP_K, n_steps, tm).transpose(1, 0, 2).reshape(n_steps, 1, n_rows)
    smem_row = lambda f: pl.BlockSpec((1, 1, n_rows), f, memory_space=pltpu.SMEM)
    return pl.pallas_call(
        functools.partial(_combine_kernel, n_steps=n_steps),
        grid=(n_steps,),
        in_specs=[smem_row(lambda i: (0, 0, 0)),
                  smem_row(lambda i: (jnp.minimum(i + 1, n_steps - 1), 0, 0)),
                  pl.BlockSpec((tm, D), lambda i: (i, 0)),
                  pl.BlockSpec((tm, LANES), lambda i: (i, 0)),
                  pl.BlockSpec((1, 1, D), lambda i: (i // per_b, 0, 0)),
                  pl.BlockSpec(memory_space=pl.ANY)],
        out_specs=pl.BlockSpec((tm, D), lambda i: (i, 0)),
        out_shape=jax.ShapeDtypeStruct((T, D), F32),
        scratch_shapes=[pltpu.VMEM((n_rows, D), F32), pltpu.VMEM((n_rows, D), F32),
                        pltpu.SemaphoreType.DMA((2,))],
        compiler_params=_params("arbitrary"),
        name="combine",
    )(rows, rows, x1, weights, gate2, y_sorted)


def _pad_cols(w, n):
    return jnp.pad(w, ((0, 0), (0, n - w.shape[1])))


def _pad_rows(w, n):
    return jnp.pad(w, ((0, n - w.shape[0]), (0, 0)))


def _layer(x, c, w_ada, b_ada, norm1_g, w_in, rwkv_mu, rwkv_w0, rwkv_w_up, rwkv_a0, rwkv_a_up,
           rwkv_g_up, rwkv_k_k, rwkv_k_a, rwkv_r_k, rwkv_ln_g, rwkv_ln_b, q_norm_g, k_norm_g,
           w_out, norm2_g, w_router, b_router, w_gate_up, b_gate_up, w_down, b_down):
    B, S, D = x.shape
    T = B * S
    W = RWKV_WIDTH
    x2 = x.reshape(T, D)

    mods = _ada(c, w_ada, b_ada)
    shift1, scale1, gate1, shift2, scale2, gate2 = [
        mods[:, j * D:(j + 1) * D].reshape(B, 1, D) for j in range(6)]

    pieces = [(w_in[:, :XW_OFF], XW_OFF),
              (w_in[:, XW_OFF:XW_OFF + DECAY_LORA], LANES),
              (w_in[:, XW_OFF + DECAY_LORA:XW_OFF + DECAY_LORA + AAA_LORA], LANES),
              (w_in[:, XW_OFF + DECAY_LORA + AAA_LORA:RWKV_PROJ], 2 * LANES),
              (w_in[:, RWKV_PROJ:], MOBA_PROJ)]
    w_in_b = jnp.concatenate([_pad_cols(w, n) for w, n in pieces], axis=1).astype(BF16)
    mu_pieces = [(rwkv_mu[None, :XW_OFF], XW_OFF),
                 (rwkv_mu[None, XW_OFF:XW_OFF + DECAY_LORA], LANES),
                 (rwkv_mu[None, XW_OFF + DECAY_LORA:XW_OFF + DECAY_LORA + AAA_LORA], LANES),
                 (rwkv_mu[None, XW_OFF + DECAY_LORA + AAA_LORA:], 2 * LANES)]
    mu = jnp.concatenate([_pad_cols(m, n) for m, n in mu_pieces], axis=1)

    p_rwkv, p_moba = _in_proj(x2, norm1_g, scale1, shift1, w_in_b, S)

    row = lambda a: a.reshape(1, W)
    prm = (mu, row(rwkv_w0), row(rwkv_a0), row(rwkv_k_k), row(rwkv_k_a), row(rwkv_r_k),
           row(rwkv_ln_g), row(rwkv_ln_b), _pad_rows(rwkv_w_up, LANES), _pad_rows(rwkv_a_up, LANES),
           _pad_rows(rwkv_g_up, 2 * LANES))
    y_rwkv = _rwkv(p_rwkv, prm, B, S)
    y_moba = _moba(p_moba, q_norm_g, k_norm_g, B, S)

    x1, h2, top_idx_t, weights = _out_proj(y_rwkv.reshape(T, W), y_moba.reshape(T, MOBA_WIDTH), x2,
                                           gate1, norm2_g, scale2, shift2, w_out.astype(BF16),
                                           w_router, b_router, S)

    src_tok, tile_expert, n_used, slot_row = _route(top_idx_t)
    y_sorted = _experts(h2, src_tok, tile_expert, n_used, w_gate_up, b_gate_up, w_down, b_down)
    out = _combine(x1, y_sorted, slot_row, weights, gate2, S)
    return out.reshape(B, S, D)


def kernel(x, c, w_ada, b_ada, norm1_g, w_in, rwkv_mu, rwkv_w0, rwkv_w_up, rwkv_a0, rwkv_a_up, rwkv_g_up, rwkv_k_k, rwkv_k_a, rwkv_r_k, rwkv_ln_g, rwkv_ln_b, q_norm_g, k_norm_g, w_out, norm2_g, w_router, b_router, w_gate_up, b_gate_up, w_down, b_down):
    for l in range(w_ada.shape[0]):
        x = _layer(x, c, w_ada[l], b_ada[l], norm1_g[l], w_in[l], rwkv_mu[l], rwkv_w0[l],
                   rwkv_w_up[l], rwkv_a0[l], rwkv_a_up[l], rwkv_g_up[l], rwkv_k_k[l], rwkv_k_a[l],
                   rwkv_r_k[l], rwkv_ln_g[l], rwkv_ln_b[l], q_norm_g[l], k_norm_g[l], w_out[l],
                   norm2_g[l], w_router[l], b_router[l], w_gate_up[l], b_gate_up[l], w_down[l],
                   b_down[l])
    return x
```
